```python
import math
import jax, jax.numpy as jnp
from jax import lax
import numpy as np


D_MODEL = 1024
BATCH = 16
SEQ = 256
DEPTH = 2
DEC_BATCH = 4
DEC_SEQ = 1024
PAST_LEN = 512

GRID_W = 64
HGRN_HEADS = 4
HGRN_HEAD_DIM = 128
HGRN_WIDTH = HGRN_HEADS * HGRN_HEAD_DIM
HGRN_CHUNK = 64
FOURIER_GROUPS = 4
FOURIER_GROUP_DIM = 64
FOURIER_WIDTH = FOURIER_GROUPS * FOURIER_GROUP_DIM
CONV_GROUPS = 4
CONV_WIDTH = 256
CONV_K = 3
N_BRANCHES = 3
IN_SPLITS = (HGRN_WIDTH, HGRN_WIDTH, HGRN_WIDTH, HGRN_WIDTH, HGRN_WIDTH, FOURIER_WIDTH, CONV_WIDTH, CONV_WIDTH, CONV_WIDTH, N_BRANCHES * D_MODEL)
IN_COLS = sum(IN_SPLITS)
N_EXPERTS = 64
TOP_K = 8
D_EXPERT = 256
D_SHARED = 256
ROUTED_SCALE = 2.5
ALPHA = (2 * DEPTH) ** 0.25
BETA = (8 * DEPTH) ** -0.25
LN_EPS = 1e-6
RMS_EPS = 1e-6
F_MIN = 1e-30

kernel_name = 'hybrid_hgrn2_fnet_conv_moe_diffusion_step'

F32 = jnp.float32


def _layer_norm(x, gain=None, bias=None):
    xf = x.astype(F32)
    xc = xf - jnp.mean(xf, axis=-1, keepdims=True)
    y = xc * lax.rsqrt(jnp.mean(xc * xc, axis=-1, keepdims=True) + LN_EPS)
    if gain is not None:
        y = y * gain + bias
    return y.astype(x.dtype)


def _grid_pos_emb(rows, dtype):
    t = jnp.arange(rows * GRID_W)
    r = (t // GRID_W).astype(F32)
    col = (t % GRID_W).astype(F32)
    quarter = D_MODEL // 4
    omega = 1.0 / (10000.0 ** (jnp.arange(quarter, dtype=F32) / quarter))
    ar = r[:, None] * omega
    ac = col[:, None] * omega
    return jnp.concatenate([jnp.sin(ar), jnp.cos(ar), jnp.sin(ac), jnp.cos(ac)], axis=-1).astype(dtype)


def _gated_chunk_scan(q, k, v, log_f, s0):
    bsz, nh, T, _ = q.shape
    dv = v.shape[-1]
    n_chunks = T // HGRN_CHUNK

    def to_chunks(a):
        a = a.reshape(bsz, nh, n_chunks, HGRN_CHUNK, a.shape[-1])
        return jnp.moveaxis(a, 2, 0)

    causal = jnp.tril(jnp.ones((HGRN_CHUNK, HGRN_CHUNK), dtype=bool))[:, :, None]

    def step(S, chunk):
        qc, kc, vc, lfc = chunk
        b = jnp.cumsum(lfc, axis=2)
        diff = jnp.where(causal, b[:, :, :, None, :] - b[:, :, None, :, :], 0.0)
        decay = jnp.where(causal, jnp.exp(diff), 0.0)
        scores = jnp.einsum('bhtd,bhtsd->bhts', qc, decay * kc[:, :, None, :, :])
        o = jnp.einsum('bhts,bhsv->bhtv', scores, vc) + jnp.einsum('bhtd,bhdv->bhtv', qc * jnp.exp(b), S)
        g = b[:, :, -1:, :]
        S = jnp.exp(g[:, :, 0, :, None]) * S + jnp.einsum('bhsd,bhsv->bhdv', kc * jnp.exp(g - b), vc)
        return S, o

    s_final, o = lax.scan(step, s0, (to_chunks(q), to_chunks(k), to_chunks(v), to_chunks(log_f)))
    o = jnp.moveaxis(o, 0, 2).reshape(bsz, nh, T, dv)
    return o, s_final


def _hgrn2_branch(u_q, u_i, u_f_fwd, u_f_bwd, u_g, lb, norm_g, s0):
    bsz, T, _ = u_q.shape

    def heads(a):
        return a.astype(F32).reshape(bsz, T, HGRN_HEADS, HGRN_HEAD_DIM).transpose(0, 2, 1, 3)

    def flip(a):
        return a[:, :, ::-1]

    q = heads(jax.nn.silu(u_q.astype(F32)))
    v = heads(u_i)
    s0 = s0.astype(F32)
    outs = []
    finals = []
    for d, u_f in enumerate((u_f_fwd, u_f_bwd)):
        z = u_f.astype(F32)
        lbd = lb[d]
        f = lbd + (1.0 - lbd) * jax.nn.sigmoid(z)
        log_f = jnp.log(jnp.maximum(f, F_MIN))
        k = (1.0 - lbd) * jax.nn.sigmoid(-z)
        qd, kd, vd, lfd = q, heads(k), v, heads(log_f)
        if d == 1:
            qd, kd, vd, lfd = flip(qd), flip(kd), flip(vd), flip(lfd)
        o, s_fin = _gated_chunk_scan(qd, kd, vd, lfd, s0[:, d])
        if d == 1:
            o = flip(o)
        outs.append(o)
        finals.append(s_fin)
    o = (outs[0] + outs[1]).transpose(0, 2, 1, 3)
    o = o * lax.rsqrt(jnp.mean(o * o, axis=-1, keepdims=True) + RMS_EPS)
    o = o.reshape(bsz, T, HGRN_WIDTH) * norm_g * jax.nn.silu(u_g.astype(F32))
    return o.astype(u_q.dtype), jnp.stack(finals, axis=1)


def _fourier_branch(u):
    bsz, T, _ = u.shape
    z = u.astype(F32).reshape(bsz, T, FOURIER_GROUPS, FOURIER_GROUP_DIM)
    z = jnp.fft.fft2(z, axes=(1, 3), norm='ortho').real
    return z.reshape(bsz, T, FOURIER_WIDTH).astype(u.dtype)


def _short_conv_branch(u_b, u_c, u_h, conv_w):
    z = u_c * u_h
    zp = jnp.pad(z, ((0, 0), (1, 1), (0, 0)))
    y = conv_w[0] * zp[:, :-2] + conv_w[1] * zp[:, 1:-1] + conv_w[2] * zp[:, 2:]
    return u_b * y


def _moe(h, w_router, b_router, w_exp_gate, w_exp_up, w_exp_down, w_sh_gate, w_sh_up, w_sh_down):
    scores = jax.nn.sigmoid(jnp.einsum('btd,de->bte', h.astype(F32), w_router.astype(F32)))
    _, idx = lax.top_k(scores + b_router.astype(F32), TOP_K)
    sel = jnp.take_along_axis(scores, idx, axis=-1)
    wts = ROUTED_SCALE * sel / (jnp.sum(sel, axis=-1, keepdims=True) + 1e-20)
    gate = jnp.sum(jax.nn.one_hot(idx, N_EXPERTS, dtype=F32) * wts[..., None], axis=-2).astype(h.dtype)
    hid = jax.nn.silu(jnp.einsum('btd,edf->btef', h, w_exp_gate)) * jnp.einsum('btd,edf->btef', h, w_exp_up)
    routed = jnp.einsum('btef,efd->btd', hid * gate[..., None], w_exp_down)
    shared = (jax.nn.silu(h @ w_sh_gate) * (h @ w_sh_up)) @ w_sh_down
    return routed + shared


def _trunk_layer(x, cond, s0, lb, w_ada, b_ada, w_in, b_gate, hgrn_norm, conv_w, w_proj_hgrn, w_proj_fourier,
                 w_proj_conv, w_out, ln1_g, ln1_b, ln2_g, ln2_b, w_router, b_router, w_exp_gate, w_exp_up,
                 w_exp_down, w_sh_gate, w_sh_up, w_sh_down):
    bsz, T, _ = x.shape
    mod = jax.nn.silu(cond) @ w_ada + b_ada
    sh1, sc1, g1, sh2, sc2, g2 = [m[:, None, :] for m in jnp.split(mod, 6, axis=-1)]
    h = _layer_norm(x) * (1.0 + sc1) + sh1
    u = h @ w_in
    uq, ui, uff, ufb, ug, ufour, ucb, ucc, uch, ugate = jnp.split(u, np.cumsum(IN_SPLITS)[:-1], axis=-1)
    a, s_final = _hgrn2_branch(uq, ui, uff, ufb, ug, lb, hgrn_norm, s0)
    f = _fourier_branch(ufour)
    cv = _short_conv_branch(ucb, ucc, uch, conv_w)
    gates = jax.nn.sigmoid(ugate + b_gate).reshape(bsz, T, N_BRANCHES, D_MODEL)
    merged = (gates[:, :, 0] * (a @ w_proj_hgrn) + gates[:, :, 1] * (f @ w_proj_fourier)
              + gates[:, :, 2] * (cv @ w_proj_conv))
    x = _layer_norm(ALPHA * x + g1 * (merged @ w_out), ln1_g, ln1_b)
    h = _layer_norm(x) * (1.0 + sc2) + sh2
    ffn = _moe(h, w_router, b_router, w_exp_gate, w_exp_up, w_exp_down, w_sh_gate, w_sh_up, w_sh_down)
    x = _layer_norm(ALPHA * x + g2 * ffn, ln2_g, ln2_b)
    return x, s_final


def setup_inputs(seed: int = 0) -> dict:
    key = jax.random.key(seed)
    ks = iter(jax.random.split(key, 32))

    def nrm(shape, scale):
        return scale * jax.random.normal(next(ks), shape, F32)

    D = D_MODEL
    return {
        'x_prompt': nrm((BATCH, SEQ, D), 1.0),
        'x_sample': nrm((DEC_BATCH, DEC_SEQ, D), 1.0),
        'c': nrm((DEC_BATCH, D), 1.0),
        'state_hgrn': nrm((DEC_BATCH, DEPTH, 2, HGRN_HEADS, HGRN_HEAD_DIM, HGRN_HEAD_DIM), 0.5),
        'c_ctx': nrm((D,), 1.0),
        'w_ada': nrm((DEPTH, D, 6 * D), D ** -0.5),
        'b_ada': nrm((DEPTH, 6 * D), 0.02),
        'w_in': nrm((DEPTH, D, IN_COLS), D ** -0.5),
        'b_gate': nrm((DEPTH, N_BRANCHES * D), 0.02),
        'hgrn_lb': nrm((DEPTH, 2, HGRN_WIDTH), 1.0),
        'hgrn_norm': 1.0 + nrm((DEPTH, HGRN_WIDTH), 0.02),
        'conv_w': nrm((DEPTH, CONV_K, CONV_WIDTH), CONV_K ** -0.5),
        'w_proj_hgrn': nrm((DEPTH, HGRN_WIDTH, D), HGRN_WIDTH ** -0.5),
        'w_proj_fourier': nrm((DEPTH, FOURIER_WIDTH, D), FOURIER_WIDTH ** -0.5),
        'w_proj_conv': nrm((DEPTH, CONV_WIDTH, D), CONV_WIDTH ** -0.5),
        'w_out': nrm((DEPTH, D, D), BETA * D ** -0.5),
        'ln1_g': 1.0 + nrm((DEPTH, D), 0.02),
        'ln1_b': nrm((DEPTH, D), 0.02),
        'ln2_g': 1.0 + nrm((DEPTH, D), 0.02),
        'ln2_b': nrm((DEPTH, D), 0.02),
        'w_router': nrm((DEPTH, D, N_EXPERTS), D ** -0.5),
        'b_router': nrm((DEPTH, N_EXPERTS), 0.01),
        'w_exp_gate': nrm((DEPTH, N_EXPERTS, D, D_EXPERT), D ** -0.5),
        'w_exp_up': nrm((DEPTH, N_EXPERTS, D, D_EXPERT), D ** -0.5),
        'w_exp_down': nrm((DEPTH, N_EXPERTS, D_EXPERT, D), BETA * D_EXPERT ** -0.5),
        'w_sh_gate': nrm((DEPTH, D, D_SHARED), D ** -0.5),
        'w_sh_up': nrm((DEPTH, D, D_SHARED), D ** -0.5),
        'w_sh_down': nrm((DEPTH, D_SHARED, D), BETA * D_SHARED ** -0.5),
    }


def reference(x_prompt, x_sample, c, state_hgrn, c_ctx, w_ada, b_ada, w_in, b_gate, hgrn_lb, hgrn_norm, conv_w,
              w_proj_hgrn, w_proj_fourier, w_proj_conv, w_out, ln1_g, ln1_b, ln2_g, ln2_b, w_router, b_router,
              w_exp_gate, w_exp_up, w_exp_down, w_sh_gate, w_sh_up, w_sh_down):
    lb_sm = jax.nn.softmax(hgrn_lb.astype(F32), axis=0)
    lb_all = jnp.cumsum(lb_sm, axis=0) - lb_sm[:1]
    rows = x_sample.shape[1] // GRID_W
    xc = x_prompt
    xs = x_sample + _grid_pos_emb(rows, x_sample.dtype)[None]
    zero_state = jnp.zeros((x_prompt.shape[0], 2, HGRN_HEADS, HGRN_HEAD_DIM, HGRN_HEAD_DIM), F32)
    ctx_cond = c_ctx[None, :]
    new_states = []
    for l in range(DEPTH):
        lw = (w_ada[l], b_ada[l], w_in[l], b_gate[l], hgrn_norm[l], conv_w[l], w_proj_hgrn[l],
              w_proj_fourier[l], w_proj_conv[l], w_out[l], ln1_g[l], ln1_b[l], ln2_g[l], ln2_b[l],
              w_router[l], b_router[l], w_exp_gate[l], w_exp_up[l], w_exp_down[l], w_sh_gate[l],
              w_sh_up[l], w_sh_down[l])
        xc, s_ctx = _trunk_layer(xc, ctx_cond, zero_state, lb_all[l], *lw)
        new_states.append(s_ctx)
        xs, _ = _trunk_layer(xs, c, state_hgrn[:, l], lb_all[l], *lw)
    state_hgrn_new = jnp.stack(new_states, axis=1)
    return (xc, xs, state_hgrn_new)
```

```python
import functools
import math

import numpy as np
import jax
import jax.numpy as jnp
from jax import lax
from jax.experimental import pallas as pl
from jax.experimental.pallas import tpu as pltpu

F32 = jnp.float32
BF16 = jnp.bfloat16

D_MODEL = 1024
BATCH = 16
SEQ = 256
DEPTH = 2
DEC_BATCH = 4
DEC_SEQ = 1024
GRID_W = 64
HEADS = 4
HEAD_DIM = 128
HGRN_WIDTH = HEADS * HEAD_DIM
FOURIER_GROUPS = 4
FOURIER_GROUP_DIM = 64
FOURIER_WIDTH = FOURIER_GROUPS * FOURIER_GROUP_DIM
CONV_WIDTH = 256
N_BRANCHES = 3
HGRN_COLS = 5 * HGRN_WIDTH
CONV_COLS = 3 * CONV_WIDTH
GATE_COLS = N_BRANCHES * D_MODEL
IN_COLS = HGRN_COLS + FOURIER_WIDTH + CONV_COLS + GATE_COLS
N_EXPERTS = 64
TOP_K = 8
D_EXPERT = 256
D_SHARED = 256
ROUTED_SCALE = 2.5
ALPHA = (2 * DEPTH) ** 0.25
LN_EPS = 1e-6
RMS_EPS = 1e-6
F_MIN = 1e-30

N_CTX = BATCH * SEQ
N_LAT = DEC_BATCH * DEC_SEQ
N_TOK = N_CTX + N_LAT
N_COND = 8

TOK_TILE = 256
MOE_TILE = 1024
CHUNK = 64
WINDOW = 64.0
STATIC_WINDOWS = 4
ADA_TILE = 1536
VMEM_LIMIT = 56 * 1024 * 1024


def _cond_row_of_tile(i, tile):
    ctx_tiles = N_CTX // tile
    per_seq = DEC_SEQ // tile
    return jnp.where(i < ctx_tiles, 0, 1 + (i - ctx_tiles) // per_seq)


def _silu(x):
    return x * jax.nn.sigmoid(x)


def _ln(x):
    mu = jnp.mean(x, axis=-1, keepdims=True)
    xc = x - mu
    var = jnp.mean(xc * xc, axis=-1, keepdims=True)
    return xc * lax.rsqrt(var + LN_EPS)


def _dot(a, b):
    return jnp.dot(a, b, preferred_element_type=F32)


def _dot_nt(a, b):
    return lax.dot_general(a, b, (((1,), (1,)), ((), ())), preferred_element_type=F32)


def _dot_tn(a, b):
    return lax.dot_general(a, b, (((0,), (0,)), ((), ())), preferred_element_type=F32)


@functools.lru_cache(maxsize=None)
def _pos_emb_table():
    rows = DEC_SEQ // GRID_W
    t = np.arange(rows * GRID_W)
    r = (t // GRID_W).astype(np.float32)
    col = (t % GRID_W).astype(np.float32)
    quarter = D_MODEL // 4
    omega = (1.0 / (np.float32(10000.0) ** (np.arange(quarter, dtype=np.float32) / np.float32(quarter)))).astype(np.float32)
    ar = (r[:, None] * omega).astype(np.float32)
    ac = (col[:, None] * omega).astype(np.float32)
    return np.concatenate([np.sin(ar), np.cos(ar), np.sin(ac), np.cos(ac)], axis=-1).astype(np.float32)


@functools.lru_cache(maxsize=None)
def _dft_time(T):
    k = np.arange(T)
    ph = 2.0 * np.pi * ((k[:, None] * k[None, :]) % T) / T
    return np.concatenate([np.cos(ph), -np.sin(ph)], axis=1).astype(np.float32)


@functools.lru_cache(maxsize=None)
def _dft_channel():
    n = FOURIER_GROUP_DIM
    k = np.arange(n)
    ph = 2.0 * np.pi * ((k[:, None] * k[None, :]) % n) / n
    eye = np.eye(FOURIER_GROUPS)
    return np.kron(eye, np.cos(ph)).astype(np.float32), np.kron(eye, np.sin(ph)).astype(np.float32)


def _assemble_kernel(xp_ref, xs_ref, pos_ref, o_ref):
    i = pl.program_id(0)

    @pl.when(i < N_CTX // TOK_TILE)
    def _():
        o_ref[...] = xp_ref[...]

    @pl.when(i >= N_CTX // TOK_TILE)
    def _():
        o_ref[...] = xs_ref[...] + pos_ref[...]


def _assemble(x_prompt, x_sample, pos):
    ctx_tiles = N_CTX // TOK_TILE
    per_seq = DEC_SEQ // TOK_TILE
    return pl.pallas_call(
        _assemble_kernel,
        out_shape=jax.ShapeDtypeStruct((N_TOK, D_MODEL), F32),
        grid=(N_TOK // TOK_TILE,),
        in_specs=[
            pl.BlockSpec((TOK_TILE, D_MODEL), lambda i: (jnp.minimum(i, ctx_tiles - 1), 0)),
            pl.BlockSpec((TOK_TILE, D_MODEL), lambda i: (jnp.maximum(i - ctx_tiles, 0), 0)),
            pl.BlockSpec((TOK_TILE, D_MODEL), lambda i: (jnp.maximum(i - ctx_tiles, 0) % per_seq, 0)),
        ],
        out_specs=pl.BlockSpec((TOK_TILE, D_MODEL), lambda i: (i, 0)),
        compiler_params=pltpu.CompilerParams(dimension_semantics=("arbitrary",)),
        name="assemble_tokens",
    )(x_prompt.reshape(N_CTX, D_MODEL), x_sample.reshape(N_LAT, D_MODEL), pos)


def _ada_kernel(c_ref, w_ref, b_ref, o_ref):
    s = _silu(c_ref[...]).astype(BF16)
    o_ref[...] = _dot(s, w_ref[...].astype(BF16)) + b_ref[...]


def _ada_mod(cond, w_ada, b_ada):
    n_col = 6 * D_MODEL
    return pl.pallas_call(
        _ada_kernel,
        out_shape=jax.ShapeDtypeStruct((DEPTH, N_COND, n_col), F32),
        grid=(DEPTH, n_col // ADA_TILE),
        in_specs=[
            pl.BlockSpec((N_COND, D_MODEL), lambda l, j: (0, 0)),
            pl.BlockSpec((None, D_MODEL, ADA_TILE), lambda l, j: (l, 0, j)),
            pl.BlockSpec((None, 1, ADA_TILE), lambda l, j: (l, 0, j)),
        ],
        out_specs=pl.BlockSpec((None, N_COND, ADA_TILE), lambda l, j: (l, 0, j)),
        compiler_params=pltpu.CompilerParams(dimension_semantics=("arbitrary", "arbitrary"),
                                             vmem_limit_bytes=VMEM_LIMIT),
        name="ada_mod",
    )(cond, w_ada, b_ada.reshape(DEPTH, 1, n_col))


def _in_proj_kernel(x_ref, mod_ref, w_ref, uh_ref, uf_ref, uc_ref, ug_ref):
    sh1 = mod_ref[0:1, :]
    sc1 = mod_ref[1:2, :]
    h = (_ln(x_ref[...]) * (1.0 + sc1) + sh1).astype(BF16)
    c0 = 0
    for ref, n in ((uh_ref, HGRN_COLS), (uf_ref, FOURIER_WIDTH), (uc_ref, CONV_COLS), (ug_ref, GATE_COLS)):
        ref[...] = _dot(h, w_ref[:, c0:c0 + n])
        c0 += n


def _in_proj(x, mod, w_in_bf16, layer):
    n_tiles = N_TOK // TOK_TILE
    widths = (HGRN_COLS, FOURIER_WIDTH, CONV_COLS, GATE_COLS)
    return pl.pallas_call(
        _in_proj_kernel,
        out_shape=[jax.ShapeDtypeStruct((N_TOK, n), F32) for n in widths],
        grid=(n_tiles,),
        in_specs=[
            pl.BlockSpec((TOK_TILE, D_MODEL), lambda i: (i, 0)),
            pl.BlockSpec((None, None, 6, D_MODEL), lambda i: (layer, _cond_row_of_tile(i, TOK_TILE), 0, 0)),
            pl.BlockSpec((None, D_MODEL, IN_COLS), lambda i: (layer, 0, 0)),
        ],
        out_specs=[pl.BlockSpec((TOK_TILE, n), lambda i: (i, 0)) for n in widths],
        compiler_params=pltpu.CompilerParams(dimension_semantics=("arbitrary",), vmem_limit_bytes=VMEM_LIMIT),
        name="in_proj",
    )(x, mod, w_in_bf16)


def _mixer_kernel(*refs, T, has_state):
    if has_state:
        (uh_ref, uf_ref, uc_ref, lb_ref, ng_ref, cw_ref, dft_ref, bdc_ref, bds_ref, s0_ref, _mix_in,
         mix_ref, st_ref, o_ref) = refs
        sfin_ref = None
    else:
        (uh_ref, uf_ref, uc_ref, lb_ref, ng_ref, cw_ref, dft_ref, bdc_ref, bds_ref,
         mix_ref, sfin_ref, st_ref, o_ref) = refs
        s0_ref = None
    n_chunks = T // CHUNK
    W = HGRN_WIDTH

    for d in range(2):
        for h in range(HEADS):
            if has_state:
                st_ref[d, h] = s0_ref[0, d, h].T
            else:
                st_ref[d, h] = jnp.zeros((HEAD_DIM, HEAD_DIM), F32)
    o_ref[...] = jnp.zeros_like(o_ref)

    row = lax.broadcasted_iota(jnp.int32, (CHUNK, CHUNK), 0)
    col = lax.broadcasted_iota(jnp.int32, (CHUNK, CHUNK), 1)
    keep = (col <= row, col >= row)
    tri = tuple(k.astype(F32).astype(BF16) for k in keep)

    def chunk_step(i, carry):
        for d in range(2):
            c = i if d == 0 else n_chunks - 1 - i
            rows = pl.ds(pl.multiple_of(c * CHUNK, CHUNK), CHUNK)
            q = _silu(uh_ref[rows, 0:W])
            v = uh_ref[rows, W:2 * W].astype(BF16)
            z = uh_ref[rows, (2 + d) * W:(3 + d) * W]
            lb = lb_ref[d:d + 1, :]
            e = jnp.exp(-jnp.abs(z))
            r = 1.0 / (1.0 + e)
            er = e * r
            pos = z >= 0.0
            sig_p = jnp.where(pos, r, er)
            sig_n = jnp.where(pos, er, r)
            f = lb + (1.0 - lb) * sig_p
            lf = jnp.log(jnp.maximum(f, F_MIN))
            k = (1.0 - lb) * sig_n
            lf_hi = lf.astype(BF16)
            lf_lo = (lf - lf_hi.astype(F32)).astype(BF16)
            b = _dot(tri[d], lf_hi) + _dot(tri[d], lf_lo)
            g = b[CHUNK - 1:CHUNK, :] if d == 0 else b[0:1, :]

            def window(p_level, q=q, k=k, b=b):
                dist = -p_level - b
                inside = jnp.logical_and(dist >= 0.0, dist < WINDOW)
                k_w = jnp.where(inside, k * jnp.exp(jnp.minimum(dist, WINDOW)), 0.0)
                q_w = q * jnp.exp(jnp.minimum(-dist, 0.0))
                return q_w.astype(BF16), k_w.astype(BF16)

            wins = [window(WINDOW * p) for p in range(STATIC_WINDOWS)]
            q_abs = (q * jnp.exp(b)).astype(BF16)
            k_end = (k * jnp.exp(g - b)).astype(BF16)
            decay = jnp.exp(g)
            outs = []
            for h in range(HEADS):
                ls = slice(h * HEAD_DIM, (h + 1) * HEAD_DIM)
                q_cat = jnp.concatenate([w[0][:, ls] for w in wins], axis=-1)
                k_cat = jnp.concatenate([w[1][:, ls] for w in wins], axis=-1)
                a = jnp.where(keep[d], _dot_nt(q_cat, k_cat), 0.0).astype(BF16)
                s_t = st_ref[d, h]
                o_h = _dot(a, v[:, ls]) + _dot_nt(q_abs[:, ls], s_t.astype(BF16))
                st_ref[d, h] = s_t * decay[:, ls] + _dot_tn(v[:, ls], k_end[:, ls])
                outs.append(o_h)
            o_ref[rows, :] += jnp.concatenate(outs, axis=-1)

            n_windows = jnp.floor(-jnp.min(b) * (1.0 / WINDOW)).astype(jnp.int32) + 1

            def extra_window(p, carry2, v=v, rows=rows, window=window, d=d):
                q_w, k_w = window(WINDOW * p.astype(F32))
                extra = []
                for h in range(HEADS):
                    ls = slice(h * HEAD_DIM, (h + 1) * HEAD_DIM)
                    a = jnp.where(keep[d], _dot_nt(q_w[:, ls], k_w[:, ls]), 0.0).astype(BF16)
                    extra.append(_dot(a, v[:, ls]))
                o_ref[rows, :] += jnp.concatenate(extra, axis=-1)
                return carry2

            lax.fori_loop(STATIC_WINDOWS, n_windows, extra_window, 0)
        return carry

    lax.fori_loop(0, n_chunks, chunk_step, 0)

    if sfin_ref is not None:
        for d in range(2):
            for h in range(HEADS):
                sfin_ref[0, d, h] = st_ref[d, h].T

    o = o_ref[...]
    parts = []
    for h in range(HEADS):
        oh = o[:, h * HEAD_DIM:(h + 1) * HEAD_DIM]
        parts.append(oh * lax.rsqrt(jnp.mean(oh * oh, axis=-1, keepdims=True) + RMS_EPS))
    a_out = jnp.concatenate(parts, axis=-1) * ng_ref[...] * _silu(uh_ref[:, 4 * W:5 * W])
    mix_ref[:, 0:W] = a_out.astype(BF16)

    zf = uf_ref[...].astype(BF16)
    zc = _dot(zf, bdc_ref[...]).astype(BF16)
    zs = _dot(zf, bds_ref[...]).astype(BF16)
    four = _dot(dft_ref[...], jnp.concatenate([zc, zs], axis=0)) * (1.0 / math.sqrt(T * FOURIER_GROUP_DIM))
    mix_ref[:, W:W + FOURIER_WIDTH] = four.astype(BF16)

    cb = uc_ref[:, 0:CONV_WIDTH]
    zz = uc_ref[:, CONV_WIDTH:2 * CONV_WIDTH] * uc_ref[:, 2 * CONV_WIDTH:3 * CONV_WIDTH]
    t_idx = lax.broadcasted_iota(jnp.int32, (T, CONV_WIDTH), 0)
    z_prev = jnp.where(t_idx == 0, 0.0, pltpu.roll(zz, 1, axis=0))
    z_next = jnp.where(t_idx == T - 1, 0.0, pltpu.roll(zz, T - 1, axis=0))
    y = cw_ref[0:1, :] * z_prev + cw_ref[1:2, :] * zz + cw_ref[2:3, :] * z_next
    mix_ref[:, W + FOURIER_WIDTH:] = (cb * y).astype(BF16)


def _mixer(uh, uf, uc, lb, norm_g, conv_w, dft, bdc, bds, T, n_seq, blk0, state0=None, layer=0, mix_prev=None):
    has_state = state0 is not None
    const2 = lambda s: (0, 0)
    in_specs = [
        pl.BlockSpec((T, HGRN_COLS), lambda s: (blk0 + s, 0)),
        pl.BlockSpec((T, FOURIER_WIDTH), lambda s: (blk0 + s, 0)),
        pl.BlockSpec((T, CONV_COLS), lambda s: (blk0 + s, 0)),
        pl.BlockSpec((2, HGRN_WIDTH), const2),
        pl.BlockSpec((1, HGRN_WIDTH), const2),
        pl.BlockSpec((3, CONV_WIDTH), const2),
        pl.BlockSpec((T, 2 * T), const2),
        pl.BlockSpec((FOURIER_WIDTH, FOURIER_WIDTH), const2),
        pl.BlockSpec((FOURIER_WIDTH, FOURIER_WIDTH), const2),
    ]
    args = [uh, uf, uc, lb, norm_g, conv_w, dft, bdc, bds]
    mix_shape = jax.ShapeDtypeStruct((N_TOK, D_MODEL), BF16)
    mix_spec = pl.BlockSpec((T, D_MODEL), lambda s: (blk0 + s, 0))
    st_block = (1, 2, HEADS, HEAD_DIM, HEAD_DIM)
    aliases = {}
    if has_state:
        in_specs.append(pl.BlockSpec((1, None) + st_block[1:], lambda s: (s, layer, 0, 0, 0, 0)))
        in_specs.append(pl.BlockSpec(memory_space=pl.ANY))
        args += [state0, mix_prev]
        aliases = {len(args) - 1: 0}
        out_shape = [mix_shape]
        out_specs = [mix_spec]
    else:
        out_shape = [mix_shape, jax.ShapeDtypeStruct((n_seq, 2, HEADS, HEAD_DIM, HEAD_DIM), F32)]
        out_specs = [mix_spec, pl.BlockSpec(st_block, lambda s: (s, 0, 0, 0, 0))]
    return pl.pallas_call(
        functools.partial(_mixer_kernel, T=T, has_state=has_state),
        out_shape=out_shape,
        grid=(n_seq,),
        in_specs=in_specs,
        out_specs=out_specs,
        scratch_shapes=[pltpu.VMEM((2, HEADS, HEAD_DIM, HEAD_DIM), F32), pltpu.VMEM((T, HGRN_WIDTH), F32)],
        input_output_aliases=aliases,
        compiler_params=pltpu.CompilerParams(dimension_semantics=("arbitrary",), vmem_limit_bytes=VMEM_LIMIT),
        name="mixer_latent" if has_state else "mixer_context",
    )(*args)


def _merge_kernel(x_ref, mix_ref, ug_ref, mod_ref, bg_ref, wp_ref, wo_ref, g1_ref, b1_ref, wrh_ref, wrl_ref, br_ref,
                  x1_ref, h2_ref, gate_ref):
    W = HGRN_WIDTH
    gates = jax.nn.sigmoid(ug_ref[...] + bg_ref[...])
    edges = (0, W, W + FOURIER_WIDTH, D_MODEL)
    merged = None
    for j in range(N_BRANCHES):
        p = _dot(mix_ref[:, edges[j]:edges[j + 1]], wp_ref[edges[j]:edges[j + 1], :])
        term = gates[:, j * D_MODEL:(j + 1) * D_MODEL] * p
        merged = term if merged is None else merged + term
    y = _dot(merged.astype(BF16), wo_ref[...])
    x1 = _ln(ALPHA * x_ref[...] + mod_ref[2:3, :] * y) * g1_ref[...] + b1_ref[...]
    x1_ref[...] = x1
    h2 = _ln(x1) * (1.0 + mod_ref[4:5, :]) + mod_ref[3:4, :]
    h2_hi = h2.astype(BF16)
    h2_ref[...] = h2_hi
    h2_lo = (h2 - h2_hi.astype(F32)).astype(BF16)
    logits = _dot(h2_hi, wrh_ref[...]) + (_dot(h2_hi, wrl_ref[...]) + _dot(h2_lo, wrh_ref[...]))
    scores = jax.nn.sigmoid(logits)
    sel = scores + br_ref[...]
    lane = lax.broadcasted_iota(jnp.int32, sel.shape, 1)
    chosen = jnp.zeros(sel.shape, jnp.bool_)
    for _ in range(TOP_K):
        top = jnp.max(sel, axis=-1, keepdims=True)
        first = jnp.min(jnp.where(sel == top, lane, N_EXPERTS), axis=-1, keepdims=True)
        hit = lane == first
        chosen = jnp.logical_or(chosen, hit)
        sel = jnp.where(hit, -jnp.inf, sel)
    picked = jnp.where(chosen, scores, 0.0)
    denom = jnp.sum(picked, axis=-1, keepdims=True) + 1e-20
    gate_ref[...] = ROUTED_SCALE * picked / denom


def _merge(x, mix, ug, mod, b_gate, wp, wo, ln_g, ln_b, wr_hi, wr_lo, b_router, layer):
    n_tiles = N_TOK // TOK_TILE
    tok = lambda n: pl.BlockSpec((TOK_TILE, n), lambda i: (i, 0))
    const2 = lambda i: (0, 0)
    return pl.pallas_call(
        _merge_kernel,
        out_shape=[jax.ShapeDtypeStruct((N_TOK, D_MODEL), F32), jax.ShapeDtypeStruct((N_TOK, D_MODEL), BF16),
                   jax.ShapeDtypeStruct((N_TOK, N_EXPERTS), F32)],
        grid=(n_tiles,),
        in_specs=[
            tok(D_MODEL), tok(D_MODEL), tok(GATE_COLS),
            pl.BlockSpec((None, None, 6, D_MODEL), lambda i: (layer, _cond_row_of_tile(i, TOK_TILE), 0, 0)),
            pl.BlockSpec((1, GATE_COLS), const2),
            pl.BlockSpec((D_MODEL, D_MODEL), const2),
            pl.BlockSpec((D_MODEL, D_MODEL), const2),
            pl.BlockSpec((1, D_MODEL), const2),
            pl.BlockSpec((1, D_MODEL), const2),
            pl.BlockSpec((D_MODEL, N_EXPERTS), const2),
            pl.BlockSpec((D_MODEL, N_EXPERTS), const2),
            pl.BlockSpec((1, N_EXPERTS), const2),
        ],
        out_specs=[tok(D_MODEL), tok(D_MODEL), tok(N_EXPERTS)],
        compiler_params=pltpu.CompilerParams(dimension_semantics=("arbitrary",), vmem_limit_bytes=VMEM_LIMIT),
        name="merge_router",
    )(x, mix, ug, mod, b_gate, wp, wo, ln_g, ln_b, wr_hi, wr_lo, b_router)


def _moe_kernel(h_ref, gate_ref, x1_ref, mod_ref, weg_ref, weu_ref, wed_ref, wsg_ref, wsu_ref, wsd_ref, g2_ref, b2_ref,
                o_ref, acc_ref):
    e = pl.program_id(1)
    h = h_ref[...]

    @pl.when(e == 0)
    def _():
        hid = _silu(_dot(h, wsg_ref[...])) * _dot(h, wsu_ref[...])
        acc_ref[...] = _dot(hid.astype(BF16), wsd_ref[...])

    lane = lax.broadcasted_iota(jnp.int32, gate_ref.shape, 1)
    g_col = jnp.sum(jnp.where(lane == e, gate_ref[...], 0.0), axis=-1, keepdims=True)
    hid = _silu(_dot(h, weg_ref[...].astype(BF16))) * _dot(h, weu_ref[...].astype(BF16)) * g_col
    acc_ref[...] += _dot(hid.astype(BF16), wed_ref[...].astype(BF16))

    @pl.when(e == N_EXPERTS - 1)
    def _():
        r = ALPHA * x1_ref[...] + mod_ref[5:6, :] * acc_ref[...]
        o_ref[...] = _ln(r) * g2_ref[...] + b2_ref[...]


def _moe(h2, gate, x1, mod, weg, weu, wed, wsg, wsu, wsd, ln_g, ln_b, layer):
    n_tiles = N_TOK // MOE_TILE
    tok = lambda n: pl.BlockSpec((MOE_TILE, n), lambda i, e: (i, 0))
    const2 = lambda i, e: (0, 0)
    return pl.pallas_call(
        _moe_kernel,
        out_shape=jax.ShapeDtypeStruct((N_TOK, D_MODEL), F32),
        grid=(n_tiles, N_EXPERTS),
        in_specs=[
            tok(D_MODEL), tok(N_EXPERTS), tok(D_MODEL),
            pl.BlockSpec((None, None, 6, D_MODEL), lambda i, e: (layer, _cond_row_of_tile(i, MOE_TILE), 0, 0)),
            pl.BlockSpec((None, None, D_MODEL, D_EXPERT), lambda i, e: (layer, e, 0, 0)),
            pl.BlockSpec((None, None, D_MODEL, D_EXPERT), lambda i, e: (layer, e, 0, 0)),
            pl.BlockSpec((None, None, D_EXPERT, D_MODEL), lambda i, e: (layer, e, 0, 0)),
            pl.BlockSpec((D_MODEL, D_SHARED), const2),
            pl.BlockSpec((D_MODEL, D_SHARED), const2),
            pl.BlockSpec((D_SHARED, D_MODEL), const2),
            pl.BlockSpec((1, D_MODEL), const2),
            pl.BlockSpec((1, D_MODEL), const2),
        ],
        out_specs=tok(D_MODEL),
        scratch_shapes=[pltpu.VMEM((MOE_TILE, D_MODEL), F32)],
        compiler_params=pltpu.CompilerParams(dimension_semantics=("arbitrary", "arbitrary"),
                                             vmem_limit_bytes=VMEM_LIMIT),
        name="moe_dense",
    )(h2, gate, x1, mod, weg, weu, wed, wsg, wsu, wsd, ln_g, ln_b)


def kernel(x_prompt, x_sample, c, state_hgrn, c_ctx, w_ada, b_ada, w_in, b_gate, hgrn_lb, hgrn_norm, conv_w,
           w_proj_hgrn, w_proj_fourier, w_proj_conv, w_out, ln1_g, ln1_b, ln2_g, ln2_b, w_router, b_router,
           w_exp_gate, w_exp_up, w_exp_down, w_sh_gate, w_sh_up, w_sh_down):
    lb_sm = jax.nn.softmax(hgrn_lb.astype(F32), axis=0)
    lb_all = jnp.cumsum(lb_sm, axis=0) - lb_sm[:1]

    pos = jnp.asarray(_pos_emb_table())
    dft_ctx = jnp.asarray(_dft_time(SEQ)).astype(BF16)
    dft_lat = jnp.asarray(_dft_time(DEC_SEQ)).astype(BF16)
    bdc, bds = (jnp.asarray(m).astype(BF16) for m in _dft_channel())

    cond = jnp.concatenate([c_ctx[None, :], c, jnp.zeros((N_COND - 1 - DEC_BATCH, D_MODEL), F32)], axis=0)
    mod = _ada_mod(cond, w_ada, b_ada).reshape(DEPTH, N_COND, 6, D_MODEL)

    w_in_b = w_in.astype(BF16)
    wp_b = jnp.concatenate([w_proj_hgrn, w_proj_fourier, w_proj_conv], axis=1).astype(BF16)
    wo_b = w_out.astype(BF16)
    wr_hi = w_router.astype(BF16)
    wr_lo = (w_router - wr_hi.astype(F32)).astype(BF16)
    wsg_b, wsu_b, wsd_b = w_sh_gate.astype(BF16), w_sh_up.astype(BF16), w_sh_down.astype(BF16)

    x = _assemble(x_prompt, x_sample, pos)
    new_states = []
    for l in range(DEPTH):
        uh, uf, uc, ug = _in_proj(x, mod, w_in_b, l)
        small = (lb_all[l], hgrn_norm[l][None, :], conv_w[l])
        mix, s_ctx = _mixer(uh, uf, uc, *small, dft_ctx, bdc, bds, T=SEQ, n_seq=BATCH, blk0=0)
        (mix,) = _mixer(uh, uf, uc, *small, dft_lat, bdc, bds, T=DEC_SEQ, n_seq=DEC_BATCH, blk0=N_CTX // DEC_SEQ,
                        state0=state_hgrn, layer=l, mix_prev=mix)
        new_states.append(s_ctx)
        x1, h2, gate = _merge(x, mix, ug, mod, b_gate[l][None, :], wp_b[l], wo_b[l], ln1_g[l][None, :],
                              ln1_b[l][None, :], wr_hi[l], wr_lo[l], b_router[l][None, :], l)
        x = _moe(h2, gate, x1, mod, w_exp_gate, w_exp_up, w_exp_down, wsg_b[l], wsu_b[l], wsd_b[l],
                 ln2_g[l][None, :], ln2_b[l][None, :], l)
    y_prompt = x[:N_CTX].reshape(BATCH, SEQ, D_MODEL)
    y_sample = x[N_CTX:].reshape(DEC_BATCH, DEC_SEQ, D_MODEL)
    return (y_prompt, y_sample, jnp.stack(new_states, axis=1))
```

```python
import functools
import math

import numpy as np
import jax
import jax.numpy as jnp
from jax import lax
from jax.experimental import pallas as pl
from jax.experimental.pallas import tpu as pltpu

F32 = jnp.float32
BF16 = jnp.bfloat16

D_MODEL = 1024
BATCH = 16
SEQ = 256
DEPTH = 2
DEC_BATCH = 4
DEC_SEQ = 1024
GRID_W = 64
HEADS = 4
HEAD_DIM = 128
HGRN_WIDTH = HEADS * HEAD_DIM
FOURIER_GROUPS = 4
FOURIER_GROUP_DIM = 64
FOURIER_WIDTH = FOURIER_GROUPS * FOURIER_GROUP_DIM
CONV_WIDTH = 256
N_BRANCHES = 3
HGRN_COLS = 5 * HGRN_WIDTH
CONV_COLS = 3 * CONV_WIDTH
GATE_COLS = N_BRANCHES * D_MODEL
IN_COLS = HGRN_COLS + FOURIER_WIDTH + CONV_COLS + GATE_COLS
N_EXPERTS = 64
TOP_K = 8
D_EXPERT = 256
D_SHARED = 256
ROUTED_SCALE = 2.5
ALPHA = (2 * DEPTH) ** 0.25
LN_EPS = 1e-6
RMS_EPS = 1e-6
F_MIN = 1e-30

N_CTX = BATCH * SEQ
N_LAT = DEC_BATCH * DEC_SEQ
N_TOK = N_CTX + N_LAT
N_COND = 8

TOK_TILE = 256
CHUNK = 64
WINDOW = 64.0
STATIC_WINDOWS = 4
ADA_TILE = 1536
VMEM_LIMIT = 56 * 1024 * 1024

N_TILES = N_TOK // TOK_TILE
ROW_CHUNK = 16
TILE_CHUNKS = (TOK_TILE * TOP_K + N_EXPERTS * (ROW_CHUNK - 1)) // ROW_CHUNK
DISPATCH_ROWS = 512
TILE_ROWS = -(-TILE_CHUNKS * ROW_CHUNK // DISPATCH_ROWS) * DISPATCH_ROWS
GMM_TILE = 256
CHUNKS_PER_GMM_TILE = GMM_TILE // ROW_CHUNK
MAX_GMM_TILES = (N_TILES * TILE_CHUNKS + N_EXPERTS * (CHUNKS_PER_GMM_TILE - 1)) // CHUNKS_PER_GMM_TILE
SORTED_ROWS = MAX_GMM_TILES * GMM_TILE


def _cond_row_of_tile(i, tile):
    ctx_tiles = N_CTX // tile
    per_seq = DEC_SEQ // tile
    return jnp.where(i < ctx_tiles, 0, 1 + (i - ctx_tiles) // per_seq)


def _silu(x):
    return x * jax.nn.sigmoid(x)


def _ln(x):
    mu = jnp.mean(x, axis=-1, keepdims=True)
    xc = x - mu
    var = jnp.mean(xc * xc, axis=-1, keepdims=True)
    return xc * lax.rsqrt(var + LN_EPS)


def _dot(a, b):
    return jnp.dot(a, b, preferred_element_type=F32)


def _dot_nt(a, b):
    return lax.dot_general(a, b, (((1,), (1,)), ((), ())), preferred_element_type=F32)


def _dot_tn(a, b):
    return lax.dot_general(a, b, (((0,), (0,)), ((), ())), preferred_element_type=F32)


@functools.lru_cache(maxsize=None)
def _pos_emb_table():
    rows = DEC_SEQ // GRID_W
    t = np.arange(rows * GRID_W)
    r = (t // GRID_W).astype(np.float32)
    col = (t % GRID_W).astype(np.float32)
    quarter = D_MODEL // 4
    omega = (1.0 / (np.float32(10000.0) ** (np.arange(quarter, dtype=np.float32) / np.float32(quarter)))).astype(np.float32)
    ar = (r[:, None] * omega).astype(np.float32)
    ac = (col[:, None] * omega).astype(np.float32)
    return np.concatenate([np.sin(ar), np.cos(ar), np.sin(ac), np.cos(ac)], axis=-1).astype(np.float32)


@functools.lru_cache(maxsize=None)
def _dft_time(T):
    k = np.arange(T)
    ph = 2.0 * np.pi * ((k[:, None] * k[None, :]) % T) / T
    return np.concatenate([np.cos(ph), -np.sin(ph)], axis=1).astype(np.float32)


@functools.lru_cache(maxsize=None)
def _dft_channel():
    n = FOURIER_GROUP_DIM
    k = np.arange(n)
    ph = 2.0 * np.pi * ((k[:, None] * k[None, :]) % n) / n
    eye = np.eye(FOURIER_GROUPS)
    return np.kron(eye, np.cos(ph)).astype(np.float32), np.kron(eye, np.sin(ph)).astype(np.float32)


def _assemble_kernel(xp_ref, xs_ref, pos_ref, o_ref):
    i = pl.program_id(0)

    @pl.when(i < N_CTX // TOK_TILE)
    def _():
        o_ref[...] = xp_ref[...]

    @pl.when(i >= N_CTX // TOK_TILE)
    def _():
        o_ref[...] = xs_ref[...] + pos_ref[...]


def _assemble(x_prompt, x_sample, pos):
    ctx_tiles = N_CTX // TOK_TILE
    per_seq = DEC_SEQ // TOK_TILE
    return pl.pallas_call(
        _assemble_kernel,
        out_shape=jax.ShapeDtypeStruct((N_TOK, D_MODEL), F32),
        grid=(N_TOK // TOK_TILE,),
        in_specs=[
            pl.BlockSpec((TOK_TILE, D_MODEL), lambda i: (jnp.minimum(i, ctx_tiles - 1), 0)),
            pl.BlockSpec((TOK_TILE, D_MODEL), lambda i: (jnp.maximum(i - ctx_tiles, 0), 0)),
            pl.BlockSpec((TOK_TILE, D_MODEL), lambda i: (jnp.maximum(i - ctx_tiles, 0) % per_seq, 0)),
        ],
        out_specs=pl.BlockSpec((TOK_TILE, D_MODEL), lambda i: (i, 0)),
        compiler_params=pltpu.CompilerParams(dimension_semantics=("arbitrary",)),
        name="assemble_tokens",
    )(x_prompt.reshape(N_CTX, D_MODEL), x_sample.reshape(N_LAT, D_MODEL), pos)


def _ada_kernel(c_ref, w_ref, b_ref, o_ref):
    s = _silu(c_ref[...]).astype(BF16)
    o_ref[...] = _dot(s, w_ref[...].astype(BF16)) + b_ref[...]


def _ada_mod(cond, w_ada, b_ada):
    n_col = 6 * D_MODEL
    return pl.pallas_call(
        _ada_kernel,
        out_shape=jax.ShapeDtypeStruct((DEPTH, N_COND, n_col), F32),
        grid=(DEPTH, n_col // ADA_TILE),
        in_specs=[
            pl.BlockSpec((N_COND, D_MODEL), lambda l, j: (0, 0)),
            pl.BlockSpec((None, D_MODEL, ADA_TILE), lambda l, j: (l, 0, j)),
            pl.BlockSpec((None, 1, ADA_TILE), lambda l, j: (l, 0, j)),
        ],
        out_specs=pl.BlockSpec((None, N_COND, ADA_TILE), lambda l, j: (l, 0, j)),
        compiler_params=pltpu.CompilerParams(dimension_semantics=("arbitrary", "arbitrary"),
                                             vmem_limit_bytes=VMEM_LIMIT),
        name="ada_mod",
    )(cond, w_ada, b_ada.reshape(DEPTH, 1, n_col))


def _in_proj_kernel(x_ref, mod_ref, w_ref, uh_ref, uf_ref, uc_ref, ug_ref):
    sh1 = mod_ref[0:1, :]
    sc1 = mod_ref[1:2, :]
    h = (_ln(x_ref[...]) * (1.0 + sc1) + sh1).astype(BF16)
    c0 = 0
    for ref, n in ((uh_ref, HGRN_COLS), (uf_ref, FOURIER_WIDTH), (uc_ref, CONV_COLS), (ug_ref, GATE_COLS)):
        ref[...] = _dot(h, w_ref[:, c0:c0 + n])
        c0 += n


def _in_proj(x, mod, w_in_bf16, layer):
    n_tiles = N_TOK // TOK_TILE
    widths = (HGRN_COLS, FOURIER_WIDTH, CONV_COLS, GATE_COLS)
    return pl.pallas_call(
        _in_proj_kernel,
        out_shape=[jax.ShapeDtypeStruct((N_TOK, n), F32) for n in widths],
        grid=(n_tiles,),
        in_specs=[
            pl.BlockSpec((TOK_TILE, D_MODEL), lambda i: (i, 0)),
            pl.BlockSpec((None, None, 6, D_MODEL), lambda i: (layer, _cond_row_of_tile(i, TOK_TILE), 0, 0)),
            pl.BlockSpec((None, D_MODEL, IN_COLS), lambda i: (layer, 0, 0)),
        ],
        out_specs=[pl.BlockSpec((TOK_TILE, n), lambda i: (i, 0)) for n in widths],
        compiler_params=pltpu.CompilerParams(dimension_semantics=("arbitrary",), vmem_limit_bytes=VMEM_LIMIT),
        name="in_proj",
    )(x, mod, w_in_bf16)


def _mixer_kernel(*refs, T, has_state):
    if has_state:
        (uh_ref, uf_ref, uc_ref, lb_ref, ng_ref, cw_ref, dft_ref, bdc_ref, bds_ref, s0_ref,
         mix_ref, st_ref, o_ref) = refs
        sfin_ref = None
    else:
        (uh_ref, uf_ref, uc_ref, lb_ref, ng_ref, cw_ref, dft_ref, bdc_ref, bds_ref,
         mix_ref, sfin_ref, st_ref, o_ref) = refs
        s0_ref = None
    n_chunks = T // CHUNK
    W = HGRN_WIDTH

    for d in range(2):
        for h in range(HEADS):
            if has_state:
                st_ref[d, h] = s0_ref[0, d, h].T
            else:
                st_ref[d, h] = jnp.zeros((HEAD_DIM, HEAD_DIM), F32)
    o_ref[...] = jnp.zeros_like(o_ref)

    row = lax.broadcasted_iota(jnp.int32, (CHUNK, CHUNK), 0)
    col = lax.broadcasted_iota(jnp.int32, (CHUNK, CHUNK), 1)
    keep = (col <= row, col >= row)
    tri = tuple(k.astype(F32).astype(BF16) for k in keep)

    def chunk_step(i, carry):
        for d in range(2):
            c = i if d == 0 else n_chunks - 1 - i
            rows = pl.ds(pl.multiple_of(c * CHUNK, CHUNK), CHUNK)
            q = _silu(uh_ref[rows, 0:W])
            v = uh_ref[rows, W:2 * W].astype(BF16)
            z = uh_ref[rows, (2 + d) * W:(3 + d) * W]
            lb = lb_ref[d:d + 1, :]
            e = jnp.exp(-jnp.abs(z))
            r = 1.0 / (1.0 + e)
            er = e * r
            pos = z >= 0.0
            sig_p = jnp.where(pos, r, er)
            sig_n = jnp.where(pos, er, r)
            f = lb + (1.0 - lb) * sig_p
            lf = jnp.log(jnp.maximum(f, F_MIN))
            k = (1.0 - lb) * sig_n
            lf_hi = lf.astype(BF16)
            lf_lo = (lf - lf_hi.astype(F32)).astype(BF16)
            b = _dot(tri[d], lf_hi) + _dot(tri[d], lf_lo)
            g = b[CHUNK - 1:CHUNK, :] if d == 0 else b[0:1, :]

            def window(p_level, q=q, k=k, b=b):
                dist = -p_level - b
                inside = jnp.logical_and(dist >= 0.0, dist < WINDOW)
                k_w = jnp.where(inside, k * jnp.exp(jnp.minimum(dist, WINDOW)), 0.0)
                q_w = q * jnp.exp(jnp.minimum(-dist, 0.0))
                return q_w.astype(BF16), k_w.astype(BF16)

            wins = [window(WINDOW * p) for p in range(STATIC_WINDOWS)]
            q_abs = (q * jnp.exp(b)).astype(BF16)
            k_end = (k * jnp.exp(g - b)).astype(BF16)
            decay = jnp.exp(g)
            outs = []
            for h in range(HEADS):
                ls = slice(h * HEAD_DIM, (h + 1) * HEAD_DIM)
                q_cat = jnp.concatenate([w[0][:, ls] for w in wins], axis=-1)
                k_cat = jnp.concatenate([w[1][:, ls] for w in wins], axis=-1)
                a = jnp.where(keep[d], _dot_nt(q_cat, k_cat), 0.0).astype(BF16)
                s_t = st_ref[d, h]
                o_h = _dot(a, v[:, ls]) + _dot_nt(q_abs[:, ls], s_t.astype(BF16))
                st_ref[d, h] = s_t * decay[:, ls] + _dot_tn(v[:, ls], k_end[:, ls])
                outs.append(o_h)
            o_ref[rows, :] += jnp.concatenate(outs, axis=-1)

            n_windows = jnp.floor(-jnp.min(b) * (1.0 / WINDOW)).astype(jnp.int32) + 1

            def extra_window(p, carry2, v=v, rows=rows, window=window, d=d):
                q_w, k_w = window(WINDOW * p.astype(F32))
                extra = []
                for h in range(HEADS):
                    ls = slice(h * HEAD_DIM, (h + 1) * HEAD_DIM)
                    a = jnp.where(keep[d], _dot_nt(q_w[:, ls], k_w[:, ls]), 0.0).astype(BF16)
                    extra.append(_dot(a, v[:, ls]))
                o_ref[rows, :] += jnp.concatenate(extra, axis=-1)
                return carry2

            lax.fori_loop(STATIC_WINDOWS, n_windows, extra_window, 0)
        return carry

    lax.fori_loop(0, n_chunks, chunk_step, 0)

    if sfin_ref is not None:
        for d in range(2):
            for h in range(HEADS):
                sfin_ref[0, d, h] = st_ref[d, h].T

    o = o_ref[...]
    parts = []
    for h in range(HEADS):
        oh = o[:, h * HEAD_DIM:(h + 1) * HEAD_DIM]
        parts.append(oh * lax.rsqrt(jnp.mean(oh * oh, axis=-1, keepdims=True) + RMS_EPS))
    a_out = jnp.concatenate(parts, axis=-1) * ng_ref[...] * _silu(uh_ref[:, 4 * W:5 * W])
    mix_ref[:, 0:W] = a_out.astype(BF16)

    zf = uf_ref[...].astype(BF16)
    zc = _dot(zf, bdc_ref[...]).astype(BF16)
    zs = _dot(zf, bds_ref[...]).astype(BF16)
    four = _dot(dft_ref[...], jnp.concatenate([zc, zs], axis=0)) * (1.0 / math.sqrt(T * FOURIER_GROUP_DIM))
    mix_ref[:, W:W + FOURIER_WIDTH] = four.astype(BF16)

    cb = uc_ref[:, 0:CONV_WIDTH]
    zz = uc_ref[:, CONV_WIDTH:2 * CONV_WIDTH] * uc_ref[:, 2 * CONV_WIDTH:3 * CONV_WIDTH]
    t_idx = lax.broadcasted_iota(jnp.int32, (T, CONV_WIDTH), 0)
    z_prev = jnp.where(t_idx == 0, 0.0, pltpu.roll(zz, 1, axis=0))
    z_next = jnp.where(t_idx == T - 1, 0.0, pltpu.roll(zz, T - 1, axis=0))
    y = cw_ref[0:1, :] * z_prev + cw_ref[1:2, :] * zz + cw_ref[2:3, :] * z_next
    mix_ref[:, W + FOURIER_WIDTH:] = (cb * y).astype(BF16)


def _mixer(uh, uf, uc, lb, norm_g, conv_w, dft, bdc, bds, T, n_seq, blk0, state0=None, layer=0):
    has_state = state0 is not None
    const2 = lambda s: (0, 0)
    in_specs = [
        pl.BlockSpec((T, HGRN_COLS), lambda s: (blk0 + s, 0)),
        pl.BlockSpec((T, FOURIER_WIDTH), lambda s: (blk0 + s, 0)),
        pl.BlockSpec((T, CONV_COLS), lambda s: (blk0 + s, 0)),
        pl.BlockSpec((2, HGRN_WIDTH), const2),
        pl.BlockSpec((1, HGRN_WIDTH), const2),
        pl.BlockSpec((3, CONV_WIDTH), const2),
        pl.BlockSpec((T, 2 * T), const2),
        pl.BlockSpec((FOURIER_WIDTH, FOURIER_WIDTH), const2),
        pl.BlockSpec((FOURIER_WIDTH, FOURIER_WIDTH), const2),
    ]
    args = [uh, uf, uc, lb, norm_g, conv_w, dft, bdc, bds]
    mix_shape = jax.ShapeDtypeStruct((n_seq * T, D_MODEL), BF16)
    mix_spec = pl.BlockSpec((T, D_MODEL), lambda s: (s, 0))
    st_block = (1, 2, HEADS, HEAD_DIM, HEAD_DIM)
    if has_state:
        in_specs.append(pl.BlockSpec((1, None) + st_block[1:], lambda s: (s, layer, 0, 0, 0, 0)))
        args.append(state0)
        out_shape = [mix_shape]
        out_specs = [mix_spec]
    else:
        out_shape = [mix_shape, jax.ShapeDtypeStruct((n_seq, 2, HEADS, HEAD_DIM, HEAD_DIM), F32)]
        out_specs = [mix_spec, pl.BlockSpec(st_block, lambda s: (s, 0, 0, 0, 0))]
    return pl.pallas_call(
        functools.partial(_mixer_kernel, T=T, has_state=has_state),
        out_shape=out_shape,
        grid=(n_seq,),
        in_specs=in_specs,
        out_specs=out_specs,
        scratch_shapes=[pltpu.VMEM((2, HEADS, HEAD_DIM, HEAD_DIM), F32), pltpu.VMEM((T, HGRN_WIDTH), F32)],
        compiler_params=pltpu.CompilerParams(dimension_semantics=("arbitrary",), vmem_limit_bytes=VMEM_LIMIT),
        name="mixer_latent" if has_state else "mixer_context",
    )(*args)


def _merge_kernel(x_ref, mixc_ref, mixl_ref, ug_ref, mod_ref, bg_ref, wp_ref, wo_ref, g1_ref, b1_ref, wrh_ref, wrl_ref,
                  br_ref, x1_ref, h2_ref, apos_ref, gk_ref, n16_ref):
    W = HGRN_WIDTH
    gates = jax.nn.sigmoid(ug_ref[...] + bg_ref[...])
    mix = jnp.where(pl.program_id(0) < N_CTX // TOK_TILE, mixc_ref[...], mixl_ref[...])
    edges = (0, W, W + FOURIER_WIDTH, D_MODEL)
    merged = None
    for j in range(N_BRANCHES):
        p = _dot(mix[:, edges[j]:edges[j + 1]], wp_ref[edges[j]:edges[j + 1], :])
        term = gates[:, j * D_MODEL:(j + 1) * D_MODEL] * p
        merged = term if merged is None else merged + term
    y = _dot(merged.astype(BF16), wo_ref[...])
    x1 = _ln(ALPHA * x_ref[...] + mod_ref[2:3, :] * y) * g1_ref[...] + b1_ref[...]
    x1_ref[...] = x1
    h2 = _ln(x1) * (1.0 + mod_ref[4:5, :]) + mod_ref[3:4, :]
    h2_hi = h2.astype(BF16)
    h2_ref[...] = h2_hi
    h2_lo = (h2 - h2_hi.astype(F32)).astype(BF16)
    logits = _dot(h2_hi, wrh_ref[...]) + (_dot(h2_hi, wrl_ref[...]) + _dot(h2_lo, wrh_ref[...]))
    scores = jax.nn.sigmoid(logits)
    sel = scores + br_ref[...]
    lane = lax.broadcasted_iota(jnp.int32, sel.shape, 1)
    hits = []
    for _ in range(TOP_K):
        top = jnp.max(sel, axis=-1, keepdims=True)
        first = jnp.min(jnp.where(sel == top, lane, N_EXPERTS), axis=-1, keepdims=True)
        hit = lane == first
        hits.append(hit)
        sel = jnp.where(hit, -jnp.inf, sel)
    chosen = functools.reduce(jnp.logical_or, hits)
    picked = jnp.where(chosen, scores, 0.0)
    denom = jnp.sum(picked, axis=-1, keepdims=True) + 1e-20
    gate = ROUTED_SCALE * picked / denom

    chosen_f = chosen.astype(F32)
    r_i = lax.broadcasted_iota(jnp.int32, (TOK_TILE, TOK_TILE), 0)
    c_i = lax.broadcasted_iota(jnp.int32, (TOK_TILE, TOK_TILE), 1)
    earlier = (c_i < r_i).astype(F32).astype(BF16)
    rank = _dot(earlier, chosen_f.astype(BF16))
    count = jnp.sum(chosen_f, axis=0, keepdims=True)
    n16 = jnp.floor((count + (ROW_CHUNK - 1)) * (1.0 / ROW_CHUNK))
    e_r = lax.broadcasted_iota(jnp.int32, (N_EXPERTS, N_EXPERTS), 0)
    e_c = lax.broadcasted_iota(jnp.int32, (N_EXPERTS, N_EXPERTS), 1)
    before = (e_r < e_c).astype(F32).astype(BF16)
    seg_start = ROW_CHUNK * _dot(jnp.broadcast_to(n16, (8, N_EXPERTS)).astype(BF16), before)[0:1, :]
    row_of = seg_start + rank
    lane_k = lax.broadcasted_iota(jnp.int32, (TOK_TILE, TOP_K), 1)
    apos = jnp.zeros((TOK_TILE, TOP_K), F32)
    gk = jnp.zeros((TOK_TILE, TOP_K), F32)
    for k, hit in enumerate(hits):
        apos = jnp.where(lane_k == k, jnp.sum(jnp.where(hit, row_of, 0.0), axis=-1, keepdims=True), apos)
        gk = jnp.where(lane_k == k, jnp.sum(jnp.where(hit, gate, 0.0), axis=-1, keepdims=True), gk)
    apos_ref[...] = apos
    gk_ref[...] = gk
    n16_ref[...] = n16


def _merge(x, mix_ctx, mix_lat, ug, mod, b_gate, wp, wo, ln_g, ln_b, wr_hi, wr_lo, b_router, layer):
    tok = lambda n: pl.BlockSpec((TOK_TILE, n), lambda i: (i, 0))
    const2 = lambda i: (0, 0)
    ctx_tiles = N_CTX // TOK_TILE
    return pl.pallas_call(
        _merge_kernel,
        out_shape=[jax.ShapeDtypeStruct((N_TOK, D_MODEL), F32), jax.ShapeDtypeStruct((N_TOK, D_MODEL), BF16),
                   jax.ShapeDtypeStruct((N_TOK, TOP_K), F32), jax.ShapeDtypeStruct((N_TOK, TOP_K), F32),
                   jax.ShapeDtypeStruct((N_TILES, 1, N_EXPERTS), F32)],
        grid=(N_TILES,),
        in_specs=[
            tok(D_MODEL),
            pl.BlockSpec((TOK_TILE, D_MODEL), lambda i: (jnp.minimum(i, ctx_tiles - 1), 0)),
            pl.BlockSpec((TOK_TILE, D_MODEL), lambda i: (jnp.maximum(i - ctx_tiles, 0), 0)),
            tok(GATE_COLS),
            pl.BlockSpec((None, None, 6, D_MODEL), lambda i: (layer, _cond_row_of_tile(i, TOK_TILE), 0, 0)),
            pl.BlockSpec((1, GATE_COLS), const2),
            pl.BlockSpec((D_MODEL, D_MODEL), const2),
            pl.BlockSpec((D_MODEL, D_MODEL), const2),
            pl.BlockSpec((1, D_MODEL), const2),
            pl.BlockSpec((1, D_MODEL), const2),
            pl.BlockSpec((D_MODEL, N_EXPERTS), const2),
            pl.BlockSpec((D_MODEL, N_EXPERTS), const2),
            pl.BlockSpec((1, N_EXPERTS), const2),
        ],
        out_specs=[tok(D_MODEL), tok(D_MODEL), tok(TOP_K), tok(TOP_K),
                   pl.BlockSpec((None, 1, N_EXPERTS), lambda i: (i, 0, 0))],
        compiler_params=pltpu.CompilerParams(dimension_semantics=("arbitrary",), vmem_limit_bytes=VMEM_LIMIT),
        name="merge_router",
    )(x, mix_ctx, mix_lat, ug, mod, b_gate, wp, wo, ln_g, ln_b, wr_hi, wr_lo, b_router)


def _routing_tables(n16):
    n16 = n16.astype(jnp.int32)
    hi = jnp.cumsum(n16, axis=1)
    lo = hi - n16
    n_chunks = hi[:, -1]
    eo = jnp.cumsum(n16, axis=0) - n16
    tc = jnp.sum(n16, axis=0)
    tiles = (tc + CHUNKS_PER_GMM_TILE - 1) // CHUNKS_PER_GMM_TILE
    tile_end = jnp.cumsum(tiles)
    tile_off = tile_end - tiles
    n_used = tile_end[-1]
    c = jnp.arange(TILE_CHUNKS, dtype=jnp.int32)
    e_of = jnp.minimum(jnp.sum(c[None, :, None] >= hi[:, None, :], axis=-1), N_EXPERTS - 1)
    take = lambda a: jnp.take_along_axis(a, e_of, axis=1)
    dst = CHUNKS_PER_GMM_TILE * tile_off[e_of] + take(eo) + (c[None, :] - take(lo))
    t = jnp.arange(MAX_GMM_TILES, dtype=jnp.int32)
    t_eff = jnp.minimum(t, n_used - 1)
    tile_expert = jnp.minimum(jnp.sum(t_eff[:, None] >= tile_end[None, :], axis=-1), N_EXPERTS - 1)
    pad_start = CHUNKS_PER_GMM_TILE * tile_off + tc
    pad_count = CHUNKS_PER_GMM_TILE * tiles - tc
    return (dst.reshape(-1).astype(jnp.int32), n_chunks, tile_expert.astype(jnp.int32), t_eff.astype(jnp.int32),
            jnp.reshape(n_used, (1,)).astype(jnp.int32), pad_start.astype(jnp.int32), pad_count.astype(jnp.int32))


def _one_hot_rows(apos_ref, weight_ref=None):
    r = lax.broadcasted_iota(jnp.int32, (TOK_TILE, TILE_ROWS), 1).astype(F32)
    pt = jnp.zeros((TOK_TILE, TILE_ROWS), F32)
    for k in range(TOP_K):
        val = 1.0 if weight_ref is None else weight_ref[:, k:k + 1]
        pt = jnp.where(r == apos_ref[:, k:k + 1], val, pt)
    return pt.astype(BF16)


def _chunk_copy(src_ref, src_chunk, dst_ref, dst_chunk, sem):
    rows = lambda c: pl.ds(pl.multiple_of(c * ROW_CHUNK, ROW_CHUNK), ROW_CHUNK)
    return pltpu.make_async_copy(src_ref.at[rows(src_chunk), :], dst_ref.at[rows(dst_chunk), :], sem)


def _dispatch_kernel(dst_ref, nch_ref, pstart_ref, pcount_ref, nu_ref, h_ref, apos_ref, xs_hbm, stage_ref, zero_ref,
                     sem):
    j = pl.program_id(0)
    buf = j % 2

    def wait_chunks(b, n):
        def body(c, carry):
            _chunk_copy(stage_ref.at[b], 0, xs_hbm, 0, sem.at[b]).wait()
            return carry
        lax.fori_loop(0, n, body, 0)

    @pl.when(j < N_TILES)
    def _():
        @pl.when(j >= 2)
        def _():
            wait_chunks(buf, nch_ref[j - 2])

        pt = _one_hot_rows(apos_ref)
        h = h_ref[...]
        for r0 in range(0, TILE_ROWS, DISPATCH_ROWS):
            @pl.when(r0 < nch_ref[j] * ROW_CHUNK)
            def _(r0=r0):
                stage_ref[buf, r0:r0 + DISPATCH_ROWS, :] = _dot_tn(pt[:, r0:r0 + DISPATCH_ROWS], h).astype(BF16)

        def send(c, carry):
            _chunk_copy(stage_ref.at[buf], c, xs_hbm, dst_ref[j * TILE_CHUNKS + c], sem.at[buf]).start()
            return carry
        lax.fori_loop(0, nch_ref[j], send, 0)

    @pl.when(j == N_TILES)
    def _():
        zero_ref[...] = jnp.zeros_like(zero_ref)

        def tile_copy(t):
            rows = pl.ds(pl.multiple_of(t * GMM_TILE, GMM_TILE), GMM_TILE)
            return pltpu.make_async_copy(zero_ref, xs_hbm.at[rows, :], sem.at[3])

        def per_expert(e, total):
            def fill(i, carry):
                _chunk_copy(zero_ref, 0, xs_hbm, pstart_ref[e] + i, sem.at[2]).start()
                return carry
            lax.fori_loop(0, pcount_ref[e], fill, 0)
            return total + pcount_ref[e]
        n_pad = lax.fori_loop(0, N_EXPERTS, per_expert, 0)

        def fill_tile(t, carry):
            tile_copy(t).start()
            return carry
        lax.fori_loop(nu_ref[0], MAX_GMM_TILES, fill_tile, 0)
        wait_chunks(0, nch_ref[N_TILES - 2])
        wait_chunks(1, nch_ref[N_TILES - 1])

        def drain(i, carry):
            _chunk_copy(zero_ref, 0, xs_hbm, 0, sem.at[2]).wait()
            return carry
        lax.fori_loop(0, n_pad, drain, 0)

        def drain_tile(t, carry):
            tile_copy(0).wait()
            return carry
        lax.fori_loop(nu_ref[0], MAX_GMM_TILES, drain_tile, 0)


def _dispatch(h2, apos, dst, n_chunks, pad_start, pad_count, n_used):
    last = N_TILES - 1
    grid_spec = pltpu.PrefetchScalarGridSpec(
        num_scalar_prefetch=5,
        grid=(N_TILES + 1,),
        in_specs=[
            pl.BlockSpec((TOK_TILE, D_MODEL), lambda j, *_: (jnp.minimum(j, last), 0)),
            pl.BlockSpec((TOK_TILE, TOP_K), lambda j, *_: (jnp.minimum(j, last), 0)),
        ],
        out_specs=pl.BlockSpec(memory_space=pl.ANY),
        scratch_shapes=[pltpu.VMEM((2, TILE_ROWS, D_MODEL), BF16), pltpu.VMEM((GMM_TILE, D_MODEL), BF16),
                        pltpu.SemaphoreType.DMA((4,))],
    )
    return pl.pallas_call(
        _dispatch_kernel,
        out_shape=jax.ShapeDtypeStruct((SORTED_ROWS, D_MODEL), BF16),
        grid_spec=grid_spec,
        compiler_params=pltpu.CompilerParams(dimension_semantics=("arbitrary",), vmem_limit_bytes=VMEM_LIMIT),
        name="moe_dispatch",
    )(dst, n_chunks, pad_start, pad_count, n_used, h2, apos)


def _gmm_kernel(te_ref, tb_ref, nu_ref, x_ref, wg_ref, wu_ref, wd_ref, y_ref, wg_b, wu_b, wd_b):
    t = pl.program_id(0)

    @pl.when(t < nu_ref[0])
    def _():
        fresh = jnp.logical_or(t == 0, te_ref[t] != te_ref[jnp.maximum(t - 1, 0)])

        @pl.when(fresh)
        def _():
            wg_b[...] = wg_ref[...].astype(BF16)
            wu_b[...] = wu_ref[...].astype(BF16)
            wd_b[...] = wd_ref[...].astype(BF16)

        x = x_ref[...]
        hid = _silu(_dot(x, wg_b[...])) * _dot(x, wu_b[...])
        y_ref[...] = _dot(hid.astype(BF16), wd_b[...]).astype(BF16)


def _gmm(xs, weg, weu, wed, tile_expert, tile_block, n_used, layer):
    w_in_spec = pl.BlockSpec((None, None, D_MODEL, D_EXPERT), lambda t, te, tb, nu: (layer, te[t], 0, 0))
    grid_spec = pltpu.PrefetchScalarGridSpec(
        num_scalar_prefetch=3,
        grid=(MAX_GMM_TILES,),
        in_specs=[
            pl.BlockSpec((GMM_TILE, D_MODEL), lambda t, te, tb, nu: (tb[t], 0)),
            w_in_spec, w_in_spec,
            pl.BlockSpec((None, None, D_EXPERT, D_MODEL), lambda t, te, tb, nu: (layer, te[t], 0, 0)),
        ],
        out_specs=pl.BlockSpec((GMM_TILE, D_MODEL), lambda t, te, tb, nu: (tb[t], 0)),
        scratch_shapes=[pltpu.VMEM((D_MODEL, D_EXPERT), BF16), pltpu.VMEM((D_MODEL, D_EXPERT), BF16),
                        pltpu.VMEM((D_EXPERT, D_MODEL), BF16)],
    )
    return pl.pallas_call(
        _gmm_kernel,
        out_shape=jax.ShapeDtypeStruct((SORTED_ROWS, D_MODEL), BF16),
        grid_spec=grid_spec,
        input_output_aliases={3: 0},
        compiler_params=pltpu.CompilerParams(dimension_semantics=("arbitrary",), vmem_limit_bytes=VMEM_LIMIT),
        name="moe_gmm",
    )(tile_expert, tile_block, n_used, xs, weg, weu, wed)


def _combine_kernel(dst_ref, nch_ref, ys_hbm, apos_ref, gk_ref, h_ref, x1_ref, mod_ref, wsg_ref, wsu_ref, wsd_ref,
                    g2_ref, b2_ref, o_ref, stage_ref, sem):
    j = pl.program_id(0)
    buf = j % 2

    def fetch(jj, b):
        def body(c, carry):
            _chunk_copy(ys_hbm, dst_ref[jj * TILE_CHUNKS + c], stage_ref.at[b], c, sem.at[b]).start()
            return carry
        lax.fori_loop(0, nch_ref[jj], body, 0)

    @pl.when(j == 0)
    def _():
        stage_ref[...] = jnp.zeros_like(stage_ref)
        fetch(0, 0)

    @pl.when(j + 1 < N_TILES)
    def _():
        fetch(j + 1, 1 - buf)

    h = h_ref[...]
    hid = _silu(_dot(h, wsg_ref[...])) * _dot(h, wsu_ref[...])
    shared = _dot(hid.astype(BF16), wsd_ref[...])
    ptw = _one_hot_rows(apos_ref, gk_ref)

    def wait(c, carry):
        _chunk_copy(ys_hbm, 0, stage_ref.at[buf], 0, sem.at[buf]).wait()
        return carry
    lax.fori_loop(0, nch_ref[j], wait, 0)

    routed = _dot(ptw, stage_ref[buf])
    r = ALPHA * x1_ref[...] + mod_ref[5:6, :] * (routed + shared)
    o_ref[...] = _ln(r) * g2_ref[...] + b2_ref[...]


def _combine(ys, apos, gk, h2, x1, mod, wsg, wsu, wsd, ln_g, ln_b, dst, n_chunks, layer):
    tok = lambda n: pl.BlockSpec((TOK_TILE, n), lambda j, *_: (j, 0))
    const2 = lambda j, *_: (0, 0)
    grid_spec = pltpu.PrefetchScalarGridSpec(
        num_scalar_prefetch=2,
        grid=(N_TILES,),
        in_specs=[
            pl.BlockSpec(memory_space=pl.ANY),
            tok(TOP_K), tok(TOP_K), tok(D_MODEL), tok(D_MODEL),
            pl.BlockSpec((None, None, 6, D_MODEL), lambda j, *_: (layer, _cond_row_of_tile(j, TOK_TILE), 0, 0)),
            pl.BlockSpec((D_MODEL, D_SHARED), const2),
            pl.BlockSpec((D_MODEL, D_SHARED), const2),
            pl.BlockSpec((D_SHARED, D_MODEL), const2),
            pl.BlockSpec((1, D_MODEL), const2),
            pl.BlockSpec((1, D_MODEL), const2),
        ],
        out_specs=tok(D_MODEL),
        scratch_shapes=[pltpu.VMEM((2, TILE_ROWS, D_MODEL), BF16), pltpu.SemaphoreType.DMA((2,))],
    )
    return pl.pallas_call(
        _combine_kernel,
        out_shape=jax.ShapeDtypeStruct((N_TOK, D_MODEL), F32),
        grid_spec=grid_spec,
        compiler_params=pltpu.CompilerParams(dimension_semantics=("arbitrary",), vmem_limit_bytes=VMEM_LIMIT),
        name="moe_combine",
    )(dst, n_chunks, ys, apos, gk, h2, x1, mod, wsg, wsu, wsd, ln_g, ln_b)


def kernel(x_prompt, x_sample, c, state_hgrn, c_ctx, w_ada, b_ada, w_in, b_gate, hgrn_lb, hgrn_norm, conv_w,
           w_proj_hgrn, w_proj_fourier, w_proj_conv, w_out, ln1_g, ln1_b, ln2_g, ln2_b, w_router, b_router,
           w_exp_gate, w_exp_up, w_exp_down, w_sh_gate, w_sh_up, w_sh_down):
    lb_sm = jax.nn.softmax(hgrn_lb.astype(F32), axis=0)
    lb_all = jnp.cumsum(lb_sm, axis=0) - lb_sm[:1]

    pos = jnp.asarray(_pos_emb_table())
    dft_ctx = jnp.asarray(_dft_time(SEQ)).astype(BF16)
    dft_lat = jnp.asarray(_dft_time(DEC_SEQ)).astype(BF16)
    bdc, bds = (jnp.asarray(m).astype(BF16) for m in _dft_channel())

    cond = jnp.concatenate([c_ctx[None, :], c, jnp.zeros((N_COND - 1 - DEC_BATCH, D_MODEL), F32)], axis=0)
    mod = _ada_mod(cond, w_ada, b_ada).reshape(DEPTH, N_COND, 6, D_MODEL)

    w_in_b = w_in.astype(BF16)
    wp_b = jnp.concatenate([w_proj_hgrn, w_proj_fourier, w_proj_conv], axis=1).astype(BF16)
    wo_b = w_out.astype(BF16)
    wr_hi = w_router.astype(BF16)
    wr_lo = (w_router - wr_hi.astype(F32)).astype(BF16)
    wsg_b, wsu_b, wsd_b = w_sh_gate.astype(BF16), w_sh_up.astype(BF16), w_sh_down.astype(BF16)

    x = _assemble(x_prompt, x_sample, pos)
    new_states = []
    for l in range(DEPTH):
        uh, uf, uc, ug = _in_proj(x, mod, w_in_b, l)
        small = (lb_all[l], hgrn_norm[l][None, :], conv_w[l])
        mix_ctx, s_ctx = _mixer(uh, uf, uc, *small, dft_ctx, bdc, bds, T=SEQ, n_seq=BATCH, blk0=0)
        (mix_lat,) = _mixer(uh, uf, uc, *small, dft_lat, bdc, bds, T=DEC_SEQ, n_seq=DEC_BATCH,
                            blk0=N_CTX // DEC_SEQ, state0=state_hgrn, layer=l)
        new_states.append(s_ctx)
        x1, h2, apos, gk, n16 = _merge(x, mix_ctx, mix_lat, ug, mod, b_gate[l][None, :], wp_b[l], wo_b[l],
                                       ln1_g[l][None, :], ln1_b[l][None, :], wr_hi[l], wr_lo[l],
                                       b_router[l][None, :], l)
        dst, n_chunks, tile_expert, tile_block, n_used, pad_start, pad_count = _routing_tables(
            n16.reshape(N_TILES, N_EXPERTS))
        xs = _dispatch(h2, apos, dst, n_chunks, pad_start, pad_count, n_used)
        ys = _gmm(xs, w_exp_gate, w_exp_up, w_exp_down, tile_expert, tile_block, n_used, l)
        x = _combine(ys, apos, gk, h2, x1, mod, wsg_b[l], wsu_b[l], wsd_b[l], ln2_g[l][None, :], ln2_b[l][None, :],
                     dst, n_chunks, l)
    y_prompt = x[:N_CTX].reshape(BATCH, SEQ, D_MODEL)
    y_sample = x[N_CTX:].reshape(DEC_BATCH, DEC_SEQ, D_MODEL)
    return (y_prompt, y_sample, jnp.stack(new_states, axis=1))
```

```python
import functools
import math

import numpy as np
import jax
import jax.numpy as jnp
from jax import lax
from jax.experimental import pallas as pl
from jax.experimental.pallas import tpu as pltpu

F32 = jnp.float32
BF16 = jnp.bfloat16

D_MODEL = 1024
BATCH = 16
SEQ = 256
DEPTH = 2
DEC_BATCH = 4
DEC_SEQ = 1024
GRID_W = 64
HEADS = 4
HEAD_DIM = 128
HGRN_WIDTH = HEADS * HEAD_DIM
FOURIER_GROUPS = 4
FOURIER_GROUP_DIM = 64
FOURIER_WIDTH = FOURIER_GROUPS * FOURIER_GROUP_DIM
CONV_WIDTH = 256
N_BRANCHES = 3
HGRN_COLS = 5 * HGRN_WIDTH
CONV_COLS = 3 * CONV_WIDTH
GATE_COLS = N_BRANCHES * D_MODEL
IN_COLS = HGRN_COLS + FOURIER_WIDTH + CONV_COLS + GATE_COLS
N_EXPERTS = 64
TOP_K = 8
D_EXPERT = 256
D_SHARED = 256
ROUTED_SCALE = 2.5
ALPHA = (2 * DEPTH) ** 0.25
LN_EPS = 1e-6
RMS_EPS = 1e-6
F_MIN = 1e-30

N_CTX = BATCH * SEQ
N_LAT = DEC_BATCH * DEC_SEQ
N_TOK = N_CTX + N_LAT
N_COND = 8

TOK_TILE = 256
CHUNK = 64
WINDOW = 64.0
STATIC_WINDOWS = 4
ADA_TILE = 1536
VMEM_LIMIT = 56 * 1024 * 1024

N_TILES = N_TOK // TOK_TILE
ROW_CHUNK = 16
TILE_CHUNKS = (TOK_TILE * TOP_K + N_EXPERTS * (ROW_CHUNK - 1)) // ROW_CHUNK
DISPATCH_ROWS = 512
TILE_ROWS = -(-TILE_CHUNKS * ROW_CHUNK // DISPATCH_ROWS) * DISPATCH_ROWS
GMM_TILE = 512
CHUNKS_PER_GMM_TILE = GMM_TILE // ROW_CHUNK
MAX_GMM_TILES = (N_TILES * TILE_CHUNKS + N_EXPERTS * (CHUNKS_PER_GMM_TILE - 1)) // CHUNKS_PER_GMM_TILE
SORTED_ROWS = MAX_GMM_TILES * GMM_TILE


def _cond_row_of_tile(i, tile):
    ctx_tiles = N_CTX // tile
    per_seq = DEC_SEQ // tile
    return jnp.where(i < ctx_tiles, 0, 1 + (i - ctx_tiles) // per_seq)


def _silu(x):
    return x * jax.nn.sigmoid(x)


def _ln(x):
    mu = jnp.mean(x, axis=-1, keepdims=True)
    xc = x - mu
    var = jnp.mean(xc * xc, axis=-1, keepdims=True)
    return xc * lax.rsqrt(var + LN_EPS)


def _dot(a, b):
    return jnp.dot(a, b, preferred_element_type=F32)


def _dot_nt(a, b):
    return lax.dot_general(a, b, (((1,), (1,)), ((), ())), preferred_element_type=F32)


def _dot_tn(a, b):
    return lax.dot_general(a, b, (((0,), (0,)), ((), ())), preferred_element_type=F32)


@functools.lru_cache(maxsize=None)
def _pos_emb_table():
    rows = DEC_SEQ // GRID_W
    t = np.arange(rows * GRID_W)
    r = (t // GRID_W).astype(np.float32)
    col = (t % GRID_W).astype(np.float32)
    quarter = D_MODEL // 4
    omega = (1.0 / (np.float32(10000.0) ** (np.arange(quarter, dtype=np.float32) / np.float32(quarter)))).astype(np.float32)
    ar = (r[:, None] * omega).astype(np.float32)
    ac = (col[:, None] * omega).astype(np.float32)
    return np.concatenate([np.sin(ar), np.cos(ar), np.sin(ac), np.cos(ac)], axis=-1).astype(np.float32)


@functools.lru_cache(maxsize=None)
def _dft_time(T):
    k = np.arange(T)
    ph = 2.0 * np.pi * ((k[:, None] * k[None, :]) % T) / T
    return np.concatenate([np.cos(ph), -np.sin(ph)], axis=1).astype(np.float32)


@functools.lru_cache(maxsize=None)
def _dft_channel():
    n = FOURIER_GROUP_DIM
    k = np.arange(n)
    ph = 2.0 * np.pi * ((k[:, None] * k[None, :]) % n) / n
    eye = np.eye(FOURIER_GROUPS)
    return np.kron(eye, np.cos(ph)).astype(np.float32), np.kron(eye, np.sin(ph)).astype(np.float32)


def _assemble_kernel(xp_ref, xs_ref, pos_ref, o_ref):
    i = pl.program_id(0)

    @pl.when(i < N_CTX // TOK_TILE)
    def _():
        o_ref[...] = xp_ref[...]

    @pl.when(i >= N_CTX // TOK_TILE)
    def _():
        o_ref[...] = xs_ref[...] + pos_ref[...]


def _assemble(x_prompt, x_sample, pos):
    ctx_tiles = N_CTX // TOK_TILE
    per_seq = DEC_SEQ // TOK_TILE
    return pl.pallas_call(
        _assemble_kernel,
        out_shape=jax.ShapeDtypeStruct((N_TOK, D_MODEL), F32),
        grid=(N_TOK // TOK_TILE,),
        in_specs=[
            pl.BlockSpec((TOK_TILE, D_MODEL), lambda i: (jnp.minimum(i, ctx_tiles - 1), 0)),
            pl.BlockSpec((TOK_TILE, D_MODEL), lambda i: (jnp.maximum(i - ctx_tiles, 0), 0)),
            pl.BlockSpec((TOK_TILE, D_MODEL), lambda i: (jnp.maximum(i - ctx_tiles, 0) % per_seq, 0)),
        ],
        out_specs=pl.BlockSpec((TOK_TILE, D_MODEL), lambda i: (i, 0)),
        compiler_params=pltpu.CompilerParams(dimension_semantics=("arbitrary",)),
        name="assemble_tokens",
    )(x_prompt.reshape(N_CTX, D_MODEL), x_sample.reshape(N_LAT, D_MODEL), pos)


def _ada_kernel(c_ref, w_ref, b_ref, o_ref):
    s = _silu(c_ref[...]).astype(BF16)
    o_ref[...] = _dot(s, w_ref[...].astype(BF16)) + b_ref[...]


def _ada_mod(cond, w_ada, b_ada):
    n_col = 6 * D_MODEL
    return pl.pallas_call(
        _ada_kernel,
        out_shape=jax.ShapeDtypeStruct((DEPTH, N_COND, n_col), F32),
        grid=(DEPTH, n_col // ADA_TILE),
        in_specs=[
            pl.BlockSpec((N_COND, D_MODEL), lambda l, j: (0, 0)),
            pl.BlockSpec((None, D_MODEL, ADA_TILE), lambda l, j: (l, 0, j)),
            pl.BlockSpec((None, 1, ADA_TILE), lambda l, j: (l, 0, j)),
        ],
        out_specs=pl.BlockSpec((None, N_COND, ADA_TILE), lambda l, j: (l, 0, j)),
        compiler_params=pltpu.CompilerParams(dimension_semantics=("arbitrary", "arbitrary"),
                                             vmem_limit_bytes=VMEM_LIMIT),
        name="ada_mod",
    )(cond, w_ada, b_ada.reshape(DEPTH, 1, n_col))


def _in_proj_kernel(x_ref, mod_ref, w_ref, uh_ref, uf_ref, uc_ref, ug_ref):
    sh1 = mod_ref[0:1, :]
    sc1 = mod_ref[1:2, :]
    h = (_ln(x_ref[...]) * (1.0 + sc1) + sh1).astype(BF16)
    c0 = 0
    for ref, n in ((uh_ref, HGRN_COLS), (uf_ref, FOURIER_WIDTH), (uc_ref, CONV_COLS), (ug_ref, GATE_COLS)):
        ref[...] = _dot(h, w_ref[:, c0:c0 + n])
        c0 += n


def _in_proj(x, mod, w_in_bf16, layer):
    n_tiles = N_TOK // TOK_TILE
    widths = (HGRN_COLS, FOURIER_WIDTH, CONV_COLS, GATE_COLS)
    return pl.pallas_call(
        _in_proj_kernel,
        out_shape=[jax.ShapeDtypeStruct((N_TOK, n), F32) for n in widths],
        grid=(n_tiles,),
        in_specs=[
            pl.BlockSpec((TOK_TILE, D_MODEL), lambda i: (i, 0)),
            pl.BlockSpec((None, None, 6, D_MODEL), lambda i: (layer, _cond_row_of_tile(i, TOK_TILE), 0, 0)),
            pl.BlockSpec((None, D_MODEL, IN_COLS), lambda i: (layer, 0, 0)),
        ],
        out_specs=[pl.BlockSpec((TOK_TILE, n), lambda i: (i, 0)) for n in widths],
        compiler_params=pltpu.CompilerParams(dimension_semantics=("arbitrary",), vmem_limit_bytes=VMEM_LIMIT),
        name="in_proj",
    )(x, mod, w_in_bf16)


def _mixer_kernel(*refs, T, has_state):
    if has_state:
        (uh_ref, uf_ref, uc_ref, lb_ref, ng_ref, cw_ref, dft_ref, bdc_ref, bds_ref, s0_ref,
         mix_ref, st_ref, o_ref) = refs
        sfin_ref = None
    else:
        (uh_ref, uf_ref, uc_ref, lb_ref, ng_ref, cw_ref, dft_ref, bdc_ref, bds_ref,
         mix_ref, sfin_ref, st_ref, o_ref) = refs
        s0_ref = None
    n_chunks = T // CHUNK
    W = HGRN_WIDTH

    for d in range(2):
        for h in range(HEADS):
            if has_state:
                st_ref[d, h] = s0_ref[0, d, h].T
            else:
                st_ref[d, h] = jnp.zeros((HEAD_DIM, HEAD_DIM), F32)
    o_ref[...] = jnp.zeros_like(o_ref)

    row = lax.broadcasted_iota(jnp.int32, (CHUNK, CHUNK), 0)
    col = lax.broadcasted_iota(jnp.int32, (CHUNK, CHUNK), 1)
    keep = (col <= row, col >= row)
    tri = tuple(k.astype(F32).astype(BF16) for k in keep)

    def chunk_step(i, carry):
        for d in range(2):
            c = i if d == 0 else n_chunks - 1 - i
            rows = pl.ds(pl.multiple_of(c * CHUNK, CHUNK), CHUNK)
            q = _silu(uh_ref[rows, 0:W])
            v = uh_ref[rows, W:2 * W].astype(BF16)
            z = uh_ref[rows, (2 + d) * W:(3 + d) * W]
            lb = lb_ref[d:d + 1, :]
            e = jnp.exp(-jnp.abs(z))
            r = 1.0 / (1.0 + e)
            er = e * r
            pos = z >= 0.0
            sig_p = jnp.where(pos, r, er)
            sig_n = jnp.where(pos, er, r)
            f = lb + (1.0 - lb) * sig_p
            lf = jnp.log(jnp.maximum(f, F_MIN))
            k = (1.0 - lb) * sig_n
            lf_hi = lf.astype(BF16)
            lf_lo = (lf - lf_hi.astype(F32)).astype(BF16)
            b = _dot(tri[d], lf_hi) + _dot(tri[d], lf_lo)
            g = b[CHUNK - 1:CHUNK, :] if d == 0 else b[0:1, :]

            def window(p_level, q=q, k=k, b=b):
                dist = -p_level - b
                inside = jnp.logical_and(dist >= 0.0, dist < WINDOW)
                k_w = jnp.where(inside, k * jnp.exp(jnp.minimum(dist, WINDOW)), 0.0)
                q_w = q * jnp.exp(jnp.minimum(-dist, 0.0))
                return q_w.astype(BF16), k_w.astype(BF16)

            wins = [window(WINDOW * p) for p in range(STATIC_WINDOWS)]
            q_abs = (q * jnp.exp(b)).astype(BF16)
            k_end = (k * jnp.exp(g - b)).astype(BF16)
            decay = jnp.exp(g)
            outs = []
            for h in range(HEADS):
                ls = slice(h * HEAD_DIM, (h + 1) * HEAD_DIM)
                q_cat = jnp.concatenate([w[0][:, ls] for w in wins], axis=-1)
                k_cat = jnp.concatenate([w[1][:, ls] for w in wins], axis=-1)
                a = jnp.where(keep[d], _dot_nt(q_cat, k_cat), 0.0).astype(BF16)
                s_t = st_ref[d, h]
                o_h = _dot(a, v[:, ls]) + _dot_nt(q_abs[:, ls], s_t.astype(BF16))
                st_ref[d, h] = s_t * decay[:, ls] + _dot_tn(v[:, ls], k_end[:, ls])
                outs.append(o_h)
            o_ref[rows, :] += jnp.concatenate(outs, axis=-1)

            n_windows = jnp.floor(-jnp.min(b) * (1.0 / WINDOW)).astype(jnp.int32) + 1

            def extra_window(p, carry2, v=v, rows=rows, window=window, d=d):
                q_w, k_w = window(WINDOW * p.astype(F32))
                extra = []
                for h in range(HEADS):
                    ls = slice(h * HEAD_DIM, (h + 1) * HEAD_DIM)
                    a = jnp.where(keep[d], _dot_nt(q_w[:, ls], k_w[:, ls]), 0.0).astype(BF16)
                    extra.append(_dot(a, v[:, ls]))
                o_ref[rows, :] += jnp.concatenate(extra, axis=-1)
                return carry2

            lax.fori_loop(STATIC_WINDOWS, n_windows, extra_window, 0)
        return carry

    lax.fori_loop(0, n_chunks, chunk_step, 0)

    if sfin_ref is not None:
        for d in range(2):
            for h in range(HEADS):
                sfin_ref[0, d, h] = st_ref[d, h].T

    o = o_ref[...]
    parts = []
    for h in range(HEADS):
        oh = o[:, h * HEAD_DIM:(h + 1) * HEAD_DIM]
        parts.append(oh * lax.rsqrt(jnp.mean(oh * oh, axis=-1, keepdims=True) + RMS_EPS))
    a_out = jnp.concatenate(parts, axis=-1) * ng_ref[...] * _silu(uh_ref[:, 4 * W:5 * W])
    mix_ref[:, 0:W] = a_out.astype(BF16)

    zf = uf_ref[...].astype(BF16)
    zc = _dot(zf, bdc_ref[...]).astype(BF16)
    zs = _dot(zf, bds_ref[...]).astype(BF16)
    four = _dot(dft_ref[...], jnp.concatenate([zc, zs], axis=0)) * (1.0 / math.sqrt(T * FOURIER_GROUP_DIM))
    mix_ref[:, W:W + FOURIER_WIDTH] = four.astype(BF16)

    cb = uc_ref[:, 0:CONV_WIDTH]
    zz = uc_ref[:, CONV_WIDTH:2 * CONV_WIDTH] * uc_ref[:, 2 * CONV_WIDTH:3 * CONV_WIDTH]
    t_idx = lax.broadcasted_iota(jnp.int32, (T, CONV_WIDTH), 0)
    z_prev = jnp.where(t_idx == 0, 0.0, pltpu.roll(zz, 1, axis=0))
    z_next = jnp.where(t_idx == T - 1, 0.0, pltpu.roll(zz, T - 1, axis=0))
    y = cw_ref[0:1, :] * z_prev + cw_ref[1:2, :] * zz + cw_ref[2:3, :] * z_next
    mix_ref[:, W + FOURIER_WIDTH:] = (cb * y).astype(BF16)


def _mixer(uh, uf, uc, lb, norm_g, conv_w, dft, bdc, bds, T, n_seq, blk0, state0=None, layer=0):
    has_state = state0 is not None
    const2 = lambda s: (0, 0)
    in_specs = [
        pl.BlockSpec((T, HGRN_COLS), lambda s: (blk0 + s, 0)),
        pl.BlockSpec((T, FOURIER_WIDTH), lambda s: (blk0 + s, 0)),
        pl.BlockSpec((T, CONV_COLS), lambda s: (blk0 + s, 0)),
        pl.BlockSpec((2, HGRN_WIDTH), const2),
        pl.BlockSpec((1, HGRN_WIDTH), const2),
        pl.BlockSpec((3, CONV_WIDTH), const2),
        pl.BlockSpec((T, 2 * T), const2),
        pl.BlockSpec((FOURIER_WIDTH, FOURIER_WIDTH), const2),
        pl.BlockSpec((FOURIER_WIDTH, FOURIER_WIDTH), const2),
    ]
    args = [uh, uf, uc, lb, norm_g, conv_w, dft, bdc, bds]
    mix_shape = jax.ShapeDtypeStruct((n_seq * T, D_MODEL), BF16)
    mix_spec = pl.BlockSpec((T, D_MODEL), lambda s: (s, 0))
    st_block = (1, 2, HEADS, HEAD_DIM, HEAD_DIM)
    if has_state:
        in_specs.append(pl.BlockSpec((1, None) + st_block[1:], lambda s: (s, layer, 0, 0, 0, 0)))
        args.append(state0)
        out_shape = [mix_shape]
        out_specs = [mix_spec]
    else:
        out_shape = [mix_shape, jax.ShapeDtypeStruct((n_seq, 2, HEADS, HEAD_DIM, HEAD_DIM), F32)]
        out_specs = [mix_spec, pl.BlockSpec(st_block, lambda s: (s, 0, 0, 0, 0))]
    return pl.pallas_call(
        functools.partial(_mixer_kernel, T=T, has_state=has_state),
        out_shape=out_shape,
        grid=(n_seq,),
        in_specs=in_specs,
        out_specs=out_specs,
        scratch_shapes=[pltpu.VMEM((2, HEADS, HEAD_DIM, HEAD_DIM), F32), pltpu.VMEM((T, HGRN_WIDTH), F32)],
        compiler_params=pltpu.CompilerParams(dimension_semantics=("arbitrary",), vmem_limit_bytes=VMEM_LIMIT),
        name="mixer_latent" if has_state else "mixer_context",
    )(*args)


def _merge_kernel(x_ref, mixc_ref, mixl_ref, ug_ref, mod_ref, bg_ref, wp_ref, wo_ref, g1_ref, b1_ref, wrh_ref, wrl_ref,
                  br_ref, x1_ref, h2_ref, apos_ref, gk_ref, n16_ref):
    W = HGRN_WIDTH
    gates = jax.nn.sigmoid(ug_ref[...] + bg_ref[...])
    mix = jnp.where(pl.program_id(0) < N_CTX // TOK_TILE, mixc_ref[...], mixl_ref[...])
    edges = (0, W, W + FOURIER_WIDTH, D_MODEL)
    merged = None
    for j in range(N_BRANCHES):
        p = _dot(mix[:, edges[j]:edges[j + 1]], wp_ref[edges[j]:edges[j + 1], :])
        term = gates[:, j * D_MODEL:(j + 1) * D_MODEL] * p
        merged = term if merged is None else merged + term
    y = _dot(merged.astype(BF16), wo_ref[...])
    x1 = _ln(ALPHA * x_ref[...] + mod_ref[2:3, :] * y) * g1_ref[...] + b1_ref[...]
    x1_ref[...] = x1
    h2 = _ln(x1) * (1.0 + mod_ref[4:5, :]) + mod_ref[3:4, :]
    h2_hi = h2.astype(BF16)
    h2_ref[...] = h2_hi
    h2_lo = (h2 - h2_hi.astype(F32)).astype(BF16)
    logits = _dot(h2_hi, wrh_ref[...]) + (_dot(h2_hi, wrl_ref[...]) + _dot(h2_lo, wrh_ref[...]))
    scores = jax.nn.sigmoid(logits)
    sel = scores + br_ref[...]
    lane_f = lax.broadcasted_iota(jnp.int32, sel.shape, 1).astype(F32)
    hits = []
    for _ in range(TOP_K):
        top = jnp.max(sel, axis=-1, keepdims=True)
        first = jnp.min(jnp.where(sel == top, lane_f, float(N_EXPERTS)), axis=-1, keepdims=True)
        hit = lane_f == first
        hits.append(hit)
        sel = jnp.where(hit, -jnp.inf, sel)
    chosen = functools.reduce(jnp.logical_or, hits)
    picked = jnp.where(chosen, scores, 0.0)
    denom = jnp.sum(picked, axis=-1, keepdims=True) + 1e-20
    gate = ROUTED_SCALE * picked / denom

    chosen_f = chosen.astype(F32)
    r_i = lax.broadcasted_iota(jnp.int32, (TOK_TILE, TOK_TILE), 0)
    c_i = lax.broadcasted_iota(jnp.int32, (TOK_TILE, TOK_TILE), 1)
    earlier = (c_i < r_i).astype(F32).astype(BF16)
    rank = _dot(earlier, chosen_f.astype(BF16))
    count = jnp.sum(chosen_f, axis=0, keepdims=True)
    n16 = jnp.floor((count + (ROW_CHUNK - 1)) * (1.0 / ROW_CHUNK))
    e_r = lax.broadcasted_iota(jnp.int32, (N_EXPERTS, N_EXPERTS), 0)
    e_c = lax.broadcasted_iota(jnp.int32, (N_EXPERTS, N_EXPERTS), 1)
    before = (e_r < e_c).astype(F32).astype(BF16)
    seg_start = ROW_CHUNK * _dot(jnp.broadcast_to(n16, (8, N_EXPERTS)).astype(BF16), before)[0:1, :]
    row_of = seg_start + rank
    lane_k = lax.broadcasted_iota(jnp.int32, (TOK_TILE, TOP_K), 1)
    apos = jnp.zeros((TOK_TILE, TOP_K), F32)
    gk = jnp.zeros((TOK_TILE, TOP_K), F32)
    for k, hit in enumerate(hits):
        apos = jnp.where(lane_k == k, jnp.sum(jnp.where(hit, row_of, 0.0), axis=-1, keepdims=True), apos)
        gk = jnp.where(lane_k == k, jnp.sum(jnp.where(hit, gate, 0.0), axis=-1, keepdims=True), gk)
    apos_ref[...] = apos
    gk_ref[...] = gk
    n16_ref[...] = n16


def _merge(x, mix_ctx, mix_lat, ug, mod, b_gate, wp, wo, ln_g, ln_b, wr_hi, wr_lo, b_router, layer):
    tok = lambda n: pl.BlockSpec((TOK_TILE, n), lambda i: (i, 0))
    const2 = lambda i: (0, 0)
    ctx_tiles = N_CTX // TOK_TILE
    return pl.pallas_call(
        _merge_kernel,
        out_shape=[jax.ShapeDtypeStruct((N_TOK, D_MODEL), F32), jax.ShapeDtypeStruct((N_TOK, D_MODEL), BF16),
                   jax.ShapeDtypeStruct((N_TOK, TOP_K), F32), jax.ShapeDtypeStruct((N_TOK, TOP_K), F32),
                   jax.ShapeDtypeStruct((N_TILES, 1, N_EXPERTS), F32)],
        grid=(N_TILES,),
        in_specs=[
            tok(D_MODEL),
            pl.BlockSpec((TOK_TILE, D_MODEL), lambda i: (jnp.minimum(i, ctx_tiles - 1), 0)),
            pl.BlockSpec((TOK_TILE, D_MODEL), lambda i: (jnp.maximum(i - ctx_tiles, 0), 0)),
            tok(GATE_COLS),
            pl.BlockSpec((None, None, 6, D_MODEL), lambda i: (layer, _cond_row_of_tile(i, TOK_TILE), 0, 0)),
            pl.BlockSpec((1, GATE_COLS), const2),
            pl.BlockSpec((D_MODEL, D_MODEL), const2),
            pl.BlockSpec((D_MODEL, D_MODEL), const2),
            pl.BlockSpec((1, D_MODEL), const2),
            pl.BlockSpec((1, D_MODEL), const2),
            pl.BlockSpec((D_MODEL, N_EXPERTS), const2),
            pl.BlockSpec((D_MODEL, N_EXPERTS), const2),
            pl.BlockSpec((1, N_EXPERTS), const2),
        ],
        out_specs=[tok(D_MODEL), tok(D_MODEL), tok(TOP_K), tok(TOP_K),
                   pl.BlockSpec((None, 1, N_EXPERTS), lambda i: (i, 0, 0))],
        compiler_params=pltpu.CompilerParams(dimension_semantics=("arbitrary",), vmem_limit_bytes=VMEM_LIMIT),
        name="merge_router",
    )(x, mix_ctx, mix_lat, ug, mod, b_gate, wp, wo, ln_g, ln_b, wr_hi, wr_lo, b_router)


def _routing_tables(n16):
    n16 = n16.astype(jnp.int32)
    hi = jnp.cumsum(n16, axis=1)
    lo = hi - n16
    n_chunks = hi[:, -1]
    eo = jnp.cumsum(n16, axis=0) - n16
    tc = jnp.sum(n16, axis=0)
    tiles = (tc + CHUNKS_PER_GMM_TILE - 1) // CHUNKS_PER_GMM_TILE
    tile_end = jnp.cumsum(tiles)
    tile_off = tile_end - tiles
    n_used = tile_end[-1]
    c = jnp.arange(TILE_CHUNKS, dtype=jnp.int32)[None, :, None]
    owner = jnp.logical_and(c >= lo[:, None, :], c < hi[:, None, :])
    shift = CHUNKS_PER_GMM_TILE * tile_off[None, :] + eo - lo
    dst = c[:, :, 0] + jnp.sum(jnp.where(owner, shift[:, None, :], 0), axis=-1)
    t = jnp.arange(MAX_GMM_TILES, dtype=jnp.int32)
    t_eff = jnp.minimum(t, n_used - 1)
    tile_expert = jnp.minimum(jnp.sum(t_eff[:, None] >= tile_end[None, :], axis=-1), N_EXPERTS - 1)
    pad_start = CHUNKS_PER_GMM_TILE * tile_off + tc
    pad_count = CHUNKS_PER_GMM_TILE * tiles - tc
    return (dst.reshape(-1).astype(jnp.int32), n_chunks, tile_expert.astype(jnp.int32), t_eff.astype(jnp.int32),
            jnp.reshape(n_used, (1,)).astype(jnp.int32), pad_start.astype(jnp.int32), pad_count.astype(jnp.int32))


def _one_hot_rows(apos_ref, weight_ref=None):
    r = lax.broadcasted_iota(jnp.int32, (TOK_TILE, TILE_ROWS), 1).astype(F32)
    pt = jnp.zeros((TOK_TILE, TILE_ROWS), F32)
    for k in range(TOP_K):
        val = 1.0 if weight_ref is None else weight_ref[:, k:k + 1]
        pt = jnp.where(r == apos_ref[:, k:k + 1], val, pt)
    return pt.astype(BF16)


def _chunk_copy(src_ref, src_chunk, dst_ref, dst_chunk, sem):
    rows = lambda c: pl.ds(pl.multiple_of(c * ROW_CHUNK, ROW_CHUNK), ROW_CHUNK)
    return pltpu.make_async_copy(src_ref.at[rows(src_chunk), :], dst_ref.at[rows(dst_chunk), :], sem)


def _dispatch_kernel(dst_ref, nch_ref, pstart_ref, pcount_ref, nu_ref, h_ref, apos_ref, xs_hbm, stage_ref, zero_ref,
                     sem):
    j = pl.program_id(0)
    buf = j % 2

    def wait_chunks(b, n):
        def body(c, carry):
            _chunk_copy(stage_ref.at[b], 0, xs_hbm, 0, sem.at[b]).wait()
            return carry
        lax.fori_loop(0, n, body, 0)

    @pl.when(j < N_TILES)
    def _():
        @pl.when(j >= 2)
        def _():
            wait_chunks(buf, nch_ref[j - 2])

        pt = _one_hot_rows(apos_ref)
        h = h_ref[...]
        for r0 in range(0, TILE_ROWS, DISPATCH_ROWS):
            stage_ref[buf, r0:r0 + DISPATCH_ROWS, :] = _dot_tn(pt[:, r0:r0 + DISPATCH_ROWS], h).astype(BF16)

        def send(c, carry):
            _chunk_copy(stage_ref.at[buf], c, xs_hbm, dst_ref[j * TILE_CHUNKS + c], sem.at[buf]).start()
            return carry
        lax.fori_loop(0, nch_ref[j], send, 0)

    @pl.when(j == N_TILES)
    def _():
        zero_ref[...] = jnp.zeros_like(zero_ref)

        def tile_copy(t):
            rows = pl.ds(pl.multiple_of(t * GMM_TILE, GMM_TILE), GMM_TILE)
            return pltpu.make_async_copy(zero_ref, xs_hbm.at[rows, :], sem.at[3])

        def per_expert(e, total):
            def fill(i, carry):
                _chunk_copy(zero_ref, 0, xs_hbm, pstart_ref[e] + i, sem.at[2]).start()
                return carry
            lax.fori_loop(0, pcount_ref[e], fill, 0)
            return total + pcount_ref[e]
        n_pad = lax.fori_loop(0, N_EXPERTS, per_expert, 0)

        def fill_tile(t, carry):
            tile_copy(t).start()
            return carry
        lax.fori_loop(nu_ref[0], MAX_GMM_TILES, fill_tile, 0)
        wait_chunks(0, nch_ref[N_TILES - 2])
        wait_chunks(1, nch_ref[N_TILES - 1])

        def drain(i, carry):
            _chunk_copy(zero_ref, 0, xs_hbm, 0, sem.at[2]).wait()
            return carry
        lax.fori_loop(0, n_pad, drain, 0)

        def drain_tile(t, carry):
            tile_copy(0).wait()
            return carry
        lax.fori_loop(nu_ref[0], MAX_GMM_TILES, drain_tile, 0)


def _dispatch(h2, apos, dst, n_chunks, pad_start, pad_count, n_used):
    last = N_TILES - 1
    grid_spec = pltpu.PrefetchScalarGridSpec(
        num_scalar_prefetch=5,
        grid=(N_TILES + 1,),
        in_specs=[
            pl.BlockSpec((TOK_TILE, D_MODEL), lambda j, *_: (jnp.minimum(j, last), 0)),
            pl.BlockSpec((TOK_TILE, TOP_K), lambda j, *_: (jnp.minimum(j, last), 0)),
        ],
        out_specs=pl.BlockSpec(memory_space=pl.ANY),
        scratch_shapes=[pltpu.VMEM((2, TILE_ROWS, D_MODEL), BF16), pltpu.VMEM((GMM_TILE, D_MODEL), BF16),
                        pltpu.SemaphoreType.DMA((4,))],
    )
    return pl.pallas_call(
        _dispatch_kernel,
        out_shape=jax.ShapeDtypeStruct((SORTED_ROWS, D_MODEL), BF16),
        grid_spec=grid_spec,
        compiler_params=pltpu.CompilerParams(dimension_semantics=("arbitrary",), vmem_limit_bytes=VMEM_LIMIT),
        name="moe_dispatch",
    )(dst, n_chunks, pad_start, pad_count, n_used, h2, apos)


def _gmm_kernel(te_ref, tb_ref, nu_ref, x_ref, wg_ref, wu_ref, wd_ref, y_ref, wg_b, wu_b, wd_b):
    t = pl.program_id(0)

    @pl.when(t < nu_ref[0])
    def _():
        fresh = jnp.logical_or(t == 0, te_ref[t] != te_ref[jnp.maximum(t - 1, 0)])

        @pl.when(fresh)
        def _():
            wg_b[...] = wg_ref[...].astype(BF16)
            wu_b[...] = wu_ref[...].astype(BF16)
            wd_b[...] = wd_ref[...].astype(BF16)

        x = x_ref[...]
        hid = _silu(_dot(x, wg_b[...])) * _dot(x, wu_b[...])
        y_ref[...] = _dot(hid.astype(BF16), wd_b[...]).astype(BF16)


def _gmm(xs, weg, weu, wed, tile_expert, tile_block, n_used, layer):
    w_in_spec = pl.BlockSpec((None, None, D_MODEL, D_EXPERT), lambda t, te, tb, nu: (layer, te[t], 0, 0))
    grid_spec = pltpu.PrefetchScalarGridSpec(
        num_scalar_prefetch=3,
        grid=(MAX_GMM_TILES,),
        in_specs=[
            pl.BlockSpec((GMM_TILE, D_MODEL), lambda t, te, tb, nu: (tb[t], 0)),
            w_in_spec, w_in_spec,
            pl.BlockSpec((None, None, D_EXPERT, D_MODEL), lambda t, te, tb, nu: (layer, te[t], 0, 0)),
        ],
        out_specs=pl.BlockSpec((GMM_TILE, D_MODEL), lambda t, te, tb, nu: (tb[t], 0)),
        scratch_shapes=[pltpu.VMEM((D_MODEL, D_EXPERT), BF16), pltpu.VMEM((D_MODEL, D_EXPERT), BF16),
                        pltpu.VMEM((D_EXPERT, D_MODEL), BF16)],
    )
    return pl.pallas_call(
        _gmm_kernel,
        out_shape=jax.ShapeDtypeStruct((SORTED_ROWS, D_MODEL), BF16),
        grid_spec=grid_spec,
        input_output_aliases={3: 0},
        compiler_params=pltpu.CompilerParams(dimension_semantics=("arbitrary",), vmem_limit_bytes=VMEM_LIMIT),
        name="moe_gmm",
    )(tile_expert, tile_block, n_used, xs, weg, weu, wed)


def _combine_kernel(dst_ref, nch_ref, ys_hbm, apos_ref, gk_ref, h_ref, x1_ref, mod_ref, wsg_ref, wsu_ref, wsd_ref,
                    g2_ref, b2_ref, o_ref, stage_ref, sem):
    j = pl.program_id(0)
    buf = j % 2

    def fetch(jj, b):
        def body(c, carry):
            _chunk_copy(ys_hbm, dst_ref[jj * TILE_CHUNKS + c], stage_ref.at[b], c, sem.at[b]).start()
            return carry
        lax.fori_loop(0, nch_ref[jj], body, 0)

    @pl.when(j == 0)
    def _():
        stage_ref[...] = jnp.zeros_like(stage_ref)
        fetch(0, 0)

    @pl.when(j + 1 < N_TILES)
    def _():
        fetch(j + 1, 1 - buf)

    h = h_ref[...]
    hid = _silu(_dot(h, wsg_ref[...])) * _dot(h, wsu_ref[...])
    shared = _dot(hid.astype(BF16), wsd_ref[...])
    ptw = _one_hot_rows(apos_ref, gk_ref)

    def wait(c, carry):
        _chunk_copy(ys_hbm, 0, stage_ref.at[buf], 0, sem.at[buf]).wait()
        return carry
    lax.fori_loop(0, nch_ref[j], wait, 0)

    routed = _dot(ptw, stage_ref[buf])
    r = ALPHA * x1_ref[...] + mod_ref[5:6, :] * (routed + shared)
    o_ref[...] = _ln(r) * g2_ref[...] + b2_ref[...]


def _combine(ys, apos, gk, h2, x1, mod, wsg, wsu, wsd, ln_g, ln_b, dst, n_chunks, layer):
    tok = lambda n: pl.BlockSpec((TOK_TILE, n), lambda j, *_: (j, 0))
    const2 = lambda j, *_: (0, 0)
    grid_spec = pltpu.PrefetchScalarGridSpec(
        num_scalar_prefetch=2,
        grid=(N_TILES,),
        in_specs=[
            pl.BlockSpec(memory_space=pl.ANY),
            tok(TOP_K), tok(TOP_K), tok(D_MODEL), tok(D_MODEL),
            pl.BlockSpec((None, None, 6, D_MODEL), lambda j, *_: (layer, _cond_row_of_tile(j, TOK_TILE), 0, 0)),
            pl.BlockSpec((D_MODEL, D_SHARED), const2),
            pl.BlockSpec((D_MODEL, D_SHARED), const2),
            pl.BlockSpec((D_SHARED, D_MODEL), const2),
            pl.BlockSpec((1, D_MODEL), const2),
            pl.BlockSpec((1, D_MODEL), const2),
        ],
        out_specs=tok(D_MODEL),
        scratch_shapes=[pltpu.VMEM((2, TILE_ROWS, D_MODEL), BF16), pltpu.SemaphoreType.DMA((2,))],
    )
    return pl.pallas_call(
        _combine_kernel,
        out_shape=jax.ShapeDtypeStruct((N_TOK, D_MODEL), F32),
        grid_spec=grid_spec,
        compiler_params=pltpu.CompilerParams(dimension_semantics=("arbitrary",), vmem_limit_bytes=VMEM_LIMIT),
        name="moe_combine",
    )(dst, n_chunks, ys, apos, gk, h2, x1, mod, wsg, wsu, wsd, ln_g, ln_b)


def kernel(x_prompt, x_sample, c, state_hgrn, c_ctx, w_ada, b_ada, w_in, b_gate, hgrn_lb, hgrn_norm, conv_w,
           w_proj_hgrn, w_proj_fourier, w_proj_conv, w_out, ln1_g, ln1_b, ln2_g, ln2_b, w_router, b_router,
           w_exp_gate, w_exp_up, w_exp_down, w_sh_gate, w_sh_up, w_sh_down):
    lb_sm = jax.nn.softmax(hgrn_lb.astype(F32), axis=0)
    lb_all = jnp.cumsum(lb_sm, axis=0) - lb_sm[:1]

    pos = jnp.asarray(_pos_emb_table())
    dft_ctx = jnp.asarray(_dft_time(SEQ)).astype(BF16)
    dft_lat = jnp.asarray(_dft_time(DEC_SEQ)).astype(BF16)
    bdc, bds = (jnp.asarray(m).astype(BF16) for m in _dft_channel())

    cond = jnp.concatenate([c_ctx[None, :], c, jnp.zeros((N_COND - 1 - DEC_BATCH, D_MODEL), F32)], axis=0)
    mod = _ada_mod(cond, w_ada, b_ada).reshape(DEPTH, N_COND, 6, D_MODEL)

    w_in_b = w_in.astype(BF16)
    wp_b = jnp.concatenate([w_proj_hgrn, w_proj_fourier, w_proj_conv], axis=1).astype(BF16)
    wo_b = w_out.astype(BF16)
    wr_hi = w_router.astype(BF16)
    wr_lo = (w_router - wr_hi.astype(F32)).astype(BF16)
    wsg_b, wsu_b, wsd_b = w_sh_gate.astype(BF16), w_sh_up.astype(BF16), w_sh_down.astype(BF16)

    x = _assemble(x_prompt, x_sample, pos)
    new_states = []
    for l in range(DEPTH):
        uh, uf, uc, ug = _in_proj(x, mod, w_in_b, l)
        small = (lb_all[l], hgrn_norm[l][None, :], conv_w[l])
        mix_ctx, s_ctx = _mixer(uh, uf, uc, *small, dft_ctx, bdc, bds, T=SEQ, n_seq=BATCH, blk0=0)
        (mix_lat,) = _mixer(uh, uf, uc, *small, dft_lat, bdc, bds, T=DEC_SEQ, n_seq=DEC_BATCH,
                            blk0=N_CTX // DEC_SEQ, state0=state_hgrn, layer=l)
        new_states.append(s_ctx)
        x1, h2, apos, gk, n16 = _merge(x, mix_ctx, mix_lat, ug, mod, b_gate[l][None, :], wp_b[l], wo_b[l],
                                       ln1_g[l][None, :], ln1_b[l][None, :], wr_hi[l], wr_lo[l],
                                       b_router[l][None, :], l)
        dst, n_chunks, tile_expert, tile_block, n_used, pad_start, pad_count = _routing_tables(
            n16.reshape(N_TILES, N_EXPERTS))
        xs = _dispatch(h2, apos, dst, n_chunks, pad_start, pad_count, n_used)
        ys = _gmm(xs, w_exp_gate, w_exp_up, w_exp_down, tile_expert, tile_block, n_used, l)
        x = _combine(ys, apos, gk, h2, x1, mod, wsg_b[l], wsu_b[l], wsd_b[l], ln2_g[l][None, :], ln2_b[l][None, :],
                     dst, n_chunks, l)
    y_prompt = x[:N_CTX].reshape(BATCH, SEQ, D_MODEL)
    y_sample = x[N_CTX:].reshape(DEC_BATCH, DEC_SEQ, D_MODEL)
    return (y_prompt, y_sample, jnp.stack(new_states, axis=1))
```

```python
import functools
import math

import numpy as np
import jax
import jax.numpy as jnp
from jax import lax
from jax.experimental import pallas as pl
from jax.experimental.pallas import tpu as pltpu

F32 = jnp.float32
BF16 = jnp.bfloat16

D_MODEL = 1024
BATCH = 16
SEQ = 256
DEPTH = 2
DEC_BATCH = 4
DEC_SEQ = 1024
GRID_W = 64
HEADS = 4
HEAD_DIM = 128
HGRN_WIDTH = HEADS * HEAD_DIM
FOURIER_GROUPS = 4
FOURIER_GROUP_DIM = 64
FOURIER_WIDTH = FOURIER_GROUPS * FOURIER_GROUP_DIM
CONV_WIDTH = 256
N_BRANCHES = 3
HGRN_COLS = 5 * HGRN_WIDTH
CONV_COLS = 3 * CONV_WIDTH
GATE_COLS = N_BRANCHES * D_MODEL
IN_COLS = HGRN_COLS + FOURIER_WIDTH + CONV_COLS + GATE_COLS
N_EXPERTS = 64
TOP_K = 8
D_EXPERT = 256
D_SHARED = 256
ROUTED_SCALE = 2.5
ALPHA = (2 * DEPTH) ** 0.25
LN_EPS = 1e-6
RMS_EPS = 1e-6
F_MIN = 1e-30

N_CTX = BATCH * SEQ
N_LAT = DEC_BATCH * DEC_SEQ
N_TOK = N_CTX + N_LAT
N_COND = 8

TOK_TILE = 256
CHUNK = 64
CHUNK_UNROLL = 2
WINDOW = 64.0
STATIC_WINDOWS = 4
ADA_TILE = 1536
VMEM_LIMIT = 56 * 1024 * 1024

N_TILES = N_TOK // TOK_TILE
ROW_CHUNK = 16
TILE_CHUNKS = (TOK_TILE * TOP_K + N_EXPERTS * (ROW_CHUNK - 1)) // ROW_CHUNK
DISPATCH_ROWS = 512
TILE_ROWS = -(-TILE_CHUNKS * ROW_CHUNK // DISPATCH_ROWS) * DISPATCH_ROWS
GMM_TILE = 256
GMM_IN_SLOTS = 3
GMM_OUT_SLOTS = 2
CHUNKS_PER_GMM_TILE = GMM_TILE // ROW_CHUNK
MAX_GMM_TILES = (N_TILES * TILE_CHUNKS + N_EXPERTS * (CHUNKS_PER_GMM_TILE - 1)) // CHUNKS_PER_GMM_TILE
SORTED_ROWS = MAX_GMM_TILES * GMM_TILE


def _cond_row_of_tile(i, tile):
    ctx_tiles = N_CTX // tile
    per_seq = DEC_SEQ // tile
    return jnp.where(i < ctx_tiles, 0, 1 + (i - ctx_tiles) // per_seq)


def _silu(x):
    return x * jax.nn.sigmoid(x)


def _ln(x):
    mu = jnp.mean(x, axis=-1, keepdims=True)
    xc = x - mu
    var = jnp.mean(xc * xc, axis=-1, keepdims=True)
    return xc * lax.rsqrt(var + LN_EPS)


def _dot(a, b):
    return jnp.dot(a, b, preferred_element_type=F32)


def _dot_nt(a, b):
    return lax.dot_general(a, b, (((1,), (1,)), ((), ())), preferred_element_type=F32)


def _dot_tn(a, b):
    return lax.dot_general(a, b, (((0,), (0,)), ((), ())), preferred_element_type=F32)


@functools.lru_cache(maxsize=None)
def _pos_emb_table():
    rows = DEC_SEQ // GRID_W
    t = np.arange(rows * GRID_W)
    r = (t // GRID_W).astype(np.float32)
    col = (t % GRID_W).astype(np.float32)
    quarter = D_MODEL // 4
    omega = (1.0 / (np.float32(10000.0) ** (np.arange(quarter, dtype=np.float32) / np.float32(quarter)))).astype(np.float32)
    ar = (r[:, None] * omega).astype(np.float32)
    ac = (col[:, None] * omega).astype(np.float32)
    return np.concatenate([np.sin(ar), np.cos(ar), np.sin(ac), np.cos(ac)], axis=-1).astype(np.float32)


@functools.lru_cache(maxsize=None)
def _dft_time(T):
    k = np.arange(T)
    ph = 2.0 * np.pi * ((k[:, None] * k[None, :]) % T) / T
    return np.concatenate([np.cos(ph), -np.sin(ph)], axis=1).astype(np.float32)


@functools.lru_cache(maxsize=None)
def _dft_channel():
    n = FOURIER_GROUP_DIM
    k = np.arange(n)
    ph = 2.0 * np.pi * ((k[:, None] * k[None, :]) % n) / n
    eye = np.eye(FOURIER_GROUPS)
    return np.kron(eye, np.cos(ph)).astype(np.float32), np.kron(eye, np.sin(ph)).astype(np.float32)


def _assemble_kernel(xp_ref, xs_ref, pos_ref, o_ref):
    i = pl.program_id(0)

    @pl.when(i < N_CTX // TOK_TILE)
    def _():
        o_ref[...] = xp_ref[...]

    @pl.when(i >= N_CTX // TOK_TILE)
    def _():
        o_ref[...] = xs_ref[...] + pos_ref[...]


def _assemble(x_prompt, x_sample, pos):
    ctx_tiles = N_CTX // TOK_TILE
    per_seq = DEC_SEQ // TOK_TILE
    return pl.pallas_call(
        _assemble_kernel,
        out_shape=jax.ShapeDtypeStruct((N_TOK, D_MODEL), F32),
        grid=(N_TOK // TOK_TILE,),
        in_specs=[
            pl.BlockSpec((TOK_TILE, D_MODEL), lambda i: (jnp.minimum(i, ctx_tiles - 1), 0)),
            pl.BlockSpec((TOK_TILE, D_MODEL), lambda i: (jnp.maximum(i - ctx_tiles, 0), 0)),
            pl.BlockSpec((TOK_TILE, D_MODEL), lambda i: (jnp.maximum(i - ctx_tiles, 0) % per_seq, 0)),
        ],
        out_specs=pl.BlockSpec((TOK_TILE, D_MODEL), lambda i: (i, 0)),
        compiler_params=pltpu.CompilerParams(dimension_semantics=("arbitrary",)),
        name="assemble_tokens",
    )(x_prompt.reshape(N_CTX, D_MODEL), x_sample.reshape(N_LAT, D_MODEL), pos)


def _ada_kernel(c_ref, w_ref, b_ref, o_ref):
    s = _silu(c_ref[...]).astype(BF16)
    o_ref[...] = _dot(s, w_ref[...].astype(BF16)) + b_ref[...]


def _ada_mod(cond, w_ada, b_ada):
    n_col = 6 * D_MODEL
    return pl.pallas_call(
        _ada_kernel,
        out_shape=jax.ShapeDtypeStruct((DEPTH, N_COND, n_col), F32),
        grid=(DEPTH, n_col // ADA_TILE),
        in_specs=[
            pl.BlockSpec((N_COND, D_MODEL), lambda l, j: (0, 0)),
            pl.BlockSpec((None, D_MODEL, ADA_TILE), lambda l, j: (l, 0, j)),
            pl.BlockSpec((None, 1, ADA_TILE), lambda l, j: (l, 0, j)),
        ],
        out_specs=pl.BlockSpec((None, N_COND, ADA_TILE), lambda l, j: (l, 0, j)),
        compiler_params=pltpu.CompilerParams(dimension_semantics=("arbitrary", "arbitrary"),
                                             vmem_limit_bytes=VMEM_LIMIT),
        name="ada_mod",
    )(cond, w_ada, b_ada.reshape(DEPTH, 1, n_col))


def _in_proj_kernel(x_ref, mod_ref, w_ref, uh_ref, uf_ref, uc_ref, ug_ref):
    sh1 = mod_ref[0:1, :]
    sc1 = mod_ref[1:2, :]
    h = (_ln(x_ref[...]) * (1.0 + sc1) + sh1).astype(BF16)
    c0 = 0
    for ref, n in ((uh_ref, HGRN_COLS), (uf_ref, FOURIER_WIDTH), (uc_ref, CONV_COLS), (ug_ref, GATE_COLS)):
        ref[...] = _dot(h, w_ref[:, c0:c0 + n])
        c0 += n


def _in_proj(x, mod, w_in_bf16, layer):
    n_tiles = N_TOK // TOK_TILE
    widths = (HGRN_COLS, FOURIER_WIDTH, CONV_COLS, GATE_COLS)
    return pl.pallas_call(
        _in_proj_kernel,
        out_shape=[jax.ShapeDtypeStruct((N_TOK, n), F32) for n in widths],
        grid=(n_tiles,),
        in_specs=[
            pl.BlockSpec((TOK_TILE, D_MODEL), lambda i: (i, 0)),
            pl.BlockSpec((None, None, 6, D_MODEL), lambda i: (layer, _cond_row_of_tile(i, TOK_TILE), 0, 0)),
            pl.BlockSpec((None, D_MODEL, IN_COLS), lambda i: (layer, 0, 0)),
        ],
        out_specs=[pl.BlockSpec((TOK_TILE, n), lambda i: (i, 0)) for n in widths],
        compiler_params=pltpu.CompilerParams(dimension_semantics=("arbitrary",), vmem_limit_bytes=VMEM_LIMIT),
        name="in_proj",
    )(x, mod, w_in_bf16)


def _mixer_kernel(*refs, T, has_state):
    if has_state:
        (uh_ref, uf_ref, uc_ref, lb_ref, ng_ref, cw_ref, dft_ref, bdc_ref, bds_ref, s0_ref,
         mix_ref, st_ref, o_ref) = refs
        sfin_ref = None
    else:
        (uh_ref, uf_ref, uc_ref, lb_ref, ng_ref, cw_ref, dft_ref, bdc_ref, bds_ref,
         mix_ref, sfin_ref, st_ref, o_ref) = refs
        s0_ref = None
    n_chunks = T // CHUNK
    W = HGRN_WIDTH

    for d in range(2):
        for h in range(HEADS):
            if has_state:
                st_ref[d, h] = s0_ref[0, d, h].T
            else:
                st_ref[d, h] = jnp.zeros((HEAD_DIM, HEAD_DIM), F32)
    o_ref[...] = jnp.zeros_like(o_ref)

    row = lax.broadcasted_iota(jnp.int32, (CHUNK, CHUNK), 0)
    col = lax.broadcasted_iota(jnp.int32, (CHUNK, CHUNK), 1)
    keep = (col <= row, col >= row)
    tri = tuple(k.astype(F32).astype(BF16) for k in keep)

    def chunk_step(i, carry):
        deep = []
        for sub, d in ((s, d) for s in range(CHUNK_UNROLL) for d in range(2)):
            c = i * CHUNK_UNROLL + sub
            c = c if d == 0 else n_chunks - 1 - c
            rows = pl.ds(pl.multiple_of(c * CHUNK, CHUNK), CHUNK)
            q = _silu(uh_ref[rows, 0:W])
            v = uh_ref[rows, W:2 * W].astype(BF16)
            z = uh_ref[rows, (2 + d) * W:(3 + d) * W]
            lb = lb_ref[d:d + 1, :]
            e = jnp.exp(-jnp.abs(z))
            r = 1.0 / (1.0 + e)
            er = e * r
            pos = z >= 0.0
            sig_p = jnp.where(pos, r, er)
            sig_n = jnp.where(pos, er, r)
            f = lb + (1.0 - lb) * sig_p
            lf = jnp.log(jnp.maximum(f, F_MIN))
            k = (1.0 - lb) * sig_n
            lf_hi = lf.astype(BF16)
            lf_lo = (lf - lf_hi.astype(F32)).astype(BF16)
            b = _dot(tri[d], lf_hi) + _dot(tri[d], lf_lo)
            g = b[CHUNK - 1:CHUNK, :] if d == 0 else b[0:1, :]

            def window(p_level, q=q, k=k, b=b):
                dist = -p_level - b
                inside = jnp.logical_and(dist >= 0.0, dist < WINDOW)
                k_w = jnp.where(inside, k * jnp.exp(jnp.minimum(dist, WINDOW)), 0.0)
                q_w = q * jnp.exp(jnp.minimum(-dist, 0.0))
                return q_w.astype(BF16), k_w.astype(BF16)

            wins = [window(WINDOW * p) for p in range(STATIC_WINDOWS)]
            q_abs = (q * jnp.exp(b)).astype(BF16)
            k_end = (k * jnp.exp(g - b)).astype(BF16)
            decay = jnp.exp(g)
            outs = []
            for h in range(HEADS):
                ls = slice(h * HEAD_DIM, (h + 1) * HEAD_DIM)
                q_cat = jnp.concatenate([w[0][:, ls] for w in wins], axis=-1)
                k_cat = jnp.concatenate([w[1][:, ls] for w in wins], axis=-1)
                a = jnp.where(keep[d], _dot_nt(q_cat, k_cat), 0.0).astype(BF16)
                s_t = st_ref[d, h]
                o_h = _dot(a, v[:, ls]) + _dot_nt(q_abs[:, ls], s_t.astype(BF16))
                st_ref[d, h] = s_t * decay[:, ls] + _dot_tn(v[:, ls], k_end[:, ls])
                outs.append(o_h)
            o_ref[rows, :] += jnp.concatenate(outs, axis=-1)
            deep.append((d, window, v, rows, jnp.min(b)))

        deepest = functools.reduce(jnp.minimum, [entry[4] for entry in deep])
        n_windows = jnp.floor(-deepest * (1.0 / WINDOW)).astype(jnp.int32) + 1

        def extra_window(p, carry2):
            for d, window, v, rows, _ in deep:
                q_w, k_w = window(WINDOW * p.astype(F32))
                extra = []
                for h in range(HEADS):
                    ls = slice(h * HEAD_DIM, (h + 1) * HEAD_DIM)
                    a = jnp.where(keep[d], _dot_nt(q_w[:, ls], k_w[:, ls]), 0.0).astype(BF16)
                    extra.append(_dot(a, v[:, ls]))
                o_ref[rows, :] += jnp.concatenate(extra, axis=-1)
            return carry2

        lax.fori_loop(STATIC_WINDOWS, n_windows, extra_window, 0)
        return carry

    lax.fori_loop(0, n_chunks // CHUNK_UNROLL, chunk_step, 0)

    if sfin_ref is not None:
        for d in range(2):
            for h in range(HEADS):
                sfin_ref[0, d, h] = st_ref[d, h].T

    o = o_ref[...]
    parts = []
    for h in range(HEADS):
        oh = o[:, h * HEAD_DIM:(h + 1) * HEAD_DIM]
        parts.append(oh * lax.rsqrt(jnp.mean(oh * oh, axis=-1, keepdims=True) + RMS_EPS))
    a_out = jnp.concatenate(parts, axis=-1) * ng_ref[...] * _silu(uh_ref[:, 4 * W:5 * W])
    mix_ref[:, 0:W] = a_out.astype(BF16)

    zf = uf_ref[...].astype(BF16)
    zc = _dot(zf, bdc_ref[...]).astype(BF16)
    zs = _dot(zf, bds_ref[...]).astype(BF16)
    four = _dot(dft_ref[...], jnp.concatenate([zc, zs], axis=0)) * (1.0 / math.sqrt(T * FOURIER_GROUP_DIM))
    mix_ref[:, W:W + FOURIER_WIDTH] = four.astype(BF16)

    cb = uc_ref[:, 0:CONV_WIDTH]
    zz = uc_ref[:, CONV_WIDTH:2 * CONV_WIDTH] * uc_ref[:, 2 * CONV_WIDTH:3 * CONV_WIDTH]
    t_idx = lax.broadcasted_iota(jnp.int32, (T, CONV_WIDTH), 0)
    z_prev = jnp.where(t_idx == 0, 0.0, pltpu.roll(zz, 1, axis=0))
    z_next = jnp.where(t_idx == T - 1, 0.0, pltpu.roll(zz, T - 1, axis=0))
    y = cw_ref[0:1, :] * z_prev + cw_ref[1:2, :] * zz + cw_ref[2:3, :] * z_next
    mix_ref[:, W + FOURIER_WIDTH:] = (cb * y).astype(BF16)


def _mixer(uh, uf, uc, lb, norm_g, conv_w, dft, bdc, bds, T, n_seq, blk0, state0=None, layer=0):
    has_state = state0 is not None
    const2 = lambda s: (0, 0)
    in_specs = [
        pl.BlockSpec((T, HGRN_COLS), lambda s: (blk0 + s, 0)),
        pl.BlockSpec((T, FOURIER_WIDTH), lambda s: (blk0 + s, 0)),
        pl.BlockSpec((T, CONV_COLS), lambda s: (blk0 + s, 0)),
        pl.BlockSpec((2, HGRN_WIDTH), const2),
        pl.BlockSpec((1, HGRN_WIDTH), const2),
        pl.BlockSpec((3, CONV_WIDTH), const2),
        pl.BlockSpec((T, 2 * T), const2),
        pl.BlockSpec((FOURIER_WIDTH, FOURIER_WIDTH), const2),
        pl.BlockSpec((FOURIER_WIDTH, FOURIER_WIDTH), const2),
    ]
    args = [uh, uf, uc, lb, norm_g, conv_w, dft, bdc, bds]
    mix_shape = jax.ShapeDtypeStruct((n_seq * T, D_MODEL), BF16)
    mix_spec = pl.BlockSpec((T, D_MODEL), lambda s: (s, 0))
    st_block = (1, 2, HEADS, HEAD_DIM, HEAD_DIM)
    if has_state:
        in_specs.append(pl.BlockSpec((1, None) + st_block[1:], lambda s: (s, layer, 0, 0, 0, 0)))
        args.append(state0)
        out_shape = [mix_shape]
        out_specs = [mix_spec]
    else:
        out_shape = [mix_shape, jax.ShapeDtypeStruct((n_seq, 2, HEADS, HEAD_DIM, HEAD_DIM), F32)]
        out_specs = [mix_spec, pl.BlockSpec(st_block, lambda s: (s, 0, 0, 0, 0))]
    return pl.pallas_call(
        functools.partial(_mixer_kernel, T=T, has_state=has_state),
        out_shape=out_shape,
        grid=(n_seq,),
        in_specs=in_specs,
        out_specs=out_specs,
        scratch_shapes=[pltpu.VMEM((2, HEADS, HEAD_DIM, HEAD_DIM), F32), pltpu.VMEM((T, HGRN_WIDTH), F32)],
        compiler_params=pltpu.CompilerParams(dimension_semantics=("arbitrary",), vmem_limit_bytes=VMEM_LIMIT),
        name="mixer_latent" if has_state else "mixer_context",
    )(*args)


def _merge_kernel(x_ref, mixc_ref, mixl_ref, ug_ref, mod_ref, bg_ref, wp_ref, wo_ref, g1_ref, b1_ref, wrh_ref, wrl_ref,
                  br_ref, x1_ref, h2_ref, apos_ref, gk_ref, n16_ref):
    W = HGRN_WIDTH
    gates = jax.nn.sigmoid(ug_ref[...] + bg_ref[...])
    mix = jnp.where(pl.program_id(0) < N_CTX // TOK_TILE, mixc_ref[...], mixl_ref[...])
    edges = (0, W, W + FOURIER_WIDTH, D_MODEL)
    merged = None
    for j in range(N_BRANCHES):
        p = _dot(mix[:, edges[j]:edges[j + 1]], wp_ref[edges[j]:edges[j + 1], :])
        term = gates[:, j * D_MODEL:(j + 1) * D_MODEL] * p
        merged = term if merged is None else merged + term
    y = _dot(merged.astype(BF16), wo_ref[...])
    x1 = _ln(ALPHA * x_ref[...] + mod_ref[2:3, :] * y) * g1_ref[...] + b1_ref[...]
    x1_ref[...] = x1
    h2 = _ln(x1) * (1.0 + mod_ref[4:5, :]) + mod_ref[3:4, :]
    h2_hi = h2.astype(BF16)
    h2_ref[...] = h2_hi
    h2_lo = (h2 - h2_hi.astype(F32)).astype(BF16)
    logits = _dot(h2_hi, wrh_ref[...]) + (_dot(h2_hi, wrl_ref[...]) + _dot(h2_lo, wrh_ref[...]))
    scores = jax.nn.sigmoid(logits)
    sel = scores + br_ref[...]
    lane_f = lax.broadcasted_iota(jnp.int32, sel.shape, 1).astype(F32)
    hits = []
    for _ in range(TOP_K):
        top = jnp.max(sel, axis=-1, keepdims=True)
        first = jnp.min(jnp.where(sel == top, lane_f, float(N_EXPERTS)), axis=-1, keepdims=True)
        hit = lane_f == first
        hits.append(hit)
        sel = jnp.where(hit, -jnp.inf, sel)
    chosen = functools.reduce(jnp.logical_or, hits)
    picked = jnp.where(chosen, scores, 0.0)
    denom = jnp.sum(picked, axis=-1, keepdims=True) + 1e-20
    gate = ROUTED_SCALE * picked / denom

    chosen_f = chosen.astype(F32)
    r_i = lax.broadcasted_iota(jnp.int32, (TOK_TILE, TOK_TILE), 0)
    c_i = lax.broadcasted_iota(jnp.int32, (TOK_TILE, TOK_TILE), 1)
    earlier = (c_i < r_i).astype(F32).astype(BF16)
    rank = _dot(earlier, chosen_f.astype(BF16))
    count = jnp.sum(chosen_f, axis=0, keepdims=True)
    n16 = jnp.floor((count + (ROW_CHUNK - 1)) * (1.0 / ROW_CHUNK))
    e_r = lax.broadcasted_iota(jnp.int32, (N_EXPERTS, N_EXPERTS), 0)
    e_c = lax.broadcasted_iota(jnp.int32, (N_EXPERTS, N_EXPERTS), 1)
    before = (e_r < e_c).astype(F32).astype(BF16)
    seg_start = ROW_CHUNK * _dot(jnp.broadcast_to(n16, (8, N_EXPERTS)).astype(BF16), before)[0:1, :]
    row_of = seg_start + rank
    lane_k = lax.broadcasted_iota(jnp.int32, (TOK_TILE, TOP_K), 1)
    apos = jnp.zeros((TOK_TILE, TOP_K), F32)
    gk = jnp.zeros((TOK_TILE, TOP_K), F32)
    for k, hit in enumerate(hits):
        apos = jnp.where(lane_k == k, jnp.sum(jnp.where(hit, row_of, 0.0), axis=-1, keepdims=True), apos)
        gk = jnp.where(lane_k == k, jnp.sum(jnp.where(hit, gate, 0.0), axis=-1, keepdims=True), gk)
    apos_ref[...] = apos
    gk_ref[...] = gk
    n16_ref[...] = n16


def _merge(x, mix_ctx, mix_lat, ug, mod, b_gate, wp, wo, ln_g, ln_b, wr_hi, wr_lo, b_router, layer):
    tok = lambda n: pl.BlockSpec((TOK_TILE, n), lambda i: (i, 0))
    const2 = lambda i: (0, 0)
    ctx_tiles = N_CTX // TOK_TILE
    return pl.pallas_call(
        _merge_kernel,
        out_shape=[jax.ShapeDtypeStruct((N_TOK, D_MODEL), F32), jax.ShapeDtypeStruct((N_TOK, D_MODEL), BF16),
                   jax.ShapeDtypeStruct((N_TOK, TOP_K), F32), jax.ShapeDtypeStruct((N_TOK, TOP_K), F32),
                   jax.ShapeDtypeStruct((N_TILES, 1, N_EXPERTS), F32)],
        grid=(N_TILES,),
        in_specs=[
            tok(D_MODEL),
            pl.BlockSpec((TOK_TILE, D_MODEL), lambda i: (jnp.minimum(i, ctx_tiles - 1), 0)),
            pl.BlockSpec((TOK_TILE, D_MODEL), lambda i: (jnp.maximum(i - ctx_tiles, 0), 0)),
            tok(GATE_COLS),
            pl.BlockSpec((None, None, 6, D_MODEL), lambda i: (layer, _cond_row_of_tile(i, TOK_TILE), 0, 0)),
            pl.BlockSpec((1, GATE_COLS), const2),
            pl.BlockSpec((D_MODEL, D_MODEL), const2),
            pl.BlockSpec((D_MODEL, D_MODEL), const2),
            pl.BlockSpec((1, D_MODEL), const2),
            pl.BlockSpec((1, D_MODEL), const2),
            pl.BlockSpec((D_MODEL, N_EXPERTS), const2),
            pl.BlockSpec((D_MODEL, N_EXPERTS), const2),
            pl.BlockSpec((1, N_EXPERTS), const2),
        ],
        out_specs=[tok(D_MODEL), tok(D_MODEL), tok(TOP_K), tok(TOP_K),
                   pl.BlockSpec((None, 1, N_EXPERTS), lambda i: (i, 0, 0))],
        compiler_params=pltpu.CompilerParams(dimension_semantics=("arbitrary",), vmem_limit_bytes=VMEM_LIMIT),
        name="merge_router",
    )(x, mix_ctx, mix_lat, ug, mod, b_gate, wp, wo, ln_g, ln_b, wr_hi, wr_lo, b_router)


def _routing_tables(n16):
    n16 = n16.astype(jnp.int32)
    hi = jnp.cumsum(n16, axis=1)
    lo = hi - n16
    n_chunks = hi[:, -1]
    eo = jnp.cumsum(n16, axis=0) - n16
    tc = jnp.sum(n16, axis=0)
    tiles = (tc + CHUNKS_PER_GMM_TILE - 1) // CHUNKS_PER_GMM_TILE
    tile_end = jnp.cumsum(tiles)
    tile_off = tile_end - tiles
    n_used = tile_end[-1]
    c = jnp.arange(TILE_CHUNKS, dtype=jnp.int32)[None, :, None]
    owner = jnp.logical_and(c >= lo[:, None, :], c < hi[:, None, :])
    shift = CHUNKS_PER_GMM_TILE * tile_off[None, :] + eo - lo
    dst = c[:, :, 0] + jnp.sum(jnp.where(owner, shift[:, None, :], 0), axis=-1)
    pad_start = CHUNKS_PER_GMM_TILE * tile_off + tc
    pad_count = CHUNKS_PER_GMM_TILE * tiles - tc
    return (dst.reshape(-1).astype(jnp.int32), n_chunks, tile_off.astype(jnp.int32), tiles.astype(jnp.int32),
            jnp.reshape(n_used, (1,)).astype(jnp.int32), pad_start.astype(jnp.int32), pad_count.astype(jnp.int32))


def _one_hot_rows(apos_ref, weight_ref=None):
    r = lax.broadcasted_iota(jnp.int32, (TOK_TILE, TILE_ROWS), 1).astype(F32)
    pt = jnp.zeros((TOK_TILE, TILE_ROWS), F32)
    for k in range(TOP_K):
        val = 1.0 if weight_ref is None else weight_ref[:, k:k + 1]
        pt = jnp.where(r == apos_ref[:, k:k + 1], val, pt)
    return pt.astype(BF16)


def _chunk_copy(src_ref, src_chunk, dst_ref, dst_chunk, sem):
    rows = lambda c: pl.ds(pl.multiple_of(c * ROW_CHUNK, ROW_CHUNK), ROW_CHUNK)
    return pltpu.make_async_copy(src_ref.at[rows(src_chunk), :], dst_ref.at[rows(dst_chunk), :], sem)


def _dispatch_kernel(dst_ref, nch_ref, pstart_ref, pcount_ref, nu_ref, h_ref, apos_ref, xs_hbm, stage_ref, zero_ref,
                     sem):
    j = pl.program_id(0)
    buf = j % 2

    def wait_chunks(b, n):
        def body(c, carry):
            _chunk_copy(stage_ref.at[b], 0, xs_hbm, 0, sem.at[b]).wait()
            return carry
        lax.fori_loop(0, n, body, 0)

    @pl.when(j < N_TILES)
    def _():
        @pl.when(j >= 2)
        def _():
            wait_chunks(buf, nch_ref[j - 2])

        pt = _one_hot_rows(apos_ref)
        h = h_ref[...]
        for r0 in range(0, TILE_ROWS, DISPATCH_ROWS):
            stage_ref[buf, r0:r0 + DISPATCH_ROWS, :] = _dot_tn(pt[:, r0:r0 + DISPATCH_ROWS], h).astype(BF16)

        def send(c, carry):
            _chunk_copy(stage_ref.at[buf], c, xs_hbm, dst_ref[j * TILE_CHUNKS + c], sem.at[buf]).start()
            return carry
        lax.fori_loop(0, nch_ref[j], send, 0)

    @pl.when(j == N_TILES)
    def _():
        zero_ref[...] = jnp.zeros_like(zero_ref)

        def tile_copy(t):
            rows = pl.ds(pl.multiple_of(t * GMM_TILE, GMM_TILE), GMM_TILE)
            return pltpu.make_async_copy(zero_ref, xs_hbm.at[rows, :], sem.at[3])

        def per_expert(e, total):
            def fill(i, carry):
                _chunk_copy(zero_ref, 0, xs_hbm, pstart_ref[e] + i, sem.at[2]).start()
                return carry
            lax.fori_loop(0, pcount_ref[e], fill, 0)
            return total + pcount_ref[e]
        n_pad = lax.fori_loop(0, N_EXPERTS, per_expert, 0)

        def fill_tile(t, carry):
            tile_copy(t).start()
            return carry
        lax.fori_loop(nu_ref[0], MAX_GMM_TILES, fill_tile, 0)
        wait_chunks(0, nch_ref[N_TILES - 2])
        wait_chunks(1, nch_ref[N_TILES - 1])

        def drain(i, carry):
            _chunk_copy(zero_ref, 0, xs_hbm, 0, sem.at[2]).wait()
            return carry
        lax.fori_loop(0, n_pad, drain, 0)

        def drain_tile(t, carry):
            tile_copy(0).wait()
            return carry
        lax.fori_loop(nu_ref[0], MAX_GMM_TILES, drain_tile, 0)


def _dispatch(h2, apos, dst, n_chunks, pad_start, pad_count, n_used):
    last = N_TILES - 1
    grid_spec = pltpu.PrefetchScalarGridSpec(
        num_scalar_prefetch=5,
        grid=(N_TILES + 1,),
        in_specs=[
            pl.BlockSpec((TOK_TILE, D_MODEL), lambda j, *_: (jnp.minimum(j, last), 0)),
            pl.BlockSpec((TOK_TILE, TOP_K), lambda j, *_: (jnp.minimum(j, last), 0)),
        ],
        out_specs=pl.BlockSpec(memory_space=pl.ANY),
        scratch_shapes=[pltpu.VMEM((2, TILE_ROWS, D_MODEL), BF16), pltpu.VMEM((GMM_TILE, D_MODEL), BF16),
                        pltpu.SemaphoreType.DMA((4,))],
    )
    return pl.pallas_call(
        _dispatch_kernel,
        out_shape=jax.ShapeDtypeStruct((SORTED_ROWS, D_MODEL), BF16),
        grid_spec=grid_spec,
        compiler_params=pltpu.CompilerParams(dimension_semantics=("arbitrary",), vmem_limit_bytes=VMEM_LIMIT),
        name="moe_dispatch",
    )(dst, n_chunks, pad_start, pad_count, n_used, h2, apos)


def _gmm_kernel(off_ref, cnt_ref, nu_ref, x_hbm, wg_ref, wu_ref, wd_ref, y_hbm, xbuf, ybuf, wg_b, wu_b, wd_b,
                in_sem, out_sem):
    e = pl.program_id(0)
    n_used = nu_ref[0]

    def rows(g):
        return pl.ds(pl.multiple_of(g * GMM_TILE, GMM_TILE), GMM_TILE)

    def fetch(g):
        slot = g % GMM_IN_SLOTS
        return pltpu.make_async_copy(x_hbm.at[rows(g), :], xbuf.at[slot], in_sem.at[slot])

    def flush(g):
        slot = g % GMM_OUT_SLOTS
        return pltpu.make_async_copy(ybuf.at[slot], y_hbm.at[rows(g), :], out_sem.at[slot])

    @pl.when(e == 0)
    def _():
        for g in range(GMM_IN_SLOTS - 1):
            @pl.when(g < n_used)
            def _(g=g):
                fetch(g).start()

    wg_b[...] = wg_ref[...].astype(BF16)
    wu_b[...] = wu_ref[...].astype(BF16)
    wd_b[...] = wd_ref[...].astype(BF16)

    def tile(g, carry):
        ahead = g + (GMM_IN_SLOTS - 1)

        @pl.when(ahead < n_used)
        def _():
            fetch(ahead).start()

        fetch(g).wait()
        x = xbuf[g % GMM_IN_SLOTS]
        hid = _silu(_dot(x, wg_b[...])) * _dot(x, wu_b[...])
        y = _dot(hid.astype(BF16), wd_b[...]).astype(BF16)

        @pl.when(g >= GMM_OUT_SLOTS)
        def _():
            flush(g - GMM_OUT_SLOTS).wait()

        ybuf[g % GMM_OUT_SLOTS] = y
        flush(g).start()
        return carry

    lax.fori_loop(off_ref[e], off_ref[e] + cnt_ref[e], tile, 0)

    @pl.when(e == N_EXPERTS - 1)
    def _():
        for k in range(GMM_OUT_SLOTS):
            @pl.when(n_used - 1 - k >= 0)
            def _(k=k):
                flush(n_used - 1 - k).wait()


def _gmm(xs, weg, weu, wed, tile_off, tiles, n_used, layer):
    w_in_spec = pl.BlockSpec((None, None, D_MODEL, D_EXPERT), lambda e, *_: (layer, e, 0, 0))
    grid_spec = pltpu.PrefetchScalarGridSpec(
        num_scalar_prefetch=3,
        grid=(N_EXPERTS,),
        in_specs=[
            pl.BlockSpec(memory_space=pl.ANY),
            w_in_spec, w_in_spec,
            pl.BlockSpec((None, None, D_EXPERT, D_MODEL), lambda e, *_: (layer, e, 0, 0)),
        ],
        out_specs=pl.BlockSpec(memory_space=pl.ANY),
        scratch_shapes=[pltpu.VMEM((GMM_IN_SLOTS, GMM_TILE, D_MODEL), BF16),
                        pltpu.VMEM((GMM_OUT_SLOTS, GMM_TILE, D_MODEL), BF16),
                        pltpu.VMEM((D_MODEL, D_EXPERT), BF16), pltpu.VMEM((D_MODEL, D_EXPERT), BF16),
                        pltpu.VMEM((D_EXPERT, D_MODEL), BF16),
                        pltpu.SemaphoreType.DMA((GMM_IN_SLOTS,)), pltpu.SemaphoreType.DMA((GMM_OUT_SLOTS,))],
    )
    return pl.pallas_call(
        _gmm_kernel,
        out_shape=jax.ShapeDtypeStruct((SORTED_ROWS, D_MODEL), BF16),
        grid_spec=grid_spec,
        input_output_aliases={3: 0},
        compiler_params=pltpu.CompilerParams(dimension_semantics=("arbitrary",), vmem_limit_bytes=VMEM_LIMIT),
        name="moe_gmm",
    )(tile_off, tiles, n_used, xs, weg, weu, wed)


def _combine_kernel(dst_ref, nch_ref, ys_hbm, apos_ref, gk_ref, h_ref, x1_ref, mod_ref, wsg_ref, wsu_ref, wsd_ref,
                    g2_ref, b2_ref, o_ref, stage_ref, sem):
    j = pl.program_id(0)
    buf = j % 2

    def fetch(jj, b):
        def body(c, carry):
            _chunk_copy(ys_hbm, dst_ref[jj * TILE_CHUNKS + c], stage_ref.at[b], c, sem.at[b]).start()
            return carry
        lax.fori_loop(0, nch_ref[jj], body, 0)

    @pl.when(j == 0)
    def _():
        stage_ref[...] = jnp.zeros_like(stage_ref)
        fetch(0, 0)

    @pl.when(j + 1 < N_TILES)
    def _():
        fetch(j + 1, 1 - buf)

    h = h_ref[...]
    hid = _silu(_dot(h, wsg_ref[...])) * _dot(h, wsu_ref[...])
    shared = _dot(hid.astype(BF16), wsd_ref[...])
    ptw = _one_hot_rows(apos_ref, gk_ref)

    def wait(c, carry):
        _chunk_copy(ys_hbm, 0, stage_ref.at[buf], 0, sem.at[buf]).wait()
        return carry
    lax.fori_loop(0, nch_ref[j], wait, 0)

    routed = _dot(ptw, stage_ref[buf])
    r = ALPHA * x1_ref[...] + mod_ref[5:6, :] * (routed + shared)
    o_ref[...] = _ln(r) * g2_ref[...] + b2_ref[...]


def _combine(ys, apos, gk, h2, x1, mod, wsg, wsu, wsd, ln_g, ln_b, dst, n_chunks, layer):
    tok = lambda n: pl.BlockSpec((TOK_TILE, n), lambda j, *_: (j, 0))
    const2 = lambda j, *_: (0, 0)
    grid_spec = pltpu.PrefetchScalarGridSpec(
        num_scalar_prefetch=2,
        grid=(N_TILES,),
        in_specs=[
            pl.BlockSpec(memory_space=pl.ANY),
            tok(TOP_K), tok(TOP_K), tok(D_MODEL), tok(D_MODEL),
            pl.BlockSpec((None, None, 6, D_MODEL), lambda j, *_: (layer, _cond_row_of_tile(j, TOK_TILE), 0, 0)),
            pl.BlockSpec((D_MODEL, D_SHARED), const2),
            pl.BlockSpec((D_MODEL, D_SHARED), const2),
            pl.BlockSpec((D_SHARED, D_MODEL), const2),
            pl.BlockSpec((1, D_MODEL), const2),
            pl.BlockSpec((1, D_MODEL), const2),
        ],
        out_specs=tok(D_MODEL),
        scratch_shapes=[pltpu.VMEM((2, TILE_ROWS, D_MODEL), BF16), pltpu.SemaphoreType.DMA((2,))],
    )
    return pl.pallas_call(
        _combine_kernel,
        out_shape=jax.ShapeDtypeStruct((N_TOK, D_MODEL), F32),
        grid_spec=grid_spec,
        compiler_params=pltpu.CompilerParams(dimension_semantics=("arbitrary",), vmem_limit_bytes=VMEM_LIMIT),
        name="moe_combine",
    )(dst, n_chunks, ys, apos, gk, h2, x1, mod, wsg, wsu, wsd, ln_g, ln_b)


def kernel(x_prompt, x_sample, c, state_hgrn, c_ctx, w_ada, b_ada, w_in, b_gate, hgrn_lb, hgrn_norm, conv_w,
           w_proj_hgrn, w_proj_fourier, w_proj_conv, w_out, ln1_g, ln1_b, ln2_g, ln2_b, w_router, b_router,
           w_exp_gate, w_exp_up, w_exp_down, w_sh_gate, w_sh_up, w_sh_down):
    lb_sm = jax.nn.softmax(hgrn_lb.astype(F32), axis=0)
    lb_all = jnp.cumsum(lb_sm, axis=0) - lb_sm[:1]

    pos = jnp.asarray(_pos_emb_table())
    dft_ctx = jnp.asarray(_dft_time(SEQ)).astype(BF16)
    dft_lat = jnp.asarray(_dft_time(DEC_SEQ)).astype(BF16)
    bdc, bds = (jnp.asarray(m).astype(BF16) for m in _dft_channel())

    cond = jnp.concatenate([c_ctx[None, :], c, jnp.zeros((N_COND - 1 - DEC_BATCH, D_MODEL), F32)], axis=0)
    mod = _ada_mod(cond, w_ada, b_ada).reshape(DEPTH, N_COND, 6, D_MODEL)

    w_in_b = w_in.astype(BF16)
    wp_b = jnp.concatenate([w_proj_hgrn, w_proj_fourier, w_proj_conv], axis=1).astype(BF16)
    wo_b = w_out.astype(BF16)
    wr_hi = w_router.astype(BF16)
    wr_lo = (w_router - wr_hi.astype(F32)).astype(BF16)
    wsg_b, wsu_b, wsd_b = w_sh_gate.astype(BF16), w_sh_up.astype(BF16), w_sh_down.astype(BF16)

    x = _assemble(x_prompt, x_sample, pos)
    new_states = []
    for l in range(DEPTH):
        uh, uf, uc, ug = _in_proj(x, mod, w_in_b, l)
        small = (lb_all[l], hgrn_norm[l][None, :], conv_w[l])
        mix_ctx, s_ctx = _mixer(uh, uf, uc, *small, dft_ctx, bdc, bds, T=SEQ, n_seq=BATCH, blk0=0)
        (mix_lat,) = _mixer(uh, uf, uc, *small, dft_lat, bdc, bds, T=DEC_SEQ, n_seq=DEC_BATCH,
                            blk0=N_CTX // DEC_SEQ, state0=state_hgrn, layer=l)
        new_states.append(s_ctx)
        x1, h2, apos, gk, n16 = _merge(x, mix_ctx, mix_lat, ug, mod, b_gate[l][None, :], wp_b[l], wo_b[l],
                                       ln1_g[l][None, :], ln1_b[l][None, :], wr_hi[l], wr_lo[l],
                                       b_router[l][None, :], l)
        dst, n_chunks, tile_off, tiles, n_used, pad_start, pad_count = _routing_tables(
            n16.reshape(N_TILES, N_EXPERTS))
        xs = _dispatch(h2, apos, dst, n_chunks, pad_start, pad_count, n_used)
        ys = _gmm(xs, w_exp_gate, w_exp_up, w_exp_down, tile_off, tiles, n_used, l)
        x = _combine(ys, apos, gk, h2, x1, mod, wsg_b[l], wsu_b[l], wsd_b[l], ln2_g[l][None, :], ln2_b[l][None, :],
                     dst, n_chunks, l)
    y_prompt = x[:N_CTX].reshape(BATCH, SEQ, D_MODEL)
    y_sample = x[N_CTX:].reshape(DEC_BATCH, DEC_SEQ, D_MODEL)
    return (y_prompt, y_sample, jnp.stack(new_states, axis=1))
```

```python
import functools
import math

import numpy as np
import jax
import jax.numpy as jnp
from jax import lax
from jax.experimental import pallas as pl
from jax.experimental.pallas import tpu as pltpu

F32 = jnp.float32
BF16 = jnp.bfloat16

D_MODEL = 1024
BATCH = 16
SEQ = 256
DEPTH = 2
DEC_BATCH = 4
DEC_SEQ = 1024
GRID_W = 64
HEADS = 4
HEAD_DIM = 128
HGRN_WIDTH = HEADS * HEAD_DIM
FOURIER_GROUPS = 4
FOURIER_GROUP_DIM = 64
FOURIER_WIDTH = FOURIER_GROUPS * FOURIER_GROUP_DIM
CONV_WIDTH = 256
N_BRANCHES = 3
HGRN_COLS = 5 * HGRN_WIDTH
CONV_COLS = 3 * CONV_WIDTH
GATE_COLS = N_BRANCHES * D_MODEL
IN_COLS = HGRN_COLS + FOURIER_WIDTH + CONV_COLS + GATE_COLS
N_EXPERTS = 64
TOP_K = 8
D_EXPERT = 256
D_SHARED = 256
ROUTED_SCALE = 2.5
ALPHA = (2 * DEPTH) ** 0.25
LN_EPS = 1e-6
RMS_EPS = 1e-6
F_MIN = 1e-30

N_CTX = BATCH * SEQ
N_LAT = DEC_BATCH * DEC_SEQ
N_TOK = N_CTX + N_LAT
N_COND = 8

TOK_TILE = 256
CHUNK = 64
CHUNK_UNROLL = 2
WINDOW = 120.0
STATIC_WINDOWS = 2
ADA_TILE = 1536
VMEM_LIMIT = 56 * 1024 * 1024

N_TILES = N_TOK // TOK_TILE
ROW_CHUNK = 16
TILE_CHUNKS = (TOK_TILE * TOP_K + N_EXPERTS * (ROW_CHUNK - 1)) // ROW_CHUNK
DISPATCH_ROWS = 512
TILE_ROWS = -(-TILE_CHUNKS * ROW_CHUNK // DISPATCH_ROWS) * DISPATCH_ROWS
GMM_TILE = 256
GMM_WIDTH = 4
GMM_AHEAD = 2
GMM_IN_SLOTS = GMM_WIDTH + GMM_AHEAD
GMM_OUT_SLOTS = 2 * GMM_WIDTH
CHUNKS_PER_GMM_TILE = GMM_TILE // ROW_CHUNK
MAX_GMM_TILES = (N_TILES * TILE_CHUNKS + N_EXPERTS * (CHUNKS_PER_GMM_TILE - 1)) // CHUNKS_PER_GMM_TILE
SORTED_ROWS = MAX_GMM_TILES * GMM_TILE


def _cond_row_of_tile(i, tile):
    ctx_tiles = N_CTX // tile
    per_seq = DEC_SEQ // tile
    return jnp.where(i < ctx_tiles, 0, 1 + (i - ctx_tiles) // per_seq)


def _silu(x):
    return x * jax.nn.sigmoid(x)


def _ln(x):
    mu = jnp.mean(x, axis=-1, keepdims=True)
    xc = x - mu
    var = jnp.mean(xc * xc, axis=-1, keepdims=True)
    return xc * lax.rsqrt(var + LN_EPS)


def _dot(a, b):
    return jnp.dot(a, b, preferred_element_type=F32)


def _dot_nt(a, b):
    return lax.dot_general(a, b, (((1,), (1,)), ((), ())), preferred_element_type=F32)


def _dot_tn(a, b):
    return lax.dot_general(a, b, (((0,), (0,)), ((), ())), preferred_element_type=F32)


@functools.lru_cache(maxsize=None)
def _pos_emb_table():
    rows = DEC_SEQ // GRID_W
    t = np.arange(rows * GRID_W)
    r = (t // GRID_W).astype(np.float32)
    col = (t % GRID_W).astype(np.float32)
    quarter = D_MODEL // 4
    omega = (1.0 / (np.float32(10000.0) ** (np.arange(quarter, dtype=np.float32) / np.float32(quarter)))).astype(np.float32)
    ar = (r[:, None] * omega).astype(np.float32)
    ac = (col[:, None] * omega).astype(np.float32)
    return np.concatenate([np.sin(ar), np.cos(ar), np.sin(ac), np.cos(ac)], axis=-1).astype(np.float32)


@functools.lru_cache(maxsize=None)
def _dft_time(T):
    k = np.arange(T)
    ph = 2.0 * np.pi * ((k[:, None] * k[None, :]) % T) / T
    return np.concatenate([np.cos(ph), -np.sin(ph)], axis=1).astype(np.float32)


@functools.lru_cache(maxsize=None)
def _dft_channel():
    n = FOURIER_GROUP_DIM
    k = np.arange(n)
    ph = 2.0 * np.pi * ((k[:, None] * k[None, :]) % n) / n
    eye = np.eye(FOURIER_GROUPS)
    return np.kron(eye, np.cos(ph)).astype(np.float32), np.kron(eye, np.sin(ph)).astype(np.float32)


def _assemble_kernel(xp_ref, xs_ref, pos_ref, o_ref):
    i = pl.program_id(0)

    @pl.when(i < N_CTX // TOK_TILE)
    def _():
        o_ref[...] = xp_ref[...]

    @pl.when(i >= N_CTX // TOK_TILE)
    def _():
        o_ref[...] = xs_ref[...] + pos_ref[...]


def _assemble(x_prompt, x_sample, pos):
    ctx_tiles = N_CTX // TOK_TILE
    per_seq = DEC_SEQ // TOK_TILE
    return pl.pallas_call(
        _assemble_kernel,
        out_shape=jax.ShapeDtypeStruct((N_TOK, D_MODEL), F32),
        grid=(N_TOK // TOK_TILE,),
        in_specs=[
            pl.BlockSpec((TOK_TILE, D_MODEL), lambda i: (jnp.minimum(i, ctx_tiles - 1), 0)),
            pl.BlockSpec((TOK_TILE, D_MODEL), lambda i: (jnp.maximum(i - ctx_tiles, 0), 0)),
            pl.BlockSpec((TOK_TILE, D_MODEL), lambda i: (jnp.maximum(i - ctx_tiles, 0) % per_seq, 0)),
        ],
        out_specs=pl.BlockSpec((TOK_TILE, D_MODEL), lambda i: (i, 0)),
        compiler_params=pltpu.CompilerParams(dimension_semantics=("arbitrary",)),
        name="assemble_tokens",
    )(x_prompt.reshape(N_CTX, D_MODEL), x_sample.reshape(N_LAT, D_MODEL), pos)


def _ada_kernel(c_ref, w_ref, b_ref, o_ref):
    s = _silu(c_ref[...]).astype(BF16)
    o_ref[...] = _dot(s, w_ref[...].astype(BF16)) + b_ref[...]


def _ada_mod(cond, w_ada, b_ada):
    n_col = 6 * D_MODEL
    return pl.pallas_call(
        _ada_kernel,
        out_shape=jax.ShapeDtypeStruct((DEPTH, N_COND, n_col), F32),
        grid=(DEPTH, n_col // ADA_TILE),
        in_specs=[
            pl.BlockSpec((N_COND, D_MODEL), lambda l, j: (0, 0)),
            pl.BlockSpec((None, D_MODEL, ADA_TILE), lambda l, j: (l, 0, j)),
            pl.BlockSpec((None, 1, ADA_TILE), lambda l, j: (l, 0, j)),
        ],
        out_specs=pl.BlockSpec((None, N_COND, ADA_TILE), lambda l, j: (l, 0, j)),
        compiler_params=pltpu.CompilerParams(dimension_semantics=("arbitrary", "arbitrary"),
                                             vmem_limit_bytes=VMEM_LIMIT),
        name="ada_mod",
    )(cond, w_ada, b_ada.reshape(DEPTH, 1, n_col))


def _in_proj_kernel(x_ref, mod_ref, w_ref, uh_ref, uf_ref, uc_ref, ug_ref):
    sh1 = mod_ref[0:1, :]
    sc1 = mod_ref[1:2, :]
    h = (_ln(x_ref[...]) * (1.0 + sc1) + sh1).astype(BF16)
    c0 = 0
    for ref, n in ((uh_ref, HGRN_COLS), (uf_ref, FOURIER_WIDTH), (uc_ref, CONV_COLS), (ug_ref, GATE_COLS)):
        ref[...] = _dot(h, w_ref[:, c0:c0 + n])
        c0 += n


def _in_proj(x, mod, w_in_bf16, layer):
    n_tiles = N_TOK // TOK_TILE
    widths = (HGRN_COLS, FOURIER_WIDTH, CONV_COLS, GATE_COLS)
    return pl.pallas_call(
        _in_proj_kernel,
        out_shape=[jax.ShapeDtypeStruct((N_TOK, n), F32) for n in widths],
        grid=(n_tiles,),
        in_specs=[
            pl.BlockSpec((TOK_TILE, D_MODEL), lambda i: (i, 0)),
            pl.BlockSpec((None, None, 6, D_MODEL), lambda i: (layer, _cond_row_of_tile(i, TOK_TILE), 0, 0)),
            pl.BlockSpec((None, D_MODEL, IN_COLS), lambda i: (layer, 0, 0)),
        ],
        out_specs=[pl.BlockSpec((TOK_TILE, n), lambda i: (i, 0)) for n in widths],
        compiler_params=pltpu.CompilerParams(dimension_semantics=("arbitrary",), vmem_limit_bytes=VMEM_LIMIT),
        name="in_proj",
    )(x, mod, w_in_bf16)


def _mixer_kernel(*refs, T, has_state):
    if has_state:
        (uh_ref, uf_ref, uc_ref, lb_ref, ng_ref, cw_ref, dft_ref, bdc_ref, bds_ref, s0_ref,
         mix_ref, st_ref, o_ref) = refs
        sfin_ref = None
    else:
        (uh_ref, uf_ref, uc_ref, lb_ref, ng_ref, cw_ref, dft_ref, bdc_ref, bds_ref,
         mix_ref, sfin_ref, st_ref, o_ref) = refs
        s0_ref = None
    n_chunks = T // CHUNK
    W = HGRN_WIDTH

    for d in range(2):
        for h in range(HEADS):
            if has_state:
                st_ref[d, h] = s0_ref[0, d, h].T
            else:
                st_ref[d, h] = jnp.zeros((HEAD_DIM, HEAD_DIM), F32)
    o_ref[...] = jnp.zeros_like(o_ref)

    row = lax.broadcasted_iota(jnp.int32, (CHUNK, CHUNK), 0)
    col = lax.broadcasted_iota(jnp.int32, (CHUNK, CHUNK), 1)
    keep = (col <= row, col >= row)
    tri = tuple(k.astype(F32).astype(BF16) for k in keep)

    def chunk_step(i, carry):
        deep = []
        for sub, d in ((s, d) for s in range(CHUNK_UNROLL) for d in range(2)):
            c = i * CHUNK_UNROLL + sub
            c = c if d == 0 else n_chunks - 1 - c
            rows = pl.ds(pl.multiple_of(c * CHUNK, CHUNK), CHUNK)
            q = _silu(uh_ref[rows, 0:W])
            v = uh_ref[rows, W:2 * W].astype(BF16)
            z = uh_ref[rows, (2 + d) * W:(3 + d) * W]
            lb = lb_ref[d:d + 1, :]
            e = jnp.exp(-jnp.abs(z))
            r = 1.0 / (1.0 + e)
            er = e * r
            pos = z >= 0.0
            sig_p = jnp.where(pos, r, er)
            sig_n = jnp.where(pos, er, r)
            f = lb + (1.0 - lb) * sig_p
            lf = jnp.log(jnp.maximum(f, F_MIN))
            k = (1.0 - lb) * sig_n
            lf_hi = lf.astype(BF16)
            lf_lo = (lf - lf_hi.astype(F32)).astype(BF16)
            b = _dot(tri[d], lf_hi) + _dot(tri[d], lf_lo)
            g = b[CHUNK - 1:CHUNK, :] if d == 0 else b[0:1, :]

            def window(p_level, q=q, k=k, b=b):
                dist = -p_level - b
                inside = jnp.logical_and(dist >= 0.0, dist < WINDOW)
                k_w = jnp.where(inside, k * jnp.exp(jnp.minimum(dist, WINDOW) - 0.5 * WINDOW), 0.0)
                q_w = q * jnp.exp(jnp.minimum(-dist, 0.0) + 0.5 * WINDOW)
                return q_w.astype(BF16), k_w.astype(BF16)

            wins = [window(WINDOW * p) for p in range(STATIC_WINDOWS)]
            q_abs = (q * jnp.exp(b)).astype(BF16)
            k_end = (k * jnp.exp(g - b)).astype(BF16)
            decay = jnp.exp(g)
            outs = []
            for h in range(HEADS):
                ls = slice(h * HEAD_DIM, (h + 1) * HEAD_DIM)
                q_cat = jnp.concatenate([w[0][:, ls] for w in wins], axis=-1)
                k_cat = jnp.concatenate([w[1][:, ls] for w in wins], axis=-1)
                a = jnp.where(keep[d], _dot_nt(q_cat, k_cat), 0.0).astype(BF16)
                s_t = st_ref[d, h]
                o_h = _dot(a, v[:, ls]) + _dot_nt(q_abs[:, ls], s_t.astype(BF16))
                st_ref[d, h] = s_t * decay[:, ls] + _dot_tn(v[:, ls], k_end[:, ls])
                outs.append(o_h)
            o_ref[rows, :] += jnp.concatenate(outs, axis=-1)
            deep.append((d, window, v, rows, jnp.min(b)))

        deepest = functools.reduce(jnp.minimum, [entry[4] for entry in deep])
        n_windows = jnp.floor(-deepest * (1.0 / WINDOW)).astype(jnp.int32) + 1

        def extra_window(p, carry2):
            for d, window, v, rows, _ in deep:
                q_w, k_w = window(WINDOW * p.astype(F32))
                extra = []
                for h in range(HEADS):
                    ls = slice(h * HEAD_DIM, (h + 1) * HEAD_DIM)
                    a = jnp.where(keep[d], _dot_nt(q_w[:, ls], k_w[:, ls]), 0.0).astype(BF16)
                    extra.append(_dot(a, v[:, ls]))
                o_ref[rows, :] += jnp.concatenate(extra, axis=-1)
            return carry2

        lax.fori_loop(STATIC_WINDOWS, n_windows, extra_window, 0)
        return carry

    lax.fori_loop(0, n_chunks // CHUNK_UNROLL, chunk_step, 0)

    if sfin_ref is not None:
        for d in range(2):
            for h in range(HEADS):
                sfin_ref[0, d, h] = st_ref[d, h].T

    o = o_ref[...]
    parts = []
    for h in range(HEADS):
        oh = o[:, h * HEAD_DIM:(h + 1) * HEAD_DIM]
        parts.append(oh * lax.rsqrt(jnp.mean(oh * oh, axis=-1, keepdims=True) + RMS_EPS))
    a_out = jnp.concatenate(parts, axis=-1) * ng_ref[...] * _silu(uh_ref[:, 4 * W:5 * W])
    mix_ref[:, 0:W] = a_out.astype(BF16)

    zf = uf_ref[...].astype(BF16)
    zc = _dot(zf, bdc_ref[...]).astype(BF16)
    zs = _dot(zf, bds_ref[...]).astype(BF16)
    four = _dot(dft_ref[...], jnp.concatenate([zc, zs], axis=0)) * (1.0 / math.sqrt(T * FOURIER_GROUP_DIM))
    mix_ref[:, W:W + FOURIER_WIDTH] = four.astype(BF16)

    cb = uc_ref[:, 0:CONV_WIDTH]
    zz = uc_ref[:, CONV_WIDTH:2 * CONV_WIDTH] * uc_ref[:, 2 * CONV_WIDTH:3 * CONV_WIDTH]
    t_idx = lax.broadcasted_iota(jnp.int32, (T, CONV_WIDTH), 0)
    z_prev = jnp.where(t_idx == 0, 0.0, pltpu.roll(zz, 1, axis=0))
    z_next = jnp.where(t_idx == T - 1, 0.0, pltpu.roll(zz, T - 1, axis=0))
    y = cw_ref[0:1, :] * z_prev + cw_ref[1:2, :] * zz + cw_ref[2:3, :] * z_next
    mix_ref[:, W + FOURIER_WIDTH:] = (cb * y).astype(BF16)


def _mixer(uh, uf, uc, lb, norm_g, conv_w, dft, bdc, bds, T, n_seq, blk0, state0=None, layer=0):
    has_state = state0 is not None
    const2 = lambda s: (0, 0)
    in_specs = [
        pl.BlockSpec((T, HGRN_COLS), lambda s: (blk0 + s, 0)),
        pl.BlockSpec((T, FOURIER_WIDTH), lambda s: (blk0 + s, 0)),
        pl.BlockSpec((T, CONV_COLS), lambda s: (blk0 + s, 0)),
        pl.BlockSpec((2, HGRN_WIDTH), const2),
        pl.BlockSpec((1, HGRN_WIDTH), const2),
        pl.BlockSpec((3, CONV_WIDTH), const2),
        pl.BlockSpec((T, 2 * T), const2),
        pl.BlockSpec((FOURIER_WIDTH, FOURIER_WIDTH), const2),
        pl.BlockSpec((FOURIER_WIDTH, FOURIER_WIDTH), const2),
    ]
    args = [uh, uf, uc, lb, norm_g, conv_w, dft, bdc, bds]
    mix_shape = jax.ShapeDtypeStruct((n_seq * T, D_MODEL), BF16)
    mix_spec = pl.BlockSpec((T, D_MODEL), lambda s: (s, 0))
    st_block = (1, 2, HEADS, HEAD_DIM, HEAD_DIM)
    if has_state:
        in_specs.append(pl.BlockSpec((1, None) + st_block[1:], lambda s: (s, layer, 0, 0, 0, 0)))
        args.append(state0)
        out_shape = [mix_shape]
        out_specs = [mix_spec]
    else:
        out_shape = [mix_shape, jax.ShapeDtypeStruct((n_seq, 2, HEADS, HEAD_DIM, HEAD_DIM), F32)]
        out_specs = [mix_spec, pl.BlockSpec(st_block, lambda s: (s, 0, 0, 0, 0))]
    return pl.pallas_call(
        functools.partial(_mixer_kernel, T=T, has_state=has_state),
        out_shape=out_shape,
        grid=(n_seq,),
        in_specs=in_specs,
        out_specs=out_specs,
        scratch_shapes=[pltpu.VMEM((2, HEADS, HEAD_DIM, HEAD_DIM), F32), pltpu.VMEM((T, HGRN_WIDTH), F32)],
        compiler_params=pltpu.CompilerParams(dimension_semantics=("arbitrary",), vmem_limit_bytes=VMEM_LIMIT),
        name="mixer_latent" if has_state else "mixer_context",
    )(*args)


def _merge_kernel(x_ref, mixc_ref, mixl_ref, ug_ref, mod_ref, bg_ref, wp_ref, wo_ref, g1_ref, b1_ref, wrh_ref, wrl_ref,
                  br_ref, x1_ref, h2_ref, apos_ref, gk_ref, n16_ref):
    W = HGRN_WIDTH
    gates = jax.nn.sigmoid(ug_ref[...] + bg_ref[...])
    mix = jnp.where(pl.program_id(0) < N_CTX // TOK_TILE, mixc_ref[...], mixl_ref[...])
    edges = (0, W, W + FOURIER_WIDTH, D_MODEL)
    merged = None
    for j in range(N_BRANCHES):
        p = _dot(mix[:, edges[j]:edges[j + 1]], wp_ref[edges[j]:edges[j + 1], :])
        term = gates[:, j * D_MODEL:(j + 1) * D_MODEL] * p
        merged = term if merged is None else merged + term
    y = _dot(merged.astype(BF16), wo_ref[...])
    x1 = _ln(ALPHA * x_ref[...] + mod_ref[2:3, :] * y) * g1_ref[...] + b1_ref[...]
    x1_ref[...] = x1
    h2 = _ln(x1) * (1.0 + mod_ref[4:5, :]) + mod_ref[3:4, :]
    h2_hi = h2.astype(BF16)
    h2_ref[...] = h2_hi
    h2_lo = (h2 - h2_hi.astype(F32)).astype(BF16)
    logits = _dot(h2_hi, wrh_ref[...]) + (_dot(h2_hi, wrl_ref[...]) + _dot(h2_lo, wrh_ref[...]))
    scores = jax.nn.sigmoid(logits)
    sel = scores + br_ref[...]
    lane_f = lax.broadcasted_iota(jnp.int32, sel.shape, 1).astype(F32)
    hits = []
    for _ in range(TOP_K):
        top = jnp.max(sel, axis=-1, keepdims=True)
        first = jnp.min(jnp.where(sel == top, lane_f, float(N_EXPERTS)), axis=-1, keepdims=True)
        hit = lane_f == first
        hits.append(hit)
        sel = jnp.where(hit, -jnp.inf, sel)
    chosen = functools.reduce(jnp.logical_or, hits)
    picked = jnp.where(chosen, scores, 0.0)
    denom = jnp.sum(picked, axis=-1, keepdims=True) + 1e-20
    gate = ROUTED_SCALE * picked / denom

    chosen_f = chosen.astype(F32)
    r_i = lax.broadcasted_iota(jnp.int32, (TOK_TILE, TOK_TILE), 0)
    c_i = lax.broadcasted_iota(jnp.int32, (TOK_TILE, TOK_TILE), 1)
    earlier = (c_i < r_i).astype(F32).astype(BF16)
    rank = _dot(earlier, chosen_f.astype(BF16))
    count = jnp.sum(chosen_f, axis=0, keepdims=True)
    n16 = jnp.floor((count + (ROW_CHUNK - 1)) * (1.0 / ROW_CHUNK))
    e_r = lax.broadcasted_iota(jnp.int32, (N_EXPERTS, N_EXPERTS), 0)
    e_c = lax.broadcasted_iota(jnp.int32, (N_EXPERTS, N_EXPERTS), 1)
    before = (e_r < e_c).astype(F32).astype(BF16)
    seg_start = ROW_CHUNK * _dot(jnp.broadcast_to(n16, (8, N_EXPERTS)).astype(BF16), before)[0:1, :]
    row_of = seg_start + rank
    lane_k = lax.broadcasted_iota(jnp.int32, (TOK_TILE, TOP_K), 1)
    apos = jnp.zeros((TOK_TILE, TOP_K), F32)
    gk = jnp.zeros((TOK_TILE, TOP_K), F32)
    for k, hit in enumerate(hits):
        apos = jnp.where(lane_k == k, jnp.sum(jnp.where(hit, row_of, 0.0), axis=-1, keepdims=True), apos)
        gk = jnp.where(lane_k == k, jnp.sum(jnp.where(hit, gate, 0.0), axis=-1, keepdims=True), gk)
    apos_ref[...] = apos
    gk_ref[...] = gk
    n16_ref[...] = n16


def _merge(x, mix_ctx, mix_lat, ug, mod, b_gate, wp, wo, ln_g, ln_b, wr_hi, wr_lo, b_router, layer):
    tok = lambda n: pl.BlockSpec((TOK_TILE, n), lambda i: (i, 0))
    const2 = lambda i: (0, 0)
    ctx_tiles = N_CTX // TOK_TILE
    return pl.pallas_call(
        _merge_kernel,
        out_shape=[jax.ShapeDtypeStruct((N_TOK, D_MODEL), F32), jax.ShapeDtypeStruct((N_TOK, D_MODEL), BF16),
                   jax.ShapeDtypeStruct((N_TOK, TOP_K), F32), jax.ShapeDtypeStruct((N_TOK, TOP_K), F32),
                   jax.ShapeDtypeStruct((N_TILES, 1, N_EXPERTS), F32)],
        grid=(N_TILES,),
        in_specs=[
            tok(D_MODEL),
            pl.BlockSpec((TOK_TILE, D_MODEL), lambda i: (jnp.minimum(i, ctx_tiles - 1), 0)),
            pl.BlockSpec((TOK_TILE, D_MODEL), lambda i: (jnp.maximum(i - ctx_tiles, 0), 0)),
            tok(GATE_COLS),
            pl.BlockSpec((None, None, 6, D_MODEL), lambda i: (layer, _cond_row_of_tile(i, TOK_TILE), 0, 0)),
            pl.BlockSpec((1, GATE_COLS), const2),
            pl.BlockSpec((D_MODEL, D_MODEL), const2),
            pl.BlockSpec((D_MODEL, D_MODEL), const2),
            pl.BlockSpec((1, D_MODEL), const2),
            pl.BlockSpec((1, D_MODEL), const2),
            pl.BlockSpec((D_MODEL, N_EXPERTS), const2),
            pl.BlockSpec((D_MODEL, N_EXPERTS), const2),
            pl.BlockSpec((1, N_EXPERTS), const2),
        ],
        out_specs=[tok(D_MODEL), tok(D_MODEL), tok(TOP_K), tok(TOP_K),
                   pl.BlockSpec((None, 1, N_EXPERTS), lambda i: (i, 0, 0))],
        compiler_params=pltpu.CompilerParams(dimension_semantics=("arbitrary",), vmem_limit_bytes=VMEM_LIMIT),
        name="merge_router",
    )(x, mix_ctx, mix_lat, ug, mod, b_gate, wp, wo, ln_g, ln_b, wr_hi, wr_lo, b_router)


def _routing_tables(n16):
    n16 = n16.astype(jnp.int32)
    hi = jnp.cumsum(n16, axis=1)
    lo = hi - n16
    n_chunks = hi[:, -1]
    eo = jnp.cumsum(n16, axis=0) - n16
    tc = jnp.sum(n16, axis=0)
    tiles = (tc + CHUNKS_PER_GMM_TILE - 1) // CHUNKS_PER_GMM_TILE
    tile_end = jnp.cumsum(tiles)
    tile_off = tile_end - tiles
    n_used = tile_end[-1]
    c = jnp.arange(TILE_CHUNKS, dtype=jnp.int32)[None, :, None]
    owner = jnp.logical_and(c >= lo[:, None, :], c < hi[:, None, :])
    shift = CHUNKS_PER_GMM_TILE * tile_off[None, :] + eo - lo
    dst = c[:, :, 0] + jnp.sum(jnp.where(owner, shift[:, None, :], 0), axis=-1)
    pad_start = CHUNKS_PER_GMM_TILE * tile_off + tc
    pad_count = CHUNKS_PER_GMM_TILE * tiles - tc
    return (dst.reshape(-1).astype(jnp.int32), n_chunks, tile_off.astype(jnp.int32), tiles.astype(jnp.int32),
            jnp.reshape(n_used, (1,)).astype(jnp.int32), pad_start.astype(jnp.int32), pad_count.astype(jnp.int32))


def _one_hot_rows(apos_ref, weight_ref=None):
    r = lax.broadcasted_iota(jnp.int32, (TOK_TILE, TILE_ROWS), 1).astype(F32)
    pt = jnp.zeros((TOK_TILE, TILE_ROWS), F32)
    for k in range(TOP_K):
        val = 1.0 if weight_ref is None else weight_ref[:, k:k + 1]
        pt = jnp.where(r == apos_ref[:, k:k + 1], val, pt)
    return pt.astype(BF16)


def _chunk_copy(src_ref, src_chunk, dst_ref, dst_chunk, sem):
    rows = lambda c: pl.ds(pl.multiple_of(c * ROW_CHUNK, ROW_CHUNK), ROW_CHUNK)
    return pltpu.make_async_copy(src_ref.at[rows(src_chunk), :], dst_ref.at[rows(dst_chunk), :], sem)


def _dispatch_kernel(dst_ref, nch_ref, pstart_ref, pcount_ref, nu_ref, h_ref, apos_ref, xs_hbm, stage_ref, zero_ref,
                     sem):
    j = pl.program_id(0)
    buf = j % 2

    def wait_chunks(b, n):
        def body(c, carry):
            _chunk_copy(stage_ref.at[b], 0, xs_hbm, 0, sem.at[b]).wait()
            return carry
        lax.fori_loop(0, n, body, 0)

    @pl.when(j < N_TILES)
    def _():
        @pl.when(j >= 2)
        def _():
            wait_chunks(buf, nch_ref[j - 2])

        pt = _one_hot_rows(apos_ref)
        h = h_ref[...]
        for r0 in range(0, TILE_ROWS, DISPATCH_ROWS):
            stage_ref[buf, r0:r0 + DISPATCH_ROWS, :] = _dot_tn(pt[:, r0:r0 + DISPATCH_ROWS], h).astype(BF16)

        def send(c, carry):
            _chunk_copy(stage_ref.at[buf], c, xs_hbm, dst_ref[j * TILE_CHUNKS + c], sem.at[buf]).start()
            return carry
        lax.fori_loop(0, nch_ref[j], send, 0)

    @pl.when(j == N_TILES)
    def _():
        zero_ref[...] = jnp.zeros_like(zero_ref)

        def tile_copy(t):
            rows = pl.ds(pl.multiple_of(t * GMM_TILE, GMM_TILE), GMM_TILE)
            return pltpu.make_async_copy(zero_ref, xs_hbm.at[rows, :], sem.at[3])

        def per_expert(e, total):
            def fill(i, carry):
                _chunk_copy(zero_ref, 0, xs_hbm, pstart_ref[e] + i, sem.at[2]).start()
                return carry
            lax.fori_loop(0, pcount_ref[e], fill, 0)
            return total + pcount_ref[e]
        n_pad = lax.fori_loop(0, N_EXPERTS, per_expert, 0)

        def fill_tile(t, carry):
            tile_copy(t).start()
            return carry
        lax.fori_loop(nu_ref[0], MAX_GMM_TILES, fill_tile, 0)
        wait_chunks(0, nch_ref[N_TILES - 2])
        wait_chunks(1, nch_ref[N_TILES - 1])

        def drain(i, carry):
            _chunk_copy(zero_ref, 0, xs_hbm, 0, sem.at[2]).wait()
            return carry
        lax.fori_loop(0, n_pad, drain, 0)

        def drain_tile(t, carry):
            tile_copy(0).wait()
            return carry
        lax.fori_loop(nu_ref[0], MAX_GMM_TILES, drain_tile, 0)


def _dispatch(h2, apos, dst, n_chunks, pad_start, pad_count, n_used):
    last = N_TILES - 1
    grid_spec = pltpu.PrefetchScalarGridSpec(
        num_scalar_prefetch=5,
        grid=(N_TILES + 1,),
        in_specs=[
            pl.BlockSpec((TOK_TILE, D_MODEL), lambda j, *_: (jnp.minimum(j, last), 0)),
            pl.BlockSpec((TOK_TILE, TOP_K), lambda j, *_: (jnp.minimum(j, last), 0)),
        ],
        out_specs=pl.BlockSpec(memory_space=pl.ANY),
        scratch_shapes=[pltpu.VMEM((2, TILE_ROWS, D_MODEL), BF16), pltpu.VMEM((GMM_TILE, D_MODEL), BF16),
                        pltpu.SemaphoreType.DMA((4,))],
    )
    return pl.pallas_call(
        _dispatch_kernel,
        out_shape=jax.ShapeDtypeStruct((SORTED_ROWS, D_MODEL), BF16),
        grid_spec=grid_spec,
        compiler_params=pltpu.CompilerParams(dimension_semantics=("arbitrary",), vmem_limit_bytes=VMEM_LIMIT),
        name="moe_dispatch",
    )(dst, n_chunks, pad_start, pad_count, n_used, h2, apos)


def _gmm_kernel(off_ref, cnt_ref, nu_ref, x_hbm, wg_ref, wu_ref, wd_ref, y_hbm, xbuf, ybuf, wg_b, wu_b, wd_b,
                in_sem, out_sem):
    e = pl.program_id(0)
    n_used = nu_ref[0]

    def rows(g):
        return pl.ds(pl.multiple_of(g * GMM_TILE, GMM_TILE), GMM_TILE)

    def fetch(g):
        slot = g % GMM_IN_SLOTS
        return pltpu.make_async_copy(x_hbm.at[rows(g), :], xbuf.at[slot], in_sem.at[slot])

    def flush(g):
        slot = g % GMM_OUT_SLOTS
        return pltpu.make_async_copy(ybuf.at[slot], y_hbm.at[rows(g), :], out_sem.at[slot])

    @pl.when(e == 0)
    def _():
        for g in range(GMM_AHEAD):
            @pl.when(g < n_used)
            def _(g=g):
                fetch(g).start()

    wg_b[...] = wg_ref[...].astype(BF16)
    wu_b[...] = wu_ref[...].astype(BF16)
    wd_b[...] = wd_ref[...].astype(BF16)

    def process(g0, width):
        for w in range(width):
            @pl.when(g0 + w + GMM_AHEAD < n_used)
            def _(w=w):
                fetch(g0 + w + GMM_AHEAD).start()
        for w in range(width):
            fetch(g0 + w).wait()
        x = jnp.concatenate([xbuf[(g0 + w) % GMM_IN_SLOTS] for w in range(width)], axis=0)
        hid = _silu(_dot(x, wg_b[...])) * _dot(x, wu_b[...])
        y = _dot(hid.astype(BF16), wd_b[...]).astype(BF16)
        for w in range(width):
            g = g0 + w

            @pl.when(g >= GMM_OUT_SLOTS)
            def _(g=g):
                flush(g - GMM_OUT_SLOTS).wait()

            ybuf[g % GMM_OUT_SLOTS] = y[w * GMM_TILE:(w + 1) * GMM_TILE]
            flush(g).start()

    first = off_ref[e]
    count = cnt_ref[e]

    done = 0
    width = GMM_WIDTH
    while width >= 1:
        trips = (count - done) // width

        def block(p, carry, done=done, width=width):
            process(first + done + width * p, width)
            return carry

        lax.fori_loop(0, trips, block, 0)
        done = done + trips * width
        width //= 2

    @pl.when(e == N_EXPERTS - 1)
    def _():
        for k in range(GMM_OUT_SLOTS):
            @pl.when(n_used - 1 - k >= 0)
            def _(k=k):
                flush(n_used - 1 - k).wait()


def _gmm(xs, weg, weu, wed, tile_off, tiles, n_used, layer):
    w_in_spec = pl.BlockSpec((None, None, D_MODEL, D_EXPERT), lambda e, *_: (layer, e, 0, 0))
    grid_spec = pltpu.PrefetchScalarGridSpec(
        num_scalar_prefetch=3,
        grid=(N_EXPERTS,),
        in_specs=[
            pl.BlockSpec(memory_space=pl.ANY),
            w_in_spec, w_in_spec,
            pl.BlockSpec((None, None, D_EXPERT, D_MODEL), lambda e, *_: (layer, e, 0, 0)),
        ],
        out_specs=pl.BlockSpec(memory_space=pl.ANY),
        scratch_shapes=[pltpu.VMEM((GMM_IN_SLOTS, GMM_TILE, D_MODEL), BF16),
                        pltpu.VMEM((GMM_OUT_SLOTS, GMM_TILE, D_MODEL), BF16),
                        pltpu.VMEM((D_MODEL, D_EXPERT), BF16), pltpu.VMEM((D_MODEL, D_EXPERT), BF16),
                        pltpu.VMEM((D_EXPERT, D_MODEL), BF16),
                        pltpu.SemaphoreType.DMA((GMM_IN_SLOTS,)), pltpu.SemaphoreType.DMA((GMM_OUT_SLOTS,))],
    )
    return pl.pallas_call(
        _gmm_kernel,
        out_shape=jax.ShapeDtypeStruct((SORTED_ROWS, D_MODEL), BF16),
        grid_spec=grid_spec,
        input_output_aliases={3: 0},
        compiler_params=pltpu.CompilerParams(dimension_semantics=("arbitrary",), vmem_limit_bytes=VMEM_LIMIT),
        name="moe_gmm",
    )(tile_off, tiles, n_used, xs, weg, weu, wed)


def _combine_kernel(dst_ref, nch_ref, ys_hbm, apos_ref, gk_ref, h_ref, x1_ref, mod_ref, wsg_ref, wsu_ref, wsd_ref,
                    g2_ref, b2_ref, o_ref, stage_ref, sem):
    j = pl.program_id(0)
    buf = j % 2

    def fetch(jj, b):
        def body(c, carry):
            _chunk_copy(ys_hbm, dst_ref[jj * TILE_CHUNKS + c], stage_ref.at[b], c, sem.at[b]).start()
            return carry
        lax.fori_loop(0, nch_ref[jj], body, 0)

    @pl.when(j == 0)
    def _():
        stage_ref[...] = jnp.zeros_like(stage_ref)
        fetch(0, 0)

    @pl.when(j + 1 < N_TILES)
    def _():
        fetch(j + 1, 1 - buf)

    h = h_ref[...]
    hid = _silu(_dot(h, wsg_ref[...])) * _dot(h, wsu_ref[...])
    shared = _dot(hid.astype(BF16), wsd_ref[...])
    ptw = _one_hot_rows(apos_ref, gk_ref)

    def wait(c, carry):
        _chunk_copy(ys_hbm, 0, stage_ref.at[buf], 0, sem.at[buf]).wait()
        return carry
    lax.fori_loop(0, nch_ref[j], wait, 0)

    routed = _dot(ptw, stage_ref[buf])
    r = ALPHA * x1_ref[...] + mod_ref[5:6, :] * (routed + shared)
    o_ref[...] = _ln(r) * g2_ref[...] + b2_ref[...]


def _combine(ys, apos, gk, h2, x1, mod, wsg, wsu, wsd, ln_g, ln_b, dst, n_chunks, layer):
    tok = lambda n: pl.BlockSpec((TOK_TILE, n), lambda j, *_: (j, 0))
    const2 = lambda j, *_: (0, 0)
    grid_spec = pltpu.PrefetchScalarGridSpec(
        num_scalar_prefetch=2,
        grid=(N_TILES,),
        in_specs=[
            pl.BlockSpec(memory_space=pl.ANY),
            tok(TOP_K), tok(TOP_K), tok(D_MODEL), tok(D_MODEL),
            pl.BlockSpec((None, None, 6, D_MODEL), lambda j, *_: (layer, _cond_row_of_tile(j, TOK_TILE), 0, 0)),
            pl.BlockSpec((D_MODEL, D_SHARED), const2),
            pl.BlockSpec((D_MODEL, D_SHARED), const2),
            pl.BlockSpec((D_SHARED, D_MODEL), const2),
            pl.BlockSpec((1, D_MODEL), const2),
            pl.BlockSpec((1, D_MODEL), const2),
        ],
        out_specs=tok(D_MODEL),
        scratch_shapes=[pltpu.VMEM((2, TILE_ROWS, D_MODEL), BF16), pltpu.SemaphoreType.DMA((2,))],
    )
    return pl.pallas_call(
        _combine_kernel,
        out_shape=jax.ShapeDtypeStruct((N_TOK, D_MODEL), F32),
        grid_spec=grid_spec,
        compiler_params=pltpu.CompilerParams(dimension_semantics=("arbitrary",), vmem_limit_bytes=VMEM_LIMIT),
        name="moe_combine",
    )(dst, n_chunks, ys, apos, gk, h2, x1, mod, wsg, wsu, wsd, ln_g, ln_b)


def kernel(x_prompt, x_sample, c, state_hgrn, c_ctx, w_ada, b_ada, w_in, b_gate, hgrn_lb, hgrn_norm, conv_w,
           w_proj_hgrn, w_proj_fourier, w_proj_conv, w_out, ln1_g, ln1_b, ln2_g, ln2_b, w_router, b_router,
           w_exp_gate, w_exp_up, w_exp_down, w_sh_gate, w_sh_up, w_sh_down):
    lb_sm = jax.nn.softmax(hgrn_lb.astype(F32), axis=0)
    lb_all = jnp.cumsum(lb_sm, axis=0) - lb_sm[:1]

    pos = jnp.asarray(_pos_emb_table())
    dft_ctx = jnp.asarray(_dft_time(SEQ)).astype(BF16)
    dft_lat = jnp.asarray(_dft_time(DEC_SEQ)).astype(BF16)
    bdc, bds = (jnp.asarray(m).astype(BF16) for m in _dft_channel())

    cond = jnp.concatenate([c_ctx[None, :], c, jnp.zeros((N_COND - 1 - DEC_BATCH, D_MODEL), F32)], axis=0)
    mod = _ada_mod(cond, w_ada, b_ada).reshape(DEPTH, N_COND, 6, D_MODEL)

    w_in_b = w_in.astype(BF16)
    wp_b = jnp.concatenate([w_proj_hgrn, w_proj_fourier, w_proj_conv], axis=1).astype(BF16)
    wo_b = w_out.astype(BF16)
    wr_hi = w_router.astype(BF16)
    wr_lo = (w_router - wr_hi.astype(F32)).astype(BF16)
    wsg_b, wsu_b, wsd_b = w_sh_gate.astype(BF16), w_sh_up.astype(BF16), w_sh_down.astype(BF16)

    x = _assemble(x_prompt, x_sample, pos)
    new_states = []
    for l in range(DEPTH):
        uh, uf, uc, ug = _in_proj(x, mod, w_in_b, l)
        small = (lb_all[l], hgrn_norm[l][None, :], conv_w[l])
        mix_ctx, s_ctx = _mixer(uh, uf, uc, *small, dft_ctx, bdc, bds, T=SEQ, n_seq=BATCH, blk0=0)
        (mix_lat,) = _mixer(uh, uf, uc, *small, dft_lat, bdc, bds, T=DEC_SEQ, n_seq=DEC_BATCH,
                            blk0=N_CTX // DEC_SEQ, state0=state_hgrn, layer=l)
        new_states.append(s_ctx)
        x1, h2, apos, gk, n16 = _merge(x, mix_ctx, mix_lat, ug, mod, b_gate[l][None, :], wp_b[l], wo_b[l],
                                       ln1_g[l][None, :], ln1_b[l][None, :], wr_hi[l], wr_lo[l],
                                       b_router[l][None, :], l)
        dst, n_chunks, tile_off, tiles, n_used, pad_start, pad_count = _routing_tables(
            n16.reshape(N_TILES, N_EXPERTS))
        xs = _dispatch(h2, apos, dst, n_chunks, pad_start, pad_count, n_used)
        ys = _gmm(xs, w_exp_gate, w_exp_up, w_exp_down, tile_off, tiles, n_used, l)
        x = _combine(ys, apos, gk, h2, x1, mod, wsg_b[l], wsu_b[l], wsd_b[l], ln2_g[l][None, :], ln2_b[l][None, :],
                     dst, n_chunks, l)
    y_prompt = x[:N_CTX].reshape(BATCH, SEQ, D_MODEL)
    y_sample = x[N_CTX:].reshape(DEC_BATCH, DEC_SEQ, D_MODEL)
    return (y_prompt, y_sample, jnp.stack(new_states, axis=1))
```

```python
import functools
import math

import numpy as np
import jax
import jax.numpy as jnp
from jax import lax
from jax.experimental import pallas as pl
from jax.experimental.pallas import tpu as pltpu

F32 = jnp.float32
BF16 = jnp.bfloat16

D_MODEL = 1024
BATCH = 16
SEQ = 256
DEPTH = 2
DEC_BATCH = 4
DEC_SEQ = 1024
GRID_W = 64
HEADS = 4
HEAD_DIM = 128
HGRN_WIDTH = HEADS * HEAD_DIM
FOURIER_GROUPS = 4
FOURIER_GROUP_DIM = 64
FOURIER_WIDTH = FOURIER_GROUPS * FOURIER_GROUP_DIM
CONV_WIDTH = 256
N_BRANCHES = 3
HGRN_COLS = 5 * HGRN_WIDTH
CONV_COLS = 3 * CONV_WIDTH
GATE_COLS = N_BRANCHES * D_MODEL
IN_COLS = HGRN_COLS + FOURIER_WIDTH + CONV_COLS + GATE_COLS
N_EXPERTS = 64
TOP_K = 8
D_EXPERT = 256
D_SHARED = 256
ROUTED_SCALE = 2.5
ALPHA = (2 * DEPTH) ** 0.25
LN_EPS = 1e-6
RMS_EPS = 1e-6
F_MIN = 1e-30

N_CTX = BATCH * SEQ
N_LAT = DEC_BATCH * DEC_SEQ
N_TOK = N_CTX + N_LAT
N_COND = 8

TOK_TILE = 256
CHUNK = 64
CHUNK_UNROLL = 2
WINDOW = 120.0
STATIC_WINDOWS = 2
ADA_TILE = 1536
VMEM_LIMIT = 56 * 1024 * 1024

N_TILES = N_TOK // TOK_TILE
ROW_CHUNK = 16
TILE_CHUNKS = (TOK_TILE * TOP_K + N_EXPERTS * (ROW_CHUNK - 1)) // ROW_CHUNK
DISPATCH_ROWS = 512
TILE_ROWS = -(-TILE_CHUNKS * ROW_CHUNK // DISPATCH_ROWS) * DISPATCH_ROWS
GMM_TILE = 256
GMM_WIDTH = 4
GMM_AHEAD = 6
GMM_IN_SLOTS = GMM_WIDTH + GMM_AHEAD
GMM_OUT_SLOTS = 2 * GMM_WIDTH
CHUNKS_PER_GMM_TILE = GMM_TILE // ROW_CHUNK
MAX_GMM_TILES = (N_TILES * TILE_CHUNKS + N_EXPERTS * (CHUNKS_PER_GMM_TILE - 1)) // CHUNKS_PER_GMM_TILE
SORTED_ROWS = MAX_GMM_TILES * GMM_TILE


def _cond_row_of_tile(i, tile):
    ctx_tiles = N_CTX // tile
    per_seq = DEC_SEQ // tile
    return jnp.where(i < ctx_tiles, 0, 1 + (i - ctx_tiles) // per_seq)


def _silu(x):
    return x * jax.nn.sigmoid(x)


def _ln(x):
    mu = jnp.mean(x, axis=-1, keepdims=True)
    xc = x - mu
    var = jnp.mean(xc * xc, axis=-1, keepdims=True)
    return xc * lax.rsqrt(var + LN_EPS)


def _dot(a, b):
    return jnp.dot(a, b, preferred_element_type=F32)


def _dot_nt(a, b):
    return lax.dot_general(a, b, (((1,), (1,)), ((), ())), preferred_element_type=F32)


def _dot_tn(a, b):
    return lax.dot_general(a, b, (((0,), (0,)), ((), ())), preferred_element_type=F32)


@functools.lru_cache(maxsize=None)
def _pos_emb_table():
    rows = DEC_SEQ // GRID_W
    t = np.arange(rows * GRID_W)
    r = (t // GRID_W).astype(np.float32)
    col = (t % GRID_W).astype(np.float32)
    quarter = D_MODEL // 4
    omega = (1.0 / (np.float32(10000.0) ** (np.arange(quarter, dtype=np.float32) / np.float32(quarter)))).astype(np.float32)
    ar = (r[:, None] * omega).astype(np.float32)
    ac = (col[:, None] * omega).astype(np.float32)
    return np.concatenate([np.sin(ar), np.cos(ar), np.sin(ac), np.cos(ac)], axis=-1).astype(np.float32)


@functools.lru_cache(maxsize=None)
def _dft_time(T):
    k = np.arange(T)
    ph = 2.0 * np.pi * ((k[:, None] * k[None, :]) % T) / T
    return np.concatenate([np.cos(ph), -np.sin(ph)], axis=1).astype(np.float32)


@functools.lru_cache(maxsize=None)
def _dft_channel():
    n = FOURIER_GROUP_DIM
    k = np.arange(n)
    ph = 2.0 * np.pi * ((k[:, None] * k[None, :]) % n) / n
    eye = np.eye(FOURIER_GROUPS)
    return np.kron(eye, np.cos(ph)).astype(np.float32), np.kron(eye, np.sin(ph)).astype(np.float32)


def _assemble_kernel(xp_ref, xs_ref, pos_ref, o_ref):
    i = pl.program_id(0)

    @pl.when(i < N_CTX // TOK_TILE)
    def _():
        o_ref[...] = xp_ref[...]

    @pl.when(i >= N_CTX // TOK_TILE)
    def _():
        o_ref[...] = xs_ref[...] + pos_ref[...]


def _assemble(x_prompt, x_sample, pos):
    ctx_tiles = N_CTX // TOK_TILE
    per_seq = DEC_SEQ // TOK_TILE
    return pl.pallas_call(
        _assemble_kernel,
        out_shape=jax.ShapeDtypeStruct((N_TOK, D_MODEL), F32),
        grid=(N_TOK // TOK_TILE,),
        in_specs=[
            pl.BlockSpec((TOK_TILE, D_MODEL), lambda i: (jnp.minimum(i, ctx_tiles - 1), 0)),
            pl.BlockSpec((TOK_TILE, D_MODEL), lambda i: (jnp.maximum(i - ctx_tiles, 0), 0)),
            pl.BlockSpec((TOK_TILE, D_MODEL), lambda i: (jnp.maximum(i - ctx_tiles, 0) % per_seq, 0)),
        ],
        out_specs=pl.BlockSpec((TOK_TILE, D_MODEL), lambda i: (i, 0)),
        compiler_params=pltpu.CompilerParams(dimension_semantics=("arbitrary",)),
        name="assemble_tokens",
    )(x_prompt.reshape(N_CTX, D_MODEL), x_sample.reshape(N_LAT, D_MODEL), pos)


def _ada_kernel(c_ref, w_ref, b_ref, o_ref):
    s = _silu(c_ref[...]).astype(BF16)
    o_ref[...] = _dot(s, w_ref[...].astype(BF16)) + b_ref[...]


def _ada_mod(cond, w_ada, b_ada):
    n_col = 6 * D_MODEL
    return pl.pallas_call(
        _ada_kernel,
        out_shape=jax.ShapeDtypeStruct((DEPTH, N_COND, n_col), F32),
        grid=(DEPTH, n_col // ADA_TILE),
        in_specs=[
            pl.BlockSpec((N_COND, D_MODEL), lambda l, j: (0, 0)),
            pl.BlockSpec((None, D_MODEL, ADA_TILE), lambda l, j: (l, 0, j)),
            pl.BlockSpec((None, 1, ADA_TILE), lambda l, j: (l, 0, j)),
        ],
        out_specs=pl.BlockSpec((None, N_COND, ADA_TILE), lambda l, j: (l, 0, j)),
        compiler_params=pltpu.CompilerParams(dimension_semantics=("arbitrary", "arbitrary"),
                                             vmem_limit_bytes=VMEM_LIMIT),
        name="ada_mod",
    )(cond, w_ada, b_ada.reshape(DEPTH, 1, n_col))


def _in_proj_kernel(x_ref, mod_ref, w_ref, uh_ref, uf_ref, uc_ref, ug_ref):
    sh1 = mod_ref[0:1, :]
    sc1 = mod_ref[1:2, :]
    h = (_ln(x_ref[...]) * (1.0 + sc1) + sh1).astype(BF16)
    c0 = 0
    for ref, n in ((uh_ref, HGRN_COLS), (uf_ref, FOURIER_WIDTH), (uc_ref, CONV_COLS), (ug_ref, GATE_COLS)):
        ref[...] = _dot(h, w_ref[:, c0:c0 + n])
        c0 += n


def _in_proj(x, mod, w_in_bf16, layer):
    n_tiles = N_TOK // TOK_TILE
    widths = (HGRN_COLS, FOURIER_WIDTH, CONV_COLS, GATE_COLS)
    return pl.pallas_call(
        _in_proj_kernel,
        out_shape=[jax.ShapeDtypeStruct((N_TOK, n), F32) for n in widths],
        grid=(n_tiles,),
        in_specs=[
            pl.BlockSpec((TOK_TILE, D_MODEL), lambda i: (i, 0)),
            pl.BlockSpec((None, None, 6, D_MODEL), lambda i: (layer, _cond_row_of_tile(i, TOK_TILE), 0, 0)),
            pl.BlockSpec((None, D_MODEL, IN_COLS), lambda i: (layer, 0, 0)),
        ],
        out_specs=[pl.BlockSpec((TOK_TILE, n), lambda i: (i, 0)) for n in widths],
        compiler_params=pltpu.CompilerParams(dimension_semantics=("arbitrary",), vmem_limit_bytes=VMEM_LIMIT),
        name="in_proj",
    )(x, mod, w_in_bf16)


def _mixer_kernel(*refs, T, has_state):
    if has_state:
        (uh_ref, uf_ref, uc_ref, lb_ref, ng_ref, cw_ref, dft_ref, bdc_ref, bds_ref, s0_ref,
         mix_ref, st_ref, o_ref) = refs
        sfin_ref = None
    else:
        (uh_ref, uf_ref, uc_ref, lb_ref, ng_ref, cw_ref, dft_ref, bdc_ref, bds_ref,
         mix_ref, sfin_ref, st_ref, o_ref) = refs
        s0_ref = None
    n_chunks = T // CHUNK
    W = HGRN_WIDTH

    for d in range(2):
        for h in range(HEADS):
            if has_state:
                st_ref[d, h] = s0_ref[0, d, h].T
            else:
                st_ref[d, h] = jnp.zeros((HEAD_DIM, HEAD_DIM), F32)
    o_ref[...] = jnp.zeros_like(o_ref)

    row = lax.broadcasted_iota(jnp.int32, (CHUNK, CHUNK), 0)
    col = lax.broadcasted_iota(jnp.int32, (CHUNK, CHUNK), 1)
    keep = (col <= row, col >= row)
    tri = tuple(k.astype(F32).astype(BF16) for k in keep)

    def chunk_step(i, carry):
        deep = []
        for sub, d in ((s, d) for s in range(CHUNK_UNROLL) for d in range(2)):
            c = i * CHUNK_UNROLL + sub
            c = c if d == 0 else n_chunks - 1 - c
            rows = pl.ds(pl.multiple_of(c * CHUNK, CHUNK), CHUNK)
            q = _silu(uh_ref[rows, 0:W])
            v = uh_ref[rows, W:2 * W].astype(BF16)
            z = uh_ref[rows, (2 + d) * W:(3 + d) * W]
            lb = lb_ref[d:d + 1, :]
            e = jnp.exp(-jnp.abs(z))
            r = 1.0 / (1.0 + e)
            er = e * r
            pos = z >= 0.0
            sig_p = jnp.where(pos, r, er)
            sig_n = jnp.where(pos, er, r)
            f = lb + (1.0 - lb) * sig_p
            lf = jnp.log(jnp.maximum(f, F_MIN))
            k = (1.0 - lb) * sig_n
            lf_hi = lf.astype(BF16)
            lf_lo = (lf - lf_hi.astype(F32)).astype(BF16)
            b = _dot(tri[d], lf_hi) + _dot(tri[d], lf_lo)
            g = b[CHUNK - 1:CHUNK, :] if d == 0 else b[0:1, :]

            def window(p_level, q=q, k=k, b=b):
                dist = -p_level - b
                inside = jnp.logical_and(dist >= 0.0, dist < WINDOW)
                k_w = jnp.where(inside, k * jnp.exp(jnp.minimum(dist, WINDOW) - 0.5 * WINDOW), 0.0)
                q_w = q * jnp.exp(jnp.minimum(-dist, 0.0) + 0.5 * WINDOW)
                return q_w.astype(BF16), k_w.astype(BF16)

            wins = [window(WINDOW * p) for p in range(STATIC_WINDOWS)]
            q_abs = (q * jnp.exp(b)).astype(BF16)
            k_end = (k * jnp.exp(g - b)).astype(BF16)
            decay = jnp.exp(g)
            outs = []
            for h in range(HEADS):
                ls = slice(h * HEAD_DIM, (h + 1) * HEAD_DIM)
                q_cat = jnp.concatenate([w[0][:, ls] for w in wins], axis=-1)
                k_cat = jnp.concatenate([w[1][:, ls] for w in wins], axis=-1)
                a = jnp.where(keep[d], _dot_nt(q_cat, k_cat), 0.0).astype(BF16)
                s_t = st_ref[d, h]
                o_h = _dot(a, v[:, ls]) + _dot_nt(q_abs[:, ls], s_t.astype(BF16))
                st_ref[d, h] = s_t * decay[:, ls] + _dot_tn(v[:, ls], k_end[:, ls])
                outs.append(o_h)
            o_ref[rows, :] += jnp.concatenate(outs, axis=-1)
            deep.append((d, window, v, rows, jnp.min(b)))

        deepest = functools.reduce(jnp.minimum, [entry[4] for entry in deep])
        n_windows = jnp.floor(-deepest * (1.0 / WINDOW)).astype(jnp.int32) + 1

        def extra_window(p, carry2):
            for d, window, v, rows, _ in deep:
                q_w, k_w = window(WINDOW * p.astype(F32))
                extra = []
                for h in range(HEADS):
                    ls = slice(h * HEAD_DIM, (h + 1) * HEAD_DIM)
                    a = jnp.where(keep[d], _dot_nt(q_w[:, ls], k_w[:, ls]), 0.0).astype(BF16)
                    extra.append(_dot(a, v[:, ls]))
                o_ref[rows, :] += jnp.concatenate(extra, axis=-1)
            return carry2

        lax.fori_loop(STATIC_WINDOWS, n_windows, extra_window, 0)
        return carry

    lax.fori_loop(0, n_chunks // CHUNK_UNROLL, chunk_step, 0)

    if sfin_ref is not None:
        for d in range(2):
            for h in range(HEADS):
                sfin_ref[0, d, h] = st_ref[d, h].T

    o = o_ref[...]
    parts = []
    for h in range(HEADS):
        oh = o[:, h * HEAD_DIM:(h + 1) * HEAD_DIM]
        parts.append(oh * lax.rsqrt(jnp.mean(oh * oh, axis=-1, keepdims=True) + RMS_EPS))
    a_out = jnp.concatenate(parts, axis=-1) * ng_ref[...] * _silu(uh_ref[:, 4 * W:5 * W])
    mix_ref[:, 0:W] = a_out.astype(BF16)

    zf = uf_ref[...].astype(BF16)
    zc = _dot(zf, bdc_ref[...]).astype(BF16)
    zs = _dot(zf, bds_ref[...]).astype(BF16)
    four = _dot(dft_ref[...], jnp.concatenate([zc, zs], axis=0)) * (1.0 / math.sqrt(T * FOURIER_GROUP_DIM))
    mix_ref[:, W:W + FOURIER_WIDTH] = four.astype(BF16)

    cb = uc_ref[:, 0:CONV_WIDTH]
    zz = uc_ref[:, CONV_WIDTH:2 * CONV_WIDTH] * uc_ref[:, 2 * CONV_WIDTH:3 * CONV_WIDTH]
    t_idx = lax.broadcasted_iota(jnp.int32, (T, CONV_WIDTH), 0)
    z_prev = jnp.where(t_idx == 0, 0.0, pltpu.roll(zz, 1, axis=0))
    z_next = jnp.where(t_idx == T - 1, 0.0, pltpu.roll(zz, T - 1, axis=0))
    y = cw_ref[0:1, :] * z_prev + cw_ref[1:2, :] * zz + cw_ref[2:3, :] * z_next
    mix_ref[:, W + FOURIER_WIDTH:] = (cb * y).astype(BF16)


def _mixer(uh, uf, uc, lb, norm_g, conv_w, dft, bdc, bds, T, n_seq, blk0, state0=None, layer=0):
    has_state = state0 is not None
    const2 = lambda s: (0, 0)
    in_specs = [
        pl.BlockSpec((T, HGRN_COLS), lambda s: (blk0 + s, 0)),
        pl.BlockSpec((T, FOURIER_WIDTH), lambda s: (blk0 + s, 0)),
        pl.BlockSpec((T, CONV_COLS), lambda s: (blk0 + s, 0)),
        pl.BlockSpec((2, HGRN_WIDTH), const2),
        pl.BlockSpec((1, HGRN_WIDTH), const2),
        pl.BlockSpec((3, CONV_WIDTH), const2),
        pl.BlockSpec((T, 2 * T), const2),
        pl.BlockSpec((FOURIER_WIDTH, FOURIER_WIDTH), const2),
        pl.BlockSpec((FOURIER_WIDTH, FOURIER_WIDTH), const2),
    ]
    args = [uh, uf, uc, lb, norm_g, conv_w, dft, bdc, bds]
    mix_shape = jax.ShapeDtypeStruct((n_seq * T, D_MODEL), BF16)
    mix_spec = pl.BlockSpec((T, D_MODEL), lambda s: (s, 0))
    st_block = (1, 2, HEADS, HEAD_DIM, HEAD_DIM)
    if has_state:
        in_specs.append(pl.BlockSpec((1, None) + st_block[1:], lambda s: (s, layer, 0, 0, 0, 0)))
        args.append(state0)
        out_shape = [mix_shape]
        out_specs = [mix_spec]
    else:
        out_shape = [mix_shape, jax.ShapeDtypeStruct((n_seq, 2, HEADS, HEAD_DIM, HEAD_DIM), F32)]
        out_specs = [mix_spec, pl.BlockSpec(st_block, lambda s: (s, 0, 0, 0, 0))]
    return pl.pallas_call(
        functools.partial(_mixer_kernel, T=T, has_state=has_state),
        out_shape=out_shape,
        grid=(n_seq,),
        in_specs=in_specs,
        out_specs=out_specs,
        scratch_shapes=[pltpu.VMEM((2, HEADS, HEAD_DIM, HEAD_DIM), F32), pltpu.VMEM((T, HGRN_WIDTH), F32)],
        compiler_params=pltpu.CompilerParams(dimension_semantics=("arbitrary",), vmem_limit_bytes=VMEM_LIMIT),
        name="mixer_latent" if has_state else "mixer_context",
    )(*args)


def _merge_kernel(x_ref, mixc_ref, mixl_ref, ug_ref, mod_ref, bg_ref, wp_ref, wo_ref, g1_ref, b1_ref, wrh_ref, wrl_ref,
                  br_ref, x1_ref, h2_ref, apos_ref, gk_ref, n16_ref):
    W = HGRN_WIDTH
    gates = jax.nn.sigmoid(ug_ref[...] + bg_ref[...])
    mix = jnp.where(pl.program_id(0) < N_CTX // TOK_TILE, mixc_ref[...], mixl_ref[...])
    edges = (0, W, W + FOURIER_WIDTH, D_MODEL)
    merged = None
    for j in range(N_BRANCHES):
        p = _dot(mix[:, edges[j]:edges[j + 1]], wp_ref[edges[j]:edges[j + 1], :])
        term = gates[:, j * D_MODEL:(j + 1) * D_MODEL] * p
        merged = term if merged is None else merged + term
    y = _dot(merged.astype(BF16), wo_ref[...])
    x1 = _ln(ALPHA * x_ref[...] + mod_ref[2:3, :] * y) * g1_ref[...] + b1_ref[...]
    x1_ref[...] = x1
    h2 = _ln(x1) * (1.0 + mod_ref[4:5, :]) + mod_ref[3:4, :]
    h2_hi = h2.astype(BF16)
    h2_ref[...] = h2_hi
    h2_lo = (h2 - h2_hi.astype(F32)).astype(BF16)
    logits = _dot(h2_hi, wrh_ref[...]) + (_dot(h2_hi, wrl_ref[...]) + _dot(h2_lo, wrh_ref[...]))
    scores = jax.nn.sigmoid(logits)
    sel = scores + br_ref[...]
    lane_f = lax.broadcasted_iota(jnp.int32, sel.shape, 1).astype(F32)
    hits = []
    for _ in range(TOP_K):
        top = jnp.max(sel, axis=-1, keepdims=True)
        first = jnp.min(jnp.where(sel == top, lane_f, float(N_EXPERTS)), axis=-1, keepdims=True)
        hit = lane_f == first
        hits.append(hit)
        sel = jnp.where(hit, -jnp.inf, sel)
    chosen = functools.reduce(jnp.logical_or, hits)
    picked = jnp.where(chosen, scores, 0.0)
    denom = jnp.sum(picked, axis=-1, keepdims=True) + 1e-20
    gate = ROUTED_SCALE * picked / denom

    chosen_f = chosen.astype(F32)
    r_i = lax.broadcasted_iota(jnp.int32, (TOK_TILE, TOK_TILE), 0)
    c_i = lax.broadcasted_iota(jnp.int32, (TOK_TILE, TOK_TILE), 1)
    earlier = (c_i < r_i).astype(F32).astype(BF16)
    rank = _dot(earlier, chosen_f.astype(BF16))
    count = jnp.sum(chosen_f, axis=0, keepdims=True)
    n16 = jnp.floor((count + (ROW_CHUNK - 1)) * (1.0 / ROW_CHUNK))
    e_r = lax.broadcasted_iota(jnp.int32, (N_EXPERTS, N_EXPERTS), 0)
    e_c = lax.broadcasted_iota(jnp.int32, (N_EXPERTS, N_EXPERTS), 1)
    before = (e_r < e_c).astype(F32).astype(BF16)
    seg_start = ROW_CHUNK * _dot(jnp.broadcast_to(n16, (8, N_EXPERTS)).astype(BF16), before)[0:1, :]
    row_of = seg_start + rank
    lane_k = lax.broadcasted_iota(jnp.int32, (TOK_TILE, TOP_K), 1)
    apos = jnp.zeros((TOK_TILE, TOP_K), F32)
    gk = jnp.zeros((TOK_TILE, TOP_K), F32)
    for k, hit in enumerate(hits):
        apos = jnp.where(lane_k == k, jnp.sum(jnp.where(hit, row_of, 0.0), axis=-1, keepdims=True), apos)
        gk = jnp.where(lane_k == k, jnp.sum(jnp.where(hit, gate, 0.0), axis=-1, keepdims=True), gk)
    apos_ref[...] = apos
    gk_ref[...] = gk
    n16_ref[...] = n16


def _merge(x, mix_ctx, mix_lat, ug, mod, b_gate, wp, wo, ln_g, ln_b, wr_hi, wr_lo, b_router, layer):
    tok = lambda n: pl.BlockSpec((TOK_TILE, n), lambda i: (i, 0))
    const2 = lambda i: (0, 0)
    ctx_tiles = N_CTX // TOK_TILE
    return pl.pallas_call(
        _merge_kernel,
        out_shape=[jax.ShapeDtypeStruct((N_TOK, D_MODEL), F32), jax.ShapeDtypeStruct((N_TOK, D_MODEL), BF16),
                   jax.ShapeDtypeStruct((N_TOK, TOP_K), F32), jax.ShapeDtypeStruct((N_TOK, TOP_K), F32),
                   jax.ShapeDtypeStruct((N_TILES, 1, N_EXPERTS), F32)],
        grid=(N_TILES,),
        in_specs=[
            tok(D_MODEL),
            pl.BlockSpec((TOK_TILE, D_MODEL), lambda i: (jnp.minimum(i, ctx_tiles - 1), 0)),
            pl.BlockSpec((TOK_TILE, D_MODEL), lambda i: (jnp.maximum(i - ctx_tiles, 0), 0)),
            tok(GATE_COLS),
            pl.BlockSpec((None, None, 6, D_MODEL), lambda i: (layer, _cond_row_of_tile(i, TOK_TILE), 0, 0)),
            pl.BlockSpec((1, GATE_COLS), const2),
            pl.BlockSpec((D_MODEL, D_MODEL), const2),
            pl.BlockSpec((D_MODEL, D_MODEL), const2),
            pl.BlockSpec((1, D_MODEL), const2),
            pl.BlockSpec((1, D_MODEL), const2),
            pl.BlockSpec((D_MODEL, N_EXPERTS), const2),
            pl.BlockSpec((D_MODEL, N_EXPERTS), const2),
            pl.BlockSpec((1, N_EXPERTS), const2),
        ],
        out_specs=[tok(D_MODEL), tok(D_MODEL), tok(TOP_K), tok(TOP_K),
                   pl.BlockSpec((None, 1, N_EXPERTS), lambda i: (i, 0, 0))],
        compiler_params=pltpu.CompilerParams(dimension_semantics=("arbitrary",), vmem_limit_bytes=VMEM_LIMIT),
        name="merge_router",
    )(x, mix_ctx, mix_lat, ug, mod, b_gate, wp, wo, ln_g, ln_b, wr_hi, wr_lo, b_router)


def _routing_tables(n16):
    n16 = n16.astype(jnp.int32)
    hi = jnp.cumsum(n16, axis=1)
    lo = hi - n16
    n_chunks = hi[:, -1]
    eo = jnp.cumsum(n16, axis=0) - n16
    tc = jnp.sum(n16, axis=0)
    tiles = (tc + CHUNKS_PER_GMM_TILE - 1) // CHUNKS_PER_GMM_TILE
    tile_end = jnp.cumsum(tiles)
    tile_off = tile_end - tiles
    n_used = tile_end[-1]
    c = jnp.arange(TILE_CHUNKS, dtype=jnp.int32)[None, :, None]
    owner = jnp.logical_and(c >= lo[:, None, :], c < hi[:, None, :])
    shift = CHUNKS_PER_GMM_TILE * tile_off[None, :] + eo - lo
    dst = c[:, :, 0] + jnp.sum(jnp.where(owner, shift[:, None, :], 0), axis=-1)
    pad_start = CHUNKS_PER_GMM_TILE * tile_off + tc
    pad_count = CHUNKS_PER_GMM_TILE * tiles - tc
    return (dst.reshape(-1).astype(jnp.int32), n_chunks, tile_off.astype(jnp.int32), tiles.astype(jnp.int32),
            jnp.reshape(n_used, (1,)).astype(jnp.int32), pad_start.astype(jnp.int32), pad_count.astype(jnp.int32))


def _one_hot_rows(apos_ref, weight_ref=None):
    r = lax.broadcasted_iota(jnp.int32, (TOK_TILE, TILE_ROWS), 1).astype(F32)
    pt = jnp.zeros((TOK_TILE, TILE_ROWS), F32)
    for k in range(TOP_K):
        val = 1.0 if weight_ref is None else weight_ref[:, k:k + 1]
        pt = jnp.where(r == apos_ref[:, k:k + 1], val, pt)
    return pt.astype(BF16)


def _chunk_copy(src_ref, src_chunk, dst_ref, dst_chunk, sem):
    rows = lambda c: pl.ds(pl.multiple_of(c * ROW_CHUNK, ROW_CHUNK), ROW_CHUNK)
    return pltpu.make_async_copy(src_ref.at[rows(src_chunk), :], dst_ref.at[rows(dst_chunk), :], sem)


def _dispatch_kernel(dst_ref, nch_ref, pstart_ref, pcount_ref, nu_ref, h_ref, apos_ref, xs_hbm, stage_ref, zero_ref,
                     sem):
    j = pl.program_id(0)
    buf = j % 2

    def wait_chunks(b, n):
        def body(c, carry):
            _chunk_copy(stage_ref.at[b], 0, xs_hbm, 0, sem.at[b]).wait()
            return carry
        lax.fori_loop(0, n, body, 0)

    @pl.when(j < N_TILES)
    def _():
        @pl.when(j >= 2)
        def _():
            wait_chunks(buf, nch_ref[j - 2])

        pt = _one_hot_rows(apos_ref)
        h = h_ref[...]
        for r0 in range(0, TILE_ROWS, DISPATCH_ROWS):
            stage_ref[buf, r0:r0 + DISPATCH_ROWS, :] = _dot_tn(pt[:, r0:r0 + DISPATCH_ROWS], h).astype(BF16)

        def send(c, carry):
            _chunk_copy(stage_ref.at[buf], c, xs_hbm, dst_ref[j * TILE_CHUNKS + c], sem.at[buf]).start()
            return carry
        lax.fori_loop(0, nch_ref[j], send, 0)

    @pl.when(j == N_TILES)
    def _():
        zero_ref[...] = jnp.zeros_like(zero_ref)

        def tile_copy(t):
            rows = pl.ds(pl.multiple_of(t * GMM_TILE, GMM_TILE), GMM_TILE)
            return pltpu.make_async_copy(zero_ref, xs_hbm.at[rows, :], sem.at[3])

        def per_expert(e, total):
            def fill(i, carry):
                _chunk_copy(zero_ref, 0, xs_hbm, pstart_ref[e] + i, sem.at[2]).start()
                return carry
            lax.fori_loop(0, pcount_ref[e], fill, 0)
            return total + pcount_ref[e]
        n_pad = lax.fori_loop(0, N_EXPERTS, per_expert, 0)

        def fill_tile(t, carry):
            tile_copy(t).start()
            return carry
        lax.fori_loop(nu_ref[0], MAX_GMM_TILES, fill_tile, 0)
        wait_chunks(0, nch_ref[N_TILES - 2])
        wait_chunks(1, nch_ref[N_TILES - 1])

        def drain(i, carry):
            _chunk_copy(zero_ref, 0, xs_hbm, 0, sem.at[2]).wait()
            return carry
        lax.fori_loop(0, n_pad, drain, 0)

        def drain_tile(t, carry):
            tile_copy(0).wait()
            return carry
        lax.fori_loop(nu_ref[0], MAX_GMM_TILES, drain_tile, 0)


def _dispatch(h2, apos, dst, n_chunks, pad_start, pad_count, n_used):
    last = N_TILES - 1
    grid_spec = pltpu.PrefetchScalarGridSpec(
        num_scalar_prefetch=5,
        grid=(N_TILES + 1,),
        in_specs=[
            pl.BlockSpec((TOK_TILE, D_MODEL), lambda j, *_: (jnp.minimum(j, last), 0)),
            pl.BlockSpec((TOK_TILE, TOP_K), lambda j, *_: (jnp.minimum(j, last), 0)),
        ],
        out_specs=pl.BlockSpec(memory_space=pl.ANY),
        scratch_shapes=[pltpu.VMEM((2, TILE_ROWS, D_MODEL), BF16), pltpu.VMEM((GMM_TILE, D_MODEL), BF16),
                        pltpu.SemaphoreType.DMA((4,))],
    )
    return pl.pallas_call(
        _dispatch_kernel,
        out_shape=jax.ShapeDtypeStruct((SORTED_ROWS, D_MODEL), BF16),
        grid_spec=grid_spec,
        compiler_params=pltpu.CompilerParams(dimension_semantics=("arbitrary",), vmem_limit_bytes=VMEM_LIMIT),
        name="moe_dispatch",
    )(dst, n_chunks, pad_start, pad_count, n_used, h2, apos)


def _gmm_kernel(off_ref, cnt_ref, nu_ref, x_hbm, wg_ref, wu_ref, wd_ref, y_hbm, xbuf, ybuf, wg_b, wu_b, wd_b,
                in_sem, out_sem):
    e = pl.program_id(0)
    n_used = nu_ref[0]

    def rows(g):
        return pl.ds(pl.multiple_of(g * GMM_TILE, GMM_TILE), GMM_TILE)

    def fetch(g):
        slot = g % GMM_IN_SLOTS
        return pltpu.make_async_copy(x_hbm.at[rows(g), :], xbuf.at[slot], in_sem.at[slot])

    def flush(g):
        slot = g % GMM_OUT_SLOTS
        return pltpu.make_async_copy(ybuf.at[slot], y_hbm.at[rows(g), :], out_sem.at[slot])

    @pl.when(e == 0)
    def _():
        for g in range(GMM_AHEAD):
            @pl.when(g < n_used)
            def _(g=g):
                fetch(g).start()

    wg_b[...] = wg_ref[...].astype(BF16)
    wu_b[...] = wu_ref[...].astype(BF16)
    wd_b[...] = wd_ref[...].astype(BF16)

    def process(g0, width):
        for w in range(width):
            @pl.when(g0 + w + GMM_AHEAD < n_used)
            def _(w=w):
                fetch(g0 + w + GMM_AHEAD).start()
        for w in range(width):
            fetch(g0 + w).wait()
        x = jnp.concatenate([xbuf[(g0 + w) % GMM_IN_SLOTS] for w in range(width)], axis=0)
        hid = _silu(_dot(x, wg_b[...])) * _dot(x, wu_b[...])
        y = _dot(hid.astype(BF16), wd_b[...]).astype(BF16)
        for w in range(width):
            g = g0 + w

            @pl.when(g >= GMM_OUT_SLOTS)
            def _(g=g):
                flush(g - GMM_OUT_SLOTS).wait()

            ybuf[g % GMM_OUT_SLOTS] = y[w * GMM_TILE:(w + 1) * GMM_TILE]
            flush(g).start()

    first = off_ref[e]
    count = cnt_ref[e]

    done = 0
    width = GMM_WIDTH
    while width >= 1:
        trips = (count - done) // width

        def block(p, carry, done=done, width=width):
            process(first + done + width * p, width)
            return carry

        lax.fori_loop(0, trips, block, 0)
        done = done + trips * width
        width //= 2

    @pl.when(e == N_EXPERTS - 1)
    def _():
        for k in range(GMM_OUT_SLOTS):
            @pl.when(n_used - 1 - k >= 0)
            def _(k=k):
                flush(n_used - 1 - k).wait()


def _gmm(xs, weg, weu, wed, tile_off, tiles, n_used, layer):
    w_in_spec = pl.BlockSpec((None, None, D_MODEL, D_EXPERT), lambda e, *_: (layer, e, 0, 0))
    grid_spec = pltpu.PrefetchScalarGridSpec(
        num_scalar_prefetch=3,
        grid=(N_EXPERTS,),
        in_specs=[
            pl.BlockSpec(memory_space=pl.ANY),
            w_in_spec, w_in_spec,
            pl.BlockSpec((None, None, D_EXPERT, D_MODEL), lambda e, *_: (layer, e, 0, 0)),
        ],
        out_specs=pl.BlockSpec(memory_space=pl.ANY),
        scratch_shapes=[pltpu.VMEM((GMM_IN_SLOTS, GMM_TILE, D_MODEL), BF16),
                        pltpu.VMEM((GMM_OUT_SLOTS, GMM_TILE, D_MODEL), BF16),
                        pltpu.VMEM((D_MODEL, D_EXPERT), BF16), pltpu.VMEM((D_MODEL, D_EXPERT), BF16),
                        pltpu.VMEM((D_EXPERT, D_MODEL), BF16),
                        pltpu.SemaphoreType.DMA((GMM_IN_SLOTS,)), pltpu.SemaphoreType.DMA((GMM_OUT_SLOTS,))],
    )
    return pl.pallas_call(
        _gmm_kernel,
        out_shape=jax.ShapeDtypeStruct((SORTED_ROWS, D_MODEL), BF16),
        grid_spec=grid_spec,
        input_output_aliases={3: 0},
        compiler_params=pltpu.CompilerParams(dimension_semantics=("arbitrary",), vmem_limit_bytes=VMEM_LIMIT),
        name="moe_gmm",
    )(tile_off, tiles, n_used, xs, weg, weu, wed)


def _combine_kernel(dst_ref, nch_ref, ys_hbm, apos_ref, gk_ref, h_ref, x1_ref, mod_ref, wsg_ref, wsu_ref, wsd_ref,
                    g2_ref, b2_ref, o_ref, stage_ref, sem):
    j = pl.program_id(0)
    buf = j % 2

    def fetch(jj, b):
        def body(c, carry):
            _chunk_copy(ys_hbm, dst_ref[jj * TILE_CHUNKS + c], stage_ref.at[b], c, sem.at[b]).start()
            return carry
        lax.fori_loop(0, nch_ref[jj], body, 0)

    @pl.when(j == 0)
    def _():
        stage_ref[...] = jnp.zeros_like(stage_ref)
        fetch(0, 0)

    @pl.when(j + 1 < N_TILES)
    def _():
        fetch(j + 1, 1 - buf)

    h = h_ref[...]
    hid = _silu(_dot(h, wsg_ref[...])) * _dot(h, wsu_ref[...])
    shared = _dot(hid.astype(BF16), wsd_ref[...])
    ptw = _one_hot_rows(apos_ref, gk_ref)

    def wait(c, carry):
        _chunk_copy(ys_hbm, 0, stage_ref.at[buf], 0, sem.at[buf]).wait()
        return carry
    lax.fori_loop(0, nch_ref[j], wait, 0)

    routed = _dot(ptw, stage_ref[buf])
    r = ALPHA * x1_ref[...] + mod_ref[5:6, :] * (routed + shared)
    o_ref[...] = _ln(r) * g2_ref[...] + b2_ref[...]


def _combine(ys, apos, gk, h2, x1, mod, wsg, wsu, wsd, ln_g, ln_b, dst, n_chunks, layer):
    tok = lambda n: pl.BlockSpec((TOK_TILE, n), lambda j, *_: (j, 0))
    const2 = lambda j, *_: (0, 0)
    grid_spec = pltpu.PrefetchScalarGridSpec(
        num_scalar_prefetch=2,
        grid=(N_TILES,),
        in_specs=[
            pl.BlockSpec(memory_space=pl.ANY),
            tok(TOP_K), tok(TOP_K), tok(D_MODEL), tok(D_MODEL),
            pl.BlockSpec((None, None, 6, D_MODEL), lambda j, *_: (layer, _cond_row_of_tile(j, TOK_TILE), 0, 0)),
            pl.BlockSpec((D_MODEL, D_SHARED), const2),
            pl.BlockSpec((D_MODEL, D_SHARED), const2),
            pl.BlockSpec((D_SHARED, D_MODEL), const2),
            pl.BlockSpec((1, D_MODEL), const2),
            pl.BlockSpec((1, D_MODEL), const2),
        ],
        out_specs=tok(D_MODEL),
        scratch_shapes=[pltpu.VMEM((2, TILE_ROWS, D_MODEL), BF16), pltpu.SemaphoreType.DMA((2,))],
    )
    return pl.pallas_call(
        _combine_kernel,
        out_shape=jax.ShapeDtypeStruct((N_TOK, D_MODEL), F32),
        grid_spec=grid_spec,
        compiler_params=pltpu.CompilerParams(dimension_semantics=("arbitrary",), vmem_limit_bytes=VMEM_LIMIT),
        name="moe_combine",
    )(dst, n_chunks, ys, apos, gk, h2, x1, mod, wsg, wsu, wsd, ln_g, ln_b)


def kernel(x_prompt, x_sample, c, state_hgrn, c_ctx, w_ada, b_ada, w_in, b_gate, hgrn_lb, hgrn_norm, conv_w,
           w_proj_hgrn, w_proj_fourier, w_proj_conv, w_out, ln1_g, ln1_b, ln2_g, ln2_b, w_router, b_router,
           w_exp_gate, w_exp_up, w_exp_down, w_sh_gate, w_sh_up, w_sh_down):
    lb_sm = jax.nn.softmax(hgrn_lb.astype(F32), axis=0)
    lb_all = jnp.cumsum(lb_sm, axis=0) - lb_sm[:1]

    pos = jnp.asarray(_pos_emb_table())
    dft_ctx = jnp.asarray(_dft_time(SEQ)).astype(BF16)
    dft_lat = jnp.asarray(_dft_time(DEC_SEQ)).astype(BF16)
    bdc, bds = (jnp.asarray(m).astype(BF16) for m in _dft_channel())

    cond = jnp.concatenate([c_ctx[None, :], c, jnp.zeros((N_COND - 1 - DEC_BATCH, D_MODEL), F32)], axis=0)
    mod = _ada_mod(cond, w_ada, b_ada).reshape(DEPTH, N_COND, 6, D_MODEL)

    w_in_b = w_in.astype(BF16)
    wp_b = jnp.concatenate([w_proj_hgrn, w_proj_fourier, w_proj_conv], axis=1).astype(BF16)
    wo_b = w_out.astype(BF16)
    wr_hi = w_router.astype(BF16)
    wr_lo = (w_router - wr_hi.astype(F32)).astype(BF16)
    wsg_b, wsu_b, wsd_b = w_sh_gate.astype(BF16), w_sh_up.astype(BF16), w_sh_down.astype(BF16)

    x = _assemble(x_prompt, x_sample, pos)
    new_states = []
    for l in range(DEPTH):
        uh, uf, uc, ug = _in_proj(x, mod, w_in_b, l)
        small = (lb_all[l], hgrn_norm[l][None, :], conv_w[l])
        mix_ctx, s_ctx = _mixer(uh, uf, uc, *small, dft_ctx, bdc, bds, T=SEQ, n_seq=BATCH, blk0=0)
        (mix_lat,) = _mixer(uh, uf, uc, *small, dft_lat, bdc, bds, T=DEC_SEQ, n_seq=DEC_BATCH,
                            blk0=N_CTX // DEC_SEQ, state0=state_hgrn, layer=l)
        new_states.append(s_ctx)
        x1, h2, apos, gk, n16 = _merge(x, mix_ctx, mix_lat, ug, mod, b_gate[l][None, :], wp_b[l], wo_b[l],
                                       ln1_g[l][None, :], ln1_b[l][None, :], wr_hi[l], wr_lo[l],
                                       b_router[l][None, :], l)
        dst, n_chunks, tile_off, tiles, n_used, pad_start, pad_count = _routing_tables(
            n16.reshape(N_TILES, N_EXPERTS))
        xs = _dispatch(h2, apos, dst, n_chunks, pad_start, pad_count, n_used)
        ys = _gmm(xs, w_exp_gate, w_exp_up, w_exp_down, tile_off, tiles, n_used, l)
        x = _combine(ys, apos, gk, h2, x1, mod, wsg_b[l], wsu_b[l], wsd_b[l], ln2_g[l][None, :], ln2_b[l][None, :],
                     dst, n_chunks, l)
    y_prompt = x[:N_CTX].reshape(BATCH, SEQ, D_MODEL)
    y_sample = x[N_CTX:].reshape(DEC_BATCH, DEC_SEQ, D_MODEL)
    return (y_prompt, y_sample, jnp.stack(new_states, axis=1))
```

```python
import functools
import math

import numpy as np
import jax
import jax.numpy as jnp
from jax import lax
from jax.experimental import pallas as pl
from jax.experimental.pallas import tpu as pltpu

F32 = jnp.float32
BF16 = jnp.bfloat16

D_MODEL = 1024
BATCH = 16
SEQ = 256
DEPTH = 2
DEC_BATCH = 4
DEC_SEQ = 1024
GRID_W = 64
HEADS = 4
HEAD_DIM = 128
HGRN_WIDTH = HEADS * HEAD_DIM
FOURIER_GROUPS = 4
FOURIER_GROUP_DIM = 64
FOURIER_WIDTH = FOURIER_GROUPS * FOURIER_GROUP_DIM
CONV_WIDTH = 256
N_BRANCHES = 3
HGRN_COLS = 5 * HGRN_WIDTH
CONV_COLS = 3 * CONV_WIDTH
GATE_COLS = N_BRANCHES * D_MODEL
IN_COLS = HGRN_COLS + FOURIER_WIDTH + CONV_COLS + GATE_COLS
N_EXPERTS = 64
TOP_K = 8
D_EXPERT = 256
D_SHARED = 256
ROUTED_SCALE = 2.5
ALPHA = (2 * DEPTH) ** 0.25
LN_EPS = 1e-6
RMS_EPS = 1e-6
F_MIN = 1e-30

N_CTX = BATCH * SEQ
N_LAT = DEC_BATCH * DEC_SEQ
N_TOK = N_CTX + N_LAT
N_COND = 8

TOK_TILE = 256
CHUNK = 64
CHUNK_UNROLL = 2
WINDOW = 120.0
STATIC_WINDOWS = 2
ADA_TILE = 1536
VMEM_LIMIT = 56 * 1024 * 1024

N_TILES = N_TOK // TOK_TILE
ROW_CHUNK = 16
TILE_CHUNKS = (TOK_TILE * TOP_K + N_EXPERTS * (ROW_CHUNK - 1)) // ROW_CHUNK
DISPATCH_ROWS = 512
TILE_ROWS = -(-TILE_CHUNKS * ROW_CHUNK // DISPATCH_ROWS) * DISPATCH_ROWS
GMM_TILE = 256
GMM_WIDTH = 4
GMM_AHEAD = 6
GMM_IN_SLOTS = GMM_WIDTH + GMM_AHEAD
GMM_OUT_SLOTS = 2 * GMM_WIDTH
CHUNKS_PER_GMM_TILE = GMM_TILE // ROW_CHUNK
MAX_GMM_TILES = (N_TILES * TILE_CHUNKS + N_EXPERTS * (CHUNKS_PER_GMM_TILE - 1)) // CHUNKS_PER_GMM_TILE
SORTED_ROWS = MAX_GMM_TILES * GMM_TILE


def _cond_row_of_tile(i, tile):
    ctx_tiles = N_CTX // tile
    per_seq = DEC_SEQ // tile
    return jnp.where(i < ctx_tiles, 0, 1 + (i - ctx_tiles) // per_seq)


def _silu(x):
    return x * jax.nn.sigmoid(x)


def _ln(x):
    mu = jnp.mean(x, axis=-1, keepdims=True)
    xc = x - mu
    var = jnp.mean(xc * xc, axis=-1, keepdims=True)
    return xc * lax.rsqrt(var + LN_EPS)


def _dot(a, b):
    return jnp.dot(a, b, preferred_element_type=F32)


def _dot_nt(a, b):
    return lax.dot_general(a, b, (((1,), (1,)), ((), ())), preferred_element_type=F32)


def _dot_tn(a, b):
    return lax.dot_general(a, b, (((0,), (0,)), ((), ())), preferred_element_type=F32)


@functools.lru_cache(maxsize=None)
def _pos_emb_table():
    rows = DEC_SEQ // GRID_W
    t = np.arange(rows * GRID_W)
    r = (t // GRID_W).astype(np.float32)
    col = (t % GRID_W).astype(np.float32)
    quarter = D_MODEL // 4
    omega = (1.0 / (np.float32(10000.0) ** (np.arange(quarter, dtype=np.float32) / np.float32(quarter)))).astype(np.float32)
    ar = (r[:, None] * omega).astype(np.float32)
    ac = (col[:, None] * omega).astype(np.float32)
    return np.concatenate([np.sin(ar), np.cos(ar), np.sin(ac), np.cos(ac)], axis=-1).astype(np.float32)


@functools.lru_cache(maxsize=None)
def _dft_time(T):
    k = np.arange(T)
    ph = 2.0 * np.pi * ((k[:, None] * k[None, :]) % T) / T
    return np.concatenate([np.cos(ph), -np.sin(ph)], axis=1).astype(np.float32)


@functools.lru_cache(maxsize=None)
def _dft_channel():
    n = FOURIER_GROUP_DIM
    k = np.arange(n)
    ph = 2.0 * np.pi * ((k[:, None] * k[None, :]) % n) / n
    eye = np.eye(FOURIER_GROUPS)
    return np.kron(eye, np.cos(ph)).astype(np.float32), np.kron(eye, np.sin(ph)).astype(np.float32)


CTX_TILES = N_CTX // TOK_TILE
LAT_TILES_PER_SEQ = DEC_SEQ // TOK_TILE


def _stream_specs(n):
    specs = [pl.BlockSpec((TOK_TILE, D_MODEL), lambda i, *_: (jnp.minimum(i, CTX_TILES - 1), 0)),
             pl.BlockSpec((TOK_TILE, D_MODEL), lambda i, *_: (jnp.maximum(i - CTX_TILES, 0), 0))]
    if n == 3:
        specs.append(pl.BlockSpec((TOK_TILE, D_MODEL),
                                  lambda i, *_: (jnp.maximum(i - CTX_TILES, 0) % LAT_TILES_PER_SEQ, 0)))
    return specs


def _stream_tile(refs):
    latent = refs[1][...]
    if len(refs) == 3:
        latent = latent + refs[2][...]
    return jnp.where(pl.program_id(0) < CTX_TILES, refs[0][...], latent)


def _ada_kernel(c_ref, w_ref, b_ref, o_ref):
    s = _silu(c_ref[...]).astype(BF16)
    o_ref[...] = _dot(s, w_ref[...].astype(BF16)) + b_ref[...]


def _ada_mod(cond, w_ada, b_ada):
    n_col = 6 * D_MODEL
    return pl.pallas_call(
        _ada_kernel,
        out_shape=jax.ShapeDtypeStruct((DEPTH, N_COND, n_col), F32),
        grid=(DEPTH, n_col // ADA_TILE),
        in_specs=[
            pl.BlockSpec((N_COND, D_MODEL), lambda l, j: (0, 0)),
            pl.BlockSpec((None, D_MODEL, ADA_TILE), lambda l, j: (l, 0, j)),
            pl.BlockSpec((None, 1, ADA_TILE), lambda l, j: (l, 0, j)),
        ],
        out_specs=pl.BlockSpec((None, N_COND, ADA_TILE), lambda l, j: (l, 0, j)),
        compiler_params=pltpu.CompilerParams(dimension_semantics=("arbitrary", "arbitrary"),
                                             vmem_limit_bytes=VMEM_LIMIT),
        name="ada_mod",
    )(cond, w_ada, b_ada.reshape(DEPTH, 1, n_col))


def _in_proj_kernel(*refs, n_stream):
    mod_ref, w_ref, uh_ref, uf_ref, uc_ref, ug_ref = refs[n_stream:]
    sh1 = mod_ref[0:1, :]
    sc1 = mod_ref[1:2, :]
    h = (_ln(_stream_tile(refs[:n_stream])) * (1.0 + sc1) + sh1).astype(BF16)
    c0 = 0
    for ref, n in ((uh_ref, HGRN_COLS), (uf_ref, FOURIER_WIDTH), (uc_ref, CONV_COLS), (ug_ref, GATE_COLS)):
        ref[...] = _dot(h, w_ref[:, c0:c0 + n])
        c0 += n


def _in_proj(stream, mod, w_in_bf16, layer):
    n_tiles = N_TOK // TOK_TILE
    widths = (HGRN_COLS, FOURIER_WIDTH, CONV_COLS, GATE_COLS)
    return pl.pallas_call(
        functools.partial(_in_proj_kernel, n_stream=len(stream)),
        out_shape=[jax.ShapeDtypeStruct((N_TOK, n), F32) for n in widths],
        grid=(n_tiles,),
        in_specs=_stream_specs(len(stream)) + [
            pl.BlockSpec((None, None, 6, D_MODEL), lambda i: (layer, _cond_row_of_tile(i, TOK_TILE), 0, 0)),
            pl.BlockSpec((None, D_MODEL, IN_COLS), lambda i: (layer, 0, 0)),
        ],
        out_specs=[pl.BlockSpec((TOK_TILE, n), lambda i: (i, 0)) for n in widths],
        compiler_params=pltpu.CompilerParams(dimension_semantics=("arbitrary",), vmem_limit_bytes=VMEM_LIMIT),
        name="in_proj",
    )(*stream, mod, w_in_bf16)


def _mixer_kernel(*refs, T, has_state):
    if has_state:
        (uh_ref, uf_ref, uc_ref, lb_ref, ng_ref, cw_ref, dft_ref, bdc_ref, bds_ref, s0_ref,
         mix_ref, st_ref, o_ref) = refs
        sfin_ref = None
    else:
        (uh_ref, uf_ref, uc_ref, lb_ref, ng_ref, cw_ref, dft_ref, bdc_ref, bds_ref,
         mix_ref, sfin_ref, st_ref, o_ref) = refs
        s0_ref = None
    n_chunks = T // CHUNK
    W = HGRN_WIDTH

    for d in range(2):
        for h in range(HEADS):
            if has_state:
                st_ref[d, h] = s0_ref[0, d, h].T
            else:
                st_ref[d, h] = jnp.zeros((HEAD_DIM, HEAD_DIM), F32)
    o_ref[...] = jnp.zeros_like(o_ref)

    row = lax.broadcasted_iota(jnp.int32, (CHUNK, CHUNK), 0)
    col = lax.broadcasted_iota(jnp.int32, (CHUNK, CHUNK), 1)
    keep = (col <= row, col >= row)
    tri = tuple(k.astype(F32).astype(BF16) for k in keep)

    def chunk_step(i, carry):
        deep = []
        for sub, d in ((s, d) for s in range(CHUNK_UNROLL) for d in range(2)):
            c = i * CHUNK_UNROLL + sub
            c = c if d == 0 else n_chunks - 1 - c
            rows = pl.ds(pl.multiple_of(c * CHUNK, CHUNK), CHUNK)
            q = _silu(uh_ref[rows, 0:W])
            v = uh_ref[rows, W:2 * W].astype(BF16)
            z = uh_ref[rows, (2 + d) * W:(3 + d) * W]
            lb = lb_ref[d:d + 1, :]
            e = jnp.exp(-jnp.abs(z))
            r = 1.0 / (1.0 + e)
            er = e * r
            pos = z >= 0.0
            sig_p = jnp.where(pos, r, er)
            sig_n = jnp.where(pos, er, r)
            f = lb + (1.0 - lb) * sig_p
            lf = jnp.log(jnp.maximum(f, F_MIN))
            k = (1.0 - lb) * sig_n
            lf_hi = lf.astype(BF16)
            lf_lo = (lf - lf_hi.astype(F32)).astype(BF16)
            b = _dot(tri[d], lf_hi) + _dot(tri[d], lf_lo)
            g = b[CHUNK - 1:CHUNK, :] if d == 0 else b[0:1, :]

            def window(p_level, q=q, k=k, b=b):
                dist = -p_level - b
                inside = jnp.logical_and(dist >= 0.0, dist < WINDOW)
                k_w = jnp.where(inside, k * jnp.exp(jnp.minimum(dist, WINDOW) - 0.5 * WINDOW), 0.0)
                q_w = q * jnp.exp(jnp.minimum(-dist, 0.0) + 0.5 * WINDOW)
                return q_w.astype(BF16), k_w.astype(BF16)

            wins = [window(WINDOW * p) for p in range(STATIC_WINDOWS)]
            q_abs = (q * jnp.exp(b)).astype(BF16)
            k_end = (k * jnp.exp(g - b)).astype(BF16)
            decay = jnp.exp(g)
            outs = []
            for h in range(HEADS):
                ls = slice(h * HEAD_DIM, (h + 1) * HEAD_DIM)
                q_cat = jnp.concatenate([w[0][:, ls] for w in wins], axis=-1)
                k_cat = jnp.concatenate([w[1][:, ls] for w in wins], axis=-1)
                a = jnp.where(keep[d], _dot_nt(q_cat, k_cat), 0.0).astype(BF16)
                s_t = st_ref[d, h]
                o_h = _dot(a, v[:, ls]) + _dot_nt(q_abs[:, ls], s_t.astype(BF16))
                st_ref[d, h] = s_t * decay[:, ls] + _dot_tn(v[:, ls], k_end[:, ls])
                outs.append(o_h)
            o_ref[rows, :] += jnp.concatenate(outs, axis=-1)
            deep.append((d, window, v, rows, jnp.min(b)))

        deepest = functools.reduce(jnp.minimum, [entry[4] for entry in deep])
        n_windows = jnp.floor(-deepest * (1.0 / WINDOW)).astype(jnp.int32) + 1

        def extra_window(p, carry2):
            for d, window, v, rows, _ in deep:
                q_w, k_w = window(WINDOW * p.astype(F32))
                extra = []
                for h in range(HEADS):
                    ls = slice(h * HEAD_DIM, (h + 1) * HEAD_DIM)
                    a = jnp.where(keep[d], _dot_nt(q_w[:, ls], k_w[:, ls]), 0.0).astype(BF16)
                    extra.append(_dot(a, v[:, ls]))
                o_ref[rows, :] += jnp.concatenate(extra, axis=-1)
            return carry2

        lax.fori_loop(STATIC_WINDOWS, n_windows, extra_window, 0)
        return carry

    lax.fori_loop(0, n_chunks // CHUNK_UNROLL, chunk_step, 0)

    if sfin_ref is not None:
        for d in range(2):
            for h in range(HEADS):
                sfin_ref[0, d, h] = st_ref[d, h].T

    o = o_ref[...]
    parts = []
    for h in range(HEADS):
        oh = o[:, h * HEAD_DIM:(h + 1) * HEAD_DIM]
        parts.append(oh * lax.rsqrt(jnp.mean(oh * oh, axis=-1, keepdims=True) + RMS_EPS))
    a_out = jnp.concatenate(parts, axis=-1) * ng_ref[...] * _silu(uh_ref[:, 4 * W:5 * W])
    mix_ref[:, 0:W] = a_out.astype(BF16)

    zf = uf_ref[...].astype(BF16)
    zc = _dot(zf, bdc_ref[...]).astype(BF16)
    zs = _dot(zf, bds_ref[...]).astype(BF16)
    four = _dot(dft_ref[...], jnp.concatenate([zc, zs], axis=0)) * (1.0 / math.sqrt(T * FOURIER_GROUP_DIM))
    mix_ref[:, W:W + FOURIER_WIDTH] = four.astype(BF16)

    cb = uc_ref[:, 0:CONV_WIDTH]
    zz = uc_ref[:, CONV_WIDTH:2 * CONV_WIDTH] * uc_ref[:, 2 * CONV_WIDTH:3 * CONV_WIDTH]
    t_idx = lax.broadcasted_iota(jnp.int32, (T, CONV_WIDTH), 0)
    z_prev = jnp.where(t_idx == 0, 0.0, pltpu.roll(zz, 1, axis=0))
    z_next = jnp.where(t_idx == T - 1, 0.0, pltpu.roll(zz, T - 1, axis=0))
    y = cw_ref[0:1, :] * z_prev + cw_ref[1:2, :] * zz + cw_ref[2:3, :] * z_next
    mix_ref[:, W + FOURIER_WIDTH:] = (cb * y).astype(BF16)


def _mixer(uh, uf, uc, lb, norm_g, conv_w, dft, bdc, bds, T, n_seq, blk0, state0=None, layer=0):
    has_state = state0 is not None
    const2 = lambda s: (0, 0)
    in_specs = [
        pl.BlockSpec((T, HGRN_COLS), lambda s: (blk0 + s, 0)),
        pl.BlockSpec((T, FOURIER_WIDTH), lambda s: (blk0 + s, 0)),
        pl.BlockSpec((T, CONV_COLS), lambda s: (blk0 + s, 0)),
        pl.BlockSpec((2, HGRN_WIDTH), const2),
        pl.BlockSpec((1, HGRN_WIDTH), const2),
        pl.BlockSpec((3, CONV_WIDTH), const2),
        pl.BlockSpec((T, 2 * T), const2),
        pl.BlockSpec((FOURIER_WIDTH, FOURIER_WIDTH), const2),
        pl.BlockSpec((FOURIER_WIDTH, FOURIER_WIDTH), const2),
    ]
    args = [uh, uf, uc, lb, norm_g, conv_w, dft, bdc, bds]
    mix_shape = jax.ShapeDtypeStruct((n_seq * T, D_MODEL), BF16)
    mix_spec = pl.BlockSpec((T, D_MODEL), lambda s: (s, 0))
    st_block = (1, 2, HEADS, HEAD_DIM, HEAD_DIM)
    if has_state:
        in_specs.append(pl.BlockSpec((1, None) + st_block[1:], lambda s: (s, layer, 0, 0, 0, 0)))
        args.append(state0)
        out_shape = [mix_shape]
        out_specs = [mix_spec]
    else:
        out_shape = [mix_shape, jax.ShapeDtypeStruct((n_seq, 2, HEADS, HEAD_DIM, HEAD_DIM), F32)]
        out_specs = [mix_spec, pl.BlockSpec(st_block, lambda s: (s, 0, 0, 0, 0))]
    return pl.pallas_call(
        functools.partial(_mixer_kernel, T=T, has_state=has_state),
        out_shape=out_shape,
        grid=(n_seq,),
        in_specs=in_specs,
        out_specs=out_specs,
        scratch_shapes=[pltpu.VMEM((2, HEADS, HEAD_DIM, HEAD_DIM), F32), pltpu.VMEM((T, HGRN_WIDTH), F32)],
        compiler_params=pltpu.CompilerParams(dimension_semantics=("arbitrary",), vmem_limit_bytes=VMEM_LIMIT),
        name="mixer_latent" if has_state else "mixer_context",
    )(*args)


def _merge_kernel(*refs, n_stream):
    (mixc_ref, mixl_ref, ug_ref, mod_ref, bg_ref, wp_ref, wo_ref, g1_ref, b1_ref, wrh_ref, wrl_ref, br_ref,
     x1_ref, h2_ref, apos_ref, gk_ref, n16_ref) = refs[n_stream:]
    W = HGRN_WIDTH
    gates = jax.nn.sigmoid(ug_ref[...] + bg_ref[...])
    mix = _stream_tile((mixc_ref, mixl_ref))
    edges = (0, W, W + FOURIER_WIDTH, D_MODEL)
    merged = None
    for j in range(N_BRANCHES):
        p = _dot(mix[:, edges[j]:edges[j + 1]], wp_ref[edges[j]:edges[j + 1], :])
        term = gates[:, j * D_MODEL:(j + 1) * D_MODEL] * p
        merged = term if merged is None else merged + term
    y = _dot(merged.astype(BF16), wo_ref[...])
    x1 = _ln(ALPHA * _stream_tile(refs[:n_stream]) + mod_ref[2:3, :] * y) * g1_ref[...] + b1_ref[...]
    x1_ref[...] = x1
    h2 = _ln(x1) * (1.0 + mod_ref[4:5, :]) + mod_ref[3:4, :]
    h2_hi = h2.astype(BF16)
    h2_ref[...] = h2_hi
    h2_lo = (h2 - h2_hi.astype(F32)).astype(BF16)
    logits = _dot(h2_hi, wrh_ref[...]) + (_dot(h2_hi, wrl_ref[...]) + _dot(h2_lo, wrh_ref[...]))
    scores = jax.nn.sigmoid(logits)
    sel = scores + br_ref[...]
    lane_f = lax.broadcasted_iota(jnp.int32, sel.shape, 1).astype(F32)
    hits = []
    for _ in range(TOP_K):
        top = jnp.max(sel, axis=-1, keepdims=True)
        first = jnp.min(jnp.where(sel == top, lane_f, float(N_EXPERTS)), axis=-1, keepdims=True)
        hit = lane_f == first
        hits.append(hit)
        sel = jnp.where(hit, -jnp.inf, sel)
    chosen = functools.reduce(jnp.logical_or, hits)
    picked = jnp.where(chosen, scores, 0.0)
    denom = jnp.sum(picked, axis=-1, keepdims=True) + 1e-20
    gate = ROUTED_SCALE * picked / denom

    chosen_f = chosen.astype(F32)
    r_i = lax.broadcasted_iota(jnp.int32, (TOK_TILE, TOK_TILE), 0)
    c_i = lax.broadcasted_iota(jnp.int32, (TOK_TILE, TOK_TILE), 1)
    earlier = (c_i < r_i).astype(F32).astype(BF16)
    rank = _dot(earlier, chosen_f.astype(BF16))
    count = jnp.sum(chosen_f, axis=0, keepdims=True)
    n16 = jnp.floor((count + (ROW_CHUNK - 1)) * (1.0 / ROW_CHUNK))
    e_r = lax.broadcasted_iota(jnp.int32, (N_EXPERTS, N_EXPERTS), 0)
    e_c = lax.broadcasted_iota(jnp.int32, (N_EXPERTS, N_EXPERTS), 1)
    before = (e_r < e_c).astype(F32).astype(BF16)
    seg_start = ROW_CHUNK * _dot(jnp.broadcast_to(n16, (8, N_EXPERTS)).astype(BF16), before)[0:1, :]
    row_of = seg_start + rank
    lane_k = lax.broadcasted_iota(jnp.int32, (TOK_TILE, TOP_K), 1)
    apos = jnp.zeros((TOK_TILE, TOP_K), F32)
    gk = jnp.zeros((TOK_TILE, TOP_K), F32)
    for k, hit in enumerate(hits):
        apos = jnp.where(lane_k == k, jnp.sum(jnp.where(hit, row_of, 0.0), axis=-1, keepdims=True), apos)
        gk = jnp.where(lane_k == k, jnp.sum(jnp.where(hit, gate, 0.0), axis=-1, keepdims=True), gk)
    apos_ref[...] = apos
    gk_ref[...] = gk
    n16_ref[...] = n16


def _merge(stream, mix_ctx, mix_lat, ug, mod, b_gate, wp, wo, ln_g, ln_b, wr_hi, wr_lo, b_router, layer):
    tok = lambda n: pl.BlockSpec((TOK_TILE, n), lambda i: (i, 0))
    const2 = lambda i: (0, 0)
    return pl.pallas_call(
        functools.partial(_merge_kernel, n_stream=len(stream)),
        out_shape=[jax.ShapeDtypeStruct((N_TOK, D_MODEL), F32), jax.ShapeDtypeStruct((N_TOK, D_MODEL), BF16),
                   jax.ShapeDtypeStruct((N_TOK, TOP_K), F32), jax.ShapeDtypeStruct((N_TOK, TOP_K), F32),
                   jax.ShapeDtypeStruct((N_TILES, 1, N_EXPERTS), F32)],
        grid=(N_TILES,),
        in_specs=_stream_specs(len(stream)) + _stream_specs(2) + [
            tok(GATE_COLS),
            pl.BlockSpec((None, None, 6, D_MODEL), lambda i: (layer, _cond_row_of_tile(i, TOK_TILE), 0, 0)),
            pl.BlockSpec((1, GATE_COLS), const2),
            pl.BlockSpec((D_MODEL, D_MODEL), const2),
            pl.BlockSpec((D_MODEL, D_MODEL), const2),
            pl.BlockSpec((1, D_MODEL), const2),
            pl.BlockSpec((1, D_MODEL), const2),
            pl.BlockSpec((D_MODEL, N_EXPERTS), const2),
            pl.BlockSpec((D_MODEL, N_EXPERTS), const2),
            pl.BlockSpec((1, N_EXPERTS), const2),
        ],
        out_specs=[tok(D_MODEL), tok(D_MODEL), tok(TOP_K), tok(TOP_K),
                   pl.BlockSpec((None, 1, N_EXPERTS), lambda i: (i, 0, 0))],
        compiler_params=pltpu.CompilerParams(dimension_semantics=("arbitrary",), vmem_limit_bytes=VMEM_LIMIT),
        name="merge_router",
    )(*stream, mix_ctx, mix_lat, ug, mod, b_gate, wp, wo, ln_g, ln_b, wr_hi, wr_lo, b_router)


def _routing_tables(n16):
    n16 = n16.astype(jnp.int32)
    hi = jnp.cumsum(n16, axis=1)
    lo = hi - n16
    n_chunks = hi[:, -1]
    eo = jnp.cumsum(n16, axis=0) - n16
    tc = jnp.sum(n16, axis=0)
    tiles = (tc + CHUNKS_PER_GMM_TILE - 1) // CHUNKS_PER_GMM_TILE
    tile_end = jnp.cumsum(tiles)
    tile_off = tile_end - tiles
    n_used = tile_end[-1]
    c = jnp.arange(TILE_CHUNKS, dtype=jnp.int32)[None, :, None]
    owner = jnp.logical_and(c >= lo[:, None, :], c < hi[:, None, :])
    shift = CHUNKS_PER_GMM_TILE * tile_off[None, :] + eo - lo
    dst = c[:, :, 0] + jnp.sum(jnp.where(owner, shift[:, None, :], 0), axis=-1)
    pad_start = CHUNKS_PER_GMM_TILE * tile_off + tc
    pad_count = CHUNKS_PER_GMM_TILE * tiles - tc
    return (dst.reshape(-1).astype(jnp.int32), n_chunks, tile_off.astype(jnp.int32), tiles.astype(jnp.int32),
            jnp.reshape(n_used, (1,)).astype(jnp.int32), pad_start.astype(jnp.int32), pad_count.astype(jnp.int32))


def _one_hot_rows(apos_ref, weight_ref=None):
    r = lax.broadcasted_iota(jnp.int32, (TOK_TILE, TILE_ROWS), 1).astype(F32)
    pt = jnp.zeros((TOK_TILE, TILE_ROWS), F32)
    for k in range(TOP_K):
        val = 1.0 if weight_ref is None else weight_ref[:, k:k + 1]
        pt = jnp.where(r == apos_ref[:, k:k + 1], val, pt)
    return pt.astype(BF16)


def _chunk_copy(src_ref, src_chunk, dst_ref, dst_chunk, sem):
    rows = lambda c: pl.ds(pl.multiple_of(c * ROW_CHUNK, ROW_CHUNK), ROW_CHUNK)
    return pltpu.make_async_copy(src_ref.at[rows(src_chunk), :], dst_ref.at[rows(dst_chunk), :], sem)


def _dispatch_kernel(dst_ref, nch_ref, pstart_ref, pcount_ref, nu_ref, h_ref, apos_ref, xs_hbm, stage_ref, zero_ref,
                     sem):
    j = pl.program_id(0)
    buf = j % 2

    def wait_chunks(b, n):
        def body(c, carry):
            _chunk_copy(stage_ref.at[b], 0, xs_hbm, 0, sem.at[b]).wait()
            return carry
        lax.fori_loop(0, n, body, 0)

    @pl.when(j < N_TILES)
    def _():
        @pl.when(j >= 2)
        def _():
            wait_chunks(buf, nch_ref[j - 2])

        pt = _one_hot_rows(apos_ref)
        h = h_ref[...]
        for r0 in range(0, TILE_ROWS, DISPATCH_ROWS):
            stage_ref[buf, r0:r0 + DISPATCH_ROWS, :] = _dot_tn(pt[:, r0:r0 + DISPATCH_ROWS], h).astype(BF16)

        def send(c, carry):
            _chunk_copy(stage_ref.at[buf], c, xs_hbm, dst_ref[j * TILE_CHUNKS + c], sem.at[buf]).start()
            return carry
        lax.fori_loop(0, nch_ref[j], send, 0)

    @pl.when(j == N_TILES)
    def _():
        zero_ref[...] = jnp.zeros_like(zero_ref)

        def tile_copy(t):
            rows = pl.ds(pl.multiple_of(t * GMM_TILE, GMM_TILE), GMM_TILE)
            return pltpu.make_async_copy(zero_ref, xs_hbm.at[rows, :], sem.at[3])

        def per_expert(e, total):
            def fill(i, carry):
                _chunk_copy(zero_ref, 0, xs_hbm, pstart_ref[e] + i, sem.at[2]).start()
                return carry
            lax.fori_loop(0, pcount_ref[e], fill, 0)
            return total + pcount_ref[e]
        n_pad = lax.fori_loop(0, N_EXPERTS, per_expert, 0)

        def fill_tile(t, carry):
            tile_copy(t).start()
            return carry
        lax.fori_loop(nu_ref[0], MAX_GMM_TILES, fill_tile, 0)
        wait_chunks(0, nch_ref[N_TILES - 2])
        wait_chunks(1, nch_ref[N_TILES - 1])

        def drain(i, carry):
            _chunk_copy(zero_ref, 0, xs_hbm, 0, sem.at[2]).wait()
            return carry
        lax.fori_loop(0, n_pad, drain, 0)

        def drain_tile(t, carry):
            tile_copy(0).wait()
            return carry
        lax.fori_loop(nu_ref[0], MAX_GMM_TILES, drain_tile, 0)


def _dispatch(h2, apos, dst, n_chunks, pad_start, pad_count, n_used):
    last = N_TILES - 1
    grid_spec = pltpu.PrefetchScalarGridSpec(
        num_scalar_prefetch=5,
        grid=(N_TILES + 1,),
        in_specs=[
            pl.BlockSpec((TOK_TILE, D_MODEL), lambda j, *_: (jnp.minimum(j, last), 0)),
            pl.BlockSpec((TOK_TILE, TOP_K), lambda j, *_: (jnp.minimum(j, last), 0)),
        ],
        out_specs=pl.BlockSpec(memory_space=pl.ANY),
        scratch_shapes=[pltpu.VMEM((2, TILE_ROWS, D_MODEL), BF16), pltpu.VMEM((GMM_TILE, D_MODEL), BF16),
                        pltpu.SemaphoreType.DMA((4,))],
    )
    return pl.pallas_call(
        _dispatch_kernel,
        out_shape=jax.ShapeDtypeStruct((SORTED_ROWS, D_MODEL), BF16),
        grid_spec=grid_spec,
        compiler_params=pltpu.CompilerParams(dimension_semantics=("arbitrary",), vmem_limit_bytes=VMEM_LIMIT),
        name="moe_dispatch",
    )(dst, n_chunks, pad_start, pad_count, n_used, h2, apos)


def _gmm_kernel(off_ref, cnt_ref, nu_ref, x_hbm, wg_ref, wu_ref, wd_ref, y_hbm, xbuf, ybuf, wg_b, wu_b, wd_b,
                in_sem, out_sem):
    e = pl.program_id(0)
    n_used = nu_ref[0]

    def rows(g):
        return pl.ds(pl.multiple_of(g * GMM_TILE, GMM_TILE), GMM_TILE)

    def fetch(g):
        slot = g % GMM_IN_SLOTS
        return pltpu.make_async_copy(x_hbm.at[rows(g), :], xbuf.at[slot], in_sem.at[slot])

    def flush(g):
        slot = g % GMM_OUT_SLOTS
        return pltpu.make_async_copy(ybuf.at[slot], y_hbm.at[rows(g), :], out_sem.at[slot])

    @pl.when(e == 0)
    def _():
        for g in range(GMM_AHEAD):
            @pl.when(g < n_used)
            def _(g=g):
                fetch(g).start()

    wg_b[...] = wg_ref[...].astype(BF16)
    wu_b[...] = wu_ref[...].astype(BF16)
    wd_b[...] = wd_ref[...].astype(BF16)

    def process(g0, width):
        for w in range(width):
            @pl.when(g0 + w + GMM_AHEAD < n_used)
            def _(w=w):
                fetch(g0 + w + GMM_AHEAD).start()
        for w in range(width):
            fetch(g0 + w).wait()
        x = jnp.concatenate([xbuf[(g0 + w) % GMM_IN_SLOTS] for w in range(width)], axis=0)
        hid = _silu(_dot(x, wg_b[...])) * _dot(x, wu_b[...])
        y = _dot(hid.astype(BF16), wd_b[...]).astype(BF16)
        for w in range(width):
            g = g0 + w

            @pl.when(g >= GMM_OUT_SLOTS)
            def _(g=g):
                flush(g - GMM_OUT_SLOTS).wait()

            ybuf[g % GMM_OUT_SLOTS] = y[w * GMM_TILE:(w + 1) * GMM_TILE]
            flush(g).start()

    first = off_ref[e]
    count = cnt_ref[e]

    done = 0
    width = GMM_WIDTH
    while width >= 1:
        trips = (count - done) // width

        def block(p, carry, done=done, width=width):
            process(first + done + width * p, width)
            return carry

        lax.fori_loop(0, trips, block, 0)
        done = done + trips * width
        width //= 2

    @pl.when(e == N_EXPERTS - 1)
    def _():
        for k in range(GMM_OUT_SLOTS):
            @pl.when(n_used - 1 - k >= 0)
            def _(k=k):
                flush(n_used - 1 - k).wait()


def _gmm(xs, weg, weu, wed, tile_off, tiles, n_used, layer):
    w_in_spec = pl.BlockSpec((None, None, D_MODEL, D_EXPERT), lambda e, *_: (layer, e, 0, 0))
    grid_spec = pltpu.PrefetchScalarGridSpec(
        num_scalar_prefetch=3,
        grid=(N_EXPERTS,),
        in_specs=[
            pl.BlockSpec(memory_space=pl.ANY),
            w_in_spec, w_in_spec,
            pl.BlockSpec((None, None, D_EXPERT, D_MODEL), lambda e, *_: (layer, e, 0, 0)),
        ],
        out_specs=pl.BlockSpec(memory_space=pl.ANY),
        scratch_shapes=[pltpu.VMEM((GMM_IN_SLOTS, GMM_TILE, D_MODEL), BF16),
                        pltpu.VMEM((GMM_OUT_SLOTS, GMM_TILE, D_MODEL), BF16),
                        pltpu.VMEM((D_MODEL, D_EXPERT), BF16), pltpu.VMEM((D_MODEL, D_EXPERT), BF16),
                        pltpu.VMEM((D_EXPERT, D_MODEL), BF16),
                        pltpu.SemaphoreType.DMA((GMM_IN_SLOTS,)), pltpu.SemaphoreType.DMA((GMM_OUT_SLOTS,))],
    )
    return pl.pallas_call(
        _gmm_kernel,
        out_shape=jax.ShapeDtypeStruct((SORTED_ROWS, D_MODEL), BF16),
        grid_spec=grid_spec,
        input_output_aliases={3: 0},
        compiler_params=pltpu.CompilerParams(dimension_semantics=("arbitrary",), vmem_limit_bytes=VMEM_LIMIT),
        name="moe_gmm",
    )(tile_off, tiles, n_used, xs, weg, weu, wed)


def _combine_kernel(dst_ref, nch_ref, ys_hbm, apos_ref, gk_ref, h_ref, x1_ref, mod_ref, wsg_ref, wsu_ref, wsd_ref,
                    g2_ref, b2_ref, oc_ref, ol_ref, stage_ref, sem):
    j = pl.program_id(0)
    buf = j % 2

    def fetch(jj, b):
        def body(c, carry):
            _chunk_copy(ys_hbm, dst_ref[jj * TILE_CHUNKS + c], stage_ref.at[b], c, sem.at[b]).start()
            return carry
        lax.fori_loop(0, nch_ref[jj], body, 0)

    @pl.when(j == 0)
    def _():
        stage_ref[...] = jnp.zeros_like(stage_ref)
        fetch(0, 0)

    @pl.when(j + 1 < N_TILES)
    def _():
        fetch(j + 1, 1 - buf)

    h = h_ref[...]
    hid = _silu(_dot(h, wsg_ref[...])) * _dot(h, wsu_ref[...])
    shared = _dot(hid.astype(BF16), wsd_ref[...])
    ptw = _one_hot_rows(apos_ref, gk_ref)

    def wait(c, carry):
        _chunk_copy(ys_hbm, 0, stage_ref.at[buf], 0, sem.at[buf]).wait()
        return carry
    lax.fori_loop(0, nch_ref[j], wait, 0)

    routed = _dot(ptw, stage_ref[buf])
    r = ALPHA * x1_ref[...] + mod_ref[5:6, :] * (routed + shared)
    out = _ln(r) * g2_ref[...] + b2_ref[...]

    @pl.when(j < CTX_TILES)
    def _():
        oc_ref[...] = out

    @pl.when(j >= CTX_TILES)
    def _():
        ol_ref[...] = out


def _combine(ys, apos, gk, h2, x1, mod, wsg, wsu, wsd, ln_g, ln_b, dst, n_chunks, layer):
    tok = lambda n: pl.BlockSpec((TOK_TILE, n), lambda j, *_: (j, 0))
    const2 = lambda j, *_: (0, 0)
    grid_spec = pltpu.PrefetchScalarGridSpec(
        num_scalar_prefetch=2,
        grid=(N_TILES,),
        in_specs=[
            pl.BlockSpec(memory_space=pl.ANY),
            tok(TOP_K), tok(TOP_K), tok(D_MODEL), tok(D_MODEL),
            pl.BlockSpec((None, None, 6, D_MODEL), lambda j, *_: (layer, _cond_row_of_tile(j, TOK_TILE), 0, 0)),
            pl.BlockSpec((D_MODEL, D_SHARED), const2),
            pl.BlockSpec((D_MODEL, D_SHARED), const2),
            pl.BlockSpec((D_SHARED, D_MODEL), const2),
            pl.BlockSpec((1, D_MODEL), const2),
            pl.BlockSpec((1, D_MODEL), const2),
        ],
        out_specs=_stream_specs(2),
        scratch_shapes=[pltpu.VMEM((2, TILE_ROWS, D_MODEL), BF16), pltpu.SemaphoreType.DMA((2,))],
    )
    return pl.pallas_call(
        _combine_kernel,
        out_shape=[jax.ShapeDtypeStruct((N_CTX, D_MODEL), F32), jax.ShapeDtypeStruct((N_LAT, D_MODEL), F32)],
        grid_spec=grid_spec,
        compiler_params=pltpu.CompilerParams(dimension_semantics=("arbitrary",), vmem_limit_bytes=VMEM_LIMIT),
        name="moe_combine",
    )(dst, n_chunks, ys, apos, gk, h2, x1, mod, wsg, wsu, wsd, ln_g, ln_b)


def kernel(x_prompt, x_sample, c, state_hgrn, c_ctx, w_ada, b_ada, w_in, b_gate, hgrn_lb, hgrn_norm, conv_w,
           w_proj_hgrn, w_proj_fourier, w_proj_conv, w_out, ln1_g, ln1_b, ln2_g, ln2_b, w_router, b_router,
           w_exp_gate, w_exp_up, w_exp_down, w_sh_gate, w_sh_up, w_sh_down):
    lb_sm = jax.nn.softmax(hgrn_lb.astype(F32), axis=0)
    lb_all = jnp.cumsum(lb_sm, axis=0) - lb_sm[:1]

    pos = jnp.asarray(_pos_emb_table())
    dft_ctx = jnp.asarray(_dft_time(SEQ)).astype(BF16)
    dft_lat = jnp.asarray(_dft_time(DEC_SEQ)).astype(BF16)
    bdc, bds = (jnp.asarray(m).astype(BF16) for m in _dft_channel())

    cond = jnp.concatenate([c_ctx[None, :], c, jnp.zeros((N_COND - 1 - DEC_BATCH, D_MODEL), F32)], axis=0)
    mod = _ada_mod(cond, w_ada, b_ada).reshape(DEPTH, N_COND, 6, D_MODEL)

    w_in_b = w_in.astype(BF16)
    wp_b = jnp.concatenate([w_proj_hgrn, w_proj_fourier, w_proj_conv], axis=1).astype(BF16)
    wo_b = w_out.astype(BF16)
    wr_hi = w_router.astype(BF16)
    wr_lo = (w_router - wr_hi.astype(F32)).astype(BF16)
    wsg_b, wsu_b, wsd_b = w_sh_gate.astype(BF16), w_sh_up.astype(BF16), w_sh_down.astype(BF16)

    stream = (x_prompt.reshape(N_CTX, D_MODEL), x_sample.reshape(N_LAT, D_MODEL), pos)
    new_states = []
    for l in range(DEPTH):
        uh, uf, uc, ug = _in_proj(stream, mod, w_in_b, l)
        small = (lb_all[l], hgrn_norm[l][None, :], conv_w[l])
        mix_ctx, s_ctx = _mixer(uh, uf, uc, *small, dft_ctx, bdc, bds, T=SEQ, n_seq=BATCH, blk0=0)
        (mix_lat,) = _mixer(uh, uf, uc, *small, dft_lat, bdc, bds, T=DEC_SEQ, n_seq=DEC_BATCH,
                            blk0=N_CTX // DEC_SEQ, state0=state_hgrn, layer=l)
        new_states.append(s_ctx)
        x1, h2, apos, gk, n16 = _merge(stream, mix_ctx, mix_lat, ug, mod, b_gate[l][None, :], wp_b[l], wo_b[l],
                                       ln1_g[l][None, :], ln1_b[l][None, :], wr_hi[l], wr_lo[l],
                                       b_router[l][None, :], l)
        dst, n_chunks, tile_off, tiles, n_used, pad_start, pad_count = _routing_tables(
            n16.reshape(N_TILES, N_EXPERTS))
        xs = _dispatch(h2, apos, dst, n_chunks, pad_start, pad_count, n_used)
        ys = _gmm(xs, w_exp_gate, w_exp_up, w_exp_down, tile_off, tiles, n_used, l)
        stream = tuple(_combine(ys, apos, gk, h2, x1, mod, wsg_b[l], wsu_b[l], wsd_b[l], ln2_g[l][None, :],
                                ln2_b[l][None, :], dst, n_chunks, l))
    y_prompt = stream[0].reshape(BATCH, SEQ, D_MODEL)
    y_sample = stream[1].reshape(DEC_BATCH, DEC_SEQ, D_MODEL)
    return (y_prompt, y_sample, jnp.stack(new_states, axis=1))
```

```python
import functools
import math

import numpy as np
import jax
import jax.numpy as jnp
from jax import lax
from jax.experimental import pallas as pl
from jax.experimental.pallas import tpu as pltpu

F32 = jnp.float32
BF16 = jnp.bfloat16

D_MODEL = 1024
BATCH = 16
SEQ = 256
DEPTH = 2
DEC_BATCH = 4
DEC_SEQ = 1024
GRID_W = 64
HEADS = 4
HEAD_DIM = 128
HGRN_WIDTH = HEADS * HEAD_DIM
FOURIER_GROUPS = 4
FOURIER_GROUP_DIM = 64
FOURIER_WIDTH = FOURIER_GROUPS * FOURIER_GROUP_DIM
CONV_WIDTH = 256
N_BRANCHES = 3
HGRN_COLS = 5 * HGRN_WIDTH
CONV_COLS = 3 * CONV_WIDTH
GATE_COLS = N_BRANCHES * D_MODEL
IN_COLS = HGRN_COLS + FOURIER_WIDTH + CONV_COLS + GATE_COLS
N_EXPERTS = 64
TOP_K = 8
D_EXPERT = 256
D_SHARED = 256
ROUTED_SCALE = 2.5
ALPHA = (2 * DEPTH) ** 0.25
LN_EPS = 1e-6
RMS_EPS = 1e-6
F_MIN = 1e-30

N_CTX = BATCH * SEQ
N_LAT = DEC_BATCH * DEC_SEQ
N_TOK = N_CTX + N_LAT
N_COND = 8

TOK_TILE = 256
CHUNK = 64
CHUNK_UNROLL = 2
WINDOW = 120.0
STATIC_WINDOWS = 2
ADA_TILE = 1536
VMEM_LIMIT = 56 * 1024 * 1024

N_TILES = N_TOK // TOK_TILE
ROW_CHUNK = 16
TILE_CHUNKS = (TOK_TILE * TOP_K + N_EXPERTS * (ROW_CHUNK - 1)) // ROW_CHUNK
DISPATCH_ROWS = 512
TILE_ROWS = -(-TILE_CHUNKS * ROW_CHUNK // DISPATCH_ROWS) * DISPATCH_ROWS
GMM_TILE = 256
GMM_WIDTH = 4
GMM_AHEAD = 6
GMM_IN_SLOTS = GMM_WIDTH + GMM_AHEAD
GMM_OUT_SLOTS = 2 * GMM_WIDTH
CHUNKS_PER_GMM_TILE = GMM_TILE // ROW_CHUNK
MAX_GMM_TILES = (N_TILES * TILE_CHUNKS + N_EXPERTS * (CHUNKS_PER_GMM_TILE - 1)) // CHUNKS_PER_GMM_TILE
SORTED_ROWS = MAX_GMM_TILES * GMM_TILE
GATE_LANES = 128
ROW_WIDTH = D_MODEL + GATE_LANES


def _cond_row_of_tile(i, tile):
    ctx_tiles = N_CTX // tile
    per_seq = DEC_SEQ // tile
    return jnp.where(i < ctx_tiles, 0, 1 + (i - ctx_tiles) // per_seq)


def _silu(x):
    return x * jax.nn.sigmoid(x)


def _ln(x):
    mu = jnp.mean(x, axis=-1, keepdims=True)
    xc = x - mu
    var = jnp.mean(xc * xc, axis=-1, keepdims=True)
    return xc * lax.rsqrt(var + LN_EPS)


def _dot(a, b):
    return jnp.dot(a, b, preferred_element_type=F32)


def _dot_nt(a, b):
    return lax.dot_general(a, b, (((1,), (1,)), ((), ())), preferred_element_type=F32)


def _dot_tn(a, b):
    return lax.dot_general(a, b, (((0,), (0,)), ((), ())), preferred_element_type=F32)


@functools.lru_cache(maxsize=None)
def _pos_emb_table():
    rows = DEC_SEQ // GRID_W
    t = np.arange(rows * GRID_W)
    r = (t // GRID_W).astype(np.float32)
    col = (t % GRID_W).astype(np.float32)
    quarter = D_MODEL // 4
    omega = (1.0 / (np.float32(10000.0) ** (np.arange(quarter, dtype=np.float32) / np.float32(quarter)))).astype(np.float32)
    ar = (r[:, None] * omega).astype(np.float32)
    ac = (col[:, None] * omega).astype(np.float32)
    return np.concatenate([np.sin(ar), np.cos(ar), np.sin(ac), np.cos(ac)], axis=-1).astype(np.float32)


@functools.lru_cache(maxsize=None)
def _dft_time(T):
    k = np.arange(T)
    ph = 2.0 * np.pi * ((k[:, None] * k[None, :]) % T) / T
    return np.concatenate([np.cos(ph), -np.sin(ph)], axis=1).astype(np.float32)


@functools.lru_cache(maxsize=None)
def _dft_channel():
    n = FOURIER_GROUP_DIM
    k = np.arange(n)
    ph = 2.0 * np.pi * ((k[:, None] * k[None, :]) % n) / n
    eye = np.eye(FOURIER_GROUPS)
    return np.kron(eye, np.cos(ph)).astype(np.float32), np.kron(eye, np.sin(ph)).astype(np.float32)


CTX_TILES = N_CTX // TOK_TILE
LAT_TILES_PER_SEQ = DEC_SEQ // TOK_TILE


def _stream_specs(n):
    specs = [pl.BlockSpec((TOK_TILE, D_MODEL), lambda i, *_: (jnp.minimum(i, CTX_TILES - 1), 0)),
             pl.BlockSpec((TOK_TILE, D_MODEL), lambda i, *_: (jnp.maximum(i - CTX_TILES, 0), 0))]
    if n == 3:
        specs.append(pl.BlockSpec((TOK_TILE, D_MODEL),
                                  lambda i, *_: (jnp.maximum(i - CTX_TILES, 0) % LAT_TILES_PER_SEQ, 0)))
    return specs


def _stream_tile(refs):
    latent = refs[1][...]
    if len(refs) == 3:
        latent = latent + refs[2][...]
    return jnp.where(pl.program_id(0) < CTX_TILES, refs[0][...], latent)


def _ada_kernel(c_ref, w_ref, b_ref, o_ref):
    s = _silu(c_ref[...]).astype(BF16)
    o_ref[...] = _dot(s, w_ref[...].astype(BF16)) + b_ref[...]


def _ada_mod(cond, w_ada, b_ada):
    n_col = 6 * D_MODEL
    return pl.pallas_call(
        _ada_kernel,
        out_shape=jax.ShapeDtypeStruct((DEPTH, N_COND, n_col), F32),
        grid=(DEPTH, n_col // ADA_TILE),
        in_specs=[
            pl.BlockSpec((N_COND, D_MODEL), lambda l, j: (0, 0)),
            pl.BlockSpec((None, D_MODEL, ADA_TILE), lambda l, j: (l, 0, j)),
            pl.BlockSpec((None, 1, ADA_TILE), lambda l, j: (l, 0, j)),
        ],
        out_specs=pl.BlockSpec((None, N_COND, ADA_TILE), lambda l, j: (l, 0, j)),
        compiler_params=pltpu.CompilerParams(dimension_semantics=("arbitrary", "arbitrary"),
                                             vmem_limit_bytes=VMEM_LIMIT),
        name="ada_mod",
    )(cond, w_ada, b_ada.reshape(DEPTH, 1, n_col))


def _in_proj_kernel(*refs, n_stream):
    mod_ref, w_ref, uh_ref, uf_ref, uc_ref, ug_ref = refs[n_stream:]
    sh1 = mod_ref[0:1, :]
    sc1 = mod_ref[1:2, :]
    h = (_ln(_stream_tile(refs[:n_stream])) * (1.0 + sc1) + sh1).astype(BF16)
    c0 = 0
    for ref, n in ((uh_ref, HGRN_COLS), (uf_ref, FOURIER_WIDTH), (uc_ref, CONV_COLS), (ug_ref, GATE_COLS)):
        ref[...] = _dot(h, w_ref[:, c0:c0 + n])
        c0 += n


def _in_proj(stream, mod, w_in_bf16, layer):
    n_tiles = N_TOK // TOK_TILE
    widths = (HGRN_COLS, FOURIER_WIDTH, CONV_COLS, GATE_COLS)
    return pl.pallas_call(
        functools.partial(_in_proj_kernel, n_stream=len(stream)),
        out_shape=[jax.ShapeDtypeStruct((N_TOK, n), F32) for n in widths],
        grid=(n_tiles,),
        in_specs=_stream_specs(len(stream)) + [
            pl.BlockSpec((None, None, 6, D_MODEL), lambda i: (layer, _cond_row_of_tile(i, TOK_TILE), 0, 0)),
            pl.BlockSpec((None, D_MODEL, IN_COLS), lambda i: (layer, 0, 0)),
        ],
        out_specs=[pl.BlockSpec((TOK_TILE, n), lambda i: (i, 0)) for n in widths],
        compiler_params=pltpu.CompilerParams(dimension_semantics=("arbitrary",), vmem_limit_bytes=VMEM_LIMIT),
        name="in_proj",
    )(*stream, mod, w_in_bf16)


def _mixer_kernel(*refs, T, has_state):
    if has_state:
        (uh_ref, uf_ref, uc_ref, lb_ref, ng_ref, cw_ref, dft_ref, bdc_ref, bds_ref, s0_ref,
         mix_ref, st_ref, o_ref) = refs
        sfin_ref = None
    else:
        (uh_ref, uf_ref, uc_ref, lb_ref, ng_ref, cw_ref, dft_ref, bdc_ref, bds_ref,
         mix_ref, sfin_ref, st_ref, o_ref) = refs
        s0_ref = None
    n_chunks = T // CHUNK
    W = HGRN_WIDTH

    for d in range(2):
        for h in range(HEADS):
            if has_state:
                st_ref[d, h] = s0_ref[0, d, h].T
            else:
                st_ref[d, h] = jnp.zeros((HEAD_DIM, HEAD_DIM), F32)
    o_ref[...] = jnp.zeros_like(o_ref)

    row = lax.broadcasted_iota(jnp.int32, (CHUNK, CHUNK), 0)
    col = lax.broadcasted_iota(jnp.int32, (CHUNK, CHUNK), 1)
    keep = (col <= row, col >= row)
    tri = tuple(k.astype(F32).astype(BF16) for k in keep)

    def chunk_step(i, carry):
        deep = []
        for sub, d in ((s, d) for s in range(CHUNK_UNROLL) for d in range(2)):
            c = i * CHUNK_UNROLL + sub
            c = c if d == 0 else n_chunks - 1 - c
            rows = pl.ds(pl.multiple_of(c * CHUNK, CHUNK), CHUNK)
            q = _silu(uh_ref[rows, 0:W])
            v = uh_ref[rows, W:2 * W].astype(BF16)
            z = uh_ref[rows, (2 + d) * W:(3 + d) * W]
            lb = lb_ref[d:d + 1, :]
            e = jnp.exp(-jnp.abs(z))
            r = 1.0 / (1.0 + e)
            er = e * r
            pos = z >= 0.0
            sig_p = jnp.where(pos, r, er)
            sig_n = jnp.where(pos, er, r)
            f = lb + (1.0 - lb) * sig_p
            lf = jnp.log(jnp.maximum(f, F_MIN))
            k = (1.0 - lb) * sig_n
            lf_hi = lf.astype(BF16)
            lf_lo = (lf - lf_hi.astype(F32)).astype(BF16)
            b = _dot(tri[d], lf_hi) + _dot(tri[d], lf_lo)
            g = b[CHUNK - 1:CHUNK, :] if d == 0 else b[0:1, :]

            def window(p_level, q=q, k=k, b=b):
                dist = -p_level - b
                inside = jnp.logical_and(dist >= 0.0, dist < WINDOW)
                k_w = jnp.where(inside, k * jnp.exp(jnp.minimum(dist, WINDOW) - 0.5 * WINDOW), 0.0)
                q_w = q * jnp.exp(jnp.minimum(-dist, 0.0) + 0.5 * WINDOW)
                return q_w.astype(BF16), k_w.astype(BF16)

            wins = [window(WINDOW * p) for p in range(STATIC_WINDOWS)]
            q_abs = (q * jnp.exp(b)).astype(BF16)
            k_end = (k * jnp.exp(g - b)).astype(BF16)
            decay = jnp.exp(g)
            outs = []
            for h in range(HEADS):
                ls = slice(h * HEAD_DIM, (h + 1) * HEAD_DIM)
                q_cat = jnp.concatenate([w[0][:, ls] for w in wins], axis=-1)
                k_cat = jnp.concatenate([w[1][:, ls] for w in wins], axis=-1)
                a = jnp.where(keep[d], _dot_nt(q_cat, k_cat), 0.0).astype(BF16)
                s_t = st_ref[d, h]
                o_h = _dot(a, v[:, ls]) + _dot_nt(q_abs[:, ls], s_t.astype(BF16))
                st_ref[d, h] = s_t * decay[:, ls] + _dot_tn(v[:, ls], k_end[:, ls])
                outs.append(o_h)
            o_ref[rows, :] += jnp.concatenate(outs, axis=-1)
            deep.append((d, window, v, rows, jnp.min(b)))

        deepest = functools.reduce(jnp.minimum, [entry[4] for entry in deep])
        n_windows = jnp.floor(-deepest * (1.0 / WINDOW)).astype(jnp.int32) + 1

        def extra_window(p, carry2):
            for d, window, v, rows, _ in deep:
                q_w, k_w = window(WINDOW * p.astype(F32))
                extra = []
                for h in range(HEADS):
                    ls = slice(h * HEAD_DIM, (h + 1) * HEAD_DIM)
                    a = jnp.where(keep[d], _dot_nt(q_w[:, ls], k_w[:, ls]), 0.0).astype(BF16)
                    extra.append(_dot(a, v[:, ls]))
                o_ref[rows, :] += jnp.concatenate(extra, axis=-1)
            return carry2

        lax.fori_loop(STATIC_WINDOWS, n_windows, extra_window, 0)
        return carry

    lax.fori_loop(0, n_chunks // CHUNK_UNROLL, chunk_step, 0)

    if sfin_ref is not None:
        for d in range(2):
            for h in range(HEADS):
                sfin_ref[0, d, h] = st_ref[d, h].T

    o = o_ref[...]
    parts = []
    for h in range(HEADS):
        oh = o[:, h * HEAD_DIM:(h + 1) * HEAD_DIM]
        parts.append(oh * lax.rsqrt(jnp.mean(oh * oh, axis=-1, keepdims=True) + RMS_EPS))
    a_out = jnp.concatenate(parts, axis=-1) * ng_ref[...] * _silu(uh_ref[:, 4 * W:5 * W])
    mix_ref[:, 0:W] = a_out.astype(BF16)

    zf = uf_ref[...].astype(BF16)
    zc = _dot(zf, bdc_ref[...]).astype(BF16)
    zs = _dot(zf, bds_ref[...]).astype(BF16)
    four = _dot(dft_ref[...], jnp.concatenate([zc, zs], axis=0)) * (1.0 / math.sqrt(T * FOURIER_GROUP_DIM))
    mix_ref[:, W:W + FOURIER_WIDTH] = four.astype(BF16)

    cb = uc_ref[:, 0:CONV_WIDTH]
    zz = uc_ref[:, CONV_WIDTH:2 * CONV_WIDTH] * uc_ref[:, 2 * CONV_WIDTH:3 * CONV_WIDTH]
    t_idx = lax.broadcasted_iota(jnp.int32, (T, CONV_WIDTH), 0)
    z_prev = jnp.where(t_idx == 0, 0.0, pltpu.roll(zz, 1, axis=0))
    z_next = jnp.where(t_idx == T - 1, 0.0, pltpu.roll(zz, T - 1, axis=0))
    y = cw_ref[0:1, :] * z_prev + cw_ref[1:2, :] * zz + cw_ref[2:3, :] * z_next
    mix_ref[:, W + FOURIER_WIDTH:] = (cb * y).astype(BF16)


def _mixer(uh, uf, uc, lb, norm_g, conv_w, dft, bdc, bds, T, n_seq, blk0, state0=None, layer=0):
    has_state = state0 is not None
    const2 = lambda s: (0, 0)
    in_specs = [
        pl.BlockSpec((T, HGRN_COLS), lambda s: (blk0 + s, 0)),
        pl.BlockSpec((T, FOURIER_WIDTH), lambda s: (blk0 + s, 0)),
        pl.BlockSpec((T, CONV_COLS), lambda s: (blk0 + s, 0)),
        pl.BlockSpec((2, HGRN_WIDTH), const2),
        pl.BlockSpec((1, HGRN_WIDTH), const2),
        pl.BlockSpec((3, CONV_WIDTH), const2),
        pl.BlockSpec((T, 2 * T), const2),
        pl.BlockSpec((FOURIER_WIDTH, FOURIER_WIDTH), const2),
        pl.BlockSpec((FOURIER_WIDTH, FOURIER_WIDTH), const2),
    ]
    args = [uh, uf, uc, lb, norm_g, conv_w, dft, bdc, bds]
    mix_shape = jax.ShapeDtypeStruct((n_seq * T, D_MODEL), BF16)
    mix_spec = pl.BlockSpec((T, D_MODEL), lambda s: (s, 0))
    st_block = (1, 2, HEADS, HEAD_DIM, HEAD_DIM)
    if has_state:
        in_specs.append(pl.BlockSpec((1, None) + st_block[1:], lambda s: (s, layer, 0, 0, 0, 0)))
        args.append(state0)
        out_shape = [mix_shape]
        out_specs = [mix_spec]
    else:
        out_shape = [mix_shape, jax.ShapeDtypeStruct((n_seq, 2, HEADS, HEAD_DIM, HEAD_DIM), F32)]
        out_specs = [mix_spec, pl.BlockSpec(st_block, lambda s: (s, 0, 0, 0, 0))]
    return pl.pallas_call(
        functools.partial(_mixer_kernel, T=T, has_state=has_state),
        out_shape=out_shape,
        grid=(n_seq,),
        in_specs=in_specs,
        out_specs=out_specs,
        scratch_shapes=[pltpu.VMEM((2, HEADS, HEAD_DIM, HEAD_DIM), F32), pltpu.VMEM((T, HGRN_WIDTH), F32)],
        compiler_params=pltpu.CompilerParams(dimension_semantics=("arbitrary",), vmem_limit_bytes=VMEM_LIMIT),
        name="mixer_latent" if has_state else "mixer_context",
    )(*args)


def _merge_kernel(*refs, n_stream):
    (mixc_ref, mixl_ref, ug_ref, mod_ref, bg_ref, wp_ref, wo_ref, g1_ref, b1_ref, wrh_ref, wrl_ref, br_ref,
     x1_ref, h2_ref, apos_ref, gate_ref, n16_ref) = refs[n_stream:]
    W = HGRN_WIDTH
    gates = jax.nn.sigmoid(ug_ref[...] + bg_ref[...])
    mix = _stream_tile((mixc_ref, mixl_ref))
    edges = (0, W, W + FOURIER_WIDTH, D_MODEL)
    merged = None
    for j in range(N_BRANCHES):
        p = _dot(mix[:, edges[j]:edges[j + 1]], wp_ref[edges[j]:edges[j + 1], :])
        term = gates[:, j * D_MODEL:(j + 1) * D_MODEL] * p
        merged = term if merged is None else merged + term
    y = _dot(merged.astype(BF16), wo_ref[...])
    x1 = _ln(ALPHA * _stream_tile(refs[:n_stream]) + mod_ref[2:3, :] * y) * g1_ref[...] + b1_ref[...]
    x1_ref[...] = x1
    h2 = _ln(x1) * (1.0 + mod_ref[4:5, :]) + mod_ref[3:4, :]
    h2_hi = h2.astype(BF16)
    h2_ref[...] = h2_hi
    h2_lo = (h2 - h2_hi.astype(F32)).astype(BF16)
    logits = _dot(h2_hi, wrh_ref[...]) + (_dot(h2_hi, wrl_ref[...]) + _dot(h2_lo, wrh_ref[...]))
    scores = jax.nn.sigmoid(logits)
    sel = scores + br_ref[...]
    lane_f = lax.broadcasted_iota(jnp.int32, sel.shape, 1).astype(F32)
    hits = []
    for _ in range(TOP_K):
        top = jnp.max(sel, axis=-1, keepdims=True)
        first = jnp.min(jnp.where(sel == top, lane_f, float(N_EXPERTS)), axis=-1, keepdims=True)
        hit = lane_f == first
        hits.append(hit)
        sel = jnp.where(hit, -jnp.inf, sel)
    chosen = functools.reduce(jnp.logical_or, hits)
    picked = jnp.where(chosen, scores, 0.0)
    denom = jnp.sum(picked, axis=-1, keepdims=True) + 1e-20
    gate = ROUTED_SCALE * picked / denom

    chosen_f = chosen.astype(F32)
    r_i = lax.broadcasted_iota(jnp.int32, (TOK_TILE, TOK_TILE), 0)
    c_i = lax.broadcasted_iota(jnp.int32, (TOK_TILE, TOK_TILE), 1)
    earlier = (c_i < r_i).astype(F32).astype(BF16)
    rank = _dot(earlier, chosen_f.astype(BF16))
    count = jnp.sum(chosen_f, axis=0, keepdims=True)
    n16 = jnp.floor((count + (ROW_CHUNK - 1)) * (1.0 / ROW_CHUNK))
    e_r = lax.broadcasted_iota(jnp.int32, (N_EXPERTS, N_EXPERTS), 0)
    e_c = lax.broadcasted_iota(jnp.int32, (N_EXPERTS, N_EXPERTS), 1)
    before = (e_r < e_c).astype(F32).astype(BF16)
    seg_start = ROW_CHUNK * _dot(jnp.broadcast_to(n16, (8, N_EXPERTS)).astype(BF16), before)[0:1, :]
    row_of = seg_start + rank
    lane_k = lax.broadcasted_iota(jnp.int32, (TOK_TILE, TOP_K), 1)
    apos = jnp.zeros((TOK_TILE, TOP_K), F32)
    for k, hit in enumerate(hits):
        apos = jnp.where(lane_k == k, jnp.sum(jnp.where(hit, row_of, 0.0), axis=-1, keepdims=True), apos)
    apos_ref[...] = apos
    gate_ref[...] = jnp.concatenate([gate, jnp.zeros((TOK_TILE, GATE_LANES - N_EXPERTS), F32)], axis=-1).astype(BF16)
    n16_ref[...] = n16


def _merge(stream, mix_ctx, mix_lat, ug, mod, b_gate, wp, wo, ln_g, ln_b, wr_hi, wr_lo, b_router, layer):
    tok = lambda n: pl.BlockSpec((TOK_TILE, n), lambda i: (i, 0))
    const2 = lambda i: (0, 0)
    return pl.pallas_call(
        functools.partial(_merge_kernel, n_stream=len(stream)),
        out_shape=[jax.ShapeDtypeStruct((N_TOK, D_MODEL), F32), jax.ShapeDtypeStruct((N_TOK, D_MODEL), BF16),
                   jax.ShapeDtypeStruct((N_TOK, TOP_K), F32), jax.ShapeDtypeStruct((N_TOK, GATE_LANES), BF16),
                   jax.ShapeDtypeStruct((N_TILES, 1, N_EXPERTS), F32)],
        grid=(N_TILES,),
        in_specs=_stream_specs(len(stream)) + _stream_specs(2) + [
            tok(GATE_COLS),
            pl.BlockSpec((None, None, 6, D_MODEL), lambda i: (layer, _cond_row_of_tile(i, TOK_TILE), 0, 0)),
            pl.BlockSpec((1, GATE_COLS), const2),
            pl.BlockSpec((D_MODEL, D_MODEL), const2),
            pl.BlockSpec((D_MODEL, D_MODEL), const2),
            pl.BlockSpec((1, D_MODEL), const2),
            pl.BlockSpec((1, D_MODEL), const2),
            pl.BlockSpec((D_MODEL, N_EXPERTS), const2),
            pl.BlockSpec((D_MODEL, N_EXPERTS), const2),
            pl.BlockSpec((1, N_EXPERTS), const2),
        ],
        out_specs=[tok(D_MODEL), tok(D_MODEL), tok(TOP_K), tok(GATE_LANES),
                   pl.BlockSpec((None, 1, N_EXPERTS), lambda i: (i, 0, 0))],
        compiler_params=pltpu.CompilerParams(dimension_semantics=("arbitrary",), vmem_limit_bytes=VMEM_LIMIT),
        name="merge_router",
    )(*stream, mix_ctx, mix_lat, ug, mod, b_gate, wp, wo, ln_g, ln_b, wr_hi, wr_lo, b_router)


def _routing_tables(n16):
    n16 = n16.astype(jnp.int32)
    hi = jnp.cumsum(n16, axis=1)
    lo = hi - n16
    n_chunks = hi[:, -1]
    eo = jnp.cumsum(n16, axis=0) - n16
    tc = jnp.sum(n16, axis=0)
    tiles = (tc + CHUNKS_PER_GMM_TILE - 1) // CHUNKS_PER_GMM_TILE
    tile_end = jnp.cumsum(tiles)
    tile_off = tile_end - tiles
    n_used = tile_end[-1]
    c = jnp.arange(TILE_CHUNKS, dtype=jnp.int32)[None, :, None]
    owner = jnp.logical_and(c >= lo[:, None, :], c < hi[:, None, :])
    shift = CHUNKS_PER_GMM_TILE * tile_off[None, :] + eo - lo
    dst = c[:, :, 0] + jnp.sum(jnp.where(owner, shift[:, None, :], 0), axis=-1)
    pad_start = CHUNKS_PER_GMM_TILE * tile_off + tc
    pad_count = CHUNKS_PER_GMM_TILE * tiles - tc
    return (dst.reshape(-1).astype(jnp.int32), n_chunks, tile_off.astype(jnp.int32), tiles.astype(jnp.int32),
            jnp.reshape(n_used, (1,)).astype(jnp.int32), pad_start.astype(jnp.int32), pad_count.astype(jnp.int32))


def _one_hot_rows(apos_ref):
    r = lax.broadcasted_iota(jnp.int32, (TOK_TILE, TILE_ROWS), 1).astype(F32)
    pt = jnp.zeros((TOK_TILE, TILE_ROWS), F32)
    for k in range(TOP_K):
        pt = jnp.where(r == apos_ref[:, k:k + 1], 1.0, pt)
    return pt.astype(BF16)


def _chunk_copy(src_ref, src_chunk, dst_ref, dst_chunk, sem):
    rows = lambda c: pl.ds(pl.multiple_of(c * ROW_CHUNK, ROW_CHUNK), ROW_CHUNK)
    return pltpu.make_async_copy(src_ref.at[rows(src_chunk), :], dst_ref.at[rows(dst_chunk), :], sem)


def _dispatch_kernel(dst_ref, nch_ref, pstart_ref, pcount_ref, nu_ref, h_ref, gate_ref, apos_ref, xs_hbm, pt_ref,
                     stage_ref, zero_ref, sem):
    j = pl.program_id(0)
    buf = j % 2

    def wait_chunks(b, n):
        def body(c, carry):
            _chunk_copy(stage_ref.at[b], 0, xs_hbm, 0, sem.at[b]).wait()
            return carry
        lax.fori_loop(0, n, body, 0)

    @pl.when(j < N_TILES)
    def _():
        @pl.when(j >= 2)
        def _():
            wait_chunks(buf, nch_ref[j - 2])

        pt = _one_hot_rows(apos_ref)
        pt_ref[...] = pt
        h = h_ref[...]
        gate = gate_ref[...]
        for r0 in range(0, TILE_ROWS, DISPATCH_ROWS):
            group = pt[:, r0:r0 + DISPATCH_ROWS]
            stage_ref[buf, r0:r0 + DISPATCH_ROWS, 0:D_MODEL] = _dot_tn(group, h).astype(BF16)
            stage_ref[buf, r0:r0 + DISPATCH_ROWS, D_MODEL:ROW_WIDTH] = _dot_tn(group, gate).astype(BF16)

        def send(c, carry):
            _chunk_copy(stage_ref.at[buf], c, xs_hbm, dst_ref[j * TILE_CHUNKS + c], sem.at[buf]).start()
            return carry
        lax.fori_loop(0, nch_ref[j], send, 0)

    @pl.when(j == N_TILES)
    def _():
        zero_ref[...] = jnp.zeros_like(zero_ref)

        def tile_copy(t):
            rows = pl.ds(pl.multiple_of(t * GMM_TILE, GMM_TILE), GMM_TILE)
            return pltpu.make_async_copy(zero_ref, xs_hbm.at[rows, :], sem.at[3])

        def per_expert(e, total):
            def fill(i, carry):
                _chunk_copy(zero_ref, 0, xs_hbm, pstart_ref[e] + i, sem.at[2]).start()
                return carry
            lax.fori_loop(0, pcount_ref[e], fill, 0)
            return total + pcount_ref[e]
        n_pad = lax.fori_loop(0, N_EXPERTS, per_expert, 0)

        def fill_tile(t, carry):
            tile_copy(t).start()
            return carry
        lax.fori_loop(nu_ref[0], MAX_GMM_TILES, fill_tile, 0)
        wait_chunks(0, nch_ref[N_TILES - 2])
        wait_chunks(1, nch_ref[N_TILES - 1])

        def drain(i, carry):
            _chunk_copy(zero_ref, 0, xs_hbm, 0, sem.at[2]).wait()
            return carry
        lax.fori_loop(0, n_pad, drain, 0)

        def drain_tile(t, carry):
            tile_copy(0).wait()
            return carry
        lax.fori_loop(nu_ref[0], MAX_GMM_TILES, drain_tile, 0)


def _dispatch(h2, gate, apos, dst, n_chunks, pad_start, pad_count, n_used):
    last = N_TILES - 1
    tok = lambda n: pl.BlockSpec((TOK_TILE, n), lambda j, *_: (jnp.minimum(j, last), 0))
    grid_spec = pltpu.PrefetchScalarGridSpec(
        num_scalar_prefetch=5,
        grid=(N_TILES + 1,),
        in_specs=[tok(D_MODEL), tok(GATE_LANES), tok(TOP_K)],
        out_specs=[pl.BlockSpec(memory_space=pl.ANY), tok(TILE_ROWS)],
        scratch_shapes=[pltpu.VMEM((2, TILE_ROWS, ROW_WIDTH), BF16), pltpu.VMEM((GMM_TILE, ROW_WIDTH), BF16),
                        pltpu.SemaphoreType.DMA((4,))],
    )
    return pl.pallas_call(
        _dispatch_kernel,
        out_shape=[jax.ShapeDtypeStruct((SORTED_ROWS, ROW_WIDTH), BF16),
                   jax.ShapeDtypeStruct((N_TOK, TILE_ROWS), BF16)],
        grid_spec=grid_spec,
        compiler_params=pltpu.CompilerParams(dimension_semantics=("arbitrary",), vmem_limit_bytes=VMEM_LIMIT),
        name="moe_dispatch",
    )(dst, n_chunks, pad_start, pad_count, n_used, h2, gate, apos)


def _gmm_kernel(off_ref, cnt_ref, nu_ref, x_hbm, wg_ref, wu_ref, wd_ref, y_hbm, xbuf, ybuf, wg_b, wu_b, wd_b,
                in_sem, out_sem):
    e = pl.program_id(0)
    n_used = nu_ref[0]

    def rows(g):
        return pl.ds(pl.multiple_of(g * GMM_TILE, GMM_TILE), GMM_TILE)

    def fetch(g):
        slot = g % GMM_IN_SLOTS
        return pltpu.make_async_copy(x_hbm.at[rows(g), :], xbuf.at[slot], in_sem.at[slot])

    def flush(g):
        slot = g % GMM_OUT_SLOTS
        return pltpu.make_async_copy(ybuf.at[slot], y_hbm.at[rows(g), 0:D_MODEL], out_sem.at[slot])

    @pl.when(e == 0)
    def _():
        for g in range(GMM_AHEAD):
            @pl.when(g < n_used)
            def _(g=g):
                fetch(g).start()

    wg_b[...] = wg_ref[...].astype(BF16)
    wu_b[...] = wu_ref[...].astype(BF16)
    wd_b[...] = wd_ref[...].astype(BF16)

    def process(g0, width):
        for w in range(width):
            @pl.when(g0 + w + GMM_AHEAD < n_used)
            def _(w=w):
                fetch(g0 + w + GMM_AHEAD).start()
        for w in range(width):
            fetch(g0 + w).wait()
        xg = jnp.concatenate([xbuf[(g0 + w) % GMM_IN_SLOTS] for w in range(width)], axis=0)
        x = xg[:, 0:D_MODEL]
        gates = xg[:, D_MODEL:ROW_WIDTH].astype(F32)
        lane = lax.broadcasted_iota(jnp.int32, gates.shape, 1)
        weight = jnp.sum(jnp.where(lane == e, gates, 0.0), axis=-1, keepdims=True)
        hid = _silu(_dot(x, wg_b[...])) * _dot(x, wu_b[...])
        y = (_dot(hid.astype(BF16), wd_b[...]) * weight).astype(BF16)
        for w in range(width):
            g = g0 + w

            @pl.when(g >= GMM_OUT_SLOTS)
            def _(g=g):
                flush(g - GMM_OUT_SLOTS).wait()

            ybuf[g % GMM_OUT_SLOTS] = y[w * GMM_TILE:(w + 1) * GMM_TILE]
            flush(g).start()

    first = off_ref[e]
    count = cnt_ref[e]

    done = 0
    width = GMM_WIDTH
    while width >= 1:
        trips = (count - done) // width

        def block(p, carry, done=done, width=width):
            process(first + done + width * p, width)
            return carry

        lax.fori_loop(0, trips, block, 0)
        done = done + trips * width
        width //= 2

    @pl.when(e == N_EXPERTS - 1)
    def _():
        for k in range(GMM_OUT_SLOTS):
            @pl.when(n_used - 1 - k >= 0)
            def _(k=k):
                flush(n_used - 1 - k).wait()


def _gmm(xs, weg, weu, wed, tile_off, tiles, n_used, layer):
    w_in_spec = pl.BlockSpec((None, None, D_MODEL, D_EXPERT), lambda e, *_: (layer, e, 0, 0))
    grid_spec = pltpu.PrefetchScalarGridSpec(
        num_scalar_prefetch=3,
        grid=(N_EXPERTS,),
        in_specs=[
            pl.BlockSpec(memory_space=pl.ANY),
            w_in_spec, w_in_spec,
            pl.BlockSpec((None, None, D_EXPERT, D_MODEL), lambda e, *_: (layer, e, 0, 0)),
        ],
        out_specs=pl.BlockSpec(memory_space=pl.ANY),
        scratch_shapes=[pltpu.VMEM((GMM_IN_SLOTS, GMM_TILE, ROW_WIDTH), BF16),
                        pltpu.VMEM((GMM_OUT_SLOTS, GMM_TILE, D_MODEL), BF16),
                        pltpu.VMEM((D_MODEL, D_EXPERT), BF16), pltpu.VMEM((D_MODEL, D_EXPERT), BF16),
                        pltpu.VMEM((D_EXPERT, D_MODEL), BF16),
                        pltpu.SemaphoreType.DMA((GMM_IN_SLOTS,)), pltpu.SemaphoreType.DMA((GMM_OUT_SLOTS,))],
    )
    return pl.pallas_call(
        _gmm_kernel,
        out_shape=jax.ShapeDtypeStruct((SORTED_ROWS, ROW_WIDTH), BF16),
        grid_spec=grid_spec,
        input_output_aliases={3: 0},
        compiler_params=pltpu.CompilerParams(dimension_semantics=("arbitrary",), vmem_limit_bytes=VMEM_LIMIT),
        name="moe_gmm",
    )(tile_off, tiles, n_used, xs, weg, weu, wed)


def _combine_kernel(dst_ref, nch_ref, ys_hbm, pt_ref, h_ref, x1_ref, mod_ref, wsg_ref, wsu_ref, wsd_ref,
                    g2_ref, b2_ref, oc_ref, ol_ref, stage_ref, sem):
    j = pl.program_id(0)
    buf = j % 2

    def chunk_fetch(slot, b, c):
        rows = lambda k: pl.ds(pl.multiple_of(k * ROW_CHUNK, ROW_CHUNK), ROW_CHUNK)
        return pltpu.make_async_copy(ys_hbm.at[rows(slot), 0:D_MODEL], stage_ref.at[b, rows(c), :], sem.at[b])

    def fetch(jj, b):
        def body(c, carry):
            chunk_fetch(dst_ref[jj * TILE_CHUNKS + c], b, c).start()
            return carry
        lax.fori_loop(0, nch_ref[jj], body, 0)

    @pl.when(j == 0)
    def _():
        stage_ref[...] = jnp.zeros_like(stage_ref)
        fetch(0, 0)

    @pl.when(j + 1 < N_TILES)
    def _():
        fetch(j + 1, 1 - buf)

    h = h_ref[...]
    hid = _silu(_dot(h, wsg_ref[...])) * _dot(h, wsu_ref[...])
    shared = _dot(hid.astype(BF16), wsd_ref[...])

    def wait(c, carry):
        chunk_fetch(0, buf, 0).wait()
        return carry
    lax.fori_loop(0, nch_ref[j], wait, 0)

    routed = _dot(pt_ref[...], stage_ref[buf])
    r = ALPHA * x1_ref[...] + mod_ref[5:6, :] * (routed + shared)
    out = _ln(r) * g2_ref[...] + b2_ref[...]

    @pl.when(j < CTX_TILES)
    def _():
        oc_ref[...] = out

    @pl.when(j >= CTX_TILES)
    def _():
        ol_ref[...] = out


def _combine(ys, pt, h2, x1, mod, wsg, wsu, wsd, ln_g, ln_b, dst, n_chunks, layer):
    tok = lambda n: pl.BlockSpec((TOK_TILE, n), lambda j, *_: (j, 0))
    const2 = lambda j, *_: (0, 0)
    grid_spec = pltpu.PrefetchScalarGridSpec(
        num_scalar_prefetch=2,
        grid=(N_TILES,),
        in_specs=[
            pl.BlockSpec(memory_space=pl.ANY),
            tok(TILE_ROWS), tok(D_MODEL), tok(D_MODEL),
            pl.BlockSpec((None, None, 6, D_MODEL), lambda j, *_: (layer, _cond_row_of_tile(j, TOK_TILE), 0, 0)),
            pl.BlockSpec((D_MODEL, D_SHARED), const2),
            pl.BlockSpec((D_MODEL, D_SHARED), const2),
            pl.BlockSpec((D_SHARED, D_MODEL), const2),
            pl.BlockSpec((1, D_MODEL), const2),
            pl.BlockSpec((1, D_MODEL), const2),
        ],
        out_specs=_stream_specs(2),
        scratch_shapes=[pltpu.VMEM((2, TILE_ROWS, D_MODEL), BF16), pltpu.SemaphoreType.DMA((2,))],
    )
    return pl.pallas_call(
        _combine_kernel,
        out_shape=[jax.ShapeDtypeStruct((N_CTX, D_MODEL), F32), jax.ShapeDtypeStruct((N_LAT, D_MODEL), F32)],
        grid_spec=grid_spec,
        compiler_params=pltpu.CompilerParams(dimension_semantics=("arbitrary",), vmem_limit_bytes=VMEM_LIMIT),
        name="moe_combine",
    )(dst, n_chunks, ys, pt, h2, x1, mod, wsg, wsu, wsd, ln_g, ln_b)


def kernel(x_prompt, x_sample, c, state_hgrn, c_ctx, w_ada, b_ada, w_in, b_gate, hgrn_lb, hgrn_norm, conv_w,
           w_proj_hgrn, w_proj_fourier, w_proj_conv, w_out, ln1_g, ln1_b, ln2_g, ln2_b, w_router, b_router,
           w_exp_gate, w_exp_up, w_exp_down, w_sh_gate, w_sh_up, w_sh_down):
    lb_sm = jax.nn.softmax(hgrn_lb.astype(F32), axis=0)
    lb_all = jnp.cumsum(lb_sm, axis=0) - lb_sm[:1]

    pos = jnp.asarray(_pos_emb_table())
    dft_ctx = jnp.asarray(_dft_time(SEQ)).astype(BF16)
    dft_lat = jnp.asarray(_dft_time(DEC_SEQ)).astype(BF16)
    bdc, bds = (jnp.asarray(m).astype(BF16) for m in _dft_channel())

    cond = jnp.concatenate([c_ctx[None, :], c, jnp.zeros((N_COND - 1 - DEC_BATCH, D_MODEL), F32)], axis=0)
    mod = _ada_mod(cond, w_ada, b_ada).reshape(DEPTH, N_COND, 6, D_MODEL)

    w_in_b = w_in.astype(BF16)
    wp_b = jnp.concatenate([w_proj_hgrn, w_proj_fourier, w_proj_conv], axis=1).astype(BF16)
    wo_b = w_out.astype(BF16)
    wr_hi = w_router.astype(BF16)
    wr_lo = (w_router - wr_hi.astype(F32)).astype(BF16)
    wsg_b, wsu_b, wsd_b = w_sh_gate.astype(BF16), w_sh_up.astype(BF16), w_sh_down.astype(BF16)

    stream = (x_prompt.reshape(N_CTX, D_MODEL), x_sample.reshape(N_LAT, D_MODEL), pos)
    new_states = []
    for l in range(DEPTH):
        uh, uf, uc, ug = _in_proj(stream, mod, w_in_b, l)
        small = (lb_all[l], hgrn_norm[l][None, :], conv_w[l])
        mix_ctx, s_ctx = _mixer(uh, uf, uc, *small, dft_ctx, bdc, bds, T=SEQ, n_seq=BATCH, blk0=0)
        (mix_lat,) = _mixer(uh, uf, uc, *small, dft_lat, bdc, bds, T=DEC_SEQ, n_seq=DEC_BATCH,
                            blk0=N_CTX // DEC_SEQ, state0=state_hgrn, layer=l)
        new_states.append(s_ctx)
        x1, h2, apos, gate, n16 = _merge(stream, mix_ctx, mix_lat, ug, mod, b_gate[l][None, :], wp_b[l], wo_b[l],
                                         ln1_g[l][None, :], ln1_b[l][None, :], wr_hi[l], wr_lo[l],
                                         b_router[l][None, :], l)
        dst, n_chunks, tile_off, tiles, n_used, pad_start, pad_count = _routing_tables(
            n16.reshape(N_TILES, N_EXPERTS))
        xs, pt = _dispatch(h2, gate, apos, dst, n_chunks, pad_start, pad_count, n_used)
        ys = _gmm(xs, w_exp_gate, w_exp_up, w_exp_down, tile_off, tiles, n_used, l)
        stream = tuple(_combine(ys, pt, h2, x1, mod, wsg_b[l], wsu_b[l], wsd_b[l], ln2_g[l][None, :],
                                ln2_b[l][None, :], dst, n_chunks, l))
    y_prompt = stream[0].reshape(BATCH, SEQ, D_MODEL)
    y_sample = stream[1].reshape(DEC_BATCH, DEC_SEQ, D_MODEL)
    return (y_prompt, y_sample, jnp.stack(new_states, axis=1))
```

```python
import functools
import math

import numpy as np
import jax
import jax.numpy as jnp
from jax import lax
from jax.experimental import pallas as pl
from jax.experimental.pallas import tpu as pltpu

F32 = jnp.float32
BF16 = jnp.bfloat16

D_MODEL = 1024
BATCH = 16
SEQ = 256
DEPTH = 2
DEC_BATCH = 4
DEC_SEQ = 1024
GRID_W = 64
HEADS = 4
HEAD_DIM = 128
HGRN_WIDTH = HEADS * HEAD_DIM
FOURIER_GROUPS = 4
FOURIER_GROUP_DIM = 64
FOURIER_WIDTH = FOURIER_GROUPS * FOURIER_GROUP_DIM
CONV_WIDTH = 256
N_BRANCHES = 3
HGRN_COLS = 5 * HGRN_WIDTH
CONV_COLS = 3 * CONV_WIDTH
GATE_COLS = N_BRANCHES * D_MODEL
IN_COLS = HGRN_COLS + FOURIER_WIDTH + CONV_COLS + GATE_COLS
N_EXPERTS = 64
TOP_K = 8
D_EXPERT = 256
D_SHARED = 256
ROUTED_SCALE = 2.5
ALPHA = (2 * DEPTH) ** 0.25
LN_EPS = 1e-6
RMS_EPS = 1e-6
F_MIN = 1e-30

N_CTX = BATCH * SEQ
N_LAT = DEC_BATCH * DEC_SEQ
N_TOK = N_CTX + N_LAT
N_COND = 8

TOK_TILE = 256
CHUNK = 64
CHUNK_UNROLL = 2
WINDOW = 120.0
STATIC_WINDOWS = 2
ADA_TILE = 1536
VMEM_LIMIT = 56 * 1024 * 1024

N_TILES = N_TOK // TOK_TILE
ROW_CHUNK = 16
CHUNK_LOOP_UNROLL = 4
TILE_CHUNKS = (TOK_TILE * TOP_K + N_EXPERTS * (ROW_CHUNK - 1)) // ROW_CHUNK
DISPATCH_ROWS = 384
TILE_ROWS = -(-TILE_CHUNKS * ROW_CHUNK // DISPATCH_ROWS) * DISPATCH_ROWS
SHORT_ROWS = TILE_ROWS - DISPATCH_ROWS
GMM_TILE = 256
GMM_WIDTH = 4
GMM_AHEAD = 6
GMM_IN_SLOTS = GMM_WIDTH + GMM_AHEAD
GMM_OUT_SLOTS = 2 * GMM_WIDTH
CHUNKS_PER_GMM_TILE = GMM_TILE // ROW_CHUNK
MAX_GMM_TILES = (N_TILES * TILE_CHUNKS + N_EXPERTS * (CHUNKS_PER_GMM_TILE - 1)) // CHUNKS_PER_GMM_TILE
SORTED_ROWS = MAX_GMM_TILES * GMM_TILE
GATE_LANES = 128
ROW_WIDTH = D_MODEL + GATE_LANES


def _cond_row_of_tile(i, tile):
    ctx_tiles = N_CTX // tile
    per_seq = DEC_SEQ // tile
    return jnp.where(i < ctx_tiles, 0, 1 + (i - ctx_tiles) // per_seq)


def _silu(x):
    return x * jax.nn.sigmoid(x)


def _ln(x):
    mu = jnp.mean(x, axis=-1, keepdims=True)
    xc = x - mu
    var = jnp.mean(xc * xc, axis=-1, keepdims=True)
    return xc * lax.rsqrt(var + LN_EPS)


def _dot(a, b):
    return jnp.dot(a, b, preferred_element_type=F32)


def _dot_nt(a, b):
    return lax.dot_general(a, b, (((1,), (1,)), ((), ())), preferred_element_type=F32)


def _dot_tn(a, b):
    return lax.dot_general(a, b, (((0,), (0,)), ((), ())), preferred_element_type=F32)


@functools.lru_cache(maxsize=None)
def _pos_emb_table():
    rows = DEC_SEQ // GRID_W
    t = np.arange(rows * GRID_W)
    r = (t // GRID_W).astype(np.float32)
    col = (t % GRID_W).astype(np.float32)
    quarter = D_MODEL // 4
    omega = (1.0 / (np.float32(10000.0) ** (np.arange(quarter, dtype=np.float32) / np.float32(quarter)))).astype(np.float32)
    ar = (r[:, None] * omega).astype(np.float32)
    ac = (col[:, None] * omega).astype(np.float32)
    return np.concatenate([np.sin(ar), np.cos(ar), np.sin(ac), np.cos(ac)], axis=-1).astype(np.float32)


@functools.lru_cache(maxsize=None)
def _dft_time(T):
    k = np.arange(T)
    ph = 2.0 * np.pi * ((k[:, None] * k[None, :]) % T) / T
    return np.concatenate([np.cos(ph), -np.sin(ph)], axis=1).astype(np.float32)


@functools.lru_cache(maxsize=None)
def _dft_channel():
    n = FOURIER_GROUP_DIM
    k = np.arange(n)
    ph = 2.0 * np.pi * ((k[:, None] * k[None, :]) % n) / n
    eye = np.eye(FOURIER_GROUPS)
    return np.kron(eye, np.cos(ph)).astype(np.float32), np.kron(eye, np.sin(ph)).astype(np.float32)


CTX_TILES = N_CTX // TOK_TILE
LAT_TILES_PER_SEQ = DEC_SEQ // TOK_TILE


def _stream_specs(n):
    specs = [pl.BlockSpec((TOK_TILE, D_MODEL), lambda i, *_: (jnp.minimum(i, CTX_TILES - 1), 0)),
             pl.BlockSpec((TOK_TILE, D_MODEL), lambda i, *_: (jnp.maximum(i - CTX_TILES, 0), 0))]
    if n == 3:
        specs.append(pl.BlockSpec((TOK_TILE, D_MODEL),
                                  lambda i, *_: (jnp.maximum(i - CTX_TILES, 0) % LAT_TILES_PER_SEQ, 0)))
    return specs


def _stream_tile(refs):
    latent = refs[1][...]
    if len(refs) == 3:
        latent = latent + refs[2][...]
    return jnp.where(pl.program_id(0) < CTX_TILES, refs[0][...], latent)


def _ada_kernel(c_ref, w_ref, b_ref, o_ref):
    s = _silu(c_ref[...]).astype(BF16)
    o_ref[...] = _dot(s, w_ref[...].astype(BF16)) + b_ref[...]


def _ada_mod(cond, w_ada, b_ada):
    n_col = 6 * D_MODEL
    return pl.pallas_call(
        _ada_kernel,
        out_shape=jax.ShapeDtypeStruct((DEPTH, N_COND, n_col), F32),
        grid=(DEPTH, n_col // ADA_TILE),
        in_specs=[
            pl.BlockSpec((N_COND, D_MODEL), lambda l, j: (0, 0)),
            pl.BlockSpec((None, D_MODEL, ADA_TILE), lambda l, j: (l, 0, j)),
            pl.BlockSpec((None, 1, ADA_TILE), lambda l, j: (l, 0, j)),
        ],
        out_specs=pl.BlockSpec((None, N_COND, ADA_TILE), lambda l, j: (l, 0, j)),
        compiler_params=pltpu.CompilerParams(dimension_semantics=("arbitrary", "arbitrary"),
                                             vmem_limit_bytes=VMEM_LIMIT),
        name="ada_mod",
    )(cond, w_ada, b_ada.reshape(DEPTH, 1, n_col))


def _in_proj_kernel(*refs, n_stream):
    mod_ref, w_ref, uh_ref, uf_ref, uc_ref, ug_ref = refs[n_stream:]
    sh1 = mod_ref[0:1, :]
    sc1 = mod_ref[1:2, :]
    h = (_ln(_stream_tile(refs[:n_stream])) * (1.0 + sc1) + sh1).astype(BF16)
    c0 = 0
    for ref, n in ((uh_ref, HGRN_COLS), (uf_ref, FOURIER_WIDTH), (uc_ref, CONV_COLS), (ug_ref, GATE_COLS)):
        ref[...] = _dot(h, w_ref[:, c0:c0 + n])
        c0 += n


def _in_proj(stream, mod, w_in_bf16, layer):
    n_tiles = N_TOK // TOK_TILE
    widths = (HGRN_COLS, FOURIER_WIDTH, CONV_COLS, GATE_COLS)
    return pl.pallas_call(
        functools.partial(_in_proj_kernel, n_stream=len(stream)),
        out_shape=[jax.ShapeDtypeStruct((N_TOK, n), F32) for n in widths],
        grid=(n_tiles,),
        in_specs=_stream_specs(len(stream)) + [
            pl.BlockSpec((None, None, 6, D_MODEL), lambda i: (layer, _cond_row_of_tile(i, TOK_TILE), 0, 0)),
            pl.BlockSpec((None, D_MODEL, IN_COLS), lambda i: (layer, 0, 0)),
        ],
        out_specs=[pl.BlockSpec((TOK_TILE, n), lambda i: (i, 0)) for n in widths],
        compiler_params=pltpu.CompilerParams(dimension_semantics=("arbitrary",), vmem_limit_bytes=VMEM_LIMIT),
        name="in_proj",
    )(*stream, mod, w_in_bf16)


def _mixer_kernel(*refs, T, has_state):
    if has_state:
        (uh_ref, uf_ref, uc_ref, lb_ref, ng_ref, cw_ref, dft_ref, bdc_ref, bds_ref, s0_ref,
         mix_ref, st_ref, o_ref) = refs
        sfin_ref = None
    else:
        (uh_ref, uf_ref, uc_ref, lb_ref, ng_ref, cw_ref, dft_ref, bdc_ref, bds_ref,
         mix_ref, sfin_ref, st_ref, o_ref) = refs
        s0_ref = None
    n_chunks = T // CHUNK
    W = HGRN_WIDTH

    for d in range(2):
        for h in range(HEADS):
            if has_state:
                st_ref[d, h] = s0_ref[0, d, h].T
            else:
                st_ref[d, h] = jnp.zeros((HEAD_DIM, HEAD_DIM), F32)
    o_ref[...] = jnp.zeros_like(o_ref)

    row = lax.broadcasted_iota(jnp.int32, (CHUNK, CHUNK), 0)
    col = lax.broadcasted_iota(jnp.int32, (CHUNK, CHUNK), 1)
    keep = (col <= row, col >= row)
    tri = tuple(k.astype(F32).astype(BF16) for k in keep)

    def chunk_step(i, carry):
        deep = []
        for sub, d in ((s, d) for s in range(CHUNK_UNROLL) for d in range(2)):
            c = i * CHUNK_UNROLL + sub
            c = c if d == 0 else n_chunks - 1 - c
            rows = pl.ds(pl.multiple_of(c * CHUNK, CHUNK), CHUNK)
            q = _silu(uh_ref[rows, 0:W])
            v = uh_ref[rows, W:2 * W].astype(BF16)
            z = uh_ref[rows, (2 + d) * W:(3 + d) * W]
            lb = lb_ref[d:d + 1, :]
            e = jnp.exp(-jnp.abs(z))
            r = 1.0 / (1.0 + e)
            er = e * r
            pos = z >= 0.0
            sig_p = jnp.where(pos, r, er)
            sig_n = jnp.where(pos, er, r)
            f = lb + (1.0 - lb) * sig_p
            lf = jnp.log(jnp.maximum(f, F_MIN))
            k = (1.0 - lb) * sig_n
            lf_hi = lf.astype(BF16)
            lf_lo = (lf - lf_hi.astype(F32)).astype(BF16)
            b = _dot(tri[d], lf_hi) + _dot(tri[d], lf_lo)
            g = b[CHUNK - 1:CHUNK, :] if d == 0 else b[0:1, :]

            def window(p_level, q=q, k=k, b=b):
                dist = -p_level - b
                inside = jnp.logical_and(dist >= 0.0, dist < WINDOW)
                k_w = jnp.where(inside, k * jnp.exp(jnp.minimum(dist, WINDOW) - 0.5 * WINDOW), 0.0)
                q_w = q * jnp.exp(jnp.minimum(-dist, 0.0) + 0.5 * WINDOW)
                return q_w.astype(BF16), k_w.astype(BF16)

            wins = [window(WINDOW * p) for p in range(STATIC_WINDOWS)]
            q_abs = (q * jnp.exp(b)).astype(BF16)
            k_end = (k * jnp.exp(g - b)).astype(BF16)
            decay = jnp.exp(g)
            outs = []
            for h in range(HEADS):
                ls = slice(h * HEAD_DIM, (h + 1) * HEAD_DIM)
                q_cat = jnp.concatenate([w[0][:, ls] for w in wins], axis=-1)
                k_cat = jnp.concatenate([w[1][:, ls] for w in wins], axis=-1)
                a = jnp.where(keep[d], _dot_nt(q_cat, k_cat), 0.0).astype(BF16)
                s_t = st_ref[d, h]
                o_h = _dot(a, v[:, ls]) + _dot_nt(q_abs[:, ls], s_t.astype(BF16))
                st_ref[d, h] = s_t * decay[:, ls] + _dot_tn(v[:, ls], k_end[:, ls])
                outs.append(o_h)
            o_ref[rows, :] += jnp.concatenate(outs, axis=-1)
            deep.append((d, window, v, rows, jnp.min(b)))

        deepest = functools.reduce(jnp.minimum, [entry[4] for entry in deep])
        n_windows = jnp.floor(-deepest * (1.0 / WINDOW)).astype(jnp.int32) + 1

        def extra_window(p, carry2):
            for d, window, v, rows, _ in deep:
                q_w, k_w = window(WINDOW * p.astype(F32))
                extra = []
                for h in range(HEADS):
                    ls = slice(h * HEAD_DIM, (h + 1) * HEAD_DIM)
                    a = jnp.where(keep[d], _dot_nt(q_w[:, ls], k_w[:, ls]), 0.0).astype(BF16)
                    extra.append(_dot(a, v[:, ls]))
                o_ref[rows, :] += jnp.concatenate(extra, axis=-1)
            return carry2

        lax.fori_loop(STATIC_WINDOWS, n_windows, extra_window, 0)
        return carry

    lax.fori_loop(0, n_chunks // CHUNK_UNROLL, chunk_step, 0)

    if sfin_ref is not None:
        for d in range(2):
            for h in range(HEADS):
                sfin_ref[0, d, h] = st_ref[d, h].T

    o = o_ref[...]
    parts = []
    for h in range(HEADS):
        oh = o[:, h * HEAD_DIM:(h + 1) * HEAD_DIM]
        parts.append(oh * lax.rsqrt(jnp.mean(oh * oh, axis=-1, keepdims=True) + RMS_EPS))
    a_out = jnp.concatenate(parts, axis=-1) * ng_ref[...] * _silu(uh_ref[:, 4 * W:5 * W])
    mix_ref[:, 0:W] = a_out.astype(BF16)

    zf = uf_ref[...].astype(BF16)
    zc = _dot(zf, bdc_ref[...]).astype(BF16)
    zs = _dot(zf, bds_ref[...]).astype(BF16)
    four = _dot(dft_ref[...], jnp.concatenate([zc, zs], axis=0)) * (1.0 / math.sqrt(T * FOURIER_GROUP_DIM))
    mix_ref[:, W:W + FOURIER_WIDTH] = four.astype(BF16)

    cb = uc_ref[:, 0:CONV_WIDTH]
    zz = uc_ref[:, CONV_WIDTH:2 * CONV_WIDTH] * uc_ref[:, 2 * CONV_WIDTH:3 * CONV_WIDTH]
    t_idx = lax.broadcasted_iota(jnp.int32, (T, CONV_WIDTH), 0)
    z_prev = jnp.where(t_idx == 0, 0.0, pltpu.roll(zz, 1, axis=0))
    z_next = jnp.where(t_idx == T - 1, 0.0, pltpu.roll(zz, T - 1, axis=0))
    y = cw_ref[0:1, :] * z_prev + cw_ref[1:2, :] * zz + cw_ref[2:3, :] * z_next
    mix_ref[:, W + FOURIER_WIDTH:] = (cb * y).astype(BF16)


def _mixer(uh, uf, uc, lb, norm_g, conv_w, dft, bdc, bds, T, n_seq, blk0, state0=None, layer=0):
    has_state = state0 is not None
    const2 = lambda s: (0, 0)
    in_specs = [
        pl.BlockSpec((T, HGRN_COLS), lambda s: (blk0 + s, 0)),
        pl.BlockSpec((T, FOURIER_WIDTH), lambda s: (blk0 + s, 0)),
        pl.BlockSpec((T, CONV_COLS), lambda s: (blk0 + s, 0)),
        pl.BlockSpec((2, HGRN_WIDTH), const2),
        pl.BlockSpec((1, HGRN_WIDTH), const2),
        pl.BlockSpec((3, CONV_WIDTH), const2),
        pl.BlockSpec((T, 2 * T), const2),
        pl.BlockSpec((FOURIER_WIDTH, FOURIER_WIDTH), const2),
        pl.BlockSpec((FOURIER_WIDTH, FOURIER_WIDTH), const2),
    ]
    args = [uh, uf, uc, lb, norm_g, conv_w, dft, bdc, bds]
    mix_shape = jax.ShapeDtypeStruct((n_seq * T, D_MODEL), BF16)
    mix_spec = pl.BlockSpec((T, D_MODEL), lambda s: (s, 0))
    st_block = (1, 2, HEADS, HEAD_DIM, HEAD_DIM)
    if has_state:
        in_specs.append(pl.BlockSpec((1, None) + st_block[1:], lambda s: (s, layer, 0, 0, 0, 0)))
        args.append(state0)
        out_shape = [mix_shape]
        out_specs = [mix_spec]
    else:
        out_shape = [mix_shape, jax.ShapeDtypeStruct((n_seq, 2, HEADS, HEAD_DIM, HEAD_DIM), F32)]
        out_specs = [mix_spec, pl.BlockSpec(st_block, lambda s: (s, 0, 0, 0, 0))]
    return pl.pallas_call(
        functools.partial(_mixer_kernel, T=T, has_state=has_state),
        out_shape=out_shape,
        grid=(n_seq,),
        in_specs=in_specs,
        out_specs=out_specs,
        scratch_shapes=[pltpu.VMEM((2, HEADS, HEAD_DIM, HEAD_DIM), F32), pltpu.VMEM((T, HGRN_WIDTH), F32)],
        compiler_params=pltpu.CompilerParams(dimension_semantics=("arbitrary",), vmem_limit_bytes=VMEM_LIMIT),
        name="mixer_latent" if has_state else "mixer_context",
    )(*args)


def _merge_kernel(*refs, n_stream):
    (mixc_ref, mixl_ref, ug_ref, mod_ref, bg_ref, wp_ref, wo_ref, g1_ref, b1_ref, wrh_ref, wrl_ref, br_ref,
     x1_ref, h2_ref, apos_ref, gate_ref, n16_ref) = refs[n_stream:]
    W = HGRN_WIDTH
    gates = jax.nn.sigmoid(ug_ref[...] + bg_ref[...])
    mix = _stream_tile((mixc_ref, mixl_ref))
    edges = (0, W, W + FOURIER_WIDTH, D_MODEL)
    merged = None
    for j in range(N_BRANCHES):
        p = _dot(mix[:, edges[j]:edges[j + 1]], wp_ref[edges[j]:edges[j + 1], :])
        term = gates[:, j * D_MODEL:(j + 1) * D_MODEL] * p
        merged = term if merged is None else merged + term
    y = _dot(merged.astype(BF16), wo_ref[...])
    x1 = _ln(ALPHA * _stream_tile(refs[:n_stream]) + mod_ref[2:3, :] * y) * g1_ref[...] + b1_ref[...]
    x1_ref[...] = x1
    h2 = _ln(x1) * (1.0 + mod_ref[4:5, :]) + mod_ref[3:4, :]
    h2_hi = h2.astype(BF16)
    h2_ref[...] = h2_hi
    h2_lo = (h2 - h2_hi.astype(F32)).astype(BF16)
    logits = _dot(h2_hi, wrh_ref[...]) + (_dot(h2_hi, wrl_ref[...]) + _dot(h2_lo, wrh_ref[...]))
    scores = jax.nn.sigmoid(logits)
    sel = scores + br_ref[...]
    lane_f = lax.broadcasted_iota(jnp.int32, sel.shape, 1).astype(F32)
    hits = []
    for _ in range(TOP_K):
        top = jnp.max(sel, axis=-1, keepdims=True)
        first = jnp.min(jnp.where(sel == top, lane_f, float(N_EXPERTS)), axis=-1, keepdims=True)
        hit = lane_f == first
        hits.append(hit)
        sel = jnp.where(hit, -jnp.inf, sel)
    chosen = functools.reduce(jnp.logical_or, hits)
    picked = jnp.where(chosen, scores, 0.0)
    denom = jnp.sum(picked, axis=-1, keepdims=True) + 1e-20
    gate = ROUTED_SCALE * picked / denom

    chosen_f = chosen.astype(F32)
    r_i = lax.broadcasted_iota(jnp.int32, (TOK_TILE, TOK_TILE), 0)
    c_i = lax.broadcasted_iota(jnp.int32, (TOK_TILE, TOK_TILE), 1)
    earlier = (c_i < r_i).astype(F32).astype(BF16)
    rank = _dot(earlier, chosen_f.astype(BF16))
    count = jnp.sum(chosen_f, axis=0, keepdims=True)
    n16 = jnp.floor((count + (ROW_CHUNK - 1)) * (1.0 / ROW_CHUNK))
    e_r = lax.broadcasted_iota(jnp.int32, (N_EXPERTS, N_EXPERTS), 0)
    e_c = lax.broadcasted_iota(jnp.int32, (N_EXPERTS, N_EXPERTS), 1)
    before = (e_r < e_c).astype(F32).astype(BF16)
    seg_start = ROW_CHUNK * _dot(jnp.broadcast_to(n16, (8, N_EXPERTS)).astype(BF16), before)[0:1, :]
    row_of = seg_start + rank
    lane_k = lax.broadcasted_iota(jnp.int32, (TOK_TILE, TOP_K), 1)
    apos = jnp.zeros((TOK_TILE, TOP_K), F32)
    for k, hit in enumerate(hits):
        apos = jnp.where(lane_k == k, jnp.sum(jnp.where(hit, row_of, 0.0), axis=-1, keepdims=True), apos)
    apos_ref[...] = apos
    gate_ref[...] = jnp.concatenate([gate, jnp.zeros((TOK_TILE, GATE_LANES - N_EXPERTS), F32)], axis=-1).astype(BF16)
    n16_ref[...] = n16


def _merge(stream, mix_ctx, mix_lat, ug, mod, b_gate, wp, wo, ln_g, ln_b, wr_hi, wr_lo, b_router, layer):
    tok = lambda n: pl.BlockSpec((TOK_TILE, n), lambda i: (i, 0))
    const2 = lambda i: (0, 0)
    return pl.pallas_call(
        functools.partial(_merge_kernel, n_stream=len(stream)),
        out_shape=[jax.ShapeDtypeStruct((N_TOK, D_MODEL), F32), jax.ShapeDtypeStruct((N_TOK, D_MODEL), BF16),
                   jax.ShapeDtypeStruct((N_TOK, TOP_K), F32), jax.ShapeDtypeStruct((N_TOK, GATE_LANES), BF16),
                   jax.ShapeDtypeStruct((N_TILES, 1, N_EXPERTS), F32)],
        grid=(N_TILES,),
        in_specs=_stream_specs(len(stream)) + _stream_specs(2) + [
            tok(GATE_COLS),
            pl.BlockSpec((None, None, 6, D_MODEL), lambda i: (layer, _cond_row_of_tile(i, TOK_TILE), 0, 0)),
            pl.BlockSpec((1, GATE_COLS), const2),
            pl.BlockSpec((D_MODEL, D_MODEL), const2),
            pl.BlockSpec((D_MODEL, D_MODEL), const2),
            pl.BlockSpec((1, D_MODEL), const2),
            pl.BlockSpec((1, D_MODEL), const2),
            pl.BlockSpec((D_MODEL, N_EXPERTS), const2),
            pl.BlockSpec((D_MODEL, N_EXPERTS), const2),
            pl.BlockSpec((1, N_EXPERTS), const2),
        ],
        out_specs=[tok(D_MODEL), tok(D_MODEL), tok(TOP_K), tok(GATE_LANES),
                   pl.BlockSpec((None, 1, N_EXPERTS), lambda i: (i, 0, 0))],
        compiler_params=pltpu.CompilerParams(dimension_semantics=("arbitrary",), vmem_limit_bytes=VMEM_LIMIT),
        name="merge_router",
    )(*stream, mix_ctx, mix_lat, ug, mod, b_gate, wp, wo, ln_g, ln_b, wr_hi, wr_lo, b_router)


def _routing_tables(n16):
    n16 = n16.astype(jnp.int32)
    hi = jnp.cumsum(n16, axis=1)
    lo = hi - n16
    n_chunks = hi[:, -1]
    eo = jnp.cumsum(n16, axis=0) - n16
    tc = jnp.sum(n16, axis=0)
    tiles = (tc + CHUNKS_PER_GMM_TILE - 1) // CHUNKS_PER_GMM_TILE
    tile_end = jnp.cumsum(tiles)
    tile_off = tile_end - tiles
    n_used = tile_end[-1]
    c = jnp.arange(TILE_CHUNKS, dtype=jnp.int32)[None, :, None]
    owner = jnp.logical_and(c >= lo[:, None, :], c < hi[:, None, :])
    shift = CHUNKS_PER_GMM_TILE * tile_off[None, :] + eo - lo
    dst = c[:, :, 0] + jnp.sum(jnp.where(owner, shift[:, None, :], 0), axis=-1)
    pad_start = CHUNKS_PER_GMM_TILE * tile_off + tc
    pad_count = CHUNKS_PER_GMM_TILE * tiles - tc
    return (dst.reshape(-1).astype(jnp.int32), n_chunks, tile_off.astype(jnp.int32), tiles.astype(jnp.int32),
            jnp.reshape(n_used, (1,)).astype(jnp.int32), pad_start.astype(jnp.int32), pad_count.astype(jnp.int32))


def _one_hot_rows(apos_ref, n_rows):
    r = lax.broadcasted_iota(jnp.int32, (TOK_TILE, n_rows), 1).astype(F32)
    pt = jnp.zeros((TOK_TILE, n_rows), F32)
    for k in range(TOP_K):
        pt = jnp.where(r == apos_ref[:, k:k + 1], 1.0, pt)
    return pt.astype(BF16)


def _chunk_copy(src_ref, src_chunk, dst_ref, dst_chunk, sem):
    rows = lambda c: pl.ds(pl.multiple_of(c * ROW_CHUNK, ROW_CHUNK), ROW_CHUNK)
    return pltpu.make_async_copy(src_ref.at[rows(src_chunk), :], dst_ref.at[rows(dst_chunk), :], sem)


def _for_each_chunk(n, fn):
    def trip(i, carry):
        for u in range(CHUNK_LOOP_UNROLL):
            c = i * CHUNK_LOOP_UNROLL + u
            if u == 0:
                fn(c)
            else:
                pl.when(c < n)(functools.partial(fn, c))
        return carry
    lax.fori_loop(0, (n + (CHUNK_LOOP_UNROLL - 1)) // CHUNK_LOOP_UNROLL, trip, 0)


def _dispatch_kernel(dst_ref, nch_ref, pstart_ref, pcount_ref, nu_ref, h_ref, gate_ref, apos_ref, xs_hbm, pt_ref,
                     stage_ref, zero_ref, sem):
    j = pl.program_id(0)
    buf = j % 2

    def wait_chunks(b, n):
        _for_each_chunk(n, lambda c: _chunk_copy(stage_ref.at[b], 0, xs_hbm, 0, sem.at[b]).wait())

    @pl.when(j < N_TILES)
    def _():
        @pl.when(j >= 2)
        def _():
            wait_chunks(buf, nch_ref[j - 2])

        def sort_rows(n_rows):
            pt = _one_hot_rows(apos_ref, n_rows)
            pt_ref[:, 0:n_rows] = pt
            if n_rows < TILE_ROWS:
                pt_ref[:, n_rows:TILE_ROWS] = jnp.zeros((TOK_TILE, TILE_ROWS - n_rows), BF16)
            h = h_ref[...]
            gate = gate_ref[...]
            for r0 in range(0, n_rows, DISPATCH_ROWS):
                group = pt[:, r0:r0 + DISPATCH_ROWS]
                stage_ref[buf, r0:r0 + DISPATCH_ROWS, 0:D_MODEL] = _dot_tn(group, h).astype(BF16)
                stage_ref[buf, r0:r0 + DISPATCH_ROWS, D_MODEL:ROW_WIDTH] = _dot_tn(group, gate).astype(BF16)

        short = nch_ref[j] * ROW_CHUNK <= SHORT_ROWS
        pl.when(short)(functools.partial(sort_rows, SHORT_ROWS))
        pl.when(jnp.logical_not(short))(functools.partial(sort_rows, TILE_ROWS))

        _for_each_chunk(nch_ref[j], lambda c: _chunk_copy(
            stage_ref.at[buf], c, xs_hbm, dst_ref[j * TILE_CHUNKS + c], sem.at[buf]).start())

    @pl.when(j == N_TILES)
    def _():
        zero_ref[...] = jnp.zeros_like(zero_ref)

        def tile_copy(t):
            rows = pl.ds(pl.multiple_of(t * GMM_TILE, GMM_TILE), GMM_TILE)
            return pltpu.make_async_copy(zero_ref, xs_hbm.at[rows, :], sem.at[3])

        def per_expert(e, total):
            def fill(i, carry):
                _chunk_copy(zero_ref, 0, xs_hbm, pstart_ref[e] + i, sem.at[2]).start()
                return carry
            lax.fori_loop(0, pcount_ref[e], fill, 0)
            return total + pcount_ref[e]
        n_pad = lax.fori_loop(0, N_EXPERTS, per_expert, 0)

        def fill_tile(t, carry):
            tile_copy(t).start()
            return carry
        lax.fori_loop(nu_ref[0], MAX_GMM_TILES, fill_tile, 0)
        wait_chunks(0, nch_ref[N_TILES - 2])
        wait_chunks(1, nch_ref[N_TILES - 1])

        def drain(i, carry):
            _chunk_copy(zero_ref, 0, xs_hbm, 0, sem.at[2]).wait()
            return carry
        lax.fori_loop(0, n_pad, drain, 0)

        def drain_tile(t, carry):
            tile_copy(0).wait()
            return carry
        lax.fori_loop(nu_ref[0], MAX_GMM_TILES, drain_tile, 0)


def _dispatch(h2, gate, apos, dst, n_chunks, pad_start, pad_count, n_used):
    last = N_TILES - 1
    tok = lambda n: pl.BlockSpec((TOK_TILE, n), lambda j, *_: (jnp.minimum(j, last), 0))
    grid_spec = pltpu.PrefetchScalarGridSpec(
        num_scalar_prefetch=5,
        grid=(N_TILES + 1,),
        in_specs=[tok(D_MODEL), tok(GATE_LANES), tok(TOP_K)],
        out_specs=[pl.BlockSpec(memory_space=pl.ANY), tok(TILE_ROWS)],
        scratch_shapes=[pltpu.VMEM((2, TILE_ROWS, ROW_WIDTH), BF16), pltpu.VMEM((GMM_TILE, ROW_WIDTH), BF16),
                        pltpu.SemaphoreType.DMA((4,))],
    )
    return pl.pallas_call(
        _dispatch_kernel,
        out_shape=[jax.ShapeDtypeStruct((SORTED_ROWS, ROW_WIDTH), BF16),
                   jax.ShapeDtypeStruct((N_TOK, TILE_ROWS), BF16)],
        grid_spec=grid_spec,
        compiler_params=pltpu.CompilerParams(dimension_semantics=("arbitrary",), vmem_limit_bytes=VMEM_LIMIT),
        name="moe_dispatch",
    )(dst, n_chunks, pad_start, pad_count, n_used, h2, gate, apos)


def _gmm_kernel(off_ref, cnt_ref, nu_ref, x_hbm, wg_ref, wu_ref, wd_ref, y_hbm, xbuf, ybuf, wg_b, wu_b, wd_b,
                in_sem, out_sem):
    e = pl.program_id(0)
    n_used = nu_ref[0]

    def rows(g):
        return pl.ds(pl.multiple_of(g * GMM_TILE, GMM_TILE), GMM_TILE)

    def fetch(g):
        slot = g % GMM_IN_SLOTS
        return pltpu.make_async_copy(x_hbm.at[rows(g), :], xbuf.at[slot], in_sem.at[slot])

    def flush(g):
        slot = g % GMM_OUT_SLOTS
        return pltpu.make_async_copy(ybuf.at[slot], y_hbm.at[rows(g), 0:D_MODEL], out_sem.at[slot])

    @pl.when(e == 0)
    def _():
        for g in range(GMM_AHEAD):
            @pl.when(g < n_used)
            def _(g=g):
                fetch(g).start()

    wg_b[...] = wg_ref[...].astype(BF16)
    wu_b[...] = wu_ref[...].astype(BF16)
    wd_b[...] = wd_ref[...].astype(BF16)

    def process(g0, width):
        for w in range(width):
            @pl.when(g0 + w + GMM_AHEAD < n_used)
            def _(w=w):
                fetch(g0 + w + GMM_AHEAD).start()
        for w in range(width):
            fetch(g0 + w).wait()
        xg = jnp.concatenate([xbuf[(g0 + w) % GMM_IN_SLOTS] for w in range(width)], axis=0)
        x = xg[:, 0:D_MODEL]
        gates = xg[:, D_MODEL:ROW_WIDTH].astype(F32)
        lane = lax.broadcasted_iota(jnp.int32, gates.shape, 1)
        weight = jnp.sum(jnp.where(lane == e, gates, 0.0), axis=-1, keepdims=True)
        hid = _silu(_dot(x, wg_b[...])) * _dot(x, wu_b[...])
        y = (_dot(hid.astype(BF16), wd_b[...]) * weight).astype(BF16)
        for w in range(width):
            g = g0 + w

            @pl.when(g >= GMM_OUT_SLOTS)
            def _(g=g):
                flush(g - GMM_OUT_SLOTS).wait()

            ybuf[g % GMM_OUT_SLOTS] = y[w * GMM_TILE:(w + 1) * GMM_TILE]
            flush(g).start()

    first = off_ref[e]
    count = cnt_ref[e]

    done = 0
    width = GMM_WIDTH
    while width >= 1:
        trips = (count - done) // width

        def block(p, carry, done=done, width=width):
            process(first + done + width * p, width)
            return carry

        lax.fori_loop(0, trips, block, 0)
        done = done + trips * width
        width //= 2

    @pl.when(e == N_EXPERTS - 1)
    def _():
        for k in range(GMM_OUT_SLOTS):
            @pl.when(n_used - 1 - k >= 0)
            def _(k=k):
                flush(n_used - 1 - k).wait()


def _gmm(xs, weg, weu, wed, tile_off, tiles, n_used, layer):
    w_in_spec = pl.BlockSpec((None, None, D_MODEL, D_EXPERT), lambda e, *_: (layer, e, 0, 0))
    grid_spec = pltpu.PrefetchScalarGridSpec(
        num_scalar_prefetch=3,
        grid=(N_EXPERTS,),
        in_specs=[
            pl.BlockSpec(memory_space=pl.ANY),
            w_in_spec, w_in_spec,
            pl.BlockSpec((None, None, D_EXPERT, D_MODEL), lambda e, *_: (layer, e, 0, 0)),
        ],
        out_specs=pl.BlockSpec(memory_space=pl.ANY),
        scratch_shapes=[pltpu.VMEM((GMM_IN_SLOTS, GMM_TILE, ROW_WIDTH), BF16),
                        pltpu.VMEM((GMM_OUT_SLOTS, GMM_TILE, D_MODEL), BF16),
                        pltpu.VMEM((D_MODEL, D_EXPERT), BF16), pltpu.VMEM((D_MODEL, D_EXPERT), BF16),
                        pltpu.VMEM((D_EXPERT, D_MODEL), BF16),
                        pltpu.SemaphoreType.DMA((GMM_IN_SLOTS,)), pltpu.SemaphoreType.DMA((GMM_OUT_SLOTS,))],
    )
    return pl.pallas_call(
        _gmm_kernel,
        out_shape=jax.ShapeDtypeStruct((SORTED_ROWS, ROW_WIDTH), BF16),
        grid_spec=grid_spec,
        input_output_aliases={3: 0},
        compiler_params=pltpu.CompilerParams(dimension_semantics=("arbitrary",), vmem_limit_bytes=VMEM_LIMIT),
        name="moe_gmm",
    )(tile_off, tiles, n_used, xs, weg, weu, wed)


def _combine_kernel(dst_ref, nch_ref, ys_hbm, pt_ref, h_ref, x1_ref, mod_ref, wsg_ref, wsu_ref, wsd_ref,
                    g2_ref, b2_ref, oc_ref, ol_ref, stage_ref, routed_ref, sem):
    j = pl.program_id(0)
    buf = j % 2

    def chunk_fetch(slot, b, c):
        rows = lambda k: pl.ds(pl.multiple_of(k * ROW_CHUNK, ROW_CHUNK), ROW_CHUNK)
        return pltpu.make_async_copy(ys_hbm.at[rows(slot), 0:D_MODEL], stage_ref.at[b, rows(c), :], sem.at[b])

    def fetch(jj, b):
        _for_each_chunk(nch_ref[jj], lambda c: chunk_fetch(dst_ref[jj * TILE_CHUNKS + c], b, c).start())

    @pl.when(j == 0)
    def _():
        stage_ref[...] = jnp.zeros_like(stage_ref)
        fetch(0, 0)

    @pl.when(j + 1 < N_TILES)
    def _():
        fetch(j + 1, 1 - buf)

    h = h_ref[...]
    hid = _silu(_dot(h, wsg_ref[...])) * _dot(h, wsu_ref[...])
    shared = _dot(hid.astype(BF16), wsd_ref[...])

    _for_each_chunk(nch_ref[j], lambda c: chunk_fetch(0, buf, 0).wait())

    def gather_rows(n_rows):
        routed_ref[...] = _dot(pt_ref[:, 0:n_rows], stage_ref[buf, 0:n_rows, :])

    short = nch_ref[j] * ROW_CHUNK <= SHORT_ROWS
    pl.when(short)(functools.partial(gather_rows, SHORT_ROWS))
    pl.when(jnp.logical_not(short))(functools.partial(gather_rows, TILE_ROWS))
    r = ALPHA * x1_ref[...] + mod_ref[5:6, :] * (routed_ref[...] + shared)
    out = _ln(r) * g2_ref[...] + b2_ref[...]

    @pl.when(j < CTX_TILES)
    def _():
        oc_ref[...] = out

    @pl.when(j >= CTX_TILES)
    def _():
        ol_ref[...] = out


def _combine(ys, pt, h2, x1, mod, wsg, wsu, wsd, ln_g, ln_b, dst, n_chunks, layer):
    tok = lambda n: pl.BlockSpec((TOK_TILE, n), lambda j, *_: (j, 0))
    const2 = lambda j, *_: (0, 0)
    grid_spec = pltpu.PrefetchScalarGridSpec(
        num_scalar_prefetch=2,
        grid=(N_TILES,),
        in_specs=[
            pl.BlockSpec(memory_space=pl.ANY),
            tok(TILE_ROWS), tok(D_MODEL), tok(D_MODEL),
            pl.BlockSpec((None, None, 6, D_MODEL), lambda j, *_: (layer, _cond_row_of_tile(j, TOK_TILE), 0, 0)),
            pl.BlockSpec((D_MODEL, D_SHARED), const2),
            pl.BlockSpec((D_MODEL, D_SHARED), const2),
            pl.BlockSpec((D_SHARED, D_MODEL), const2),
            pl.BlockSpec((1, D_MODEL), const2),
            pl.BlockSpec((1, D_MODEL), const2),
        ],
        out_specs=_stream_specs(2),
        scratch_shapes=[pltpu.VMEM((2, TILE_ROWS, D_MODEL), BF16), pltpu.VMEM((TOK_TILE, D_MODEL), F32),
                        pltpu.SemaphoreType.DMA((2,))],
    )
    return pl.pallas_call(
        _combine_kernel,
        out_shape=[jax.ShapeDtypeStruct((N_CTX, D_MODEL), F32), jax.ShapeDtypeStruct((N_LAT, D_MODEL), F32)],
        grid_spec=grid_spec,
        compiler_params=pltpu.CompilerParams(dimension_semantics=("arbitrary",), vmem_limit_bytes=VMEM_LIMIT),
        name="moe_combine",
    )(dst, n_chunks, ys, pt, h2, x1, mod, wsg, wsu, wsd, ln_g, ln_b)


def kernel(x_prompt, x_sample, c, state_hgrn, c_ctx, w_ada, b_ada, w_in, b_gate, hgrn_lb, hgrn_norm, conv_w,
           w_proj_hgrn, w_proj_fourier, w_proj_conv, w_out, ln1_g, ln1_b, ln2_g, ln2_b, w_router, b_router,
           w_exp_gate, w_exp_up, w_exp_down, w_sh_gate, w_sh_up, w_sh_down):
    lb_sm = jax.nn.softmax(hgrn_lb.astype(F32), axis=0)
    lb_all = jnp.cumsum(lb_sm, axis=0) - lb_sm[:1]

    pos = jnp.asarray(_pos_emb_table())
    dft_ctx = jnp.asarray(_dft_time(SEQ)).astype(BF16)
    dft_lat = jnp.asarray(_dft_time(DEC_SEQ)).astype(BF16)
    bdc, bds = (jnp.asarray(m).astype(BF16) for m in _dft_channel())

    cond = jnp.concatenate([c_ctx[None, :], c, jnp.zeros((N_COND - 1 - DEC_BATCH, D_MODEL), F32)], axis=0)
    mod = _ada_mod(cond, w_ada, b_ada).reshape(DEPTH, N_COND, 6, D_MODEL)

    w_in_b = w_in.astype(BF16)
    wp_b = jnp.concatenate([w_proj_hgrn, w_proj_fourier, w_proj_conv], axis=1).astype(BF16)
    wo_b = w_out.astype(BF16)
    wr_hi = w_router.astype(BF16)
    wr_lo = (w_router - wr_hi.astype(F32)).astype(BF16)
    wsg_b, wsu_b, wsd_b = w_sh_gate.astype(BF16), w_sh_up.astype(BF16), w_sh_down.astype(BF16)

    stream = (x_prompt.reshape(N_CTX, D_MODEL), x_sample.reshape(N_LAT, D_MODEL), pos)
    new_states = []
    for l in range(DEPTH):
        uh, uf, uc, ug = _in_proj(stream, mod, w_in_b, l)
        small = (lb_all[l], hgrn_norm[l][None, :], conv_w[l])
        mix_ctx, s_ctx = _mixer(uh, uf, uc, *small, dft_ctx, bdc, bds, T=SEQ, n_seq=BATCH, blk0=0)
        (mix_lat,) = _mixer(uh, uf, uc, *small, dft_lat, bdc, bds, T=DEC_SEQ, n_seq=DEC_BATCH,
                            blk0=N_CTX // DEC_SEQ, state0=state_hgrn, layer=l)
        new_states.append(s_ctx)
        x1, h2, apos, gate, n16 = _merge(stream, mix_ctx, mix_lat, ug, mod, b_gate[l][None, :], wp_b[l], wo_b[l],
                                         ln1_g[l][None, :], ln1_b[l][None, :], wr_hi[l], wr_lo[l],
                                         b_router[l][None, :], l)
        dst, n_chunks, tile_off, tiles, n_used, pad_start, pad_count = _routing_tables(
            n16.reshape(N_TILES, N_EXPERTS))
        xs, pt = _dispatch(h2, gate, apos, dst, n_chunks, pad_start, pad_count, n_used)
        ys = _gmm(xs, w_exp_gate, w_exp_up, w_exp_down, tile_off, tiles, n_used, l)
        stream = tuple(_combine(ys, pt, h2, x1, mod, wsg_b[l], wsu_b[l], wsd_b[l], ln2_g[l][None, :],
                                ln2_b[l][None, :], dst, n_chunks, l))
    y_prompt = stream[0].reshape(BATCH, SEQ, D_MODEL)
    y_sample = stream[1].reshape(DEC_BATCH, DEC_SEQ, D_MODEL)
    return (y_prompt, y_sample, jnp.stack(new_states, axis=1))
```

```python
import functools
import math

import numpy as np
import jax
import jax.numpy as jnp
from jax import lax
from jax.experimental import pallas as pl
from jax.experimental.pallas import tpu as pltpu

F32 = jnp.float32
BF16 = jnp.bfloat16

D_MODEL = 1024
BATCH = 16
SEQ = 256
DEPTH = 2
DEC_BATCH = 4
DEC_SEQ = 1024
GRID_W = 64
HEADS = 4
HEAD_DIM = 128
HGRN_WIDTH = HEADS * HEAD_DIM
FOURIER_GROUPS = 4
FOURIER_GROUP_DIM = 64
FOURIER_WIDTH = FOURIER_GROUPS * FOURIER_GROUP_DIM
CONV_WIDTH = 256
N_BRANCHES = 3
HGRN_COLS = 5 * HGRN_WIDTH
CONV_COLS = 3 * CONV_WIDTH
GATE_COLS = N_BRANCHES * D_MODEL
IN_COLS = HGRN_COLS + FOURIER_WIDTH + CONV_COLS + GATE_COLS
N_EXPERTS = 64
TOP_K = 8
D_EXPERT = 256
D_SHARED = 256
ROUTED_SCALE = 2.5
ALPHA = (2 * DEPTH) ** 0.25
LN_EPS = 1e-6
RMS_EPS = 1e-6
F_MIN = 1e-30

N_CTX = BATCH * SEQ
N_LAT = DEC_BATCH * DEC_SEQ
N_TOK = N_CTX + N_LAT
N_COND = 8

TOK_TILE = 256
CHUNK = 64
CHUNK_UNROLL = 2
WINDOW = 120.0
STATIC_WINDOWS = 2
ADA_TILE = 1536
VMEM_LIMIT = 56 * 1024 * 1024

N_TILES = N_TOK // TOK_TILE
ROW_CHUNK = 16
CHUNK_LOOP_UNROLL = 4
TILE_CHUNKS = (TOK_TILE * TOP_K + N_EXPERTS * (ROW_CHUNK - 1)) // ROW_CHUNK
RUN_CHUNKS = (2, 1)
MAX_PAIRS = TILE_CHUNKS // 2
RUN_TABLE_STRIDE = (MAX_PAIRS, N_EXPERTS)
DISPATCH_ROWS = 384
TILE_ROWS = -(-TILE_CHUNKS * ROW_CHUNK // DISPATCH_ROWS) * DISPATCH_ROWS
SHORT_ROWS = TILE_ROWS - DISPATCH_ROWS
GMM_TILE = 256
GMM_WIDTH = 4
GMM_AHEAD = 6
GMM_IN_SLOTS = GMM_WIDTH + GMM_AHEAD
GMM_OUT_SLOTS = 2 * GMM_WIDTH
CHUNKS_PER_GMM_TILE = GMM_TILE // ROW_CHUNK
MAX_GMM_TILES = (N_TILES * TILE_CHUNKS + N_EXPERTS * (CHUNKS_PER_GMM_TILE - 1)) // CHUNKS_PER_GMM_TILE
SORTED_ROWS = MAX_GMM_TILES * GMM_TILE
GATE_LANES = 128
ROW_WIDTH = D_MODEL + GATE_LANES


def _cond_row_of_tile(i, tile):
    ctx_tiles = N_CTX // tile
    per_seq = DEC_SEQ // tile
    return jnp.where(i < ctx_tiles, 0, 1 + (i - ctx_tiles) // per_seq)


def _silu(x):
    return x * jax.nn.sigmoid(x)


def _ln(x):
    mu = jnp.mean(x, axis=-1, keepdims=True)
    xc = x - mu
    var = jnp.mean(xc * xc, axis=-1, keepdims=True)
    return xc * lax.rsqrt(var + LN_EPS)


def _dot(a, b):
    return jnp.dot(a, b, preferred_element_type=F32)


def _dot_nt(a, b):
    return lax.dot_general(a, b, (((1,), (1,)), ((), ())), preferred_element_type=F32)


def _dot_tn(a, b):
    return lax.dot_general(a, b, (((0,), (0,)), ((), ())), preferred_element_type=F32)


@functools.lru_cache(maxsize=None)
def _pos_emb_table():
    rows = DEC_SEQ // GRID_W
    t = np.arange(rows * GRID_W)
    r = (t // GRID_W).astype(np.float32)
    col = (t % GRID_W).astype(np.float32)
    quarter = D_MODEL // 4
    omega = (1.0 / (np.float32(10000.0) ** (np.arange(quarter, dtype=np.float32) / np.float32(quarter)))).astype(np.float32)
    ar = (r[:, None] * omega).astype(np.float32)
    ac = (col[:, None] * omega).astype(np.float32)
    return np.concatenate([np.sin(ar), np.cos(ar), np.sin(ac), np.cos(ac)], axis=-1).astype(np.float32)


@functools.lru_cache(maxsize=None)
def _dft_time(T):
    k = np.arange(T)
    ph = 2.0 * np.pi * ((k[:, None] * k[None, :]) % T) / T
    return np.concatenate([np.cos(ph), -np.sin(ph)], axis=1).astype(np.float32)


@functools.lru_cache(maxsize=None)
def _dft_channel():
    n = FOURIER_GROUP_DIM
    k = np.arange(n)
    ph = 2.0 * np.pi * ((k[:, None] * k[None, :]) % n) / n
    eye = np.eye(FOURIER_GROUPS)
    return np.kron(eye, np.cos(ph)).astype(np.float32), np.kron(eye, np.sin(ph)).astype(np.float32)


CTX_TILES = N_CTX // TOK_TILE
LAT_TILES_PER_SEQ = DEC_SEQ // TOK_TILE


def _stream_specs(n):
    specs = [pl.BlockSpec((TOK_TILE, D_MODEL), lambda i, *_: (jnp.minimum(i, CTX_TILES - 1), 0)),
             pl.BlockSpec((TOK_TILE, D_MODEL), lambda i, *_: (jnp.maximum(i - CTX_TILES, 0), 0))]
    if n == 3:
        specs.append(pl.BlockSpec((TOK_TILE, D_MODEL),
                                  lambda i, *_: (jnp.maximum(i - CTX_TILES, 0) % LAT_TILES_PER_SEQ, 0)))
    return specs


def _stream_tile(refs):
    latent = refs[1][...]
    if len(refs) == 3:
        latent = latent + refs[2][...]
    return jnp.where(pl.program_id(0) < CTX_TILES, refs[0][...], latent)


def _ada_kernel(c_ref, w_ref, b_ref, o_ref):
    s = _silu(c_ref[...]).astype(BF16)
    o_ref[...] = _dot(s, w_ref[...].astype(BF16)) + b_ref[...]


def _ada_mod(cond, w_ada, b_ada):
    n_col = 6 * D_MODEL
    return pl.pallas_call(
        _ada_kernel,
        out_shape=jax.ShapeDtypeStruct((DEPTH, N_COND, n_col), F32),
        grid=(DEPTH, n_col // ADA_TILE),
        in_specs=[
            pl.BlockSpec((N_COND, D_MODEL), lambda l, j: (0, 0)),
            pl.BlockSpec((None, D_MODEL, ADA_TILE), lambda l, j: (l, 0, j)),
            pl.BlockSpec((None, 1, ADA_TILE), lambda l, j: (l, 0, j)),
        ],
        out_specs=pl.BlockSpec((None, N_COND, ADA_TILE), lambda l, j: (l, 0, j)),
        compiler_params=pltpu.CompilerParams(dimension_semantics=("arbitrary", "arbitrary"),
                                             vmem_limit_bytes=VMEM_LIMIT),
        name="ada_mod",
    )(cond, w_ada, b_ada.reshape(DEPTH, 1, n_col))


def _in_proj_kernel(*refs, n_stream):
    mod_ref, w_ref, bg_ref, uh_ref, uf_ref, uc_ref, gates_ref = refs[n_stream:]
    sh1 = mod_ref[0:1, :]
    sc1 = mod_ref[1:2, :]
    h = (_ln(_stream_tile(refs[:n_stream])) * (1.0 + sc1) + sh1).astype(BF16)
    c0 = 0
    for ref, n in ((uh_ref, HGRN_COLS), (uf_ref, FOURIER_WIDTH), (uc_ref, CONV_COLS)):
        ref[...] = _dot(h, w_ref[:, c0:c0 + n])
        c0 += n
    gates_ref[...] = jax.nn.sigmoid(_dot(h, w_ref[:, c0:c0 + GATE_COLS]) + bg_ref[...]).astype(BF16)


def _in_proj(stream, mod, w_in_bf16, b_gate, layer):
    n_tiles = N_TOK // TOK_TILE
    outs = ((HGRN_COLS, F32), (FOURIER_WIDTH, F32), (CONV_COLS, F32), (GATE_COLS, BF16))
    return pl.pallas_call(
        functools.partial(_in_proj_kernel, n_stream=len(stream)),
        out_shape=[jax.ShapeDtypeStruct((N_TOK, n), dt) for n, dt in outs],
        grid=(n_tiles,),
        in_specs=_stream_specs(len(stream)) + [
            pl.BlockSpec((None, None, 6, D_MODEL), lambda i: (layer, _cond_row_of_tile(i, TOK_TILE), 0, 0)),
            pl.BlockSpec((None, D_MODEL, IN_COLS), lambda i: (layer, 0, 0)),
            pl.BlockSpec((1, GATE_COLS), lambda i: (0, 0)),
        ],
        out_specs=[pl.BlockSpec((TOK_TILE, n), lambda i: (i, 0)) for n, _ in outs],
        compiler_params=pltpu.CompilerParams(dimension_semantics=("arbitrary",), vmem_limit_bytes=VMEM_LIMIT),
        name="in_proj",
    )(*stream, mod, w_in_bf16, b_gate)


def _mixer_kernel(*refs, T, has_state):
    if has_state:
        (uh_ref, uf_ref, uc_ref, lb_ref, ng_ref, cw_ref, dft_ref, bdc_ref, bds_ref, s0_ref,
         mix_ref, st_ref, o_ref) = refs
        sfin_ref = None
    else:
        (uh_ref, uf_ref, uc_ref, lb_ref, ng_ref, cw_ref, dft_ref, bdc_ref, bds_ref,
         mix_ref, sfin_ref, st_ref, o_ref) = refs
        s0_ref = None
    n_chunks = T // CHUNK
    W = HGRN_WIDTH

    for d in range(2):
        for h in range(HEADS):
            if has_state:
                st_ref[d, h] = s0_ref[0, d, h].T
            else:
                st_ref[d, h] = jnp.zeros((HEAD_DIM, HEAD_DIM), F32)
    o_ref[...] = jnp.zeros_like(o_ref)

    row = lax.broadcasted_iota(jnp.int32, (CHUNK, CHUNK), 0)
    col = lax.broadcasted_iota(jnp.int32, (CHUNK, CHUNK), 1)
    keep = (col <= row, col >= row)
    tri = tuple(k.astype(F32).astype(BF16) for k in keep)

    def chunk_step(i, carry):
        deep = []
        for sub, d in ((s, d) for s in range(CHUNK_UNROLL) for d in range(2)):
            c = i * CHUNK_UNROLL + sub
            c = c if d == 0 else n_chunks - 1 - c
            rows = pl.ds(pl.multiple_of(c * CHUNK, CHUNK), CHUNK)
            q = _silu(uh_ref[rows, 0:W])
            v = uh_ref[rows, W:2 * W].astype(BF16)
            z = uh_ref[rows, (2 + d) * W:(3 + d) * W]
            lb = lb_ref[d:d + 1, :]
            e = jnp.exp(-jnp.abs(z))
            r = 1.0 / (1.0 + e)
            er = e * r
            pos = z >= 0.0
            sig_p = jnp.where(pos, r, er)
            sig_n = jnp.where(pos, er, r)
            f = lb + (1.0 - lb) * sig_p
            lf = jnp.log(jnp.maximum(f, F_MIN))
            k = (1.0 - lb) * sig_n
            lf_hi = lf.astype(BF16)
            lf_lo = (lf - lf_hi.astype(F32)).astype(BF16)
            b = _dot(tri[d], lf_hi) + _dot(tri[d], lf_lo)
            g = b[CHUNK - 1:CHUNK, :] if d == 0 else b[0:1, :]

            def window(p_level, q=q, k=k, b=b):
                dist = -p_level - b
                inside = jnp.logical_and(dist >= 0.0, dist < WINDOW)
                k_w = jnp.where(inside, k * jnp.exp(jnp.minimum(dist, WINDOW) - 0.5 * WINDOW), 0.0)
                q_w = q * jnp.exp(jnp.minimum(-dist, 0.0) + 0.5 * WINDOW)
                return q_w.astype(BF16), k_w.astype(BF16)

            wins = [window(WINDOW * p) for p in range(STATIC_WINDOWS)]
            q_abs = (q * jnp.exp(b)).astype(BF16)
            k_end = (k * jnp.exp(g - b)).astype(BF16)
            decay = jnp.exp(g)
            outs = []
            for h in range(HEADS):
                ls = slice(h * HEAD_DIM, (h + 1) * HEAD_DIM)
                q_cat = jnp.concatenate([w[0][:, ls] for w in wins], axis=-1)
                k_cat = jnp.concatenate([w[1][:, ls] for w in wins], axis=-1)
                a = jnp.where(keep[d], _dot_nt(q_cat, k_cat), 0.0).astype(BF16)
                s_t = st_ref[d, h]
                o_h = _dot(a, v[:, ls]) + _dot_nt(q_abs[:, ls], s_t.astype(BF16))
                st_ref[d, h] = s_t * decay[:, ls] + _dot_tn(v[:, ls], k_end[:, ls])
                outs.append(o_h)
            o_ref[rows, :] += jnp.concatenate(outs, axis=-1)
            deep.append((d, window, v, rows, jnp.min(b)))

        deepest = functools.reduce(jnp.minimum, [entry[4] for entry in deep])
        n_windows = jnp.floor(-deepest * (1.0 / WINDOW)).astype(jnp.int32) + 1

        def extra_window(p, carry2):
            for d, window, v, rows, _ in deep:
                q_w, k_w = window(WINDOW * p.astype(F32))
                extra = []
                for h in range(HEADS):
                    ls = slice(h * HEAD_DIM, (h + 1) * HEAD_DIM)
                    a = jnp.where(keep[d], _dot_nt(q_w[:, ls], k_w[:, ls]), 0.0).astype(BF16)
                    extra.append(_dot(a, v[:, ls]))
                o_ref[rows, :] += jnp.concatenate(extra, axis=-1)
            return carry2

        lax.fori_loop(STATIC_WINDOWS, n_windows, extra_window, 0)
        return carry

    lax.fori_loop(0, n_chunks // CHUNK_UNROLL, chunk_step, 0)

    if sfin_ref is not None:
        for d in range(2):
            for h in range(HEADS):
                sfin_ref[0, d, h] = st_ref[d, h].T

    o = o_ref[...]
    parts = []
    for h in range(HEADS):
        oh = o[:, h * HEAD_DIM:(h + 1) * HEAD_DIM]
        parts.append(oh * lax.rsqrt(jnp.mean(oh * oh, axis=-1, keepdims=True) + RMS_EPS))
    a_out = jnp.concatenate(parts, axis=-1) * ng_ref[...] * _silu(uh_ref[:, 4 * W:5 * W])
    mix_ref[:, 0:W] = a_out.astype(BF16)

    zf = uf_ref[...].astype(BF16)
    zc = _dot(zf, bdc_ref[...]).astype(BF16)
    zs = _dot(zf, bds_ref[...]).astype(BF16)
    four = _dot(dft_ref[...], jnp.concatenate([zc, zs], axis=0)) * (1.0 / math.sqrt(T * FOURIER_GROUP_DIM))
    mix_ref[:, W:W + FOURIER_WIDTH] = four.astype(BF16)

    cb = uc_ref[:, 0:CONV_WIDTH]
    zz = uc_ref[:, CONV_WIDTH:2 * CONV_WIDTH] * uc_ref[:, 2 * CONV_WIDTH:3 * CONV_WIDTH]
    t_idx = lax.broadcasted_iota(jnp.int32, (T, CONV_WIDTH), 0)
    z_prev = jnp.where(t_idx == 0, 0.0, pltpu.roll(zz, 1, axis=0))
    z_next = jnp.where(t_idx == T - 1, 0.0, pltpu.roll(zz, T - 1, axis=0))
    y = cw_ref[0:1, :] * z_prev + cw_ref[1:2, :] * zz + cw_ref[2:3, :] * z_next
    mix_ref[:, W + FOURIER_WIDTH:] = (cb * y).astype(BF16)


def _mixer(uh, uf, uc, lb, norm_g, conv_w, dft, bdc, bds, T, n_seq, blk0, state0=None, layer=0):
    has_state = state0 is not None
    const2 = lambda s: (0, 0)
    in_specs = [
        pl.BlockSpec((T, HGRN_COLS), lambda s: (blk0 + s, 0)),
        pl.BlockSpec((T, FOURIER_WIDTH), lambda s: (blk0 + s, 0)),
        pl.BlockSpec((T, CONV_COLS), lambda s: (blk0 + s, 0)),
        pl.BlockSpec((2, HGRN_WIDTH), const2),
        pl.BlockSpec((1, HGRN_WIDTH), const2),
        pl.BlockSpec((3, CONV_WIDTH), const2),
        pl.BlockSpec((T, 2 * T), const2),
        pl.BlockSpec((FOURIER_WIDTH, FOURIER_WIDTH), const2),
        pl.BlockSpec((FOURIER_WIDTH, FOURIER_WIDTH), const2),
    ]
    args = [uh, uf, uc, lb, norm_g, conv_w, dft, bdc, bds]
    mix_shape = jax.ShapeDtypeStruct((n_seq * T, D_MODEL), BF16)
    mix_spec = pl.BlockSpec((T, D_MODEL), lambda s: (s, 0))
    st_block = (1, 2, HEADS, HEAD_DIM, HEAD_DIM)
    if has_state:
        in_specs.append(pl.BlockSpec((1, None) + st_block[1:], lambda s: (s, layer, 0, 0, 0, 0)))
        args.append(state0)
        out_shape = [mix_shape]
        out_specs = [mix_spec]
    else:
        out_shape = [mix_shape, jax.ShapeDtypeStruct((n_seq, 2, HEADS, HEAD_DIM, HEAD_DIM), F32)]
        out_specs = [mix_spec, pl.BlockSpec(st_block, lambda s: (s, 0, 0, 0, 0))]
    return pl.pallas_call(
        functools.partial(_mixer_kernel, T=T, has_state=has_state),
        out_shape=out_shape,
        grid=(n_seq,),
        in_specs=in_specs,
        out_specs=out_specs,
        scratch_shapes=[pltpu.VMEM((2, HEADS, HEAD_DIM, HEAD_DIM), F32), pltpu.VMEM((T, HGRN_WIDTH), F32)],
        compiler_params=pltpu.CompilerParams(dimension_semantics=("arbitrary",), vmem_limit_bytes=VMEM_LIMIT),
        name="mixer_latent" if has_state else "mixer_context",
    )(*args)


def _merge_kernel(*refs, n_stream):
    (mixc_ref, mixl_ref, gates_ref, mod_ref, wp_ref, wo_ref, g1_ref, b1_ref, wrh_ref, wrl_ref, br_ref,
     x1_ref, h2_ref, apos_ref, gate_ref, n16_ref) = refs[n_stream:]
    W = HGRN_WIDTH
    gates = gates_ref[...].astype(F32)
    mix = _stream_tile((mixc_ref, mixl_ref))
    edges = (0, W, W + FOURIER_WIDTH, D_MODEL)
    merged = None
    for j in range(N_BRANCHES):
        p = _dot(mix[:, edges[j]:edges[j + 1]], wp_ref[edges[j]:edges[j + 1], :])
        term = gates[:, j * D_MODEL:(j + 1) * D_MODEL] * p
        merged = term if merged is None else merged + term
    y = _dot(merged.astype(BF16), wo_ref[...])
    x1 = _ln(ALPHA * _stream_tile(refs[:n_stream]) + mod_ref[2:3, :] * y) * g1_ref[...] + b1_ref[...]
    x1_ref[...] = x1
    h2 = _ln(x1) * (1.0 + mod_ref[4:5, :]) + mod_ref[3:4, :]
    h2_hi = h2.astype(BF16)
    h2_ref[...] = h2_hi
    h2_lo = (h2 - h2_hi.astype(F32)).astype(BF16)
    logits = _dot(h2_hi, wrh_ref[...]) + (_dot(h2_hi, wrl_ref[...]) + _dot(h2_lo, wrh_ref[...]))
    scores = jax.nn.sigmoid(logits)
    sel = scores + br_ref[...]
    lane_f = lax.broadcasted_iota(jnp.int32, sel.shape, 1).astype(F32)
    hits = []
    for _ in range(TOP_K):
        top = jnp.max(sel, axis=-1, keepdims=True)
        first = jnp.min(jnp.where(sel == top, lane_f, float(N_EXPERTS)), axis=-1, keepdims=True)
        hit = lane_f == first
        hits.append(hit)
        sel = jnp.where(hit, -jnp.inf, sel)
    chosen = functools.reduce(jnp.logical_or, hits)
    picked = jnp.where(chosen, scores, 0.0)
    denom = jnp.sum(picked, axis=-1, keepdims=True) + 1e-20
    gate = ROUTED_SCALE * picked / denom

    chosen_f = chosen.astype(F32)
    r_i = lax.broadcasted_iota(jnp.int32, (TOK_TILE, TOK_TILE), 0)
    c_i = lax.broadcasted_iota(jnp.int32, (TOK_TILE, TOK_TILE), 1)
    earlier = (c_i < r_i).astype(F32).astype(BF16)
    rank = _dot(earlier, chosen_f.astype(BF16))
    count = jnp.sum(chosen_f, axis=0, keepdims=True)
    n16 = jnp.floor((count + (ROW_CHUNK - 1)) * (1.0 / ROW_CHUNK))
    e_r = lax.broadcasted_iota(jnp.int32, (N_EXPERTS, N_EXPERTS), 0)
    e_c = lax.broadcasted_iota(jnp.int32, (N_EXPERTS, N_EXPERTS), 1)
    before = (e_r < e_c).astype(F32).astype(BF16)
    seg_start = ROW_CHUNK * _dot(jnp.broadcast_to(n16, (8, N_EXPERTS)).astype(BF16), before)[0:1, :]
    row_of = seg_start + rank
    lane_k = lax.broadcasted_iota(jnp.int32, (TOK_TILE, TOP_K), 1)
    apos = jnp.zeros((TOK_TILE, TOP_K), F32)
    for k, hit in enumerate(hits):
        apos = jnp.where(lane_k == k, jnp.sum(jnp.where(hit, row_of, 0.0), axis=-1, keepdims=True), apos)
    apos_ref[...] = apos
    gate_ref[...] = jnp.concatenate([gate, jnp.zeros((TOK_TILE, GATE_LANES - N_EXPERTS), F32)], axis=-1).astype(BF16)
    n16_ref[...] = n16


def _merge(stream, mix_ctx, mix_lat, gates, mod, wp, wo, ln_g, ln_b, wr_hi, wr_lo, b_router, layer):
    tok = lambda n: pl.BlockSpec((TOK_TILE, n), lambda i: (i, 0))
    const2 = lambda i: (0, 0)
    return pl.pallas_call(
        functools.partial(_merge_kernel, n_stream=len(stream)),
        out_shape=[jax.ShapeDtypeStruct((N_TOK, D_MODEL), F32), jax.ShapeDtypeStruct((N_TOK, D_MODEL), BF16),
                   jax.ShapeDtypeStruct((N_TOK, TOP_K), F32), jax.ShapeDtypeStruct((N_TOK, GATE_LANES), BF16),
                   jax.ShapeDtypeStruct((N_TILES, 1, N_EXPERTS), F32)],
        grid=(N_TILES,),
        in_specs=_stream_specs(len(stream)) + _stream_specs(2) + [
            tok(GATE_COLS),
            pl.BlockSpec((None, None, 6, D_MODEL), lambda i: (layer, _cond_row_of_tile(i, TOK_TILE), 0, 0)),
            pl.BlockSpec((D_MODEL, D_MODEL), const2),
            pl.BlockSpec((D_MODEL, D_MODEL), const2),
            pl.BlockSpec((1, D_MODEL), const2),
            pl.BlockSpec((1, D_MODEL), const2),
            pl.BlockSpec((D_MODEL, N_EXPERTS), const2),
            pl.BlockSpec((D_MODEL, N_EXPERTS), const2),
            pl.BlockSpec((1, N_EXPERTS), const2),
        ],
        out_specs=[tok(D_MODEL), tok(D_MODEL), tok(TOP_K), tok(GATE_LANES),
                   pl.BlockSpec((None, 1, N_EXPERTS), lambda i: (i, 0, 0))],
        compiler_params=pltpu.CompilerParams(dimension_semantics=("arbitrary",), vmem_limit_bytes=VMEM_LIMIT),
        name="merge_router",
    )(*stream, mix_ctx, mix_lat, gates, mod, wp, wo, ln_g, ln_b, wr_hi, wr_lo, b_router)


def _routing_tables(n16):
    n16 = n16.astype(jnp.int32)
    hi = jnp.cumsum(n16, axis=1)
    lo = hi - n16
    n_chunks = hi[:, -1]
    eo = jnp.cumsum(n16, axis=0) - n16
    tc = jnp.sum(n16, axis=0)
    tiles = (tc + CHUNKS_PER_GMM_TILE - 1) // CHUNKS_PER_GMM_TILE
    tile_end = jnp.cumsum(tiles)
    tile_off = tile_end - tiles
    n_used = tile_end[-1]
    slot = CHUNKS_PER_GMM_TILE * tile_off[None, :] + eo
    copies = (_copy_runs(n16 // 2, lo, slot, 2, MAX_PAIRS)
              + _copy_runs(n16 % 2, lo + n16 - 1, slot + n16 - 1, 1, N_EXPERTS))
    pad_start = CHUNKS_PER_GMM_TILE * tile_off + tc
    pad_count = CHUNKS_PER_GMM_TILE * tiles - tc
    return (copies, n_chunks, tile_off.astype(jnp.int32), tiles.astype(jnp.int32),
            jnp.reshape(n_used, (1,)).astype(jnp.int32), pad_start.astype(jnp.int32), pad_count.astype(jnp.int32))


def _copy_runs(count, first_src, first_dst, stride, max_runs):
    hi = jnp.cumsum(count, axis=1)
    lo = hi - count
    r = jnp.arange(max_runs, dtype=jnp.int32)[None, :, None]
    owner = jnp.logical_and(r >= lo[:, None, :], r < hi[:, None, :])

    def table(first):
        picked = jnp.sum(jnp.where(owner, (first - stride * lo)[:, None, :], 0), axis=-1)
        return (stride * r[:, :, 0] + picked).reshape(-1).astype(jnp.int32)

    return table(first_src), table(first_dst), hi[:, -1].astype(jnp.int32)


def _one_hot_rows(apos_ref, n_rows):
    r = lax.broadcasted_iota(jnp.int32, (TOK_TILE, n_rows), 1).astype(F32)
    pt = jnp.zeros((TOK_TILE, n_rows), F32)
    for k in range(TOP_K):
        pt = jnp.where(r == apos_ref[:, k:k + 1], 1.0, pt)
    return pt.astype(BF16)


def _chunk_copy(src_ref, src_chunk, dst_ref, dst_chunk, sem):
    rows = lambda c: pl.ds(pl.multiple_of(c * ROW_CHUNK, ROW_CHUNK), ROW_CHUNK)
    return pltpu.make_async_copy(src_ref.at[rows(src_chunk), :], dst_ref.at[rows(dst_chunk), :], sem)


def _run_rows(chunk, n_chunks):
    return pl.ds(pl.multiple_of(chunk * ROW_CHUNK, ROW_CHUNK), n_chunks * ROW_CHUNK)


def _for_each_chunk(n, fn):
    def trip(i, carry):
        for u in range(CHUNK_LOOP_UNROLL):
            c = i * CHUNK_LOOP_UNROLL + u
            if u == 0:
                fn(c)
            else:
                pl.when(c < n)(functools.partial(fn, c))
        return carry
    lax.fori_loop(0, (n + (CHUNK_LOOP_UNROLL - 1)) // CHUNK_LOOP_UNROLL, trip, 0)


def _dispatch_kernel(psrc_ref, pdst_ref, pn_ref, ssrc_ref, sdst_ref, sn_ref, nch_ref, pstart_ref, pcount_ref, nu_ref,
                     h_ref, gate_ref, apos_ref, xs_hbm, pt_ref, stage_ref, zero_ref, sem):
    j = pl.program_id(0)
    buf = j % 2
    runs = ((psrc_ref, pdst_ref, pn_ref), (ssrc_ref, sdst_ref, sn_ref))
    pad_sem, tile_sem = 2 * len(runs), 2 * len(runs) + 1

    def run_copy(k, b, src_chunk, slot):
        return pltpu.make_async_copy(stage_ref.at[b, _run_rows(src_chunk, RUN_CHUNKS[k]), :],
                                     xs_hbm.at[_run_rows(slot, RUN_CHUNKS[k]), :], sem.at[2 * k + b])

    def wait_tile(b, jj):
        for k, (_, _, n_ref) in enumerate(runs):
            _for_each_chunk(n_ref[jj], lambda r, k=k: run_copy(k, b, 0, 0).wait())

    @pl.when(j < N_TILES)
    def _():
        @pl.when(j >= 2)
        def _():
            wait_tile(buf, j - 2)

        def sort_rows(n_rows):
            pt = _one_hot_rows(apos_ref, n_rows)
            pt_ref[:, 0:n_rows] = pt
            if n_rows < TILE_ROWS:
                pt_ref[:, n_rows:TILE_ROWS] = jnp.zeros((TOK_TILE, TILE_ROWS - n_rows), BF16)
            h = h_ref[...]
            gate = gate_ref[...]
            for r0 in range(0, n_rows, DISPATCH_ROWS):
                group = pt[:, r0:r0 + DISPATCH_ROWS]
                stage_ref[buf, r0:r0 + DISPATCH_ROWS, 0:D_MODEL] = _dot_tn(group, h).astype(BF16)
                stage_ref[buf, r0:r0 + DISPATCH_ROWS, D_MODEL:ROW_WIDTH] = _dot_tn(group, gate).astype(BF16)

        short = nch_ref[j] * ROW_CHUNK <= SHORT_ROWS
        pl.when(short)(functools.partial(sort_rows, SHORT_ROWS))
        pl.when(jnp.logical_not(short))(functools.partial(sort_rows, TILE_ROWS))

        for k, (src_ref, slot_ref, n_ref) in enumerate(runs):
            base = j * RUN_TABLE_STRIDE[k]
            _for_each_chunk(n_ref[j], lambda r, k=k, base=base, src_ref=src_ref, slot_ref=slot_ref: run_copy(
                k, buf, src_ref[base + r], slot_ref[base + r]).start())

    @pl.when(j == N_TILES)
    def _():
        zero_ref[...] = jnp.zeros_like(zero_ref)

        def tile_copy(t):
            rows = pl.ds(pl.multiple_of(t * GMM_TILE, GMM_TILE), GMM_TILE)
            return pltpu.make_async_copy(zero_ref, xs_hbm.at[rows, :], sem.at[tile_sem])

        def per_expert(e, total):
            def fill(i, carry):
                _chunk_copy(zero_ref, 0, xs_hbm, pstart_ref[e] + i, sem.at[pad_sem]).start()
                return carry
            lax.fori_loop(0, pcount_ref[e], fill, 0)
            return total + pcount_ref[e]
        n_pad = lax.fori_loop(0, N_EXPERTS, per_expert, 0)

        def fill_tile(t, carry):
            tile_copy(t).start()
            return carry
        lax.fori_loop(nu_ref[0], MAX_GMM_TILES, fill_tile, 0)
        wait_tile(0, N_TILES - 2)
        wait_tile(1, N_TILES - 1)

        def drain(i, carry):
            _chunk_copy(zero_ref, 0, xs_hbm, 0, sem.at[pad_sem]).wait()
            return carry
        lax.fori_loop(0, n_pad, drain, 0)

        def drain_tile(t, carry):
            tile_copy(0).wait()
            return carry
        lax.fori_loop(nu_ref[0], MAX_GMM_TILES, drain_tile, 0)


def _dispatch(h2, gate, apos, copies, n_chunks, pad_start, pad_count, n_used):
    last = N_TILES - 1
    tok = lambda n: pl.BlockSpec((TOK_TILE, n), lambda j, *_: (jnp.minimum(j, last), 0))
    grid_spec = pltpu.PrefetchScalarGridSpec(
        num_scalar_prefetch=len(copies) + 4,
        grid=(N_TILES + 1,),
        in_specs=[tok(D_MODEL), tok(GATE_LANES), tok(TOP_K)],
        out_specs=[pl.BlockSpec(memory_space=pl.ANY), tok(TILE_ROWS)],
        scratch_shapes=[pltpu.VMEM((2, TILE_ROWS, ROW_WIDTH), BF16), pltpu.VMEM((GMM_TILE, ROW_WIDTH), BF16),
                        pltpu.SemaphoreType.DMA((2 * len(RUN_CHUNKS) + 2,))],
    )
    return pl.pallas_call(
        _dispatch_kernel,
        out_shape=[jax.ShapeDtypeStruct((SORTED_ROWS, ROW_WIDTH), BF16),
                   jax.ShapeDtypeStruct((N_TOK, TILE_ROWS), BF16)],
        grid_spec=grid_spec,
        compiler_params=pltpu.CompilerParams(dimension_semantics=("arbitrary",), vmem_limit_bytes=VMEM_LIMIT),
        name="moe_dispatch",
    )(*copies, n_chunks, pad_start, pad_count, n_used, h2, gate, apos)


def _gmm_kernel(off_ref, cnt_ref, nu_ref, x_hbm, wg_ref, wu_ref, wd_ref, y_hbm, xbuf, ybuf, wg_b, wu_b, wd_b,
                in_sem, out_sem):
    e = pl.program_id(0)
    n_used = nu_ref[0]

    def rows(g):
        return pl.ds(pl.multiple_of(g * GMM_TILE, GMM_TILE), GMM_TILE)

    def fetch(g):
        slot = g % GMM_IN_SLOTS
        return pltpu.make_async_copy(x_hbm.at[rows(g), :], xbuf.at[slot], in_sem.at[slot])

    def flush(g):
        slot = g % GMM_OUT_SLOTS
        return pltpu.make_async_copy(ybuf.at[slot], y_hbm.at[rows(g), 0:D_MODEL], out_sem.at[slot])

    @pl.when(e == 0)
    def _():
        for g in range(GMM_AHEAD):
            @pl.when(g < n_used)
            def _(g=g):
                fetch(g).start()

    wg_b[...] = wg_ref[...].astype(BF16)
    wu_b[...] = wu_ref[...].astype(BF16)
    wd_b[...] = wd_ref[...].astype(BF16)

    def process(g0, width):
        for w in range(width):
            @pl.when(g0 + w + GMM_AHEAD < n_used)
            def _(w=w):
                fetch(g0 + w + GMM_AHEAD).start()
        for w in range(width):
            fetch(g0 + w).wait()
        xg = jnp.concatenate([xbuf[(g0 + w) % GMM_IN_SLOTS] for w in range(width)], axis=0)
        x = xg[:, 0:D_MODEL]
        gates = xg[:, D_MODEL:ROW_WIDTH].astype(F32)
        lane = lax.broadcasted_iota(jnp.int32, gates.shape, 1)
        weight = jnp.sum(jnp.where(lane == e, gates, 0.0), axis=-1, keepdims=True)
        hid = _silu(_dot(x, wg_b[...])) * _dot(x, wu_b[...])
        y = (_dot(hid.astype(BF16), wd_b[...]) * weight).astype(BF16)
        for w in range(width):
            g = g0 + w

            @pl.when(g >= GMM_OUT_SLOTS)
            def _(g=g):
                flush(g - GMM_OUT_SLOTS).wait()

            ybuf[g % GMM_OUT_SLOTS] = y[w * GMM_TILE:(w + 1) * GMM_TILE]
            flush(g).start()

    first = off_ref[e]
    count = cnt_ref[e]

    done = 0
    width = GMM_WIDTH
    while width >= 1:
        trips = (count - done) // width

        def block(p, carry, done=done, width=width):
            process(first + done + width * p, width)
            return carry

        lax.fori_loop(0, trips, block, 0)
        done = done + trips * width
        width //= 2

    @pl.when(e == N_EXPERTS - 1)
    def _():
        for k in range(GMM_OUT_SLOTS):
            @pl.when(n_used - 1 - k >= 0)
            def _(k=k):
                flush(n_used - 1 - k).wait()


def _gmm(xs, weg, weu, wed, tile_off, tiles, n_used, layer):
    w_in_spec = pl.BlockSpec((None, None, D_MODEL, D_EXPERT), lambda e, *_: (layer, e, 0, 0))
    grid_spec = pltpu.PrefetchScalarGridSpec(
        num_scalar_prefetch=3,
        grid=(N_EXPERTS,),
        in_specs=[
            pl.BlockSpec(memory_space=pl.ANY),
            w_in_spec, w_in_spec,
            pl.BlockSpec((None, None, D_EXPERT, D_MODEL), lambda e, *_: (layer, e, 0, 0)),
        ],
        out_specs=pl.BlockSpec(memory_space=pl.ANY),
        scratch_shapes=[pltpu.VMEM((GMM_IN_SLOTS, GMM_TILE, ROW_WIDTH), BF16),
                        pltpu.VMEM((GMM_OUT_SLOTS, GMM_TILE, D_MODEL), BF16),
                        pltpu.VMEM((D_MODEL, D_EXPERT), BF16), pltpu.VMEM((D_MODEL, D_EXPERT), BF16),
                        pltpu.VMEM((D_EXPERT, D_MODEL), BF16),
                        pltpu.SemaphoreType.DMA((GMM_IN_SLOTS,)), pltpu.SemaphoreType.DMA((GMM_OUT_SLOTS,))],
    )
    return pl.pallas_call(
        _gmm_kernel,
        out_shape=jax.ShapeDtypeStruct((SORTED_ROWS, ROW_WIDTH), BF16),
        grid_spec=grid_spec,
        input_output_aliases={3: 0},
        compiler_params=pltpu.CompilerParams(dimension_semantics=("arbitrary",), vmem_limit_bytes=VMEM_LIMIT),
        name="moe_gmm",
    )(tile_off, tiles, n_used, xs, weg, weu, wed)


def _combine_kernel(psrc_ref, pdst_ref, pn_ref, ssrc_ref, sdst_ref, sn_ref, nch_ref, ys_hbm, pt_ref, h_ref, x1_ref,
                    mod_ref, wsg_ref, wsu_ref, wsd_ref, g2_ref, b2_ref, oc_ref, ol_ref, stage_ref, routed_ref, sem):
    j = pl.program_id(0)
    buf = j % 2
    runs = ((psrc_ref, pdst_ref, pn_ref), (ssrc_ref, sdst_ref, sn_ref))

    def run_fetch(k, b, slot, chunk):
        return pltpu.make_async_copy(ys_hbm.at[_run_rows(slot, RUN_CHUNKS[k]), 0:D_MODEL],
                                     stage_ref.at[b, _run_rows(chunk, RUN_CHUNKS[k]), :], sem.at[2 * k + b])

    def fetch(jj, b):
        for k, (chunk_ref, slot_ref, n_ref) in enumerate(runs):
            base = jj * RUN_TABLE_STRIDE[k]
            _for_each_chunk(n_ref[jj], lambda r, k=k, base=base, chunk_ref=chunk_ref, slot_ref=slot_ref: run_fetch(
                k, b, slot_ref[base + r], chunk_ref[base + r]).start())

    @pl.when(j == 0)
    def _():
        stage_ref[...] = jnp.zeros_like(stage_ref)
        fetch(0, 0)

    @pl.when(j + 1 < N_TILES)
    def _():
        fetch(j + 1, 1 - buf)

    h = h_ref[...]
    hid = _silu(_dot(h, wsg_ref[...])) * _dot(h, wsu_ref[...])
    shared = _dot(hid.astype(BF16), wsd_ref[...])

    for k, (_, _, n_ref) in enumerate(runs):
        _for_each_chunk(n_ref[j], lambda r, k=k: run_fetch(k, buf, 0, 0).wait())

    def gather_rows(n_rows):
        routed_ref[...] = _dot(pt_ref[:, 0:n_rows], stage_ref[buf, 0:n_rows, :])

    short = nch_ref[j] * ROW_CHUNK <= SHORT_ROWS
    pl.when(short)(functools.partial(gather_rows, SHORT_ROWS))
    pl.when(jnp.logical_not(short))(functools.partial(gather_rows, TILE_ROWS))
    r = ALPHA * x1_ref[...] + mod_ref[5:6, :] * (routed_ref[...] + shared)
    out = _ln(r) * g2_ref[...] + b2_ref[...]

    @pl.when(j < CTX_TILES)
    def _():
        oc_ref[...] = out

    @pl.when(j >= CTX_TILES)
    def _():
        ol_ref[...] = out


def _combine(ys, pt, h2, x1, mod, wsg, wsu, wsd, ln_g, ln_b, copies, n_chunks, layer):
    tok = lambda n: pl.BlockSpec((TOK_TILE, n), lambda j, *_: (j, 0))
    const2 = lambda j, *_: (0, 0)
    grid_spec = pltpu.PrefetchScalarGridSpec(
        num_scalar_prefetch=len(copies) + 1,
        grid=(N_TILES,),
        in_specs=[
            pl.BlockSpec(memory_space=pl.ANY),
            tok(TILE_ROWS), tok(D_MODEL), tok(D_MODEL),
            pl.BlockSpec((None, None, 6, D_MODEL), lambda j, *_: (layer, _cond_row_of_tile(j, TOK_TILE), 0, 0)),
            pl.BlockSpec((D_MODEL, D_SHARED), const2),
            pl.BlockSpec((D_MODEL, D_SHARED), const2),
            pl.BlockSpec((D_SHARED, D_MODEL), const2),
            pl.BlockSpec((1, D_MODEL), const2),
            pl.BlockSpec((1, D_MODEL), const2),
        ],
        out_specs=_stream_specs(2),
        scratch_shapes=[pltpu.VMEM((2, TILE_ROWS, D_MODEL), BF16), pltpu.VMEM((TOK_TILE, D_MODEL), F32),
                        pltpu.SemaphoreType.DMA((2 * len(RUN_CHUNKS),))],
    )
    return pl.pallas_call(
        _combine_kernel,
        out_shape=[jax.ShapeDtypeStruct((N_CTX, D_MODEL), F32), jax.ShapeDtypeStruct((N_LAT, D_MODEL), F32)],
        grid_spec=grid_spec,
        compiler_params=pltpu.CompilerParams(dimension_semantics=("arbitrary",), vmem_limit_bytes=VMEM_LIMIT),
        name="moe_combine",
    )(*copies, n_chunks, ys, pt, h2, x1, mod, wsg, wsu, wsd, ln_g, ln_b)


def kernel(x_prompt, x_sample, c, state_hgrn, c_ctx, w_ada, b_ada, w_in, b_gate, hgrn_lb, hgrn_norm, conv_w,
           w_proj_hgrn, w_proj_fourier, w_proj_conv, w_out, ln1_g, ln1_b, ln2_g, ln2_b, w_router, b_router,
           w_exp_gate, w_exp_up, w_exp_down, w_sh_gate, w_sh_up, w_sh_down):
    lb_sm = jax.nn.softmax(hgrn_lb.astype(F32), axis=0)
    lb_all = jnp.cumsum(lb_sm, axis=0) - lb_sm[:1]

    pos = jnp.asarray(_pos_emb_table())
    dft_ctx = jnp.asarray(_dft_time(SEQ)).astype(BF16)
    dft_lat = jnp.asarray(_dft_time(DEC_SEQ)).astype(BF16)
    bdc, bds = (jnp.asarray(m).astype(BF16) for m in _dft_channel())

    cond = jnp.concatenate([c_ctx[None, :], c, jnp.zeros((N_COND - 1 - DEC_BATCH, D_MODEL), F32)], axis=0)
    mod = _ada_mod(cond, w_ada, b_ada).reshape(DEPTH, N_COND, 6, D_MODEL)

    w_in_b = w_in.astype(BF16)
    wp_b = jnp.concatenate([w_proj_hgrn, w_proj_fourier, w_proj_conv], axis=1).astype(BF16)
    wo_b = w_out.astype(BF16)
    wr_hi = w_router.astype(BF16)
    wr_lo = (w_router - wr_hi.astype(F32)).astype(BF16)
    wsg_b, wsu_b, wsd_b = w_sh_gate.astype(BF16), w_sh_up.astype(BF16), w_sh_down.astype(BF16)

    stream = (x_prompt.reshape(N_CTX, D_MODEL), x_sample.reshape(N_LAT, D_MODEL), pos)
    new_states = []
    for l in range(DEPTH):
        uh, uf, uc, gates = _in_proj(stream, mod, w_in_b, b_gate[l][None, :], l)
        small = (lb_all[l], hgrn_norm[l][None, :], conv_w[l])
        mix_ctx, s_ctx = _mixer(uh, uf, uc, *small, dft_ctx, bdc, bds, T=SEQ, n_seq=BATCH, blk0=0)
        (mix_lat,) = _mixer(uh, uf, uc, *small, dft_lat, bdc, bds, T=DEC_SEQ, n_seq=DEC_BATCH,
                            blk0=N_CTX // DEC_SEQ, state0=state_hgrn, layer=l)
        new_states.append(s_ctx)
        x1, h2, apos, gate, n16 = _merge(stream, mix_ctx, mix_lat, gates, mod, wp_b[l], wo_b[l],
                                         ln1_g[l][None, :], ln1_b[l][None, :], wr_hi[l], wr_lo[l],
                                         b_router[l][None, :], l)
        copies, n_chunks, tile_off, tiles, n_used, pad_start, pad_count = _routing_tables(
            n16.reshape(N_TILES, N_EXPERTS))
        xs, pt = _dispatch(h2, gate, apos, copies, n_chunks, pad_start, pad_count, n_used)
        ys = _gmm(xs, w_exp_gate, w_exp_up, w_exp_down, tile_off, tiles, n_used, l)
        stream = tuple(_combine(ys, pt, h2, x1, mod, wsg_b[l], wsu_b[l], wsd_b[l], ln2_g[l][None, :],
                                ln2_b[l][None, :], copies, n_chunks, l))
    y_prompt = stream[0].reshape(BATCH, SEQ, D_MODEL)
    y_sample = stream[1].reshape(DEC_BATCH, DEC_SEQ, D_MODEL)
    return (y_prompt, y_sample, jnp.stack(new_states, axis=1))
```

```python
import functools
import math

import numpy as np
import jax
import jax.numpy as jnp
from jax import lax
from jax.experimental import pallas as pl
from jax.experimental.pallas import tpu as pltpu

F32 = jnp.float32
BF16 = jnp.bfloat16

D_MODEL = 1024
BATCH = 16
SEQ = 256
DEPTH = 2
DEC_BATCH = 4
DEC_SEQ = 1024
GRID_W = 64
HEADS = 4
HEAD_DIM = 128
HGRN_WIDTH = HEADS * HEAD_DIM
FOURIER_GROUPS = 4
FOURIER_GROUP_DIM = 64
FOURIER_WIDTH = FOURIER_GROUPS * FOURIER_GROUP_DIM
CONV_WIDTH = 256
N_BRANCHES = 3
HGRN_COLS = 5 * HGRN_WIDTH
CONV_COLS = 3 * CONV_WIDTH
GATE_COLS = N_BRANCHES * D_MODEL
IN_COLS = HGRN_COLS + FOURIER_WIDTH + CONV_COLS + GATE_COLS
N_EXPERTS = 64
TOP_K = 8
D_EXPERT = 256
D_SHARED = 256
ROUTED_SCALE = 2.5
ALPHA = (2 * DEPTH) ** 0.25
LN_EPS = 1e-6
RMS_EPS = 1e-6
F_MIN = 1e-30

N_CTX = BATCH * SEQ
N_LAT = DEC_BATCH * DEC_SEQ
N_TOK = N_CTX + N_LAT
N_COND = 8

TOK_TILE = 256
CHUNK = 64
CHUNK_UNROLL = 2
WINDOW = 120.0
STATIC_WINDOWS = 2
ADA_TILE = 1536
VMEM_LIMIT = 56 * 1024 * 1024

N_TILES = N_TOK // TOK_TILE
ROW_CHUNK = 16
CHUNK_LOOP_UNROLL = 4
TILE_CHUNKS = (TOK_TILE * TOP_K + N_EXPERTS * (ROW_CHUNK - 1)) // ROW_CHUNK
RUN_CHUNKS = (2, 1)
MAX_PAIRS = TILE_CHUNKS // 2
RUN_TABLE_STRIDE = (MAX_PAIRS, N_EXPERTS)
DISPATCH_ROWS = 192
EXTENT_STEP = 2 * DISPATCH_ROWS
TILE_ROWS = -(-TILE_CHUNKS * ROW_CHUNK // EXTENT_STEP) * EXTENT_STEP
SHORT_ROWS = TILE_ROWS - EXTENT_STEP
GMM_TILE = 256
GMM_WIDTH = 4
GMM_AHEAD = 6
GMM_IN_SLOTS = GMM_WIDTH + GMM_AHEAD
GMM_OUT_SLOTS = 2 * GMM_WIDTH
CHUNKS_PER_GMM_TILE = GMM_TILE // ROW_CHUNK
MAX_GMM_TILES = (N_TILES * TILE_CHUNKS + N_EXPERTS * (CHUNKS_PER_GMM_TILE - 1)) // CHUNKS_PER_GMM_TILE
SORTED_ROWS = MAX_GMM_TILES * GMM_TILE
GATE_LANES = 128
ROW_WIDTH = D_MODEL + GATE_LANES


def _cond_row_of_tile(i, tile):
    ctx_tiles = N_CTX // tile
    per_seq = DEC_SEQ // tile
    return jnp.where(i < ctx_tiles, 0, 1 + (i - ctx_tiles) // per_seq)


def _silu(x):
    return x * jax.nn.sigmoid(x)


def _ln(x):
    mu = jnp.mean(x, axis=-1, keepdims=True)
    xc = x - mu
    var = jnp.mean(xc * xc, axis=-1, keepdims=True)
    return xc * lax.rsqrt(var + LN_EPS)


def _dot(a, b):
    return jnp.dot(a, b, preferred_element_type=F32)


def _dot_nt(a, b):
    return lax.dot_general(a, b, (((1,), (1,)), ((), ())), preferred_element_type=F32)


def _dot_tn(a, b):
    return lax.dot_general(a, b, (((0,), (0,)), ((), ())), preferred_element_type=F32)


@functools.lru_cache(maxsize=None)
def _pos_emb_table():
    rows = DEC_SEQ // GRID_W
    t = np.arange(rows * GRID_W)
    r = (t // GRID_W).astype(np.float32)
    col = (t % GRID_W).astype(np.float32)
    quarter = D_MODEL // 4
    omega = (1.0 / (np.float32(10000.0) ** (np.arange(quarter, dtype=np.float32) / np.float32(quarter)))).astype(np.float32)
    ar = (r[:, None] * omega).astype(np.float32)
    ac = (col[:, None] * omega).astype(np.float32)
    return np.concatenate([np.sin(ar), np.cos(ar), np.sin(ac), np.cos(ac)], axis=-1).astype(np.float32)


@functools.lru_cache(maxsize=None)
def _dft_time(T):
    k = np.arange(T)
    ph = 2.0 * np.pi * ((k[:, None] * k[None, :]) % T) / T
    return np.concatenate([np.cos(ph), -np.sin(ph)], axis=1).astype(np.float32)


@functools.lru_cache(maxsize=None)
def _dft_channel():
    n = FOURIER_GROUP_DIM
    k = np.arange(n)
    ph = 2.0 * np.pi * ((k[:, None] * k[None, :]) % n) / n
    eye = np.eye(FOURIER_GROUPS)
    return np.kron(eye, np.cos(ph)).astype(np.float32), np.kron(eye, np.sin(ph)).astype(np.float32)


CTX_TILES = N_CTX // TOK_TILE
LAT_TILES_PER_SEQ = DEC_SEQ // TOK_TILE


def _stream_specs(n):
    specs = [pl.BlockSpec((TOK_TILE, D_MODEL), lambda i, *_: (jnp.minimum(i, CTX_TILES - 1), 0)),
             pl.BlockSpec((TOK_TILE, D_MODEL), lambda i, *_: (jnp.maximum(i - CTX_TILES, 0), 0))]
    if n == 3:
        specs.append(pl.BlockSpec((TOK_TILE, D_MODEL),
                                  lambda i, *_: (jnp.maximum(i - CTX_TILES, 0) % LAT_TILES_PER_SEQ, 0)))
    return specs


def _stream_tile(refs):
    latent = refs[1][...]
    if len(refs) == 3:
        latent = latent + refs[2][...]
    return jnp.where(pl.program_id(0) < CTX_TILES, refs[0][...], latent)


def _ada_kernel(c_ref, w_ref, b_ref, o_ref):
    s = _silu(c_ref[...]).astype(BF16)
    o_ref[...] = _dot(s, w_ref[...].astype(BF16)) + b_ref[...]


def _ada_mod(cond, w_ada, b_ada):
    n_col = 6 * D_MODEL
    return pl.pallas_call(
        _ada_kernel,
        out_shape=jax.ShapeDtypeStruct((DEPTH, N_COND, n_col), F32),
        grid=(DEPTH, n_col // ADA_TILE),
        in_specs=[
            pl.BlockSpec((N_COND, D_MODEL), lambda l, j: (0, 0)),
            pl.BlockSpec((None, D_MODEL, ADA_TILE), lambda l, j: (l, 0, j)),
            pl.BlockSpec((None, 1, ADA_TILE), lambda l, j: (l, 0, j)),
        ],
        out_specs=pl.BlockSpec((None, N_COND, ADA_TILE), lambda l, j: (l, 0, j)),
        compiler_params=pltpu.CompilerParams(dimension_semantics=("arbitrary", "arbitrary"),
                                             vmem_limit_bytes=VMEM_LIMIT),
        name="ada_mod",
    )(cond, w_ada, b_ada.reshape(DEPTH, 1, n_col))


def _in_proj_kernel(*refs, n_stream):
    mod_ref, w_ref, bg_ref, uh_ref, uf_ref, uc_ref, gates_ref = refs[n_stream:]
    sh1 = mod_ref[0:1, :]
    sc1 = mod_ref[1:2, :]
    h = (_ln(_stream_tile(refs[:n_stream])) * (1.0 + sc1) + sh1).astype(BF16)
    c0 = 0
    for ref, n in ((uh_ref, HGRN_COLS), (uf_ref, FOURIER_WIDTH), (uc_ref, CONV_COLS)):
        ref[...] = _dot(h, w_ref[:, c0:c0 + n])
        c0 += n
    gates_ref[...] = jax.nn.sigmoid(_dot(h, w_ref[:, c0:c0 + GATE_COLS]) + bg_ref[...]).astype(BF16)


def _in_proj(stream, mod, w_in_bf16, b_gate, layer):
    n_tiles = N_TOK // TOK_TILE
    outs = ((HGRN_COLS, F32), (FOURIER_WIDTH, F32), (CONV_COLS, F32), (GATE_COLS, BF16))
    return pl.pallas_call(
        functools.partial(_in_proj_kernel, n_stream=len(stream)),
        out_shape=[jax.ShapeDtypeStruct((N_TOK, n), dt) for n, dt in outs],
        grid=(n_tiles,),
        in_specs=_stream_specs(len(stream)) + [
            pl.BlockSpec((None, None, 6, D_MODEL), lambda i: (layer, _cond_row_of_tile(i, TOK_TILE), 0, 0)),
            pl.BlockSpec((None, D_MODEL, IN_COLS), lambda i: (layer, 0, 0)),
            pl.BlockSpec((1, GATE_COLS), lambda i: (0, 0)),
        ],
        out_specs=[pl.BlockSpec((TOK_TILE, n), lambda i: (i, 0)) for n, _ in outs],
        compiler_params=pltpu.CompilerParams(dimension_semantics=("arbitrary",), vmem_limit_bytes=VMEM_LIMIT),
        name="in_proj",
    )(*stream, mod, w_in_bf16, b_gate)


def _mixer_kernel(*refs, T, has_state):
    if has_state:
        (uh_ref, uf_ref, uc_ref, lb_ref, ng_ref, cw_ref, dft_ref, bdc_ref, bds_ref, s0_ref,
         mix_ref, st_ref, o_ref) = refs
        sfin_ref = None
    else:
        (uh_ref, uf_ref, uc_ref, lb_ref, ng_ref, cw_ref, dft_ref, bdc_ref, bds_ref,
         mix_ref, sfin_ref, st_ref, o_ref) = refs
        s0_ref = None
    n_chunks = T // CHUNK
    W = HGRN_WIDTH

    for d in range(2):
        for h in range(HEADS):
            if has_state:
                st_ref[d, h] = s0_ref[0, d, h].T
            else:
                st_ref[d, h] = jnp.zeros((HEAD_DIM, HEAD_DIM), F32)
    o_ref[...] = jnp.zeros_like(o_ref)

    row = lax.broadcasted_iota(jnp.int32, (CHUNK, CHUNK), 0)
    col = lax.broadcasted_iota(jnp.int32, (CHUNK, CHUNK), 1)
    keep = (col <= row, col >= row)
    tri = tuple(k.astype(F32).astype(BF16) for k in keep)

    def chunk_step(i, carry):
        deep = []
        for sub, d in ((s, d) for s in range(CHUNK_UNROLL) for d in range(2)):
            c = i * CHUNK_UNROLL + sub
            c = c if d == 0 else n_chunks - 1 - c
            rows = pl.ds(pl.multiple_of(c * CHUNK, CHUNK), CHUNK)
            q = _silu(uh_ref[rows, 0:W])
            v = uh_ref[rows, W:2 * W].astype(BF16)
            z = uh_ref[rows, (2 + d) * W:(3 + d) * W]
            lb = lb_ref[d:d + 1, :]
            e = jnp.exp(-jnp.abs(z))
            r = 1.0 / (1.0 + e)
            er = e * r
            pos = z >= 0.0
            sig_p = jnp.where(pos, r, er)
            sig_n = jnp.where(pos, er, r)
            f = lb + (1.0 - lb) * sig_p
            lf = jnp.log(jnp.maximum(f, F_MIN))
            k = (1.0 - lb) * sig_n
            lf_hi = lf.astype(BF16)
            lf_lo = (lf - lf_hi.astype(F32)).astype(BF16)
            b = _dot(tri[d], lf_hi) + _dot(tri[d], lf_lo)
            g = b[CHUNK - 1:CHUNK, :] if d == 0 else b[0:1, :]

            def window(p_level, q=q, k=k, b=b):
                dist = -p_level - b
                inside = jnp.logical_and(dist >= 0.0, dist < WINDOW)
                k_w = jnp.where(inside, k * jnp.exp(jnp.minimum(dist, WINDOW) - 0.5 * WINDOW), 0.0)
                q_w = q * jnp.exp(jnp.minimum(-dist, 0.0) + 0.5 * WINDOW)
                return q_w.astype(BF16), k_w.astype(BF16)

            wins = [window(WINDOW * p) for p in range(STATIC_WINDOWS)]
            q_abs = (q * jnp.exp(b)).astype(BF16)
            k_end = (k * jnp.exp(g - b)).astype(BF16)
            decay = jnp.exp(g)
            outs = []
            for h in range(HEADS):
                ls = slice(h * HEAD_DIM, (h + 1) * HEAD_DIM)
                q_cat = jnp.concatenate([w[0][:, ls] for w in wins], axis=-1)
                k_cat = jnp.concatenate([w[1][:, ls] for w in wins], axis=-1)
                a = jnp.where(keep[d], _dot_nt(q_cat, k_cat), 0.0).astype(BF16)
                s_t = st_ref[d, h]
                o_h = _dot(a, v[:, ls]) + _dot_nt(q_abs[:, ls], s_t.astype(BF16))
                st_ref[d, h] = s_t * decay[:, ls] + _dot_tn(v[:, ls], k_end[:, ls])
                outs.append(o_h)
            o_ref[rows, :] += jnp.concatenate(outs, axis=-1)
            deep.append((d, window, v, rows, jnp.min(b)))

        deepest = functools.reduce(jnp.minimum, [entry[4] for entry in deep])
        n_windows = jnp.floor(-deepest * (1.0 / WINDOW)).astype(jnp.int32) + 1

        def extra_window(p, carry2):
            for d, window, v, rows, _ in deep:
                q_w, k_w = window(WINDOW * p.astype(F32))
                extra = []
                for h in range(HEADS):
                    ls = slice(h * HEAD_DIM, (h + 1) * HEAD_DIM)
                    a = jnp.where(keep[d], _dot_nt(q_w[:, ls], k_w[:, ls]), 0.0).astype(BF16)
                    extra.append(_dot(a, v[:, ls]))
                o_ref[rows, :] += jnp.concatenate(extra, axis=-1)
            return carry2

        lax.fori_loop(STATIC_WINDOWS, n_windows, extra_window, 0)
        return carry

    lax.fori_loop(0, n_chunks // CHUNK_UNROLL, chunk_step, 0)

    if sfin_ref is not None:
        for d in range(2):
            for h in range(HEADS):
                sfin_ref[0, d, h] = st_ref[d, h].T

    o = o_ref[...]
    parts = []
    for h in range(HEADS):
        oh = o[:, h * HEAD_DIM:(h + 1) * HEAD_DIM]
        parts.append(oh * lax.rsqrt(jnp.mean(oh * oh, axis=-1, keepdims=True) + RMS_EPS))
    a_out = jnp.concatenate(parts, axis=-1) * ng_ref[...] * _silu(uh_ref[:, 4 * W:5 * W])
    mix_ref[:, 0:W] = a_out.astype(BF16)

    zf = uf_ref[...].astype(BF16)
    zc = _dot(zf, bdc_ref[...]).astype(BF16)
    zs = _dot(zf, bds_ref[...]).astype(BF16)
    four = _dot(dft_ref[...], jnp.concatenate([zc, zs], axis=0)) * (1.0 / math.sqrt(T * FOURIER_GROUP_DIM))
    mix_ref[:, W:W + FOURIER_WIDTH] = four.astype(BF16)

    cb = uc_ref[:, 0:CONV_WIDTH]
    zz = uc_ref[:, CONV_WIDTH:2 * CONV_WIDTH] * uc_ref[:, 2 * CONV_WIDTH:3 * CONV_WIDTH]
    t_idx = lax.broadcasted_iota(jnp.int32, (T, CONV_WIDTH), 0)
    z_prev = jnp.where(t_idx == 0, 0.0, pltpu.roll(zz, 1, axis=0))
    z_next = jnp.where(t_idx == T - 1, 0.0, pltpu.roll(zz, T - 1, axis=0))
    y = cw_ref[0:1, :] * z_prev + cw_ref[1:2, :] * zz + cw_ref[2:3, :] * z_next
    mix_ref[:, W + FOURIER_WIDTH:] = (cb * y).astype(BF16)


def _mixer(uh, uf, uc, lb, norm_g, conv_w, dft, bdc, bds, T, n_seq, blk0, state0=None, layer=0):
    has_state = state0 is not None
    const2 = lambda s: (0, 0)
    in_specs = [
        pl.BlockSpec((T, HGRN_COLS), lambda s: (blk0 + s, 0)),
        pl.BlockSpec((T, FOURIER_WIDTH), lambda s: (blk0 + s, 0)),
        pl.BlockSpec((T, CONV_COLS), lambda s: (blk0 + s, 0)),
        pl.BlockSpec((2, HGRN_WIDTH), const2),
        pl.BlockSpec((1, HGRN_WIDTH), const2),
        pl.BlockSpec((3, CONV_WIDTH), const2),
        pl.BlockSpec((T, 2 * T), const2),
        pl.BlockSpec((FOURIER_WIDTH, FOURIER_WIDTH), const2),
        pl.BlockSpec((FOURIER_WIDTH, FOURIER_WIDTH), const2),
    ]
    args = [uh, uf, uc, lb, norm_g, conv_w, dft, bdc, bds]
    mix_shape = jax.ShapeDtypeStruct((n_seq * T, D_MODEL), BF16)
    mix_spec = pl.BlockSpec((T, D_MODEL), lambda s: (s, 0))
    st_block = (1, 2, HEADS, HEAD_DIM, HEAD_DIM)
    if has_state:
        in_specs.append(pl.BlockSpec((1, None) + st_block[1:], lambda s: (s, layer, 0, 0, 0, 0)))
        args.append(state0)
        out_shape = [mix_shape]
        out_specs = [mix_spec]
    else:
        out_shape = [mix_shape, jax.ShapeDtypeStruct((n_seq, 2, HEADS, HEAD_DIM, HEAD_DIM), F32)]
        out_specs = [mix_spec, pl.BlockSpec(st_block, lambda s: (s, 0, 0, 0, 0))]
    return pl.pallas_call(
        functools.partial(_mixer_kernel, T=T, has_state=has_state),
        out_shape=out_shape,
        grid=(n_seq,),
        in_specs=in_specs,
        out_specs=out_specs,
        scratch_shapes=[pltpu.VMEM((2, HEADS, HEAD_DIM, HEAD_DIM), F32), pltpu.VMEM((T, HGRN_WIDTH), F32)],
        compiler_params=pltpu.CompilerParams(dimension_semantics=("arbitrary",), vmem_limit_bytes=VMEM_LIMIT),
        name="mixer_latent" if has_state else "mixer_context",
    )(*args)


def _merge_kernel(*refs, n_stream):
    (mixc_ref, mixl_ref, gates_ref, mod_ref, wp_ref, wo_ref, g1_ref, b1_ref, wrh_ref, wrl_ref, br_ref,
     x1_ref, h2_ref, apos_ref, gate_ref, n16_ref) = refs[n_stream:]
    W = HGRN_WIDTH
    gates = gates_ref[...].astype(F32)
    mix = _stream_tile((mixc_ref, mixl_ref))
    edges = (0, W, W + FOURIER_WIDTH, D_MODEL)
    merged = None
    for j in range(N_BRANCHES):
        p = _dot(mix[:, edges[j]:edges[j + 1]], wp_ref[edges[j]:edges[j + 1], :])
        term = gates[:, j * D_MODEL:(j + 1) * D_MODEL] * p
        merged = term if merged is None else merged + term
    y = _dot(merged.astype(BF16), wo_ref[...])
    x1 = _ln(ALPHA * _stream_tile(refs[:n_stream]) + mod_ref[2:3, :] * y) * g1_ref[...] + b1_ref[...]
    x1_ref[...] = x1
    h2 = _ln(x1) * (1.0 + mod_ref[4:5, :]) + mod_ref[3:4, :]
    h2_hi = h2.astype(BF16)
    h2_ref[...] = h2_hi
    h2_lo = (h2 - h2_hi.astype(F32)).astype(BF16)
    logits = _dot(h2_hi, wrh_ref[...]) + (_dot(h2_hi, wrl_ref[...]) + _dot(h2_lo, wrh_ref[...]))
    scores = jax.nn.sigmoid(logits)
    sel = scores + br_ref[...]
    lane_f = lax.broadcasted_iota(jnp.int32, sel.shape, 1).astype(F32)
    hits = []
    for _ in range(TOP_K):
        top = jnp.max(sel, axis=-1, keepdims=True)
        first = jnp.min(jnp.where(sel == top, lane_f, float(N_EXPERTS)), axis=-1, keepdims=True)
        hit = lane_f == first
        hits.append(hit)
        sel = jnp.where(hit, -jnp.inf, sel)
    chosen = functools.reduce(jnp.logical_or, hits)
    picked = jnp.where(chosen, scores, 0.0)
    denom = jnp.sum(picked, axis=-1, keepdims=True) + 1e-20
    gate = ROUTED_SCALE * picked / denom

    chosen_f = chosen.astype(F32)
    r_i = lax.broadcasted_iota(jnp.int32, (TOK_TILE, TOK_TILE), 0)
    c_i = lax.broadcasted_iota(jnp.int32, (TOK_TILE, TOK_TILE), 1)
    earlier = (c_i < r_i).astype(F32).astype(BF16)
    rank = _dot(earlier, chosen_f.astype(BF16))
    count = jnp.sum(chosen_f, axis=0, keepdims=True)
    n16 = jnp.floor((count + (ROW_CHUNK - 1)) * (1.0 / ROW_CHUNK))
    e_r = lax.broadcasted_iota(jnp.int32, (N_EXPERTS, N_EXPERTS), 0)
    e_c = lax.broadcasted_iota(jnp.int32, (N_EXPERTS, N_EXPERTS), 1)
    before = (e_r < e_c).astype(F32).astype(BF16)
    seg_start = ROW_CHUNK * _dot(jnp.broadcast_to(n16, (8, N_EXPERTS)).astype(BF16), before)[0:1, :]
    row_of = seg_start + rank
    lane_k = lax.broadcasted_iota(jnp.int32, (TOK_TILE, TOP_K), 1)
    apos = jnp.zeros((TOK_TILE, TOP_K), F32)
    for k, hit in enumerate(hits):
        apos = jnp.where(lane_k == k, jnp.sum(jnp.where(hit, row_of, 0.0), axis=-1, keepdims=True), apos)
    apos_ref[...] = apos
    gate_ref[...] = jnp.concatenate([gate, jnp.zeros((TOK_TILE, GATE_LANES - N_EXPERTS), F32)], axis=-1).astype(BF16)
    n16_ref[...] = n16


def _merge(stream, mix_ctx, mix_lat, gates, mod, wp, wo, ln_g, ln_b, wr_hi, wr_lo, b_router, layer):
    tok = lambda n: pl.BlockSpec((TOK_TILE, n), lambda i: (i, 0))
    const2 = lambda i: (0, 0)
    return pl.pallas_call(
        functools.partial(_merge_kernel, n_stream=len(stream)),
        out_shape=[jax.ShapeDtypeStruct((N_TOK, D_MODEL), F32), jax.ShapeDtypeStruct((N_TOK, D_MODEL), BF16),
                   jax.ShapeDtypeStruct((N_TOK, TOP_K), F32), jax.ShapeDtypeStruct((N_TOK, GATE_LANES), BF16),
                   jax.ShapeDtypeStruct((N_TILES, 1, N_EXPERTS), F32)],
        grid=(N_TILES,),
        in_specs=_stream_specs(len(stream)) + _stream_specs(2) + [
            tok(GATE_COLS),
            pl.BlockSpec((None, None, 6, D_MODEL), lambda i: (layer, _cond_row_of_tile(i, TOK_TILE), 0, 0)),
            pl.BlockSpec((D_MODEL, D_MODEL), const2),
            pl.BlockSpec((D_MODEL, D_MODEL), const2),
            pl.BlockSpec((1, D_MODEL), const2),
            pl.BlockSpec((1, D_MODEL), const2),
            pl.BlockSpec((D_MODEL, N_EXPERTS), const2),
            pl.BlockSpec((D_MODEL, N_EXPERTS), const2),
            pl.BlockSpec((1, N_EXPERTS), const2),
        ],
        out_specs=[tok(D_MODEL), tok(D_MODEL), tok(TOP_K), tok(GATE_LANES),
                   pl.BlockSpec((None, 1, N_EXPERTS), lambda i: (i, 0, 0))],
        compiler_params=pltpu.CompilerParams(dimension_semantics=("arbitrary",), vmem_limit_bytes=VMEM_LIMIT),
        name="merge_router",
    )(*stream, mix_ctx, mix_lat, gates, mod, wp, wo, ln_g, ln_b, wr_hi, wr_lo, b_router)


def _routing_tables(n16):
    n16 = n16.astype(jnp.int32)
    hi = jnp.cumsum(n16, axis=1)
    lo = hi - n16
    n_chunks = hi[:, -1]
    eo = jnp.cumsum(n16, axis=0) - n16
    tc = jnp.sum(n16, axis=0)
    tiles = (tc + CHUNKS_PER_GMM_TILE - 1) // CHUNKS_PER_GMM_TILE
    tile_end = jnp.cumsum(tiles)
    tile_off = tile_end - tiles
    n_used = tile_end[-1]
    slot = CHUNKS_PER_GMM_TILE * tile_off[None, :] + eo
    copies = (_copy_runs(n16 // 2, lo, slot, 2, MAX_PAIRS)
              + _copy_runs(n16 % 2, lo + n16 - 1, slot + n16 - 1, 1, N_EXPERTS))
    pad_start = CHUNKS_PER_GMM_TILE * tile_off + tc
    pad_count = CHUNKS_PER_GMM_TILE * tiles - tc
    return (copies, n_chunks, tile_off.astype(jnp.int32), tiles.astype(jnp.int32),
            jnp.reshape(n_used, (1,)).astype(jnp.int32), pad_start.astype(jnp.int32), pad_count.astype(jnp.int32))


def _copy_runs(count, first_src, first_dst, stride, max_runs):
    hi = jnp.cumsum(count, axis=1)
    lo = hi - count
    r = jnp.arange(max_runs, dtype=jnp.int32)[None, :, None]
    owner = jnp.logical_and(r >= lo[:, None, :], r < hi[:, None, :])

    def table(first):
        picked = jnp.sum(jnp.where(owner, (first - stride * lo)[:, None, :], 0), axis=-1)
        return (stride * r[:, :, 0] + picked).reshape(-1).astype(jnp.int32)

    return table(first_src), table(first_dst), hi[:, -1].astype(jnp.int32)


def _one_hot_rows(apos_ref, n_rows):
    r = lax.broadcasted_iota(jnp.int32, (TOK_TILE, n_rows), 1).astype(F32)
    pt = jnp.zeros((TOK_TILE, n_rows), F32)
    for k in range(TOP_K):
        pt = jnp.where(r == apos_ref[:, k:k + 1], 1.0, pt)
    return pt.astype(BF16)


def _chunk_copy(src_ref, src_chunk, dst_ref, dst_chunk, sem):
    rows = lambda c: pl.ds(pl.multiple_of(c * ROW_CHUNK, ROW_CHUNK), ROW_CHUNK)
    return pltpu.make_async_copy(src_ref.at[rows(src_chunk), :], dst_ref.at[rows(dst_chunk), :], sem)


def _run_rows(chunk, n_chunks):
    return pl.ds(pl.multiple_of(chunk * ROW_CHUNK, ROW_CHUNK), n_chunks * ROW_CHUNK)


def _for_each_chunk(n, fn):
    def trip(i, carry):
        for u in range(CHUNK_LOOP_UNROLL):
            c = i * CHUNK_LOOP_UNROLL + u
            if u == 0:
                fn(c)
            else:
                pl.when(c < n)(functools.partial(fn, c))
        return carry
    lax.fori_loop(0, (n + (CHUNK_LOOP_UNROLL - 1)) // CHUNK_LOOP_UNROLL, trip, 0)


def _dispatch_kernel(psrc_ref, pdst_ref, pn_ref, ssrc_ref, sdst_ref, sn_ref, nch_ref, pstart_ref, pcount_ref, nu_ref,
                     h_ref, gate_ref, apos_ref, xs_hbm, pt_ref, stage_ref, zero_ref, sem):
    j = pl.program_id(0)
    buf = j % 2
    runs = ((psrc_ref, pdst_ref, pn_ref), (ssrc_ref, sdst_ref, sn_ref))
    pad_sem, tile_sem = 2 * len(runs), 2 * len(runs) + 1

    def run_copy(k, b, src_chunk, slot):
        return pltpu.make_async_copy(stage_ref.at[b, _run_rows(src_chunk, RUN_CHUNKS[k]), :],
                                     xs_hbm.at[_run_rows(slot, RUN_CHUNKS[k]), :], sem.at[2 * k + b])

    def wait_tile(b, jj):
        for k, (_, _, n_ref) in enumerate(runs):
            _for_each_chunk(n_ref[jj], lambda r, k=k: run_copy(k, b, 0, 0).wait())

    @pl.when(j < N_TILES)
    def _():
        @pl.when(j >= 2)
        def _():
            wait_tile(buf, j - 2)

        def sort_rows(n_rows):
            pt = _one_hot_rows(apos_ref, n_rows)
            pt_ref[:, 0:n_rows] = pt
            if n_rows < TILE_ROWS:
                pt_ref[:, n_rows:TILE_ROWS] = jnp.zeros((TOK_TILE, TILE_ROWS - n_rows), BF16)
            h = h_ref[...]
            gate = gate_ref[...]
            for r0 in range(0, n_rows, DISPATCH_ROWS):
                group = pt[:, r0:r0 + DISPATCH_ROWS]
                stage_ref[buf, r0:r0 + DISPATCH_ROWS, 0:D_MODEL] = _dot_tn(group, h).astype(BF16)
                stage_ref[buf, r0:r0 + DISPATCH_ROWS, D_MODEL:ROW_WIDTH] = _dot_tn(group, gate).astype(BF16)

        short = nch_ref[j] * ROW_CHUNK <= SHORT_ROWS
        pl.when(short)(functools.partial(sort_rows, SHORT_ROWS))
        pl.when(jnp.logical_not(short))(functools.partial(sort_rows, TILE_ROWS))

        for k, (src_ref, slot_ref, n_ref) in enumerate(runs):
            base = j * RUN_TABLE_STRIDE[k]
            _for_each_chunk(n_ref[j], lambda r, k=k, base=base, src_ref=src_ref, slot_ref=slot_ref: run_copy(
                k, buf, src_ref[base + r], slot_ref[base + r]).start())

    @pl.when(j == N_TILES)
    def _():
        zero_ref[...] = jnp.zeros_like(zero_ref)

        def tile_copy(t):
            rows = pl.ds(pl.multiple_of(t * GMM_TILE, GMM_TILE), GMM_TILE)
            return pltpu.make_async_copy(zero_ref, xs_hbm.at[rows, :], sem.at[tile_sem])

        def per_expert(e, total):
            def fill(i, carry):
                _chunk_copy(zero_ref, 0, xs_hbm, pstart_ref[e] + i, sem.at[pad_sem]).start()
                return carry
            lax.fori_loop(0, pcount_ref[e], fill, 0)
            return total + pcount_ref[e]
        n_pad = lax.fori_loop(0, N_EXPERTS, per_expert, 0)

        def fill_tile(t, carry):
            tile_copy(t).start()
            return carry
        lax.fori_loop(nu_ref[0], MAX_GMM_TILES, fill_tile, 0)
        wait_tile(0, N_TILES - 2)
        wait_tile(1, N_TILES - 1)

        def drain(i, carry):
            _chunk_copy(zero_ref, 0, xs_hbm, 0, sem.at[pad_sem]).wait()
            return carry
        lax.fori_loop(0, n_pad, drain, 0)

        def drain_tile(t, carry):
            tile_copy(0).wait()
            return carry
        lax.fori_loop(nu_ref[0], MAX_GMM_TILES, drain_tile, 0)


def _dispatch(h2, gate, apos, copies, n_chunks, pad_start, pad_count, n_used):
    last = N_TILES - 1
    tok = lambda n: pl.BlockSpec((TOK_TILE, n), lambda j, *_: (jnp.minimum(j, last), 0))
    grid_spec = pltpu.PrefetchScalarGridSpec(
        num_scalar_prefetch=len(copies) + 4,
        grid=(N_TILES + 1,),
        in_specs=[tok(D_MODEL), tok(GATE_LANES), tok(TOP_K)],
        out_specs=[pl.BlockSpec(memory_space=pl.ANY), tok(TILE_ROWS)],
        scratch_shapes=[pltpu.VMEM((2, TILE_ROWS, ROW_WIDTH), BF16), pltpu.VMEM((GMM_TILE, ROW_WIDTH), BF16),
                        pltpu.SemaphoreType.DMA((2 * len(RUN_CHUNKS) + 2,))],
    )
    return pl.pallas_call(
        _dispatch_kernel,
        out_shape=[jax.ShapeDtypeStruct((SORTED_ROWS, ROW_WIDTH), BF16),
                   jax.ShapeDtypeStruct((N_TOK, TILE_ROWS), BF16)],
        grid_spec=grid_spec,
        compiler_params=pltpu.CompilerParams(dimension_semantics=("arbitrary",), vmem_limit_bytes=VMEM_LIMIT),
        name="moe_dispatch",
    )(*copies, n_chunks, pad_start, pad_count, n_used, h2, gate, apos)


def _gmm_kernel(off_ref, cnt_ref, nu_ref, x_hbm, wg_ref, wu_ref, wd_ref, y_hbm, xbuf, ybuf, wg_b, wu_b, wd_b,
                in_sem, out_sem):
    e = pl.program_id(0)
    n_used = nu_ref[0]

    def rows(g):
        return pl.ds(pl.multiple_of(g * GMM_TILE, GMM_TILE), GMM_TILE)

    def fetch(g):
        slot = g % GMM_IN_SLOTS
        return pltpu.make_async_copy(x_hbm.at[rows(g), :], xbuf.at[slot], in_sem.at[slot])

    def flush(g):
        slot = g % GMM_OUT_SLOTS
        return pltpu.make_async_copy(ybuf.at[slot], y_hbm.at[rows(g), 0:D_MODEL], out_sem.at[slot])

    @pl.when(e == 0)
    def _():
        for g in range(GMM_AHEAD):
            @pl.when(g < n_used)
            def _(g=g):
                fetch(g).start()

    wg_b[...] = wg_ref[...].astype(BF16)
    wu_b[...] = wu_ref[...].astype(BF16)
    wd_b[...] = wd_ref[...].astype(BF16)

    def process(g0, width):
        for w in range(width):
            @pl.when(g0 + w + GMM_AHEAD < n_used)
            def _(w=w):
                fetch(g0 + w + GMM_AHEAD).start()
        for w in range(width):
            fetch(g0 + w).wait()
        xg = jnp.concatenate([xbuf[(g0 + w) % GMM_IN_SLOTS] for w in range(width)], axis=0)
        x = xg[:, 0:D_MODEL]
        gates = xg[:, D_MODEL:ROW_WIDTH].astype(F32)
        lane = lax.broadcasted_iota(jnp.int32, gates.shape, 1)
        weight = jnp.sum(jnp.where(lane == e, gates, 0.0), axis=-1, keepdims=True)
        hid = _silu(_dot(x, wg_b[...])) * _dot(x, wu_b[...])
        y = (_dot(hid.astype(BF16), wd_b[...]) * weight).astype(BF16)
        for w in range(width):
            g = g0 + w

            @pl.when(g >= GMM_OUT_SLOTS)
            def _(g=g):
                flush(g - GMM_OUT_SLOTS).wait()

            ybuf[g % GMM_OUT_SLOTS] = y[w * GMM_TILE:(w + 1) * GMM_TILE]
            flush(g).start()

    first = off_ref[e]
    count = cnt_ref[e]

    done = 0
    width = GMM_WIDTH
    while width >= 1:
        trips = (count - done) // width

        def block(p, carry, done=done, width=width):
            process(first + done + width * p, width)
            return carry

        lax.fori_loop(0, trips, block, 0)
        done = done + trips * width
        width //= 2

    @pl.when(e == N_EXPERTS - 1)
    def _():
        for k in range(GMM_OUT_SLOTS):
            @pl.when(n_used - 1 - k >= 0)
            def _(k=k):
                flush(n_used - 1 - k).wait()


def _gmm(xs, weg, weu, wed, tile_off, tiles, n_used, layer):
    w_in_spec = pl.BlockSpec((None, None, D_MODEL, D_EXPERT), lambda e, *_: (layer, e, 0, 0))
    grid_spec = pltpu.PrefetchScalarGridSpec(
        num_scalar_prefetch=3,
        grid=(N_EXPERTS,),
        in_specs=[
            pl.BlockSpec(memory_space=pl.ANY),
            w_in_spec, w_in_spec,
            pl.BlockSpec((None, None, D_EXPERT, D_MODEL), lambda e, *_: (layer, e, 0, 0)),
        ],
        out_specs=pl.BlockSpec(memory_space=pl.ANY),
        scratch_shapes=[pltpu.VMEM((GMM_IN_SLOTS, GMM_TILE, ROW_WIDTH), BF16),
                        pltpu.VMEM((GMM_OUT_SLOTS, GMM_TILE, D_MODEL), BF16),
                        pltpu.VMEM((D_MODEL, D_EXPERT), BF16), pltpu.VMEM((D_MODEL, D_EXPERT), BF16),
                        pltpu.VMEM((D_EXPERT, D_MODEL), BF16),
                        pltpu.SemaphoreType.DMA((GMM_IN_SLOTS,)), pltpu.SemaphoreType.DMA((GMM_OUT_SLOTS,))],
    )
    return pl.pallas_call(
        _gmm_kernel,
        out_shape=jax.ShapeDtypeStruct((SORTED_ROWS, ROW_WIDTH), BF16),
        grid_spec=grid_spec,
        input_output_aliases={3: 0},
        compiler_params=pltpu.CompilerParams(dimension_semantics=("arbitrary",), vmem_limit_bytes=VMEM_LIMIT),
        name="moe_gmm",
    )(tile_off, tiles, n_used, xs, weg, weu, wed)


def _combine_kernel(psrc_ref, pdst_ref, pn_ref, ssrc_ref, sdst_ref, sn_ref, nch_ref, ys_hbm, pt_ref, h_ref, x1_ref,
                    mod_ref, wsg_ref, wsu_ref, wsd_ref, g2_ref, b2_ref, oc_ref, ol_ref, stage_ref, routed_ref, sem):
    j = pl.program_id(0)
    buf = j % 2
    runs = ((psrc_ref, pdst_ref, pn_ref), (ssrc_ref, sdst_ref, sn_ref))

    def run_fetch(k, b, slot, chunk):
        return pltpu.make_async_copy(ys_hbm.at[_run_rows(slot, RUN_CHUNKS[k]), 0:D_MODEL],
                                     stage_ref.at[b, _run_rows(chunk, RUN_CHUNKS[k]), :], sem.at[2 * k + b])

    def fetch(jj, b):
        for k, (chunk_ref, slot_ref, n_ref) in enumerate(runs):
            base = jj * RUN_TABLE_STRIDE[k]
            _for_each_chunk(n_ref[jj], lambda r, k=k, base=base, chunk_ref=chunk_ref, slot_ref=slot_ref: run_fetch(
                k, b, slot_ref[base + r], chunk_ref[base + r]).start())

    @pl.when(j == 0)
    def _():
        stage_ref[...] = jnp.zeros_like(stage_ref)
        fetch(0, 0)

    @pl.when(j + 1 < N_TILES)
    def _():
        fetch(j + 1, 1 - buf)

    h = h_ref[...]
    hid = _silu(_dot(h, wsg_ref[...])) * _dot(h, wsu_ref[...])
    shared = _dot(hid.astype(BF16), wsd_ref[...])

    for k, (_, _, n_ref) in enumerate(runs):
        _for_each_chunk(n_ref[j], lambda r, k=k: run_fetch(k, buf, 0, 0).wait())

    def gather_rows(n_rows):
        routed_ref[...] = _dot(pt_ref[:, 0:n_rows], stage_ref[buf, 0:n_rows, :])

    short = nch_ref[j] * ROW_CHUNK <= SHORT_ROWS
    pl.when(short)(functools.partial(gather_rows, SHORT_ROWS))
    pl.when(jnp.logical_not(short))(functools.partial(gather_rows, TILE_ROWS))
    r = ALPHA * x1_ref[...] + mod_ref[5:6, :] * (routed_ref[...] + shared)
    out = _ln(r) * g2_ref[...] + b2_ref[...]

    @pl.when(j < CTX_TILES)
    def _():
        oc_ref[...] = out

    @pl.when(j >= CTX_TILES)
    def _():
        ol_ref[...] = out


def _combine(ys, pt, h2, x1, mod, wsg, wsu, wsd, ln_g, ln_b, copies, n_chunks, layer):
    tok = lambda n: pl.BlockSpec((TOK_TILE, n), lambda j, *_: (j, 0))
    const2 = lambda j, *_: (0, 0)
    grid_spec = pltpu.PrefetchScalarGridSpec(
        num_scalar_prefetch=len(copies) + 1,
        grid=(N_TILES,),
        in_specs=[
            pl.BlockSpec(memory_space=pl.ANY),
            tok(TILE_ROWS), tok(D_MODEL), tok(D_MODEL),
            pl.BlockSpec((None, None, 6, D_MODEL), lambda j, *_: (layer, _cond_row_of_tile(j, TOK_TILE), 0, 0)),
            pl.BlockSpec((D_MODEL, D_SHARED), const2),
            pl.BlockSpec((D_MODEL, D_SHARED), const2),
            pl.BlockSpec((D_SHARED, D_MODEL), const2),
            pl.BlockSpec((1, D_MODEL), const2),
            pl.BlockSpec((1, D_MODEL), const2),
        ],
        out_specs=_stream_specs(2),
        scratch_shapes=[pltpu.VMEM((2, TILE_ROWS, D_MODEL), BF16), pltpu.VMEM((TOK_TILE, D_MODEL), F32),
                        pltpu.SemaphoreType.DMA((2 * len(RUN_CHUNKS),))],
    )
    return pl.pallas_call(
        _combine_kernel,
        out_shape=[jax.ShapeDtypeStruct((N_CTX, D_MODEL), F32), jax.ShapeDtypeStruct((N_LAT, D_MODEL), F32)],
        grid_spec=grid_spec,
        compiler_params=pltpu.CompilerParams(dimension_semantics=("arbitrary",), vmem_limit_bytes=VMEM_LIMIT),
        name="moe_combine",
    )(*copies, n_chunks, ys, pt, h2, x1, mod, wsg, wsu, wsd, ln_g, ln_b)


def kernel(x_prompt, x_sample, c, state_hgrn, c_ctx, w_ada, b_ada, w_in, b_gate, hgrn_lb, hgrn_norm, conv_w,
           w_proj_hgrn, w_proj_fourier, w_proj_conv, w_out, ln1_g, ln1_b, ln2_g, ln2_b, w_router, b_router,
           w_exp_gate, w_exp_up, w_exp_down, w_sh_gate, w_sh_up, w_sh_down):
    lb_sm = jax.nn.softmax(hgrn_lb.astype(F32), axis=0)
    lb_all = jnp.cumsum(lb_sm, axis=0) - lb_sm[:1]

    pos = jnp.asarray(_pos_emb_table())
    dft_ctx = jnp.asarray(_dft_time(SEQ)).astype(BF16)
    dft_lat = jnp.asarray(_dft_time(DEC_SEQ)).astype(BF16)
    bdc, bds = (jnp.asarray(m).astype(BF16) for m in _dft_channel())

    cond = jnp.concatenate([c_ctx[None, :], c, jnp.zeros((N_COND - 1 - DEC_BATCH, D_MODEL), F32)], axis=0)
    mod = _ada_mod(cond, w_ada, b_ada).reshape(DEPTH, N_COND, 6, D_MODEL)

    w_in_b = w_in.astype(BF16)
    wp_b = jnp.concatenate([w_proj_hgrn, w_proj_fourier, w_proj_conv], axis=1).astype(BF16)
    wo_b = w_out.astype(BF16)
    wr_hi = w_router.astype(BF16)
    wr_lo = (w_router - wr_hi.astype(F32)).astype(BF16)
    wsg_b, wsu_b, wsd_b = w_sh_gate.astype(BF16), w_sh_up.astype(BF16), w_sh_down.astype(BF16)

    stream = (x_prompt.reshape(N_CTX, D_MODEL), x_sample.reshape(N_LAT, D_MODEL), pos)
    new_states = []
    for l in range(DEPTH):
        uh, uf, uc, gates = _in_proj(stream, mod, w_in_b, b_gate[l][None, :], l)
        small = (lb_all[l], hgrn_norm[l][None, :], conv_w[l])
        mix_ctx, s_ctx = _mixer(uh, uf, uc, *small, dft_ctx, bdc, bds, T=SEQ, n_seq=BATCH, blk0=0)
        (mix_lat,) = _mixer(uh, uf, uc, *small, dft_lat, bdc, bds, T=DEC_SEQ, n_seq=DEC_BATCH,
                            blk0=N_CTX // DEC_SEQ, state0=state_hgrn, layer=l)
        new_states.append(s_ctx)
        x1, h2, apos, gate, n16 = _merge(stream, mix_ctx, mix_lat, gates, mod, wp_b[l], wo_b[l],
                                         ln1_g[l][None, :], ln1_b[l][None, :], wr_hi[l], wr_lo[l],
                                         b_router[l][None, :], l)
        copies, n_chunks, tile_off, tiles, n_used, pad_start, pad_count = _routing_tables(
            n16.reshape(N_TILES, N_EXPERTS))
        xs, pt = _dispatch(h2, gate, apos, copies, n_chunks, pad_start, pad_count, n_used)
        ys = _gmm(xs, w_exp_gate, w_exp_up, w_exp_down, tile_off, tiles, n_used, l)
        stream = tuple(_combine(ys, pt, h2, x1, mod, wsg_b[l], wsu_b[l], wsd_b[l], ln2_g[l][None, :],
                                ln2_b[l][None, :], copies, n_chunks, l))
    y_prompt = stream[0].reshape(BATCH, SEQ, D_MODEL)
    y_sample = stream[1].reshape(DEC_BATCH, DEC_SEQ, D_MODEL)
    return (y_prompt, y_sample, jnp.stack(new_states, axis=1))
```

```python
import functools
import math

import numpy as np
import jax
import jax.numpy as jnp
from jax import lax
from jax.experimental import pallas as pl
from jax.experimental.pallas import tpu as pltpu

F32 = jnp.float32
BF16 = jnp.bfloat16

D_MODEL = 1024
BATCH = 16
SEQ = 256
DEPTH = 2
DEC_BATCH = 4
DEC_SEQ = 1024
GRID_W = 64
HEADS = 4
HEAD_DIM = 128
HGRN_WIDTH = HEADS * HEAD_DIM
FOURIER_GROUPS = 4
FOURIER_GROUP_DIM = 64
FOURIER_WIDTH = FOURIER_GROUPS * FOURIER_GROUP_DIM
CONV_WIDTH = 256
N_BRANCHES = 3
HGRN_COLS = 5 * HGRN_WIDTH
CONV_COLS = 3 * CONV_WIDTH
GATE_COLS = N_BRANCHES * D_MODEL
IN_COLS = HGRN_COLS + FOURIER_WIDTH + CONV_COLS + GATE_COLS
N_EXPERTS = 64
TOP_K = 8
D_EXPERT = 256
D_SHARED = 256
ROUTED_SCALE = 2.5
ALPHA = (2 * DEPTH) ** 0.25
LN_EPS = 1e-6
RMS_EPS = 1e-6
F_MIN = 1e-30

N_CTX = BATCH * SEQ
N_LAT = DEC_BATCH * DEC_SEQ
N_TOK = N_CTX + N_LAT
N_COND = 8

TOK_TILE = 256
CHUNK = 64
CHUNK_UNROLL = 2
WINDOW = 120.0
STATIC_WINDOWS = 2
ADA_TILE = 1536
VMEM_LIMIT = 56 * 1024 * 1024

N_TILES = N_TOK // TOK_TILE
ROW_CHUNK = 16
CHUNK_LOOP_UNROLL = 4
TILE_CHUNKS = (TOK_TILE * TOP_K + N_EXPERTS * (ROW_CHUNK - 1)) // ROW_CHUNK
RUN_CHUNKS = (2, 1)
MAX_PAIRS = TILE_CHUNKS // 2
RUN_TABLE_STRIDE = (MAX_PAIRS, N_EXPERTS)
DISPATCH_ROWS = 192
EXTENT_STEP = 2 * DISPATCH_ROWS
TILE_ROWS = -(-TILE_CHUNKS * ROW_CHUNK // EXTENT_STEP) * EXTENT_STEP
SHORT_ROWS = TILE_ROWS - EXTENT_STEP
GMM_TILE = 256
GMM_WIDTH = 4
GMM_AHEAD = 6
GMM_IN_SLOTS = GMM_WIDTH + GMM_AHEAD
GMM_OUT_SLOTS = 2 * GMM_WIDTH
CHUNKS_PER_GMM_TILE = GMM_TILE // ROW_CHUNK
MAX_GMM_TILES = (N_TILES * TILE_CHUNKS + N_EXPERTS * (CHUNKS_PER_GMM_TILE - 1)) // CHUNKS_PER_GMM_TILE
SORTED_ROWS = MAX_GMM_TILES * GMM_TILE
GATE_LANES = 128
ROW_WIDTH = D_MODEL + GATE_LANES


def _cond_row_of_tile(i, tile):
    ctx_tiles = N_CTX // tile
    per_seq = DEC_SEQ // tile
    return jnp.where(i < ctx_tiles, 0, 1 + (i - ctx_tiles) // per_seq)


def _silu(x):
    return x * jax.nn.sigmoid(x)


def _ln(x):
    mu = jnp.mean(x, axis=-1, keepdims=True)
    xc = x - mu
    var = jnp.mean(xc * xc, axis=-1, keepdims=True)
    return xc * lax.rsqrt(var + LN_EPS)


def _dot(a, b):
    return jnp.dot(a, b, preferred_element_type=F32)


def _dot_nt(a, b):
    return lax.dot_general(a, b, (((1,), (1,)), ((), ())), preferred_element_type=F32)


def _dot_tn(a, b):
    return lax.dot_general(a, b, (((0,), (0,)), ((), ())), preferred_element_type=F32)


@functools.lru_cache(maxsize=None)
def _pos_emb_table():
    rows = DEC_SEQ // GRID_W
    t = np.arange(rows * GRID_W)
    r = (t // GRID_W).astype(np.float32)
    col = (t % GRID_W).astype(np.float32)
    quarter = D_MODEL // 4
    omega = (1.0 / (np.float32(10000.0) ** (np.arange(quarter, dtype=np.float32) / np.float32(quarter)))).astype(np.float32)
    ar = (r[:, None] * omega).astype(np.float32)
    ac = (col[:, None] * omega).astype(np.float32)
    return np.concatenate([np.sin(ar), np.cos(ar), np.sin(ac), np.cos(ac)], axis=-1).astype(np.float32)


@functools.lru_cache(maxsize=None)
def _dft_time(T):
    k = np.arange(T)
    ph = 2.0 * np.pi * ((k[:, None] * k[None, :]) % T) / T
    return np.concatenate([np.cos(ph), -np.sin(ph)], axis=1).astype(np.float32)


@functools.lru_cache(maxsize=None)
def _dft_channel():
    n = FOURIER_GROUP_DIM
    k = np.arange(n)
    ph = 2.0 * np.pi * ((k[:, None] * k[None, :]) % n) / n
    eye = np.eye(FOURIER_GROUPS)
    return np.kron(eye, np.cos(ph)).astype(np.float32), np.kron(eye, np.sin(ph)).astype(np.float32)


CTX_TILES = N_CTX // TOK_TILE
LAT_TILES_PER_SEQ = DEC_SEQ // TOK_TILE


def _stream_specs(n):
    specs = [pl.BlockSpec((TOK_TILE, D_MODEL), lambda i, *_: (jnp.minimum(i, CTX_TILES - 1), 0)),
             pl.BlockSpec((TOK_TILE, D_MODEL), lambda i, *_: (jnp.maximum(i - CTX_TILES, 0), 0))]
    if n == 3:
        specs.append(pl.BlockSpec((TOK_TILE, D_MODEL),
                                  lambda i, *_: (jnp.maximum(i - CTX_TILES, 0) % LAT_TILES_PER_SEQ, 0)))
    return specs


def _stream_tile(refs):
    latent = refs[1][...]
    if len(refs) == 3:
        latent = latent + refs[2][...]
    return jnp.where(pl.program_id(0) < CTX_TILES, refs[0][...], latent)


def _ada_kernel(c_ref, w_ref, b_ref, o_ref):
    s = _silu(c_ref[...]).astype(BF16)
    o_ref[...] = _dot(s, w_ref[...].astype(BF16)) + b_ref[...]


def _ada_mod(cond, w_ada, b_ada):
    n_col = 6 * D_MODEL
    return pl.pallas_call(
        _ada_kernel,
        out_shape=jax.ShapeDtypeStruct((DEPTH, N_COND, n_col), F32),
        grid=(DEPTH, n_col // ADA_TILE),
        in_specs=[
            pl.BlockSpec((N_COND, D_MODEL), lambda l, j: (0, 0)),
            pl.BlockSpec((None, D_MODEL, ADA_TILE), lambda l, j: (l, 0, j)),
            pl.BlockSpec((None, 1, ADA_TILE), lambda l, j: (l, 0, j)),
        ],
        out_specs=pl.BlockSpec((None, N_COND, ADA_TILE), lambda l, j: (l, 0, j)),
        compiler_params=pltpu.CompilerParams(dimension_semantics=("arbitrary", "arbitrary"),
                                             vmem_limit_bytes=VMEM_LIMIT),
        name="ada_mod",
    )(cond, w_ada, b_ada.reshape(DEPTH, 1, n_col))


def _in_proj_kernel(*refs, n_stream):
    mod_ref, w_ref, bg_ref, uh_ref, uf_ref, uc_ref, gates_ref = refs[n_stream:]
    sh1 = mod_ref[0:1, :]
    sc1 = mod_ref[1:2, :]
    h = (_ln(_stream_tile(refs[:n_stream])) * (1.0 + sc1) + sh1).astype(BF16)
    c0 = 0
    for ref, n in ((uh_ref, HGRN_COLS), (uf_ref, FOURIER_WIDTH), (uc_ref, CONV_COLS)):
        ref[...] = _dot(h, w_ref[:, c0:c0 + n].astype(BF16))
        c0 += n
    gates_ref[...] = jax.nn.sigmoid(_dot(h, w_ref[:, c0:c0 + GATE_COLS].astype(BF16)) + bg_ref[...]).astype(BF16)


def _in_proj(stream, mod, w_in, b_gate, layer):
    n_tiles = N_TOK // TOK_TILE
    outs = ((HGRN_COLS, F32), (FOURIER_WIDTH, F32), (CONV_COLS, F32), (GATE_COLS, BF16))
    return pl.pallas_call(
        functools.partial(_in_proj_kernel, n_stream=len(stream)),
        out_shape=[jax.ShapeDtypeStruct((N_TOK, n), dt) for n, dt in outs],
        grid=(n_tiles,),
        in_specs=_stream_specs(len(stream)) + [
            pl.BlockSpec((None, None, 6, D_MODEL), lambda i: (layer, _cond_row_of_tile(i, TOK_TILE), 0, 0)),
            pl.BlockSpec((None, D_MODEL, IN_COLS), lambda i: (layer, 0, 0), pipeline_mode=pl.Buffered(1)),
            pl.BlockSpec((1, GATE_COLS), lambda i: (0, 0)),
        ],
        out_specs=[pl.BlockSpec((TOK_TILE, n), lambda i: (i, 0)) for n, _ in outs],
        compiler_params=pltpu.CompilerParams(dimension_semantics=("arbitrary",), vmem_limit_bytes=VMEM_LIMIT),
        name="in_proj",
    )(*stream, mod, w_in, b_gate)


def _mixer_kernel(*refs, T, has_state):
    if has_state:
        (uh_ref, uf_ref, uc_ref, lb_ref, ng_ref, cw_ref, dft_ref, bdc_ref, bds_ref, s0_ref,
         mix_ref, st_ref, o_ref) = refs
        sfin_ref = None
    else:
        (uh_ref, uf_ref, uc_ref, lb_ref, ng_ref, cw_ref, dft_ref, bdc_ref, bds_ref,
         mix_ref, sfin_ref, st_ref, o_ref) = refs
        s0_ref = None
    n_chunks = T // CHUNK
    W = HGRN_WIDTH

    for d in range(2):
        for h in range(HEADS):
            if has_state:
                st_ref[d, h] = s0_ref[0, d, h].T
            else:
                st_ref[d, h] = jnp.zeros((HEAD_DIM, HEAD_DIM), F32)
    o_ref[...] = jnp.zeros_like(o_ref)

    row = lax.broadcasted_iota(jnp.int32, (CHUNK, CHUNK), 0)
    col = lax.broadcasted_iota(jnp.int32, (CHUNK, CHUNK), 1)
    keep = (col <= row, col >= row)
    tri = tuple(k.astype(F32).astype(BF16) for k in keep)

    def chunk_step(i, carry):
        deep = []
        for sub, d in ((s, d) for s in range(CHUNK_UNROLL) for d in range(2)):
            c = i * CHUNK_UNROLL + sub
            c = c if d == 0 else n_chunks - 1 - c
            rows = pl.ds(pl.multiple_of(c * CHUNK, CHUNK), CHUNK)
            q = _silu(uh_ref[rows, 0:W])
            v = uh_ref[rows, W:2 * W].astype(BF16)
            z = uh_ref[rows, (2 + d) * W:(3 + d) * W]
            lb = lb_ref[d:d + 1, :]
            e = jnp.exp(-jnp.abs(z))
            r = 1.0 / (1.0 + e)
            er = e * r
            pos = z >= 0.0
            sig_p = jnp.where(pos, r, er)
            sig_n = jnp.where(pos, er, r)
            f = lb + (1.0 - lb) * sig_p
            lf = jnp.log(jnp.maximum(f, F_MIN))
            k = (1.0 - lb) * sig_n
            lf_hi = lf.astype(BF16)
            lf_lo = (lf - lf_hi.astype(F32)).astype(BF16)
            b = _dot(tri[d], lf_hi) + _dot(tri[d], lf_lo)
            g = b[CHUNK - 1:CHUNK, :] if d == 0 else b[0:1, :]

            def window(p_level, q=q, k=k, b=b):
                dist = -p_level - b
                inside = jnp.logical_and(dist >= 0.0, dist < WINDOW)
                k_w = jnp.where(inside, k * jnp.exp(jnp.minimum(dist, WINDOW) - 0.5 * WINDOW), 0.0)
                q_w = q * jnp.exp(jnp.minimum(-dist, 0.0) + 0.5 * WINDOW)
                return q_w.astype(BF16), k_w.astype(BF16)

            wins = [window(WINDOW * p) for p in range(STATIC_WINDOWS)]
            q_abs = (q * jnp.exp(b)).astype(BF16)
            k_end = (k * jnp.exp(g - b)).astype(BF16)
            decay = jnp.exp(g)
            outs = []
            for h in range(HEADS):
                ls = slice(h * HEAD_DIM, (h + 1) * HEAD_DIM)
                q_cat = jnp.concatenate([w[0][:, ls] for w in wins], axis=-1)
                k_cat = jnp.concatenate([w[1][:, ls] for w in wins], axis=-1)
                a = jnp.where(keep[d], _dot_nt(q_cat, k_cat), 0.0).astype(BF16)
                s_t = st_ref[d, h]
                o_h = _dot(a, v[:, ls]) + _dot_nt(q_abs[:, ls], s_t.astype(BF16))
                st_ref[d, h] = s_t * decay[:, ls] + _dot_tn(v[:, ls], k_end[:, ls])
                outs.append(o_h)
            o_ref[rows, :] += jnp.concatenate(outs, axis=-1)
            deep.append((d, window, v, rows, jnp.min(b)))

        deepest = functools.reduce(jnp.minimum, [entry[4] for entry in deep])
        n_windows = jnp.floor(-deepest * (1.0 / WINDOW)).astype(jnp.int32) + 1

        def extra_window(p, carry2):
            for d, window, v, rows, _ in deep:
                q_w, k_w = window(WINDOW * p.astype(F32))
                extra = []
                for h in range(HEADS):
                    ls = slice(h * HEAD_DIM, (h + 1) * HEAD_DIM)
                    a = jnp.where(keep[d], _dot_nt(q_w[:, ls], k_w[:, ls]), 0.0).astype(BF16)
                    extra.append(_dot(a, v[:, ls]))
                o_ref[rows, :] += jnp.concatenate(extra, axis=-1)
            return carry2

        lax.fori_loop(STATIC_WINDOWS, n_windows, extra_window, 0)
        return carry

    lax.fori_loop(0, n_chunks // CHUNK_UNROLL, chunk_step, 0)

    if sfin_ref is not None:
        for d in range(2):
            for h in range(HEADS):
                sfin_ref[0, d, h] = st_ref[d, h].T

    o = o_ref[...]
    parts = []
    for h in range(HEADS):
        oh = o[:, h * HEAD_DIM:(h + 1) * HEAD_DIM]
        parts.append(oh * lax.rsqrt(jnp.mean(oh * oh, axis=-1, keepdims=True) + RMS_EPS))
    a_out = jnp.concatenate(parts, axis=-1) * ng_ref[...] * _silu(uh_ref[:, 4 * W:5 * W])
    mix_ref[:, 0:W] = a_out.astype(BF16)

    zf = uf_ref[...].astype(BF16)
    zc = _dot(zf, bdc_ref[...]).astype(BF16)
    zs = _dot(zf, bds_ref[...]).astype(BF16)
    four = _dot(dft_ref[...], jnp.concatenate([zc, zs], axis=0)) * (1.0 / math.sqrt(T * FOURIER_GROUP_DIM))
    mix_ref[:, W:W + FOURIER_WIDTH] = four.astype(BF16)

    cb = uc_ref[:, 0:CONV_WIDTH]
    zz = uc_ref[:, CONV_WIDTH:2 * CONV_WIDTH] * uc_ref[:, 2 * CONV_WIDTH:3 * CONV_WIDTH]
    t_idx = lax.broadcasted_iota(jnp.int32, (T, CONV_WIDTH), 0)
    z_prev = jnp.where(t_idx == 0, 0.0, pltpu.roll(zz, 1, axis=0))
    z_next = jnp.where(t_idx == T - 1, 0.0, pltpu.roll(zz, T - 1, axis=0))
    y = cw_ref[0:1, :] * z_prev + cw_ref[1:2, :] * zz + cw_ref[2:3, :] * z_next
    mix_ref[:, W + FOURIER_WIDTH:] = (cb * y).astype(BF16)


def _mixer(uh, uf, uc, lb, norm_g, conv_w, dft, bdc, bds, T, n_seq, blk0, state0=None, layer=0):
    has_state = state0 is not None
    const2 = lambda s: (0, 0)
    in_specs = [
        pl.BlockSpec((T, HGRN_COLS), lambda s: (blk0 + s, 0)),
        pl.BlockSpec((T, FOURIER_WIDTH), lambda s: (blk0 + s, 0)),
        pl.BlockSpec((T, CONV_COLS), lambda s: (blk0 + s, 0)),
        pl.BlockSpec((2, HGRN_WIDTH), const2),
        pl.BlockSpec((1, HGRN_WIDTH), const2),
        pl.BlockSpec((3, CONV_WIDTH), const2),
        pl.BlockSpec((T, 2 * T), const2),
        pl.BlockSpec((FOURIER_WIDTH, FOURIER_WIDTH), const2),
        pl.BlockSpec((FOURIER_WIDTH, FOURIER_WIDTH), const2),
    ]
    args = [uh, uf, uc, lb, norm_g, conv_w, dft, bdc, bds]
    mix_shape = jax.ShapeDtypeStruct((n_seq * T, D_MODEL), BF16)
    mix_spec = pl.BlockSpec((T, D_MODEL), lambda s: (s, 0))
    st_block = (1, 2, HEADS, HEAD_DIM, HEAD_DIM)
    if has_state:
        in_specs.append(pl.BlockSpec((1, None) + st_block[1:], lambda s: (s, layer, 0, 0, 0, 0)))
        args.append(state0)
        out_shape = [mix_shape]
        out_specs = [mix_spec]
    else:
        out_shape = [mix_shape, jax.ShapeDtypeStruct((n_seq, 2, HEADS, HEAD_DIM, HEAD_DIM), F32)]
        out_specs = [mix_spec, pl.BlockSpec(st_block, lambda s: (s, 0, 0, 0, 0))]
    return pl.pallas_call(
        functools.partial(_mixer_kernel, T=T, has_state=has_state),
        out_shape=out_shape,
        grid=(n_seq,),
        in_specs=in_specs,
        out_specs=out_specs,
        scratch_shapes=[pltpu.VMEM((2, HEADS, HEAD_DIM, HEAD_DIM), F32), pltpu.VMEM((T, HGRN_WIDTH), F32)],
        compiler_params=pltpu.CompilerParams(dimension_semantics=("arbitrary",), vmem_limit_bytes=VMEM_LIMIT),
        name="mixer_latent" if has_state else "mixer_context",
    )(*args)


def _merge_kernel(*refs, n_stream):
    (mixc_ref, mixl_ref, gates_ref, mod_ref, wp_ref, wo_ref, g1_ref, b1_ref, wrh_ref, wrl_ref, br_ref,
     x1_ref, h2_ref, apos_ref, gate_ref, n16_ref) = refs[n_stream:]
    W = HGRN_WIDTH
    gates = gates_ref[...].astype(F32)
    mix = _stream_tile((mixc_ref, mixl_ref))
    edges = (0, W, W + FOURIER_WIDTH, D_MODEL)
    merged = None
    for j in range(N_BRANCHES):
        p = _dot(mix[:, edges[j]:edges[j + 1]], wp_ref[edges[j]:edges[j + 1], :])
        term = gates[:, j * D_MODEL:(j + 1) * D_MODEL] * p
        merged = term if merged is None else merged + term
    y = _dot(merged.astype(BF16), wo_ref[...])
    x1 = _ln(ALPHA * _stream_tile(refs[:n_stream]) + mod_ref[2:3, :] * y) * g1_ref[...] + b1_ref[...]
    x1_ref[...] = x1
    h2 = _ln(x1) * (1.0 + mod_ref[4:5, :]) + mod_ref[3:4, :]
    h2_hi = h2.astype(BF16)
    h2_ref[...] = h2_hi
    h2_lo = (h2 - h2_hi.astype(F32)).astype(BF16)
    logits = _dot(h2_hi, wrh_ref[...]) + (_dot(h2_hi, wrl_ref[...]) + _dot(h2_lo, wrh_ref[...]))
    scores = jax.nn.sigmoid(logits)
    sel = scores + br_ref[...]
    lane_f = lax.broadcasted_iota(jnp.int32, sel.shape, 1).astype(F32)
    hits = []
    for _ in range(TOP_K):
        top = jnp.max(sel, axis=-1, keepdims=True)
        first = jnp.min(jnp.where(sel == top, lane_f, float(N_EXPERTS)), axis=-1, keepdims=True)
        hit = lane_f == first
        hits.append(hit)
        sel = jnp.where(hit, -jnp.inf, sel)
    chosen = functools.reduce(jnp.logical_or, hits)
    picked = jnp.where(chosen, scores, 0.0)
    denom = jnp.sum(picked, axis=-1, keepdims=True) + 1e-20
    gate = ROUTED_SCALE * picked / denom

    chosen_f = chosen.astype(F32)
    r_i = lax.broadcasted_iota(jnp.int32, (TOK_TILE, TOK_TILE), 0)
    c_i = lax.broadcasted_iota(jnp.int32, (TOK_TILE, TOK_TILE), 1)
    earlier = (c_i < r_i).astype(F32).astype(BF16)
    rank = _dot(earlier, chosen_f.astype(BF16))
    count = jnp.sum(chosen_f, axis=0, keepdims=True)
    n16 = jnp.floor((count + (ROW_CHUNK - 1)) * (1.0 / ROW_CHUNK))
    e_r = lax.broadcasted_iota(jnp.int32, (N_EXPERTS, N_EXPERTS), 0)
    e_c = lax.broadcasted_iota(jnp.int32, (N_EXPERTS, N_EXPERTS), 1)
    before = (e_r < e_c).astype(F32).astype(BF16)
    seg_start = ROW_CHUNK * _dot(jnp.broadcast_to(n16, (8, N_EXPERTS)).astype(BF16), before)[0:1, :]
    row_of = seg_start + rank
    lane_k = lax.broadcasted_iota(jnp.int32, (TOK_TILE, TOP_K), 1)
    apos = jnp.zeros((TOK_TILE, TOP_K), F32)
    for k, hit in enumerate(hits):
        apos = jnp.where(lane_k == k, jnp.sum(jnp.where(hit, row_of, 0.0), axis=-1, keepdims=True), apos)
    apos_ref[...] = apos
    gate_ref[...] = jnp.concatenate([gate, jnp.zeros((TOK_TILE, GATE_LANES - N_EXPERTS), F32)], axis=-1).astype(BF16)
    n16_ref[...] = n16


def _merge(stream, mix_ctx, mix_lat, gates, mod, wp, wo, ln_g, ln_b, wr_hi, wr_lo, b_router, layer):
    tok = lambda n: pl.BlockSpec((TOK_TILE, n), lambda i: (i, 0))
    const2 = lambda i: (0, 0)
    return pl.pallas_call(
        functools.partial(_merge_kernel, n_stream=len(stream)),
        out_shape=[jax.ShapeDtypeStruct((N_TOK, D_MODEL), F32), jax.ShapeDtypeStruct((N_TOK, D_MODEL), BF16),
                   jax.ShapeDtypeStruct((N_TOK, TOP_K), F32), jax.ShapeDtypeStruct((N_TOK, GATE_LANES), BF16),
                   jax.ShapeDtypeStruct((N_TILES, 1, N_EXPERTS), F32)],
        grid=(N_TILES,),
        in_specs=_stream_specs(len(stream)) + _stream_specs(2) + [
            tok(GATE_COLS),
            pl.BlockSpec((None, None, 6, D_MODEL), lambda i: (layer, _cond_row_of_tile(i, TOK_TILE), 0, 0)),
            pl.BlockSpec((D_MODEL, D_MODEL), const2),
            pl.BlockSpec((D_MODEL, D_MODEL), const2),
            pl.BlockSpec((1, D_MODEL), const2),
            pl.BlockSpec((1, D_MODEL), const2),
            pl.BlockSpec((D_MODEL, N_EXPERTS), const2),
            pl.BlockSpec((D_MODEL, N_EXPERTS), const2),
            pl.BlockSpec((1, N_EXPERTS), const2),
        ],
        out_specs=[tok(D_MODEL), tok(D_MODEL), tok(TOP_K), tok(GATE_LANES),
                   pl.BlockSpec((None, 1, N_EXPERTS), lambda i: (i, 0, 0))],
        compiler_params=pltpu.CompilerParams(dimension_semantics=("arbitrary",), vmem_limit_bytes=VMEM_LIMIT),
        name="merge_router",
    )(*stream, mix_ctx, mix_lat, gates, mod, wp, wo, ln_g, ln_b, wr_hi, wr_lo, b_router)


def _routing_tables(n16):
    n16 = n16.astype(jnp.int32)
    hi = jnp.cumsum(n16, axis=1)
    lo = hi - n16
    n_chunks = hi[:, -1]
    eo = jnp.cumsum(n16, axis=0) - n16
    tc = jnp.sum(n16, axis=0)
    tiles = (tc + CHUNKS_PER_GMM_TILE - 1) // CHUNKS_PER_GMM_TILE
    tile_end = jnp.cumsum(tiles)
    tile_off = tile_end - tiles
    n_used = tile_end[-1]
    slot = CHUNKS_PER_GMM_TILE * tile_off[None, :] + eo
    copies = (_copy_runs(n16 // 2, lo, slot, 2, MAX_PAIRS)
              + _copy_runs(n16 % 2, lo + n16 - 1, slot + n16 - 1, 1, N_EXPERTS))
    pad_start = CHUNKS_PER_GMM_TILE * tile_off + tc
    pad_count = CHUNKS_PER_GMM_TILE * tiles - tc
    return (copies, n_chunks, tile_off.astype(jnp.int32), tiles.astype(jnp.int32),
            jnp.reshape(n_used, (1,)).astype(jnp.int32), pad_start.astype(jnp.int32), pad_count.astype(jnp.int32))


def _copy_runs(count, first_src, first_dst, stride, max_runs):
    hi = jnp.cumsum(count, axis=1)
    lo = hi - count
    r = jnp.arange(max_runs, dtype=jnp.int32)[None, :, None]
    owner = jnp.logical_and(r >= lo[:, None, :], r < hi[:, None, :])

    def table(first):
        picked = jnp.sum(jnp.where(owner, (first - stride * lo)[:, None, :], 0), axis=-1)
        return (stride * r[:, :, 0] + picked).reshape(-1).astype(jnp.int32)

    return table(first_src), table(first_dst), hi[:, -1].astype(jnp.int32)


def _one_hot_rows(apos_ref, n_rows):
    r = lax.broadcasted_iota(jnp.int32, (TOK_TILE, n_rows), 1).astype(F32)
    pt = jnp.zeros((TOK_TILE, n_rows), F32)
    for k in range(TOP_K):
        pt = jnp.where(r == apos_ref[:, k:k + 1], 1.0, pt)
    return pt.astype(BF16)


def _chunk_copy(src_ref, src_chunk, dst_ref, dst_chunk, sem):
    rows = lambda c: pl.ds(pl.multiple_of(c * ROW_CHUNK, ROW_CHUNK), ROW_CHUNK)
    return pltpu.make_async_copy(src_ref.at[rows(src_chunk), :], dst_ref.at[rows(dst_chunk), :], sem)


def _run_rows(chunk, n_chunks):
    return pl.ds(pl.multiple_of(chunk * ROW_CHUNK, ROW_CHUNK), n_chunks * ROW_CHUNK)


def _for_each_chunk(n, fn):
    def trip(i, carry):
        for u in range(CHUNK_LOOP_UNROLL):
            c = i * CHUNK_LOOP_UNROLL + u
            if u == 0:
                fn(c)
            else:
                pl.when(c < n)(functools.partial(fn, c))
        return carry
    lax.fori_loop(0, (n + (CHUNK_LOOP_UNROLL - 1)) // CHUNK_LOOP_UNROLL, trip, 0)


def _dispatch_kernel(psrc_ref, pdst_ref, pn_ref, ssrc_ref, sdst_ref, sn_ref, nch_ref, pstart_ref, pcount_ref, nu_ref,
                     h_ref, gate_ref, apos_ref, xs_hbm, pt_ref, stage_ref, zero_ref, sem):
    j = pl.program_id(0)
    buf = j % 2
    runs = ((psrc_ref, pdst_ref, pn_ref), (ssrc_ref, sdst_ref, sn_ref))
    pad_sem, tile_sem = 2 * len(runs), 2 * len(runs) + 1

    def run_copy(k, b, src_chunk, slot):
        return pltpu.make_async_copy(stage_ref.at[b, _run_rows(src_chunk, RUN_CHUNKS[k]), :],
                                     xs_hbm.at[_run_rows(slot, RUN_CHUNKS[k]), :], sem.at[2 * k + b])

    def wait_tile(b, jj):
        for k, (_, _, n_ref) in enumerate(runs):
            _for_each_chunk(n_ref[jj], lambda r, k=k: run_copy(k, b, 0, 0).wait())

    @pl.when(j < N_TILES)
    def _():
        @pl.when(j >= 2)
        def _():
            wait_tile(buf, j - 2)

        def sort_rows(n_rows):
            pt = _one_hot_rows(apos_ref, n_rows)
            pt_ref[:, 0:n_rows] = pt
            if n_rows < TILE_ROWS:
                pt_ref[:, n_rows:TILE_ROWS] = jnp.zeros((TOK_TILE, TILE_ROWS - n_rows), BF16)
            h = h_ref[...]
            gate = gate_ref[...]
            for r0 in range(0, n_rows, DISPATCH_ROWS):
                group = pt[:, r0:r0 + DISPATCH_ROWS]
                stage_ref[buf, r0:r0 + DISPATCH_ROWS, 0:D_MODEL] = _dot_tn(group, h).astype(BF16)
                stage_ref[buf, r0:r0 + DISPATCH_ROWS, D_MODEL:ROW_WIDTH] = _dot_tn(group, gate).astype(BF16)

        short = nch_ref[j] * ROW_CHUNK <= SHORT_ROWS
        pl.when(short)(functools.partial(sort_rows, SHORT_ROWS))
        pl.when(jnp.logical_not(short))(functools.partial(sort_rows, TILE_ROWS))

        for k, (src_ref, slot_ref, n_ref) in enumerate(runs):
            base = j * RUN_TABLE_STRIDE[k]
            _for_each_chunk(n_ref[j], lambda r, k=k, base=base, src_ref=src_ref, slot_ref=slot_ref: run_copy(
                k, buf, src_ref[base + r], slot_ref[base + r]).start())

    @pl.when(j == N_TILES)
    def _():
        zero_ref[...] = jnp.zeros_like(zero_ref)

        def tile_copy(t):
            rows = pl.ds(pl.multiple_of(t * GMM_TILE, GMM_TILE), GMM_TILE)
            return pltpu.make_async_copy(zero_ref, xs_hbm.at[rows, :], sem.at[tile_sem])

        def per_expert(e, total):
            def fill(i, carry):
                _chunk_copy(zero_ref, 0, xs_hbm, pstart_ref[e] + i, sem.at[pad_sem]).start()
                return carry
            lax.fori_loop(0, pcount_ref[e], fill, 0)
            return total + pcount_ref[e]
        n_pad = lax.fori_loop(0, N_EXPERTS, per_expert, 0)

        def fill_tile(t, carry):
            tile_copy(t).start()
            return carry
        lax.fori_loop(nu_ref[0], MAX_GMM_TILES, fill_tile, 0)
        wait_tile(0, N_TILES - 2)
        wait_tile(1, N_TILES - 1)

        def drain(i, carry):
            _chunk_copy(zero_ref, 0, xs_hbm, 0, sem.at[pad_sem]).wait()
            return carry
        lax.fori_loop(0, n_pad, drain, 0)

        def drain_tile(t, carry):
            tile_copy(0).wait()
            return carry
        lax.fori_loop(nu_ref[0], MAX_GMM_TILES, drain_tile, 0)


def _dispatch(h2, gate, apos, copies, n_chunks, pad_start, pad_count, n_used):
    last = N_TILES - 1
    tok = lambda n: pl.BlockSpec((TOK_TILE, n), lambda j, *_: (jnp.minimum(j, last), 0))
    grid_spec = pltpu.PrefetchScalarGridSpec(
        num_scalar_prefetch=len(copies) + 4,
        grid=(N_TILES + 1,),
        in_specs=[tok(D_MODEL), tok(GATE_LANES), tok(TOP_K)],
        out_specs=[pl.BlockSpec(memory_space=pl.ANY), tok(TILE_ROWS)],
        scratch_shapes=[pltpu.VMEM((2, TILE_ROWS, ROW_WIDTH), BF16), pltpu.VMEM((GMM_TILE, ROW_WIDTH), BF16),
                        pltpu.SemaphoreType.DMA((2 * len(RUN_CHUNKS) + 2,))],
    )
    return pl.pallas_call(
        _dispatch_kernel,
        out_shape=[jax.ShapeDtypeStruct((SORTED_ROWS, ROW_WIDTH), BF16),
                   jax.ShapeDtypeStruct((N_TOK, TILE_ROWS), BF16)],
        grid_spec=grid_spec,
        compiler_params=pltpu.CompilerParams(dimension_semantics=("arbitrary",), vmem_limit_bytes=VMEM_LIMIT),
        name="moe_dispatch",
    )(*copies, n_chunks, pad_start, pad_count, n_used, h2, gate, apos)


def _gmm_kernel(off_ref, cnt_ref, nu_ref, x_hbm, wg_ref, wu_ref, wd_ref, y_hbm, xbuf, ybuf, in_sem, out_sem):
    e = pl.program_id(0)
    n_used = nu_ref[0]

    def rows(g):
        return pl.ds(pl.multiple_of(g * GMM_TILE, GMM_TILE), GMM_TILE)

    def fetch(g):
        slot = g % GMM_IN_SLOTS
        return pltpu.make_async_copy(x_hbm.at[rows(g), :], xbuf.at[slot], in_sem.at[slot])

    def flush(g):
        slot = g % GMM_OUT_SLOTS
        return pltpu.make_async_copy(ybuf.at[slot], y_hbm.at[rows(g), 0:D_MODEL], out_sem.at[slot])

    @pl.when(e == 0)
    def _():
        for g in range(GMM_AHEAD):
            @pl.when(g < n_used)
            def _(g=g):
                fetch(g).start()


    def process(g0, width):
        for w in range(width):
            @pl.when(g0 + w + GMM_AHEAD < n_used)
            def _(w=w):
                fetch(g0 + w + GMM_AHEAD).start()
        for w in range(width):
            fetch(g0 + w).wait()
        xg = jnp.concatenate([xbuf[(g0 + w) % GMM_IN_SLOTS] for w in range(width)], axis=0)
        x = xg[:, 0:D_MODEL]
        gates = xg[:, D_MODEL:ROW_WIDTH].astype(F32)
        lane = lax.broadcasted_iota(jnp.int32, gates.shape, 1)
        weight = jnp.sum(jnp.where(lane == e, gates, 0.0), axis=-1, keepdims=True)
        hid = _silu(_dot(x, wg_ref[...].astype(BF16))) * _dot(x, wu_ref[...].astype(BF16))
        y = (_dot(hid.astype(BF16), wd_ref[...].astype(BF16)) * weight).astype(BF16)
        for w in range(width):
            g = g0 + w

            @pl.when(g >= GMM_OUT_SLOTS)
            def _(g=g):
                flush(g - GMM_OUT_SLOTS).wait()

            ybuf[g % GMM_OUT_SLOTS] = y[w * GMM_TILE:(w + 1) * GMM_TILE]
            flush(g).start()

    first = off_ref[e]
    count = cnt_ref[e]

    done = 0
    width = GMM_WIDTH
    while width >= 1:
        trips = (count - done) // width

        def block(p, carry, done=done, width=width):
            process(first + done + width * p, width)
            return carry

        lax.fori_loop(0, trips, block, 0)
        done = done + trips * width
        width //= 2

    @pl.when(e == N_EXPERTS - 1)
    def _():
        for k in range(GMM_OUT_SLOTS):
            @pl.when(n_used - 1 - k >= 0)
            def _(k=k):
                flush(n_used - 1 - k).wait()


def _gmm(xs, weg, weu, wed, tile_off, tiles, n_used, layer):
    w_in_spec = pl.BlockSpec((None, None, D_MODEL, D_EXPERT), lambda e, *_: (layer, e, 0, 0))
    grid_spec = pltpu.PrefetchScalarGridSpec(
        num_scalar_prefetch=3,
        grid=(N_EXPERTS,),
        in_specs=[
            pl.BlockSpec(memory_space=pl.ANY),
            w_in_spec, w_in_spec,
            pl.BlockSpec((None, None, D_EXPERT, D_MODEL), lambda e, *_: (layer, e, 0, 0)),
        ],
        out_specs=pl.BlockSpec(memory_space=pl.ANY),
        scratch_shapes=[pltpu.VMEM((GMM_IN_SLOTS, GMM_TILE, ROW_WIDTH), BF16),
                        pltpu.VMEM((GMM_OUT_SLOTS, GMM_TILE, D_MODEL), BF16),
                        pltpu.SemaphoreType.DMA((GMM_IN_SLOTS,)), pltpu.SemaphoreType.DMA((GMM_OUT_SLOTS,))],
    )
    return pl.pallas_call(
        _gmm_kernel,
        out_shape=jax.ShapeDtypeStruct((SORTED_ROWS, ROW_WIDTH), BF16),
        grid_spec=grid_spec,
        input_output_aliases={3: 0},
        compiler_params=pltpu.CompilerParams(dimension_semantics=("arbitrary",), vmem_limit_bytes=VMEM_LIMIT),
        name="moe_gmm",
    )(tile_off, tiles, n_used, xs, weg, weu, wed)


def _combine_kernel(psrc_ref, pdst_ref, pn_ref, ssrc_ref, sdst_ref, sn_ref, nch_ref, ys_hbm, pt_ref, h_ref, x1_ref,
                    mod_ref, wsg_ref, wsu_ref, wsd_ref, g2_ref, b2_ref, oc_ref, ol_ref, stage_ref, routed_ref, sem):
    j = pl.program_id(0)
    buf = j % 2
    runs = ((psrc_ref, pdst_ref, pn_ref), (ssrc_ref, sdst_ref, sn_ref))

    def run_fetch(k, b, slot, chunk):
        return pltpu.make_async_copy(ys_hbm.at[_run_rows(slot, RUN_CHUNKS[k]), 0:D_MODEL],
                                     stage_ref.at[b, _run_rows(chunk, RUN_CHUNKS[k]), :], sem.at[2 * k + b])

    def fetch(jj, b):
        for k, (chunk_ref, slot_ref, n_ref) in enumerate(runs):
            base = jj * RUN_TABLE_STRIDE[k]
            _for_each_chunk(n_ref[jj], lambda r, k=k, base=base, chunk_ref=chunk_ref, slot_ref=slot_ref: run_fetch(
                k, b, slot_ref[base + r], chunk_ref[base + r]).start())

    @pl.when(j == 0)
    def _():
        stage_ref[...] = jnp.zeros_like(stage_ref)
        fetch(0, 0)

    @pl.when(j + 1 < N_TILES)
    def _():
        fetch(j + 1, 1 - buf)

    h = h_ref[...]
    hid = _silu(_dot(h, wsg_ref[...])) * _dot(h, wsu_ref[...])
    shared = _dot(hid.astype(BF16), wsd_ref[...])

    for k, (_, _, n_ref) in enumerate(runs):
        _for_each_chunk(n_ref[j], lambda r, k=k: run_fetch(k, buf, 0, 0).wait())

    def gather_rows(n_rows):
        routed_ref[...] = _dot(pt_ref[:, 0:n_rows], stage_ref[buf, 0:n_rows, :])

    short = nch_ref[j] * ROW_CHUNK <= SHORT_ROWS
    pl.when(short)(functools.partial(gather_rows, SHORT_ROWS))
    pl.when(jnp.logical_not(short))(functools.partial(gather_rows, TILE_ROWS))
    r = ALPHA * x1_ref[...] + mod_ref[5:6, :] * (routed_ref[...] + shared)
    out = _ln(r) * g2_ref[...] + b2_ref[...]

    @pl.when(j < CTX_TILES)
    def _():
        oc_ref[...] = out

    @pl.when(j >= CTX_TILES)
    def _():
        ol_ref[...] = out


def _combine(ys, pt, h2, x1, mod, wsg, wsu, wsd, ln_g, ln_b, copies, n_chunks, layer):
    tok = lambda n: pl.BlockSpec((TOK_TILE, n), lambda j, *_: (j, 0))
    const2 = lambda j, *_: (0, 0)
    grid_spec = pltpu.PrefetchScalarGridSpec(
        num_scalar_prefetch=len(copies) + 1,
        grid=(N_TILES,),
        in_specs=[
            pl.BlockSpec(memory_space=pl.ANY),
            tok(TILE_ROWS), tok(D_MODEL), tok(D_MODEL),
            pl.BlockSpec((None, None, 6, D_MODEL), lambda j, *_: (layer, _cond_row_of_tile(j, TOK_TILE), 0, 0)),
            pl.BlockSpec((D_MODEL, D_SHARED), const2),
            pl.BlockSpec((D_MODEL, D_SHARED), const2),
            pl.BlockSpec((D_SHARED, D_MODEL), const2),
            pl.BlockSpec((1, D_MODEL), const2),
            pl.BlockSpec((1, D_MODEL), const2),
        ],
        out_specs=_stream_specs(2),
        scratch_shapes=[pltpu.VMEM((2, TILE_ROWS, D_MODEL), BF16), pltpu.VMEM((TOK_TILE, D_MODEL), F32),
                        pltpu.SemaphoreType.DMA((2 * len(RUN_CHUNKS),))],
    )
    return pl.pallas_call(
        _combine_kernel,
        out_shape=[jax.ShapeDtypeStruct((N_CTX, D_MODEL), F32), jax.ShapeDtypeStruct((N_LAT, D_MODEL), F32)],
        grid_spec=grid_spec,
        compiler_params=pltpu.CompilerParams(dimension_semantics=("arbitrary",), vmem_limit_bytes=VMEM_LIMIT),
        name="moe_combine",
    )(*copies, n_chunks, ys, pt, h2, x1, mod, wsg, wsu, wsd, ln_g, ln_b)


def kernel(x_prompt, x_sample, c, state_hgrn, c_ctx, w_ada, b_ada, w_in, b_gate, hgrn_lb, hgrn_norm, conv_w,
           w_proj_hgrn, w_proj_fourier, w_proj_conv, w_out, ln1_g, ln1_b, ln2_g, ln2_b, w_router, b_router,
           w_exp_gate, w_exp_up, w_exp_down, w_sh_gate, w_sh_up, w_sh_down):
    lb_sm = jax.nn.softmax(hgrn_lb.astype(F32), axis=0)
    lb_all = jnp.cumsum(lb_sm, axis=0) - lb_sm[:1]

    pos = jnp.asarray(_pos_emb_table())
    dft_ctx = jnp.asarray(_dft_time(SEQ)).astype(BF16)
    dft_lat = jnp.asarray(_dft_time(DEC_SEQ)).astype(BF16)
    bdc, bds = (jnp.asarray(m).astype(BF16) for m in _dft_channel())

    cond = jnp.concatenate([c_ctx[None, :], c, jnp.zeros((N_COND - 1 - DEC_BATCH, D_MODEL), F32)], axis=0)
    mod = _ada_mod(cond, w_ada, b_ada).reshape(DEPTH, N_COND, 6, D_MODEL)

    wp_b = jnp.concatenate([w_proj_hgrn, w_proj_fourier, w_proj_conv], axis=1).astype(BF16)
    wo_b = w_out.astype(BF16)
    wr_hi = w_router.astype(BF16)
    wr_lo = (w_router - wr_hi.astype(F32)).astype(BF16)
    wsg_b, wsu_b, wsd_b = w_sh_gate.astype(BF16), w_sh_up.astype(BF16), w_sh_down.astype(BF16)

    stream = (x_prompt.reshape(N_CTX, D_MODEL), x_sample.reshape(N_LAT, D_MODEL), pos)
    new_states = []
    for l in range(DEPTH):
        uh, uf, uc, gates = _in_proj(stream, mod, w_in, b_gate[l][None, :], l)
        small = (lb_all[l], hgrn_norm[l][None, :], conv_w[l])
        mix_ctx, s_ctx = _mixer(uh, uf, uc, *small, dft_ctx, bdc, bds, T=SEQ, n_seq=BATCH, blk0=0)
        (mix_lat,) = _mixer(uh, uf, uc, *small, dft_lat, bdc, bds, T=DEC_SEQ, n_seq=DEC_BATCH,
                            blk0=N_CTX // DEC_SEQ, state0=state_hgrn, layer=l)
        new_states.append(s_ctx)
        x1, h2, apos, gate, n16 = _merge(stream, mix_ctx, mix_lat, gates, mod, wp_b[l], wo_b[l],
                                         ln1_g[l][None, :], ln1_b[l][None, :], wr_hi[l], wr_lo[l],
                                         b_router[l][None, :], l)
        copies, n_chunks, tile_off, tiles, n_used, pad_start, pad_count = _routing_tables(
            n16.reshape(N_TILES, N_EXPERTS))
        xs, pt = _dispatch(h2, gate, apos, copies, n_chunks, pad_start, pad_count, n_used)
        ys = _gmm(xs, w_exp_gate, w_exp_up, w_exp_down, tile_off, tiles, n_used, l)
        stream = tuple(_combine(ys, pt, h2, x1, mod, wsg_b[l], wsu_b[l], wsd_b[l], ln2_g[l][None, :],
                                ln2_b[l][None, :], copies, n_chunks, l))
    y_prompt = stream[0].reshape(BATCH, SEQ, D_MODEL)
    y_sample = stream[1].reshape(DEC_BATCH, DEC_SEQ, D_MODEL)
    return (y_prompt, y_sample, jnp.stack(new_states, axis=1))
```

```python
import functools
import math

import numpy as np
import jax
import jax.numpy as jnp
from jax import lax
from jax.experimental import pallas as pl
from jax.experimental.pallas import tpu as pltpu

F32 = jnp.float32
BF16 = jnp.bfloat16

D_MODEL = 1024
BATCH = 16
SEQ = 256
DEPTH = 2
DEC_BATCH = 4
DEC_SEQ = 1024
GRID_W = 64
HEADS = 4
HEAD_DIM = 128
HGRN_WIDTH = HEADS * HEAD_DIM
FOURIER_GROUPS = 4
FOURIER_GROUP_DIM = 64
FOURIER_WIDTH = FOURIER_GROUPS * FOURIER_GROUP_DIM
CONV_WIDTH = 256
N_BRANCHES = 3
HGRN_COLS = 5 * HGRN_WIDTH
CONV_COLS = 3 * CONV_WIDTH
GATE_COLS = N_BRANCHES * D_MODEL
IN_COLS = HGRN_COLS + FOURIER_WIDTH + CONV_COLS + GATE_COLS
N_EXPERTS = 64
TOP_K = 8
D_EXPERT = 256
D_SHARED = 256
ROUTED_SCALE = 2.5
ALPHA = (2 * DEPTH) ** 0.25
LN_EPS = 1e-6
RMS_EPS = 1e-6
F_MIN = 1e-30

N_CTX = BATCH * SEQ
N_LAT = DEC_BATCH * DEC_SEQ
N_TOK = N_CTX + N_LAT
N_COND = 8

TOK_TILE = 256
CHUNK = 64
CHUNK_UNROLL = 2
WINDOW = 120.0
STATIC_WINDOWS = 2
ADA_TILE = 1536
VMEM_LIMIT = 56 * 1024 * 1024

N_TILES = N_TOK // TOK_TILE
ROW_CHUNK = 16
CHUNK_LOOP_UNROLL = 4
TILE_CHUNKS = (TOK_TILE * TOP_K + N_EXPERTS * (ROW_CHUNK - 1)) // ROW_CHUNK
RUN_CHUNKS = (2, 1)
MAX_PAIRS = TILE_CHUNKS // 2
RUN_TABLE_STRIDE = (MAX_PAIRS, N_EXPERTS)
DISPATCH_ROWS = 192
EXTENT_STEP = 2 * DISPATCH_ROWS
TILE_ROWS = -(-TILE_CHUNKS * ROW_CHUNK // EXTENT_STEP) * EXTENT_STEP
SHORT_ROWS = TILE_ROWS - EXTENT_STEP
GMM_TILE = 256
GMM_WIDTH = 4
GMM_AHEAD = 6
GMM_IN_SLOTS = GMM_WIDTH + GMM_AHEAD
GMM_OUT_SLOTS = 2 * GMM_WIDTH
CHUNKS_PER_GMM_TILE = GMM_TILE // ROW_CHUNK
MAX_GMM_TILES = (N_TILES * TILE_CHUNKS + N_EXPERTS * (CHUNKS_PER_GMM_TILE - 1)) // CHUNKS_PER_GMM_TILE
SORTED_ROWS = MAX_GMM_TILES * GMM_TILE
GATE_LANES = 128
ROW_WIDTH = D_MODEL + GATE_LANES


def _cond_row_of_tile(i, tile):
    ctx_tiles = N_CTX // tile
    per_seq = DEC_SEQ // tile
    return jnp.where(i < ctx_tiles, 0, 1 + (i - ctx_tiles) // per_seq)


def _silu(x):
    return x * jax.nn.sigmoid(x)


def _ln(x):
    mu = jnp.mean(x, axis=-1, keepdims=True)
    xc = x - mu
    var = jnp.mean(xc * xc, axis=-1, keepdims=True)
    return xc * lax.rsqrt(var + LN_EPS)


def _dot(a, b):
    return jnp.dot(a, b, preferred_element_type=F32)


def _dot_nt(a, b):
    return lax.dot_general(a, b, (((1,), (1,)), ((), ())), preferred_element_type=F32)


def _dot_tn(a, b):
    return lax.dot_general(a, b, (((0,), (0,)), ((), ())), preferred_element_type=F32)


@functools.lru_cache(maxsize=None)
def _pos_emb_table():
    rows = DEC_SEQ // GRID_W
    t = np.arange(rows * GRID_W)
    r = (t // GRID_W).astype(np.float32)
    col = (t % GRID_W).astype(np.float32)
    quarter = D_MODEL // 4
    omega = (1.0 / (np.float32(10000.0) ** (np.arange(quarter, dtype=np.float32) / np.float32(quarter)))).astype(np.float32)
    ar = (r[:, None] * omega).astype(np.float32)
    ac = (col[:, None] * omega).astype(np.float32)
    return np.concatenate([np.sin(ar), np.cos(ar), np.sin(ac), np.cos(ac)], axis=-1).astype(np.float32)


@functools.lru_cache(maxsize=None)
def _dft_time(T):
    k = np.arange(T)
    ph = 2.0 * np.pi * ((k[:, None] * k[None, :]) % T) / T
    return np.concatenate([np.cos(ph), -np.sin(ph)], axis=1).astype(np.float32)


@functools.lru_cache(maxsize=None)
def _dft_channel():
    n = FOURIER_GROUP_DIM
    k = np.arange(n)
    ph = 2.0 * np.pi * ((k[:, None] * k[None, :]) % n) / n
    eye = np.eye(FOURIER_GROUPS)
    return np.kron(eye, np.cos(ph)).astype(np.float32), np.kron(eye, np.sin(ph)).astype(np.float32)


CTX_TILES = N_CTX // TOK_TILE
LAT_TILES_PER_SEQ = DEC_SEQ // TOK_TILE


def _stream_specs(n):
    specs = [pl.BlockSpec((TOK_TILE, D_MODEL), lambda i, *_: (jnp.minimum(i, CTX_TILES - 1), 0)),
             pl.BlockSpec((TOK_TILE, D_MODEL), lambda i, *_: (jnp.maximum(i - CTX_TILES, 0), 0))]
    if n == 3:
        specs.append(pl.BlockSpec((TOK_TILE, D_MODEL),
                                  lambda i, *_: (jnp.maximum(i - CTX_TILES, 0) % LAT_TILES_PER_SEQ, 0)))
    return specs


def _stream_tile(refs):
    latent = refs[1][...]
    if len(refs) == 3:
        latent = latent + refs[2][...]
    return jnp.where(pl.program_id(0) < CTX_TILES, refs[0][...], latent)


def _ada_kernel(c_ref, w_ref, b_ref, o_ref):
    s = _silu(c_ref[...]).astype(BF16)
    o_ref[...] = _dot(s, w_ref[...].astype(BF16)) + b_ref[...]


def _ada_mod(cond, w_ada, b_ada):
    n_col = 6 * D_MODEL
    return pl.pallas_call(
        _ada_kernel,
        out_shape=jax.ShapeDtypeStruct((DEPTH, N_COND, n_col), F32),
        grid=(DEPTH, n_col // ADA_TILE),
        in_specs=[
            pl.BlockSpec((N_COND, D_MODEL), lambda l, j: (0, 0)),
            pl.BlockSpec((None, D_MODEL, ADA_TILE), lambda l, j: (l, 0, j)),
            pl.BlockSpec((None, 1, ADA_TILE), lambda l, j: (l, 0, j)),
        ],
        out_specs=pl.BlockSpec((None, N_COND, ADA_TILE), lambda l, j: (l, 0, j)),
        compiler_params=pltpu.CompilerParams(dimension_semantics=("arbitrary", "arbitrary"),
                                             vmem_limit_bytes=VMEM_LIMIT),
        name="ada_mod",
    )(cond, w_ada, b_ada.reshape(DEPTH, 1, n_col))


def _in_proj_kernel(*refs, n_stream):
    mod_ref, w_ref, bg_ref, uh_ref, uf_ref, uc_ref, gates_ref = refs[n_stream:]
    sh1 = mod_ref[0:1, :]
    sc1 = mod_ref[1:2, :]
    h = (_ln(_stream_tile(refs[:n_stream])) * (1.0 + sc1) + sh1).astype(BF16)
    c0 = 0
    for ref, n in ((uh_ref, HGRN_COLS), (uf_ref, FOURIER_WIDTH), (uc_ref, CONV_COLS)):
        ref[...] = _dot(h, w_ref[:, c0:c0 + n].astype(BF16))
        c0 += n
    gates_ref[...] = jax.nn.sigmoid(_dot(h, w_ref[:, c0:c0 + GATE_COLS].astype(BF16)) + bg_ref[...]).astype(BF16)


def _in_proj(stream, mod, w_in, b_gate, layer):
    n_tiles = N_TOK // TOK_TILE
    outs = ((HGRN_COLS, F32), (FOURIER_WIDTH, F32), (CONV_COLS, F32), (GATE_COLS, BF16))
    return pl.pallas_call(
        functools.partial(_in_proj_kernel, n_stream=len(stream)),
        out_shape=[jax.ShapeDtypeStruct((N_TOK, n), dt) for n, dt in outs],
        grid=(n_tiles,),
        in_specs=_stream_specs(len(stream)) + [
            pl.BlockSpec((None, None, 6, D_MODEL), lambda i: (layer, _cond_row_of_tile(i, TOK_TILE), 0, 0)),
            pl.BlockSpec((None, D_MODEL, IN_COLS), lambda i: (layer, 0, 0), pipeline_mode=pl.Buffered(1)),
            pl.BlockSpec((1, GATE_COLS), lambda i: (0, 0)),
        ],
        out_specs=[pl.BlockSpec((TOK_TILE, n), lambda i: (i, 0)) for n, _ in outs],
        compiler_params=pltpu.CompilerParams(dimension_semantics=("arbitrary",), vmem_limit_bytes=VMEM_LIMIT),
        name="in_proj",
    )(*stream, mod, w_in, b_gate)


def _mixer_kernel(*refs, T, has_state, n_prev):
    uh_ref, uf_ref, uc_ref, lb_ref, ng_ref, cw_ref, dft_ref, bdc_ref, bds_ref = refs[:9]
    s0_ref = sfin_ref = prev_ref = None
    if has_state:
        s0_ref, mix_ref, st_ref, o_ref = refs[9:]
    elif n_prev:
        prev_ref, mix_ref, sfin_ref, st_ref, o_ref = refs[9:]
    else:
        mix_ref, sfin_ref, st_ref, o_ref = refs[9:]
    n_chunks = T // CHUNK
    W = HGRN_WIDTH

    for d in range(2):
        for h in range(HEADS):
            if has_state:
                st_ref[d, h] = s0_ref[0, d, h].T
            else:
                st_ref[d, h] = jnp.zeros((HEAD_DIM, HEAD_DIM), F32)
    o_ref[...] = jnp.zeros_like(o_ref)

    row = lax.broadcasted_iota(jnp.int32, (CHUNK, CHUNK), 0)
    col = lax.broadcasted_iota(jnp.int32, (CHUNK, CHUNK), 1)
    keep = (col <= row, col >= row)
    tri = tuple(k.astype(F32).astype(BF16) for k in keep)

    def chunk_step(i, carry):
        deep = []
        for sub, d in ((s, d) for s in range(CHUNK_UNROLL) for d in range(2)):
            c = i * CHUNK_UNROLL + sub
            c = c if d == 0 else n_chunks - 1 - c
            rows = pl.ds(pl.multiple_of(c * CHUNK, CHUNK), CHUNK)
            q = _silu(uh_ref[rows, 0:W])
            v = uh_ref[rows, W:2 * W].astype(BF16)
            z = uh_ref[rows, (2 + d) * W:(3 + d) * W]
            lb = lb_ref[d:d + 1, :]
            e = jnp.exp(-jnp.abs(z))
            r = 1.0 / (1.0 + e)
            er = e * r
            pos = z >= 0.0
            sig_p = jnp.where(pos, r, er)
            sig_n = jnp.where(pos, er, r)
            f = lb + (1.0 - lb) * sig_p
            lf = jnp.log(jnp.maximum(f, F_MIN))
            k = (1.0 - lb) * sig_n
            lf_hi = lf.astype(BF16)
            lf_lo = (lf - lf_hi.astype(F32)).astype(BF16)
            b = _dot(tri[d], lf_hi) + _dot(tri[d], lf_lo)
            g = b[CHUNK - 1:CHUNK, :] if d == 0 else b[0:1, :]

            def window(p_level, q=q, k=k, b=b):
                dist = -p_level - b
                inside = jnp.logical_and(dist >= 0.0, dist < WINDOW)
                k_w = jnp.where(inside, k * jnp.exp(jnp.minimum(dist, WINDOW) - 0.5 * WINDOW), 0.0)
                q_w = q * jnp.exp(jnp.minimum(-dist, 0.0) + 0.5 * WINDOW)
                return q_w.astype(BF16), k_w.astype(BF16)

            wins = [window(WINDOW * p) for p in range(STATIC_WINDOWS)]
            q_abs = (q * jnp.exp(b)).astype(BF16)
            k_end = (k * jnp.exp(g - b)).astype(BF16)
            decay = jnp.exp(g)
            outs = []
            for h in range(HEADS):
                ls = slice(h * HEAD_DIM, (h + 1) * HEAD_DIM)
                q_cat = jnp.concatenate([w[0][:, ls] for w in wins], axis=-1)
                k_cat = jnp.concatenate([w[1][:, ls] for w in wins], axis=-1)
                a = jnp.where(keep[d], _dot_nt(q_cat, k_cat), 0.0).astype(BF16)
                s_t = st_ref[d, h]
                o_h = _dot(a, v[:, ls]) + _dot_nt(q_abs[:, ls], s_t.astype(BF16))
                st_ref[d, h] = s_t * decay[:, ls] + _dot_tn(v[:, ls], k_end[:, ls])
                outs.append(o_h)
            o_ref[rows, :] += jnp.concatenate(outs, axis=-1)
            deep.append((d, window, v, rows, jnp.min(b)))

        deepest = functools.reduce(jnp.minimum, [entry[4] for entry in deep])
        n_windows = jnp.floor(-deepest * (1.0 / WINDOW)).astype(jnp.int32) + 1

        def extra_window(p, carry2):
            for d, window, v, rows, _ in deep:
                q_w, k_w = window(WINDOW * p.astype(F32))
                extra = []
                for h in range(HEADS):
                    ls = slice(h * HEAD_DIM, (h + 1) * HEAD_DIM)
                    a = jnp.where(keep[d], _dot_nt(q_w[:, ls], k_w[:, ls]), 0.0).astype(BF16)
                    extra.append(_dot(a, v[:, ls]))
                o_ref[rows, :] += jnp.concatenate(extra, axis=-1)
            return carry2

        lax.fori_loop(STATIC_WINDOWS, n_windows, extra_window, 0)
        return carry

    lax.fori_loop(0, n_chunks // CHUNK_UNROLL, chunk_step, 0)

    if sfin_ref is not None:
        if n_prev:
            sfin_ref[0, 0:n_prev] = prev_ref[0]
        for d in range(2):
            for h in range(HEADS):
                sfin_ref[0, n_prev, d, h] = st_ref[d, h].T

    o = o_ref[...]
    parts = []
    for h in range(HEADS):
        oh = o[:, h * HEAD_DIM:(h + 1) * HEAD_DIM]
        parts.append(oh * lax.rsqrt(jnp.mean(oh * oh, axis=-1, keepdims=True) + RMS_EPS))
    a_out = jnp.concatenate(parts, axis=-1) * ng_ref[...] * _silu(uh_ref[:, 4 * W:5 * W])
    mix_ref[:, 0:W] = a_out.astype(BF16)

    zf = uf_ref[...].astype(BF16)
    zc = _dot(zf, bdc_ref[...]).astype(BF16)
    zs = _dot(zf, bds_ref[...]).astype(BF16)
    four = _dot(dft_ref[...], jnp.concatenate([zc, zs], axis=0)) * (1.0 / math.sqrt(T * FOURIER_GROUP_DIM))
    mix_ref[:, W:W + FOURIER_WIDTH] = four.astype(BF16)

    cb = uc_ref[:, 0:CONV_WIDTH]
    zz = uc_ref[:, CONV_WIDTH:2 * CONV_WIDTH] * uc_ref[:, 2 * CONV_WIDTH:3 * CONV_WIDTH]
    t_idx = lax.broadcasted_iota(jnp.int32, (T, CONV_WIDTH), 0)
    z_prev = jnp.where(t_idx == 0, 0.0, pltpu.roll(zz, 1, axis=0))
    z_next = jnp.where(t_idx == T - 1, 0.0, pltpu.roll(zz, T - 1, axis=0))
    y = cw_ref[0:1, :] * z_prev + cw_ref[1:2, :] * zz + cw_ref[2:3, :] * z_next
    mix_ref[:, W + FOURIER_WIDTH:] = (cb * y).astype(BF16)


def _mixer(uh, uf, uc, lb, norm_g, conv_w, dft, bdc, bds, T, n_seq, blk0, state0=None, layer=0, prev_states=None):
    has_state = state0 is not None
    n_prev = 0 if prev_states is None else prev_states.shape[1]
    const2 = lambda s: (0, 0)
    in_specs = [
        pl.BlockSpec((T, HGRN_COLS), lambda s: (blk0 + s, 0)),
        pl.BlockSpec((T, FOURIER_WIDTH), lambda s: (blk0 + s, 0)),
        pl.BlockSpec((T, CONV_COLS), lambda s: (blk0 + s, 0)),
        pl.BlockSpec((2, HGRN_WIDTH), const2),
        pl.BlockSpec((1, HGRN_WIDTH), const2),
        pl.BlockSpec((3, CONV_WIDTH), const2),
        pl.BlockSpec((T, 2 * T), const2),
        pl.BlockSpec((FOURIER_WIDTH, FOURIER_WIDTH), const2),
        pl.BlockSpec((FOURIER_WIDTH, FOURIER_WIDTH), const2),
    ]
    args = [uh, uf, uc, lb, norm_g, conv_w, dft, bdc, bds]
    mix_shape = jax.ShapeDtypeStruct((n_seq * T, D_MODEL), BF16)
    mix_spec = pl.BlockSpec((T, D_MODEL), lambda s: (s, 0))
    st_block = (1, 2, HEADS, HEAD_DIM, HEAD_DIM)
    if has_state:
        in_specs.append(pl.BlockSpec((1, None) + st_block[1:], lambda s: (s, layer, 0, 0, 0, 0)))
        args.append(state0)
        out_shape = [mix_shape]
        out_specs = [mix_spec]
    else:
        stack_block = lambda n: pl.BlockSpec((1, n) + st_block[1:], lambda s: (s, 0, 0, 0, 0, 0))
        if n_prev:
            in_specs.append(stack_block(n_prev))
            args.append(prev_states)
        out_shape = [mix_shape, jax.ShapeDtypeStruct((n_seq, n_prev + 1) + st_block[1:], F32)]
        out_specs = [mix_spec, stack_block(n_prev + 1)]
    return pl.pallas_call(
        functools.partial(_mixer_kernel, T=T, has_state=has_state, n_prev=n_prev),
        out_shape=out_shape,
        grid=(n_seq,),
        in_specs=in_specs,
        out_specs=out_specs,
        scratch_shapes=[pltpu.VMEM((2, HEADS, HEAD_DIM, HEAD_DIM), F32), pltpu.VMEM((T, HGRN_WIDTH), F32)],
        compiler_params=pltpu.CompilerParams(dimension_semantics=("arbitrary",), vmem_limit_bytes=VMEM_LIMIT),
        name="mixer_latent" if has_state else "mixer_context",
    )(*args)


def _merge_kernel(*refs, n_stream):
    (mixc_ref, mixl_ref, gates_ref, mod_ref, wp_ref, wo_ref, g1_ref, b1_ref, wrh_ref, wrl_ref, br_ref,
     x1_ref, h2_ref, apos_ref, gate_ref, n16_ref) = refs[n_stream:]
    W = HGRN_WIDTH
    gates = gates_ref[...].astype(F32)
    mix = _stream_tile((mixc_ref, mixl_ref))
    edges = (0, W, W + FOURIER_WIDTH, D_MODEL)
    merged = None
    for j in range(N_BRANCHES):
        p = _dot(mix[:, edges[j]:edges[j + 1]], wp_ref[edges[j]:edges[j + 1], :])
        term = gates[:, j * D_MODEL:(j + 1) * D_MODEL] * p
        merged = term if merged is None else merged + term
    y = _dot(merged.astype(BF16), wo_ref[...])
    x1 = _ln(ALPHA * _stream_tile(refs[:n_stream]) + mod_ref[2:3, :] * y) * g1_ref[...] + b1_ref[...]
    x1_ref[...] = x1
    h2 = _ln(x1) * (1.0 + mod_ref[4:5, :]) + mod_ref[3:4, :]
    h2_hi = h2.astype(BF16)
    h2_ref[...] = h2_hi
    h2_lo = (h2 - h2_hi.astype(F32)).astype(BF16)
    logits = _dot(h2_hi, wrh_ref[...]) + (_dot(h2_hi, wrl_ref[...]) + _dot(h2_lo, wrh_ref[...]))
    scores = jax.nn.sigmoid(logits)
    sel = scores + br_ref[...]
    lane_f = lax.broadcasted_iota(jnp.int32, sel.shape, 1).astype(F32)
    hits = []
    for _ in range(TOP_K):
        top = jnp.max(sel, axis=-1, keepdims=True)
        first = jnp.min(jnp.where(sel == top, lane_f, float(N_EXPERTS)), axis=-1, keepdims=True)
        hit = lane_f == first
        hits.append(hit)
        sel = jnp.where(hit, -jnp.inf, sel)
    chosen = functools.reduce(jnp.logical_or, hits)
    picked = jnp.where(chosen, scores, 0.0)
    denom = jnp.sum(picked, axis=-1, keepdims=True) + 1e-20
    gate = ROUTED_SCALE * picked / denom

    chosen_f = chosen.astype(F32)
    r_i = lax.broadcasted_iota(jnp.int32, (TOK_TILE, TOK_TILE), 0)
    c_i = lax.broadcasted_iota(jnp.int32, (TOK_TILE, TOK_TILE), 1)
    earlier = (c_i < r_i).astype(F32).astype(BF16)
    rank = _dot(earlier, chosen_f.astype(BF16))
    count = jnp.sum(chosen_f, axis=0, keepdims=True)
    n16 = jnp.floor((count + (ROW_CHUNK - 1)) * (1.0 / ROW_CHUNK))
    e_r = lax.broadcasted_iota(jnp.int32, (N_EXPERTS, N_EXPERTS), 0)
    e_c = lax.broadcasted_iota(jnp.int32, (N_EXPERTS, N_EXPERTS), 1)
    before = (e_r < e_c).astype(F32).astype(BF16)
    seg_start = ROW_CHUNK * _dot(jnp.broadcast_to(n16, (8, N_EXPERTS)).astype(BF16), before)[0:1, :]
    row_of = seg_start + rank
    lane_k = lax.broadcasted_iota(jnp.int32, (TOK_TILE, TOP_K), 1)
    apos = jnp.zeros((TOK_TILE, TOP_K), F32)
    for k, hit in enumerate(hits):
        apos = jnp.where(lane_k == k, jnp.sum(jnp.where(hit, row_of, 0.0), axis=-1, keepdims=True), apos)
    apos_ref[...] = apos
    gate_ref[...] = jnp.concatenate([gate, jnp.zeros((TOK_TILE, GATE_LANES - N_EXPERTS), F32)], axis=-1).astype(BF16)
    n16_ref[...] = n16


def _merge(stream, mix_ctx, mix_lat, gates, mod, wp, wo, ln_g, ln_b, wr_hi, wr_lo, b_router, layer):
    tok = lambda n: pl.BlockSpec((TOK_TILE, n), lambda i: (i, 0))
    const2 = lambda i: (0, 0)
    return pl.pallas_call(
        functools.partial(_merge_kernel, n_stream=len(stream)),
        out_shape=[jax.ShapeDtypeStruct((N_TOK, D_MODEL), F32), jax.ShapeDtypeStruct((N_TOK, D_MODEL), BF16),
                   jax.ShapeDtypeStruct((N_TOK, TOP_K), F32), jax.ShapeDtypeStruct((N_TOK, GATE_LANES), BF16),
                   jax.ShapeDtypeStruct((N_TILES, 1, N_EXPERTS), F32)],
        grid=(N_TILES,),
        in_specs=_stream_specs(len(stream)) + _stream_specs(2) + [
            tok(GATE_COLS),
            pl.BlockSpec((None, None, 6, D_MODEL), lambda i: (layer, _cond_row_of_tile(i, TOK_TILE), 0, 0)),
            pl.BlockSpec((D_MODEL, D_MODEL), const2),
            pl.BlockSpec((D_MODEL, D_MODEL), const2),
            pl.BlockSpec((1, D_MODEL), const2),
            pl.BlockSpec((1, D_MODEL), const2),
            pl.BlockSpec((D_MODEL, N_EXPERTS), const2),
            pl.BlockSpec((D_MODEL, N_EXPERTS), const2),
            pl.BlockSpec((1, N_EXPERTS), const2),
        ],
        out_specs=[tok(D_MODEL), tok(D_MODEL), tok(TOP_K), tok(GATE_LANES),
                   pl.BlockSpec((None, 1, N_EXPERTS), lambda i: (i, 0, 0))],
        compiler_params=pltpu.CompilerParams(dimension_semantics=("arbitrary",), vmem_limit_bytes=VMEM_LIMIT),
        name="merge_router",
    )(*stream, mix_ctx, mix_lat, gates, mod, wp, wo, ln_g, ln_b, wr_hi, wr_lo, b_router)


def _routing_tables(n16):
    n16 = n16.astype(jnp.int32)
    hi = jnp.cumsum(n16, axis=1)
    lo = hi - n16
    n_chunks = hi[:, -1]
    eo = jnp.cumsum(n16, axis=0) - n16
    tc = jnp.sum(n16, axis=0)
    tiles = (tc + CHUNKS_PER_GMM_TILE - 1) // CHUNKS_PER_GMM_TILE
    tile_end = jnp.cumsum(tiles)
    tile_off = tile_end - tiles
    n_used = tile_end[-1]
    slot = CHUNKS_PER_GMM_TILE * tile_off[None, :] + eo
    copies = (_copy_runs(n16 // 2, lo, slot, 2, MAX_PAIRS)
              + _copy_runs(n16 % 2, lo + n16 - 1, slot + n16 - 1, 1, N_EXPERTS))
    pad_start = CHUNKS_PER_GMM_TILE * tile_off + tc
    pad_count = CHUNKS_PER_GMM_TILE * tiles - tc
    return (copies, n_chunks, tile_off.astype(jnp.int32), tiles.astype(jnp.int32),
            jnp.reshape(n_used, (1,)).astype(jnp.int32), pad_start.astype(jnp.int32), pad_count.astype(jnp.int32))


def _copy_runs(count, first_src, first_dst, stride, max_runs):
    hi = jnp.cumsum(count, axis=1)
    lo = hi - count
    r = jnp.arange(max_runs, dtype=jnp.int32)[None, :, None]
    owner = jnp.logical_and(r >= lo[:, None, :], r < hi[:, None, :])

    def table(first):
        picked = jnp.sum(jnp.where(owner, (first - stride * lo)[:, None, :], 0), axis=-1)
        return (stride * r[:, :, 0] + picked).reshape(-1).astype(jnp.int32)

    return table(first_src), table(first_dst), hi[:, -1].astype(jnp.int32)


def _one_hot_rows(apos_ref, n_rows):
    r = lax.broadcasted_iota(jnp.int32, (TOK_TILE, n_rows), 1).astype(F32)
    pt = jnp.zeros((TOK_TILE, n_rows), F32)
    for k in range(TOP_K):
        pt = jnp.where(r == apos_ref[:, k:k + 1], 1.0, pt)
    return pt.astype(BF16)


def _chunk_copy(src_ref, src_chunk, dst_ref, dst_chunk, sem):
    rows = lambda c: pl.ds(pl.multiple_of(c * ROW_CHUNK, ROW_CHUNK), ROW_CHUNK)
    return pltpu.make_async_copy(src_ref.at[rows(src_chunk), :], dst_ref.at[rows(dst_chunk), :], sem)


def _run_rows(chunk, n_chunks):
    return pl.ds(pl.multiple_of(chunk * ROW_CHUNK, ROW_CHUNK), n_chunks * ROW_CHUNK)


def _for_each_chunk(n, fn):
    def trip(i, carry):
        for u in range(CHUNK_LOOP_UNROLL):
            c = i * CHUNK_LOOP_UNROLL + u
            if u == 0:
                fn(c)
            else:
                pl.when(c < n)(functools.partial(fn, c))
        return carry
    lax.fori_loop(0, (n + (CHUNK_LOOP_UNROLL - 1)) // CHUNK_LOOP_UNROLL, trip, 0)


def _dispatch_kernel(psrc_ref, pdst_ref, pn_ref, ssrc_ref, sdst_ref, sn_ref, nch_ref, pstart_ref, pcount_ref, nu_ref,
                     h_ref, gate_ref, apos_ref, xs_hbm, pt_ref, stage_ref, zero_ref, sem):
    j = pl.program_id(0)
    buf = j % 2
    runs = ((psrc_ref, pdst_ref, pn_ref), (ssrc_ref, sdst_ref, sn_ref))
    pad_sem, tile_sem = 2 * len(runs), 2 * len(runs) + 1

    def run_copy(k, b, src_chunk, slot):
        return pltpu.make_async_copy(stage_ref.at[b, _run_rows(src_chunk, RUN_CHUNKS[k]), :],
                                     xs_hbm.at[_run_rows(slot, RUN_CHUNKS[k]), :], sem.at[2 * k + b])

    def wait_tile(b, jj):
        for k, (_, _, n_ref) in enumerate(runs):
            _for_each_chunk(n_ref[jj], lambda r, k=k: run_copy(k, b, 0, 0).wait())

    @pl.when(j < N_TILES)
    def _():
        @pl.when(j >= 2)
        def _():
            wait_tile(buf, j - 2)

        def sort_rows(n_rows):
            pt = _one_hot_rows(apos_ref, n_rows)
            pt_ref[:, 0:n_rows] = pt
            if n_rows < TILE_ROWS:
                pt_ref[:, n_rows:TILE_ROWS] = jnp.zeros((TOK_TILE, TILE_ROWS - n_rows), BF16)
            h = h_ref[...]
            gate = gate_ref[...]
            for r0 in range(0, n_rows, DISPATCH_ROWS):
                group = pt[:, r0:r0 + DISPATCH_ROWS]
                stage_ref[buf, r0:r0 + DISPATCH_ROWS, 0:D_MODEL] = _dot_tn(group, h).astype(BF16)
                stage_ref[buf, r0:r0 + DISPATCH_ROWS, D_MODEL:ROW_WIDTH] = _dot_tn(group, gate).astype(BF16)

        short = nch_ref[j] * ROW_CHUNK <= SHORT_ROWS
        pl.when(short)(functools.partial(sort_rows, SHORT_ROWS))
        pl.when(jnp.logical_not(short))(functools.partial(sort_rows, TILE_ROWS))

        for k, (src_ref, slot_ref, n_ref) in enumerate(runs):
            base = j * RUN_TABLE_STRIDE[k]
            _for_each_chunk(n_ref[j], lambda r, k=k, base=base, src_ref=src_ref, slot_ref=slot_ref: run_copy(
                k, buf, src_ref[base + r], slot_ref[base + r]).start())

    @pl.when(j == N_TILES)
    def _():
        zero_ref[...] = jnp.zeros_like(zero_ref)

        def tile_copy(t):
            rows = pl.ds(pl.multiple_of(t * GMM_TILE, GMM_TILE), GMM_TILE)
            return pltpu.make_async_copy(zero_ref, xs_hbm.at[rows, :], sem.at[tile_sem])

        def per_expert(e, total):
            def fill(i, carry):
                _chunk_copy(zero_ref, 0, xs_hbm, pstart_ref[e] + i, sem.at[pad_sem]).start()
                return carry
            lax.fori_loop(0, pcount_ref[e], fill, 0)
            return total + pcount_ref[e]
        n_pad = lax.fori_loop(0, N_EXPERTS, per_expert, 0)

        def fill_tile(t, carry):
            tile_copy(t).start()
            return carry
        lax.fori_loop(nu_ref[0], MAX_GMM_TILES, fill_tile, 0)
        wait_tile(0, N_TILES - 2)
        wait_tile(1, N_TILES - 1)

        def drain(i, carry):
            _chunk_copy(zero_ref, 0, xs_hbm, 0, sem.at[pad_sem]).wait()
            return carry
        lax.fori_loop(0, n_pad, drain, 0)

        def drain_tile(t, carry):
            tile_copy(0).wait()
            return carry
        lax.fori_loop(nu_ref[0], MAX_GMM_TILES, drain_tile, 0)


def _dispatch(h2, gate, apos, copies, n_chunks, pad_start, pad_count, n_used):
    last = N_TILES - 1
    tok = lambda n: pl.BlockSpec((TOK_TILE, n), lambda j, *_: (jnp.minimum(j, last), 0))
    grid_spec = pltpu.PrefetchScalarGridSpec(
        num_scalar_prefetch=len(copies) + 4,
        grid=(N_TILES + 1,),
        in_specs=[tok(D_MODEL), tok(GATE_LANES), tok(TOP_K)],
        out_specs=[pl.BlockSpec(memory_space=pl.ANY), tok(TILE_ROWS)],
        scratch_shapes=[pltpu.VMEM((2, TILE_ROWS, ROW_WIDTH), BF16), pltpu.VMEM((GMM_TILE, ROW_WIDTH), BF16),
                        pltpu.SemaphoreType.DMA((2 * len(RUN_CHUNKS) + 2,))],
    )
    return pl.pallas_call(
        _dispatch_kernel,
        out_shape=[jax.ShapeDtypeStruct((SORTED_ROWS, ROW_WIDTH), BF16),
                   jax.ShapeDtypeStruct((N_TOK, TILE_ROWS), BF16)],
        grid_spec=grid_spec,
        compiler_params=pltpu.CompilerParams(dimension_semantics=("arbitrary",), vmem_limit_bytes=VMEM_LIMIT),
        name="moe_dispatch",
    )(*copies, n_chunks, pad_start, pad_count, n_used, h2, gate, apos)


def _gmm_kernel(off_ref, cnt_ref, nu_ref, x_hbm, wg_ref, wu_ref, wd_ref, y_hbm, xbuf, ybuf, in_sem, out_sem):
    e = pl.program_id(0)
    n_used = nu_ref[0]

    def rows(g):
        return pl.ds(pl.multiple_of(g * GMM_TILE, GMM_TILE), GMM_TILE)

    def fetch(g):
        slot = g % GMM_IN_SLOTS
        return pltpu.make_async_copy(x_hbm.at[rows(g), :], xbuf.at[slot], in_sem.at[slot])

    def flush(g):
        slot = g % GMM_OUT_SLOTS
        return pltpu.make_async_copy(ybuf.at[slot], y_hbm.at[rows(g), 0:D_MODEL], out_sem.at[slot])

    @pl.when(e == 0)
    def _():
        for g in range(GMM_AHEAD):
            @pl.when(g < n_used)
            def _(g=g):
                fetch(g).start()


    def process(g0, width):
        for w in range(width):
            @pl.when(g0 + w + GMM_AHEAD < n_used)
            def _(w=w):
                fetch(g0 + w + GMM_AHEAD).start()
        for w in range(width):
            fetch(g0 + w).wait()
        xg = jnp.concatenate([xbuf[(g0 + w) % GMM_IN_SLOTS] for w in range(width)], axis=0)
        x = xg[:, 0:D_MODEL]
        gates = xg[:, D_MODEL:ROW_WIDTH].astype(F32)
        lane = lax.broadcasted_iota(jnp.int32, gates.shape, 1)
        weight = jnp.sum(jnp.where(lane == e, gates, 0.0), axis=-1, keepdims=True)
        hid = _silu(_dot(x, wg_ref[...].astype(BF16))) * _dot(x, wu_ref[...].astype(BF16))
        y = (_dot(hid.astype(BF16), wd_ref[...].astype(BF16)) * weight).astype(BF16)
        for w in range(width):
            g = g0 + w

            @pl.when(g >= GMM_OUT_SLOTS)
            def _(g=g):
                flush(g - GMM_OUT_SLOTS).wait()

            ybuf[g % GMM_OUT_SLOTS] = y[w * GMM_TILE:(w + 1) * GMM_TILE]
            flush(g).start()

    first = off_ref[e]
    count = cnt_ref[e]

    done = 0
    width = GMM_WIDTH
    while width >= 1:
        trips = (count - done) // width

        def block(p, carry, done=done, width=width):
            process(first + done + width * p, width)
            return carry

        lax.fori_loop(0, trips, block, 0)
        done = done + trips * width
        width //= 2

    @pl.when(e == N_EXPERTS - 1)
    def _():
        for k in range(GMM_OUT_SLOTS):
            @pl.when(n_used - 1 - k >= 0)
            def _(k=k):
                flush(n_used - 1 - k).wait()


def _gmm(xs, weg, weu, wed, tile_off, tiles, n_used, layer):
    w_in_spec = pl.BlockSpec((None, None, D_MODEL, D_EXPERT), lambda e, *_: (layer, e, 0, 0))
    grid_spec = pltpu.PrefetchScalarGridSpec(
        num_scalar_prefetch=3,
        grid=(N_EXPERTS,),
        in_specs=[
            pl.BlockSpec(memory_space=pl.ANY),
            w_in_spec, w_in_spec,
            pl.BlockSpec((None, None, D_EXPERT, D_MODEL), lambda e, *_: (layer, e, 0, 0)),
        ],
        out_specs=pl.BlockSpec(memory_space=pl.ANY),
        scratch_shapes=[pltpu.VMEM((GMM_IN_SLOTS, GMM_TILE, ROW_WIDTH), BF16),
                        pltpu.VMEM((GMM_OUT_SLOTS, GMM_TILE, D_MODEL), BF16),
                        pltpu.SemaphoreType.DMA((GMM_IN_SLOTS,)), pltpu.SemaphoreType.DMA((GMM_OUT_SLOTS,))],
    )
    return pl.pallas_call(
        _gmm_kernel,
        out_shape=jax.ShapeDtypeStruct((SORTED_ROWS, ROW_WIDTH), BF16),
        grid_spec=grid_spec,
        input_output_aliases={3: 0},
        compiler_params=pltpu.CompilerParams(dimension_semantics=("arbitrary",), vmem_limit_bytes=VMEM_LIMIT),
        name="moe_gmm",
    )(tile_off, tiles, n_used, xs, weg, weu, wed)


def _combine_kernel(psrc_ref, pdst_ref, pn_ref, ssrc_ref, sdst_ref, sn_ref, nch_ref, ys_hbm, pt_ref, h_ref, x1_ref,
                    mod_ref, wsg_ref, wsu_ref, wsd_ref, g2_ref, b2_ref, oc_ref, ol_ref, stage_ref, routed_ref, sem):
    j = pl.program_id(0)
    buf = j % 2
    runs = ((psrc_ref, pdst_ref, pn_ref), (ssrc_ref, sdst_ref, sn_ref))

    def run_fetch(k, b, slot, chunk):
        return pltpu.make_async_copy(ys_hbm.at[_run_rows(slot, RUN_CHUNKS[k]), 0:D_MODEL],
                                     stage_ref.at[b, _run_rows(chunk, RUN_CHUNKS[k]), :], sem.at[2 * k + b])

    def fetch(jj, b):
        for k, (chunk_ref, slot_ref, n_ref) in enumerate(runs):
            base = jj * RUN_TABLE_STRIDE[k]
            _for_each_chunk(n_ref[jj], lambda r, k=k, base=base, chunk_ref=chunk_ref, slot_ref=slot_ref: run_fetch(
                k, b, slot_ref[base + r], chunk_ref[base + r]).start())

    @pl.when(j == 0)
    def _():
        stage_ref[...] = jnp.zeros_like(stage_ref)
        fetch(0, 0)

    @pl.when(j + 1 < N_TILES)
    def _():
        fetch(j + 1, 1 - buf)

    h = h_ref[...]
    hid = _silu(_dot(h, wsg_ref[...])) * _dot(h, wsu_ref[...])
    shared = _dot(hid.astype(BF16), wsd_ref[...])

    for k, (_, _, n_ref) in enumerate(runs):
        _for_each_chunk(n_ref[j], lambda r, k=k: run_fetch(k, buf, 0, 0).wait())

    def gather_rows(n_rows):
        routed_ref[...] = _dot(pt_ref[:, 0:n_rows], stage_ref[buf, 0:n_rows, :])

    short = nch_ref[j] * ROW_CHUNK <= SHORT_ROWS
    pl.when(short)(functools.partial(gather_rows, SHORT_ROWS))
    pl.when(jnp.logical_not(short))(functools.partial(gather_rows, TILE_ROWS))
    r = ALPHA * x1_ref[...] + mod_ref[5:6, :] * (routed_ref[...] + shared)
    out = _ln(r) * g2_ref[...] + b2_ref[...]

    @pl.when(j < CTX_TILES)
    def _():
        oc_ref[...] = out

    @pl.when(j >= CTX_TILES)
    def _():
        ol_ref[...] = out


def _combine(ys, pt, h2, x1, mod, wsg, wsu, wsd, ln_g, ln_b, copies, n_chunks, layer):
    tok = lambda n: pl.BlockSpec((TOK_TILE, n), lambda j, *_: (j, 0))
    const2 = lambda j, *_: (0, 0)
    grid_spec = pltpu.PrefetchScalarGridSpec(
        num_scalar_prefetch=len(copies) + 1,
        grid=(N_TILES,),
        in_specs=[
            pl.BlockSpec(memory_space=pl.ANY),
            tok(TILE_ROWS), tok(D_MODEL), tok(D_MODEL),
            pl.BlockSpec((None, None, 6, D_MODEL), lambda j, *_: (layer, _cond_row_of_tile(j, TOK_TILE), 0, 0)),
            pl.BlockSpec((D_MODEL, D_SHARED), const2),
            pl.BlockSpec((D_MODEL, D_SHARED), const2),
            pl.BlockSpec((D_SHARED, D_MODEL), const2),
            pl.BlockSpec((1, D_MODEL), const2),
            pl.BlockSpec((1, D_MODEL), const2),
        ],
        out_specs=_stream_specs(2),
        scratch_shapes=[pltpu.VMEM((2, TILE_ROWS, D_MODEL), BF16), pltpu.VMEM((TOK_TILE, D_MODEL), F32),
                        pltpu.SemaphoreType.DMA((2 * len(RUN_CHUNKS),))],
    )
    return pl.pallas_call(
        _combine_kernel,
        out_shape=[jax.ShapeDtypeStruct((N_CTX, D_MODEL), F32), jax.ShapeDtypeStruct((N_LAT, D_MODEL), F32)],
        grid_spec=grid_spec,
        compiler_params=pltpu.CompilerParams(dimension_semantics=("arbitrary",), vmem_limit_bytes=VMEM_LIMIT),
        name="moe_combine",
    )(*copies, n_chunks, ys, pt, h2, x1, mod, wsg, wsu, wsd, ln_g, ln_b)


def kernel(x_prompt, x_sample, c, state_hgrn, c_ctx, w_ada, b_ada, w_in, b_gate, hgrn_lb, hgrn_norm, conv_w,
           w_proj_hgrn, w_proj_fourier, w_proj_conv, w_out, ln1_g, ln1_b, ln2_g, ln2_b, w_router, b_router,
           w_exp_gate, w_exp_up, w_exp_down, w_sh_gate, w_sh_up, w_sh_down):
    lb_sm = jax.nn.softmax(hgrn_lb.astype(F32), axis=0)
    lb_all = jnp.cumsum(lb_sm, axis=0) - lb_sm[:1]

    pos = jnp.asarray(_pos_emb_table())
    dft_ctx = jnp.asarray(_dft_time(SEQ)).astype(BF16)
    dft_lat = jnp.asarray(_dft_time(DEC_SEQ)).astype(BF16)
    bdc, bds = (jnp.asarray(m).astype(BF16) for m in _dft_channel())

    cond = jnp.concatenate([c_ctx[None, :], c, jnp.zeros((N_COND - 1 - DEC_BATCH, D_MODEL), F32)], axis=0)
    mod = _ada_mod(cond, w_ada, b_ada).reshape(DEPTH, N_COND, 6, D_MODEL)

    wp_b = jnp.concatenate([w_proj_hgrn, w_proj_fourier, w_proj_conv], axis=1).astype(BF16)
    wo_b = w_out.astype(BF16)
    wr_hi = w_router.astype(BF16)
    wr_lo = (w_router - wr_hi.astype(F32)).astype(BF16)
    wsg_b, wsu_b, wsd_b = w_sh_gate.astype(BF16), w_sh_up.astype(BF16), w_sh_down.astype(BF16)

    stream = (x_prompt.reshape(N_CTX, D_MODEL), x_sample.reshape(N_LAT, D_MODEL), pos)
    states = None
    for l in range(DEPTH):
        uh, uf, uc, gates = _in_proj(stream, mod, w_in, b_gate[l][None, :], l)
        small = (lb_all[l], hgrn_norm[l][None, :], conv_w[l])
        mix_ctx, states = _mixer(uh, uf, uc, *small, dft_ctx, bdc, bds, T=SEQ, n_seq=BATCH, blk0=0,
                                 prev_states=states)
        (mix_lat,) = _mixer(uh, uf, uc, *small, dft_lat, bdc, bds, T=DEC_SEQ, n_seq=DEC_BATCH,
                            blk0=N_CTX // DEC_SEQ, state0=state_hgrn, layer=l)
        x1, h2, apos, gate, n16 = _merge(stream, mix_ctx, mix_lat, gates, mod, wp_b[l], wo_b[l],
                                         ln1_g[l][None, :], ln1_b[l][None, :], wr_hi[l], wr_lo[l],
                                         b_router[l][None, :], l)
        copies, n_chunks, tile_off, tiles, n_used, pad_start, pad_count = _routing_tables(
            n16.reshape(N_TILES, N_EXPERTS))
        xs, pt = _dispatch(h2, gate, apos, copies, n_chunks, pad_start, pad_count, n_used)
        ys = _gmm(xs, w_exp_gate, w_exp_up, w_exp_down, tile_off, tiles, n_used, l)
        stream = tuple(_combine(ys, pt, h2, x1, mod, wsg_b[l], wsu_b[l], wsd_b[l], ln2_g[l][None, :],
                                ln2_b[l][None, :], copies, n_chunks, l))
    y_prompt = stream[0].reshape(BATCH, SEQ, D_MODEL)
    y_sample = stream[1].reshape(DEC_BATCH, DEC_SEQ, D_MODEL)
    return (y_prompt, y_sample, states)
```

```python
import functools
import math

import numpy as np
import jax
import jax.numpy as jnp
from jax import lax
from jax.experimental import pallas as pl
from jax.experimental.pallas import tpu as pltpu

F32 = jnp.float32
BF16 = jnp.bfloat16

D_MODEL = 1024
BATCH = 16
SEQ = 256
DEPTH = 2
DEC_BATCH = 4
DEC_SEQ = 1024
GRID_W = 64
HEADS = 4
HEAD_DIM = 128
HGRN_WIDTH = HEADS * HEAD_DIM
FOURIER_GROUPS = 4
FOURIER_GROUP_DIM = 64
FOURIER_WIDTH = FOURIER_GROUPS * FOURIER_GROUP_DIM
CONV_WIDTH = 256
N_BRANCHES = 3
HGRN_COLS = 5 * HGRN_WIDTH
CONV_COLS = 3 * CONV_WIDTH
GATE_COLS = N_BRANCHES * D_MODEL
IN_COLS = HGRN_COLS + FOURIER_WIDTH + CONV_COLS + GATE_COLS
N_EXPERTS = 64
TOP_K = 8
D_EXPERT = 256
D_SHARED = 256
ROUTED_SCALE = 2.5
ALPHA = (2 * DEPTH) ** 0.25
LN_EPS = 1e-6
RMS_EPS = 1e-6
F_MIN = 1e-30

N_CTX = BATCH * SEQ
N_LAT = DEC_BATCH * DEC_SEQ
N_TOK = N_CTX + N_LAT
N_COND = 8

TOK_TILE = 256
CHUNK = 64
CHUNK_UNROLL = 2
WINDOW = 120.0
STATIC_WINDOWS = 2
ADA_TILE = 1536
VMEM_LIMIT = 56 * 1024 * 1024

N_TILES = N_TOK // TOK_TILE
ROW_CHUNK = 16
CHUNK_LOOP_UNROLL = 4
TILE_CHUNKS = (TOK_TILE * TOP_K + N_EXPERTS * (ROW_CHUNK - 1)) // ROW_CHUNK
RUN_CHUNKS = (2, 1)
MAX_PAIRS = TILE_CHUNKS // 2
RUN_TABLE_STRIDE = (MAX_PAIRS, N_EXPERTS)
DISPATCH_ROWS = 192
EXTENT_STEP = 2 * DISPATCH_ROWS
TILE_ROWS = -(-TILE_CHUNKS * ROW_CHUNK // EXTENT_STEP) * EXTENT_STEP
SHORT_ROWS = TILE_ROWS - EXTENT_STEP
GMM_TILE = 256
GMM_WIDTH = 4
GMM_AHEAD = 6
GMM_IN_SLOTS = GMM_WIDTH + GMM_AHEAD
GMM_OUT_SLOTS = 2 * GMM_WIDTH
CHUNKS_PER_GMM_TILE = GMM_TILE // ROW_CHUNK
MAX_GMM_TILES = (N_TILES * TILE_CHUNKS + N_EXPERTS * (CHUNKS_PER_GMM_TILE - 1)) // CHUNKS_PER_GMM_TILE
SORTED_ROWS = MAX_GMM_TILES * GMM_TILE
GATE_LANES = 128
ROW_WIDTH = D_MODEL + GATE_LANES


def _cond_row_of_tile(i, tile):
    ctx_tiles = N_CTX // tile
    per_seq = DEC_SEQ // tile
    return jnp.where(i < ctx_tiles, 0, 1 + (i - ctx_tiles) // per_seq)


def _silu(x):
    return x * jax.nn.sigmoid(x)


def _ln(x):
    mu = jnp.mean(x, axis=-1, keepdims=True)
    xc = x - mu
    var = jnp.mean(xc * xc, axis=-1, keepdims=True)
    return xc * lax.rsqrt(var + LN_EPS)


def _dot(a, b):
    return jnp.dot(a, b, preferred_element_type=F32)


def _dot_nt(a, b):
    return lax.dot_general(a, b, (((1,), (1,)), ((), ())), preferred_element_type=F32)


def _dot_tn(a, b):
    return lax.dot_general(a, b, (((0,), (0,)), ((), ())), preferred_element_type=F32)


@functools.lru_cache(maxsize=None)
def _pos_emb_table():
    rows = DEC_SEQ // GRID_W
    t = np.arange(rows * GRID_W)
    r = (t // GRID_W).astype(np.float32)
    col = (t % GRID_W).astype(np.float32)
    quarter = D_MODEL // 4
    omega = (1.0 / (np.float32(10000.0) ** (np.arange(quarter, dtype=np.float32) / np.float32(quarter)))).astype(np.float32)
    ar = (r[:, None] * omega).astype(np.float32)
    ac = (col[:, None] * omega).astype(np.float32)
    return np.concatenate([np.sin(ar), np.cos(ar), np.sin(ac), np.cos(ac)], axis=-1).astype(np.float32)


@functools.lru_cache(maxsize=None)
def _dft_time(T):
    k = np.arange(T)
    ph = 2.0 * np.pi * ((k[:, None] * k[None, :]) % T) / T
    return np.concatenate([np.cos(ph), -np.sin(ph)], axis=1).astype(np.float32)


@functools.lru_cache(maxsize=None)
def _dft_channel():
    n = FOURIER_GROUP_DIM
    k = np.arange(n)
    ph = 2.0 * np.pi * ((k[:, None] * k[None, :]) % n) / n
    eye = np.eye(FOURIER_GROUPS)
    return np.kron(eye, np.cos(ph)).astype(np.float32), np.kron(eye, np.sin(ph)).astype(np.float32)


CTX_TILES = N_CTX // TOK_TILE
LAT_TILES_PER_SEQ = DEC_SEQ // TOK_TILE


def _stream_specs(n):
    specs = [pl.BlockSpec((TOK_TILE, D_MODEL), lambda i, *_: (jnp.minimum(i, CTX_TILES - 1), 0)),
             pl.BlockSpec((TOK_TILE, D_MODEL), lambda i, *_: (jnp.maximum(i - CTX_TILES, 0), 0))]
    if n == 3:
        specs.append(pl.BlockSpec((TOK_TILE, D_MODEL),
                                  lambda i, *_: (jnp.maximum(i - CTX_TILES, 0) % LAT_TILES_PER_SEQ, 0)))
    return specs


def _stream_tile(refs):
    latent = refs[1][...]
    if len(refs) == 3:
        latent = latent + refs[2][...]
    return jnp.where(pl.program_id(0) < CTX_TILES, refs[0][...], latent)


def _ada_kernel(c_ref, w_ref, b_ref, o_ref):
    s = _silu(c_ref[...]).astype(BF16)
    o_ref[...] = _dot(s, w_ref[...].astype(BF16)) + b_ref[...]


def _ada_mod(cond, w_ada, b_ada):
    n_col = 6 * D_MODEL
    return pl.pallas_call(
        _ada_kernel,
        out_shape=jax.ShapeDtypeStruct((DEPTH, N_COND, n_col), F32),
        grid=(DEPTH, n_col // ADA_TILE),
        in_specs=[
            pl.BlockSpec((N_COND, D_MODEL), lambda l, j: (0, 0)),
            pl.BlockSpec((None, D_MODEL, ADA_TILE), lambda l, j: (l, 0, j)),
            pl.BlockSpec((None, 1, ADA_TILE), lambda l, j: (l, 0, j)),
        ],
        out_specs=pl.BlockSpec((None, N_COND, ADA_TILE), lambda l, j: (l, 0, j)),
        compiler_params=pltpu.CompilerParams(dimension_semantics=("arbitrary", "arbitrary"),
                                             vmem_limit_bytes=VMEM_LIMIT),
        name="ada_mod",
    )(cond, w_ada, b_ada.reshape(DEPTH, 1, n_col))


def _in_proj_kernel(*refs, n_stream):
    mod_ref, w_ref, bg_ref, uh_ref, uf_ref, uc_ref, gates_ref = refs[n_stream:]
    sh1 = mod_ref[0:1, :]
    sc1 = mod_ref[1:2, :]
    h = (_ln(_stream_tile(refs[:n_stream])) * (1.0 + sc1) + sh1).astype(BF16)
    c0 = 0
    for ref, n in ((uh_ref, HGRN_COLS), (uf_ref, FOURIER_WIDTH), (uc_ref, CONV_COLS)):
        ref[...] = _dot(h, w_ref[:, c0:c0 + n].astype(BF16))
        c0 += n
    gates_ref[...] = jax.nn.sigmoid(_dot(h, w_ref[:, c0:c0 + GATE_COLS].astype(BF16)) + bg_ref[...]).astype(BF16)


def _in_proj(stream, mod, w_in, b_gate, layer):
    n_tiles = N_TOK // TOK_TILE
    outs = ((HGRN_COLS, F32), (FOURIER_WIDTH, F32), (CONV_COLS, F32), (GATE_COLS, BF16))
    return pl.pallas_call(
        functools.partial(_in_proj_kernel, n_stream=len(stream)),
        out_shape=[jax.ShapeDtypeStruct((N_TOK, n), dt) for n, dt in outs],
        grid=(n_tiles,),
        in_specs=_stream_specs(len(stream)) + [
            pl.BlockSpec((None, None, 6, D_MODEL), lambda i: (layer, _cond_row_of_tile(i, TOK_TILE), 0, 0)),
            pl.BlockSpec((None, D_MODEL, IN_COLS), lambda i: (layer, 0, 0), pipeline_mode=pl.Buffered(1)),
            pl.BlockSpec((None, 1, GATE_COLS), lambda i: (layer, 0, 0)),
        ],
        out_specs=[pl.BlockSpec((TOK_TILE, n), lambda i: (i, 0)) for n, _ in outs],
        compiler_params=pltpu.CompilerParams(dimension_semantics=("arbitrary",), vmem_limit_bytes=VMEM_LIMIT),
        name="in_proj",
    )(*stream, mod, w_in, b_gate)


def _mixer_kernel(*refs, T, has_state, n_prev):
    uh_ref, uf_ref, uc_ref, lb_ref, ng_ref, cw_ref, dft_ref, bdc_ref, bds_ref = refs[:9]
    s0_ref = sfin_ref = prev_ref = None
    if has_state:
        s0_ref, mix_ref, st_ref, o_ref = refs[9:]
    elif n_prev:
        prev_ref, mix_ref, sfin_ref, st_ref, o_ref = refs[9:]
    else:
        mix_ref, sfin_ref, st_ref, o_ref = refs[9:]
    n_chunks = T // CHUNK
    W = HGRN_WIDTH

    for d in range(2):
        for h in range(HEADS):
            if has_state:
                st_ref[d, h] = s0_ref[0, d, h].T
            else:
                st_ref[d, h] = jnp.zeros((HEAD_DIM, HEAD_DIM), F32)
    o_ref[...] = jnp.zeros_like(o_ref)

    row = lax.broadcasted_iota(jnp.int32, (CHUNK, CHUNK), 0)
    col = lax.broadcasted_iota(jnp.int32, (CHUNK, CHUNK), 1)
    keep = (col <= row, col >= row)
    tri = tuple(k.astype(F32).astype(BF16) for k in keep)

    def chunk_step(i, carry):
        deep = []
        for sub, d in ((s, d) for s in range(CHUNK_UNROLL) for d in range(2)):
            c = i * CHUNK_UNROLL + sub
            c = c if d == 0 else n_chunks - 1 - c
            rows = pl.ds(pl.multiple_of(c * CHUNK, CHUNK), CHUNK)
            q = _silu(uh_ref[rows, 0:W])
            v = uh_ref[rows, W:2 * W].astype(BF16)
            z = uh_ref[rows, (2 + d) * W:(3 + d) * W]
            lb = lb_ref[d:d + 1, :]
            e = jnp.exp(-jnp.abs(z))
            r = 1.0 / (1.0 + e)
            er = e * r
            pos = z >= 0.0
            sig_p = jnp.where(pos, r, er)
            sig_n = jnp.where(pos, er, r)
            f = lb + (1.0 - lb) * sig_p
            lf = jnp.log(jnp.maximum(f, F_MIN))
            k = (1.0 - lb) * sig_n
            lf_hi = lf.astype(BF16)
            lf_lo = (lf - lf_hi.astype(F32)).astype(BF16)
            b = _dot(tri[d], lf_hi) + _dot(tri[d], lf_lo)
            g = b[CHUNK - 1:CHUNK, :] if d == 0 else b[0:1, :]

            def window(p_level, q=q, k=k, b=b):
                dist = -p_level - b
                inside = jnp.logical_and(dist >= 0.0, dist < WINDOW)
                k_w = jnp.where(inside, k * jnp.exp(jnp.minimum(dist, WINDOW) - 0.5 * WINDOW), 0.0)
                q_w = q * jnp.exp(jnp.minimum(-dist, 0.0) + 0.5 * WINDOW)
                return q_w.astype(BF16), k_w.astype(BF16)

            wins = [window(WINDOW * p) for p in range(STATIC_WINDOWS)]
            q_abs = (q * jnp.exp(b)).astype(BF16)
            k_end = (k * jnp.exp(g - b)).astype(BF16)
            decay = jnp.exp(g)
            outs = []
            for h in range(HEADS):
                ls = slice(h * HEAD_DIM, (h + 1) * HEAD_DIM)
                q_cat = jnp.concatenate([w[0][:, ls] for w in wins], axis=-1)
                k_cat = jnp.concatenate([w[1][:, ls] for w in wins], axis=-1)
                a = jnp.where(keep[d], _dot_nt(q_cat, k_cat), 0.0).astype(BF16)
                s_t = st_ref[d, h]
                o_h = _dot(a, v[:, ls]) + _dot_nt(q_abs[:, ls], s_t.astype(BF16))
                st_ref[d, h] = s_t * decay[:, ls] + _dot_tn(v[:, ls], k_end[:, ls])
                outs.append(o_h)
            o_ref[rows, :] += jnp.concatenate(outs, axis=-1)
            deep.append((d, window, v, rows, jnp.min(b)))

        deepest = functools.reduce(jnp.minimum, [entry[4] for entry in deep])
        n_windows = jnp.floor(-deepest * (1.0 / WINDOW)).astype(jnp.int32) + 1

        def extra_window(p, carry2):
            for d, window, v, rows, _ in deep:
                q_w, k_w = window(WINDOW * p.astype(F32))
                extra = []
                for h in range(HEADS):
                    ls = slice(h * HEAD_DIM, (h + 1) * HEAD_DIM)
                    a = jnp.where(keep[d], _dot_nt(q_w[:, ls], k_w[:, ls]), 0.0).astype(BF16)
                    extra.append(_dot(a, v[:, ls]))
                o_ref[rows, :] += jnp.concatenate(extra, axis=-1)
            return carry2

        lax.fori_loop(STATIC_WINDOWS, n_windows, extra_window, 0)
        return carry

    lax.fori_loop(0, n_chunks // CHUNK_UNROLL, chunk_step, 0)

    if sfin_ref is not None:
        if n_prev:
            sfin_ref[0, 0:n_prev] = prev_ref[0]
        for d in range(2):
            for h in range(HEADS):
                sfin_ref[0, n_prev, d, h] = st_ref[d, h].T

    o = o_ref[...]
    parts = []
    for h in range(HEADS):
        oh = o[:, h * HEAD_DIM:(h + 1) * HEAD_DIM]
        parts.append(oh * lax.rsqrt(jnp.mean(oh * oh, axis=-1, keepdims=True) + RMS_EPS))
    a_out = jnp.concatenate(parts, axis=-1) * ng_ref[...] * _silu(uh_ref[:, 4 * W:5 * W])
    mix_ref[:, 0:W] = a_out.astype(BF16)

    zf = uf_ref[...].astype(BF16)
    zc = _dot(zf, bdc_ref[...]).astype(BF16)
    zs = _dot(zf, bds_ref[...]).astype(BF16)
    four = _dot(dft_ref[...], jnp.concatenate([zc, zs], axis=0)) * (1.0 / math.sqrt(T * FOURIER_GROUP_DIM))
    mix_ref[:, W:W + FOURIER_WIDTH] = four.astype(BF16)

    cb = uc_ref[:, 0:CONV_WIDTH]
    zz = uc_ref[:, CONV_WIDTH:2 * CONV_WIDTH] * uc_ref[:, 2 * CONV_WIDTH:3 * CONV_WIDTH]
    t_idx = lax.broadcasted_iota(jnp.int32, (T, CONV_WIDTH), 0)
    z_prev = jnp.where(t_idx == 0, 0.0, pltpu.roll(zz, 1, axis=0))
    z_next = jnp.where(t_idx == T - 1, 0.0, pltpu.roll(zz, T - 1, axis=0))
    y = cw_ref[0:1, :] * z_prev + cw_ref[1:2, :] * zz + cw_ref[2:3, :] * z_next
    mix_ref[:, W + FOURIER_WIDTH:] = (cb * y).astype(BF16)


def _mixer(uh, uf, uc, lb, norm_g, conv_w, dft, bdc, bds, T, n_seq, blk0, state0=None, layer=0, prev_states=None):
    has_state = state0 is not None
    n_prev = 0 if prev_states is None else prev_states.shape[1]
    const2 = lambda s: (0, 0)
    in_specs = [
        pl.BlockSpec((T, HGRN_COLS), lambda s: (blk0 + s, 0)),
        pl.BlockSpec((T, FOURIER_WIDTH), lambda s: (blk0 + s, 0)),
        pl.BlockSpec((T, CONV_COLS), lambda s: (blk0 + s, 0)),
        pl.BlockSpec((2, HGRN_WIDTH), const2),
        pl.BlockSpec((1, HGRN_WIDTH), const2),
        pl.BlockSpec((3, CONV_WIDTH), const2),
        pl.BlockSpec((T, 2 * T), const2),
        pl.BlockSpec((FOURIER_WIDTH, FOURIER_WIDTH), const2),
        pl.BlockSpec((FOURIER_WIDTH, FOURIER_WIDTH), const2),
    ]
    args = [uh, uf, uc, lb, norm_g, conv_w, dft, bdc, bds]
    mix_shape = jax.ShapeDtypeStruct((n_seq * T, D_MODEL), BF16)
    mix_spec = pl.BlockSpec((T, D_MODEL), lambda s: (s, 0))
    st_block = (1, 2, HEADS, HEAD_DIM, HEAD_DIM)
    if has_state:
        in_specs.append(pl.BlockSpec((1, None) + st_block[1:], lambda s: (s, layer, 0, 0, 0, 0)))
        args.append(state0)
        out_shape = [mix_shape]
        out_specs = [mix_spec]
    else:
        stack_block = lambda n: pl.BlockSpec((1, n) + st_block[1:], lambda s: (s, 0, 0, 0, 0, 0))
        if n_prev:
            in_specs.append(stack_block(n_prev))
            args.append(prev_states)
        out_shape = [mix_shape, jax.ShapeDtypeStruct((n_seq, n_prev + 1) + st_block[1:], F32)]
        out_specs = [mix_spec, stack_block(n_prev + 1)]
    return pl.pallas_call(
        functools.partial(_mixer_kernel, T=T, has_state=has_state, n_prev=n_prev),
        out_shape=out_shape,
        grid=(n_seq,),
        in_specs=in_specs,
        out_specs=out_specs,
        scratch_shapes=[pltpu.VMEM((2, HEADS, HEAD_DIM, HEAD_DIM), F32), pltpu.VMEM((T, HGRN_WIDTH), F32)],
        compiler_params=pltpu.CompilerParams(dimension_semantics=("arbitrary",), vmem_limit_bytes=VMEM_LIMIT),
        name="mixer_latent" if has_state else "mixer_context",
    )(*args)


def _merge_kernel(*refs, n_stream):
    (mixc_ref, mixl_ref, gates_ref, mod_ref, wp_ref, wo_ref, g1_ref, b1_ref, wrh_ref, wrl_ref, br_ref,
     x1_ref, h2_ref, apos_ref, gate_ref, n16_ref) = refs[n_stream:]
    W = HGRN_WIDTH
    gates = gates_ref[...].astype(F32)
    mix = _stream_tile((mixc_ref, mixl_ref))
    edges = (0, W, W + FOURIER_WIDTH, D_MODEL)
    merged = None
    for j in range(N_BRANCHES):
        p = _dot(mix[:, edges[j]:edges[j + 1]], wp_ref[edges[j]:edges[j + 1], :])
        term = gates[:, j * D_MODEL:(j + 1) * D_MODEL] * p
        merged = term if merged is None else merged + term
    y = _dot(merged.astype(BF16), wo_ref[...])
    x1 = _ln(ALPHA * _stream_tile(refs[:n_stream]) + mod_ref[2:3, :] * y) * g1_ref[...] + b1_ref[...]
    x1_ref[...] = x1
    h2 = _ln(x1) * (1.0 + mod_ref[4:5, :]) + mod_ref[3:4, :]
    h2_hi = h2.astype(BF16)
    h2_ref[...] = h2_hi
    h2_lo = (h2 - h2_hi.astype(F32)).astype(BF16)
    logits = _dot(h2_hi, wrh_ref[...]) + (_dot(h2_hi, wrl_ref[...]) + _dot(h2_lo, wrh_ref[...]))
    scores = jax.nn.sigmoid(logits)
    sel = scores + br_ref[...]
    lane_f = lax.broadcasted_iota(jnp.int32, sel.shape, 1).astype(F32)
    hits = []
    for _ in range(TOP_K):
        top = jnp.max(sel, axis=-1, keepdims=True)
        first = jnp.min(jnp.where(sel == top, lane_f, float(N_EXPERTS)), axis=-1, keepdims=True)
        hit = lane_f == first
        hits.append(hit)
        sel = jnp.where(hit, -jnp.inf, sel)
    chosen = functools.reduce(jnp.logical_or, hits)
    picked = jnp.where(chosen, scores, 0.0)
    denom = jnp.sum(picked, axis=-1, keepdims=True) + 1e-20
    gate = ROUTED_SCALE * picked / denom

    chosen_f = chosen.astype(F32)
    r_i = lax.broadcasted_iota(jnp.int32, (TOK_TILE, TOK_TILE), 0)
    c_i = lax.broadcasted_iota(jnp.int32, (TOK_TILE, TOK_TILE), 1)
    earlier = (c_i < r_i).astype(F32).astype(BF16)
    rank = _dot(earlier, chosen_f.astype(BF16))
    count = jnp.sum(chosen_f, axis=0, keepdims=True)
    n16 = jnp.floor((count + (ROW_CHUNK - 1)) * (1.0 / ROW_CHUNK))
    e_r = lax.broadcasted_iota(jnp.int32, (N_EXPERTS, N_EXPERTS), 0)
    e_c = lax.broadcasted_iota(jnp.int32, (N_EXPERTS, N_EXPERTS), 1)
    before = (e_r < e_c).astype(F32).astype(BF16)
    seg_start = ROW_CHUNK * _dot(jnp.broadcast_to(n16, (8, N_EXPERTS)).astype(BF16), before)[0:1, :]
    row_of = seg_start + rank
    lane_k = lax.broadcasted_iota(jnp.int32, (TOK_TILE, TOP_K), 1)
    apos = jnp.zeros((TOK_TILE, TOP_K), F32)
    for k, hit in enumerate(hits):
        apos = jnp.where(lane_k == k, jnp.sum(jnp.where(hit, row_of, 0.0), axis=-1, keepdims=True), apos)
    apos_ref[...] = apos
    gate_ref[...] = jnp.concatenate([gate, jnp.zeros((TOK_TILE, GATE_LANES - N_EXPERTS), F32)], axis=-1).astype(BF16)
    n16_ref[...] = n16


def _merge(stream, mix_ctx, mix_lat, gates, mod, wp, wo, ln_g, ln_b, wr_hi, wr_lo, b_router, layer):
    tok = lambda n: pl.BlockSpec((TOK_TILE, n), lambda i: (i, 0))
    per_layer = lambda r, c: pl.BlockSpec((None, r, c), lambda i: (layer, 0, 0))
    return pl.pallas_call(
        functools.partial(_merge_kernel, n_stream=len(stream)),
        out_shape=[jax.ShapeDtypeStruct((N_TOK, D_MODEL), F32), jax.ShapeDtypeStruct((N_TOK, D_MODEL), BF16),
                   jax.ShapeDtypeStruct((N_TOK, TOP_K), F32), jax.ShapeDtypeStruct((N_TOK, GATE_LANES), BF16),
                   jax.ShapeDtypeStruct((N_TILES, 1, N_EXPERTS), F32)],
        grid=(N_TILES,),
        in_specs=_stream_specs(len(stream)) + _stream_specs(2) + [
            tok(GATE_COLS),
            pl.BlockSpec((None, None, 6, D_MODEL), lambda i: (layer, _cond_row_of_tile(i, TOK_TILE), 0, 0)),
            per_layer(D_MODEL, D_MODEL), per_layer(D_MODEL, D_MODEL), per_layer(1, D_MODEL), per_layer(1, D_MODEL),
            per_layer(D_MODEL, N_EXPERTS), per_layer(D_MODEL, N_EXPERTS), per_layer(1, N_EXPERTS),
        ],
        out_specs=[tok(D_MODEL), tok(D_MODEL), tok(TOP_K), tok(GATE_LANES),
                   pl.BlockSpec((None, 1, N_EXPERTS), lambda i: (i, 0, 0))],
        compiler_params=pltpu.CompilerParams(dimension_semantics=("arbitrary",), vmem_limit_bytes=VMEM_LIMIT),
        name="merge_router",
    )(*stream, mix_ctx, mix_lat, gates, mod, wp, wo, ln_g, ln_b, wr_hi, wr_lo, b_router)


def _routing_tables(n16):
    n16 = n16.astype(jnp.int32)
    hi = jnp.cumsum(n16, axis=1)
    lo = hi - n16
    n_chunks = hi[:, -1]
    eo = jnp.cumsum(n16, axis=0) - n16
    tc = jnp.sum(n16, axis=0)
    tiles = (tc + CHUNKS_PER_GMM_TILE - 1) // CHUNKS_PER_GMM_TILE
    tile_end = jnp.cumsum(tiles)
    tile_off = tile_end - tiles
    n_used = tile_end[-1]
    slot = CHUNKS_PER_GMM_TILE * tile_off[None, :] + eo
    copies = (_copy_runs(n16 // 2, lo, slot, 2, MAX_PAIRS)
              + _copy_runs(n16 % 2, lo + n16 - 1, slot + n16 - 1, 1, N_EXPERTS))
    pad_start = CHUNKS_PER_GMM_TILE * tile_off + tc
    pad_count = CHUNKS_PER_GMM_TILE * tiles - tc
    return (copies, n_chunks, tile_off.astype(jnp.int32), tiles.astype(jnp.int32),
            jnp.reshape(n_used, (1,)).astype(jnp.int32), pad_start.astype(jnp.int32), pad_count.astype(jnp.int32))


def _copy_runs(count, first_src, first_dst, stride, max_runs):
    hi = jnp.cumsum(count, axis=1)
    lo = hi - count
    r = jnp.arange(max_runs, dtype=jnp.int32)[None, :, None]
    owner = jnp.logical_and(r >= lo[:, None, :], r < hi[:, None, :])

    def table(first):
        picked = jnp.sum(jnp.where(owner, (first - stride * lo)[:, None, :], 0), axis=-1)
        return (stride * r[:, :, 0] + picked).reshape(-1).astype(jnp.int32)

    return table(first_src), table(first_dst), hi[:, -1].astype(jnp.int32)


def _one_hot_rows(apos_ref, n_rows):
    r = lax.broadcasted_iota(jnp.int32, (TOK_TILE, n_rows), 1).astype(F32)
    pt = jnp.zeros((TOK_TILE, n_rows), F32)
    for k in range(TOP_K):
        pt = jnp.where(r == apos_ref[:, k:k + 1], 1.0, pt)
    return pt.astype(BF16)


def _chunk_copy(src_ref, src_chunk, dst_ref, dst_chunk, sem):
    rows = lambda c: pl.ds(pl.multiple_of(c * ROW_CHUNK, ROW_CHUNK), ROW_CHUNK)
    return pltpu.make_async_copy(src_ref.at[rows(src_chunk), :], dst_ref.at[rows(dst_chunk), :], sem)


def _run_rows(chunk, n_chunks):
    return pl.ds(pl.multiple_of(chunk * ROW_CHUNK, ROW_CHUNK), n_chunks * ROW_CHUNK)


def _for_each_chunk(n, fn):
    def trip(i, carry):
        for u in range(CHUNK_LOOP_UNROLL):
            c = i * CHUNK_LOOP_UNROLL + u
            if u == 0:
                fn(c)
            else:
                pl.when(c < n)(functools.partial(fn, c))
        return carry
    lax.fori_loop(0, (n + (CHUNK_LOOP_UNROLL - 1)) // CHUNK_LOOP_UNROLL, trip, 0)


def _dispatch_kernel(psrc_ref, pdst_ref, pn_ref, ssrc_ref, sdst_ref, sn_ref, nch_ref, pstart_ref, pcount_ref, nu_ref,
                     h_ref, gate_ref, apos_ref, xs_hbm, pt_ref, stage_ref, zero_ref, sem):
    j = pl.program_id(0)
    buf = j % 2
    runs = ((psrc_ref, pdst_ref, pn_ref), (ssrc_ref, sdst_ref, sn_ref))
    pad_sem, tile_sem = 2 * len(runs), 2 * len(runs) + 1

    def run_copy(k, b, src_chunk, slot):
        return pltpu.make_async_copy(stage_ref.at[b, _run_rows(src_chunk, RUN_CHUNKS[k]), :],
                                     xs_hbm.at[_run_rows(slot, RUN_CHUNKS[k]), :], sem.at[2 * k + b])

    def wait_tile(b, jj):
        for k, (_, _, n_ref) in enumerate(runs):
            _for_each_chunk(n_ref[jj], lambda r, k=k: run_copy(k, b, 0, 0).wait())

    @pl.when(j < N_TILES)
    def _():
        @pl.when(j >= 2)
        def _():
            wait_tile(buf, j - 2)

        def sort_rows(n_rows):
            pt = _one_hot_rows(apos_ref, n_rows)
            pt_ref[:, 0:n_rows] = pt
            if n_rows < TILE_ROWS:
                pt_ref[:, n_rows:TILE_ROWS] = jnp.zeros((TOK_TILE, TILE_ROWS - n_rows), BF16)
            h = h_ref[...]
            gate = gate_ref[...]
            for r0 in range(0, n_rows, DISPATCH_ROWS):
                group = pt[:, r0:r0 + DISPATCH_ROWS]
                stage_ref[buf, r0:r0 + DISPATCH_ROWS, 0:D_MODEL] = _dot_tn(group, h).astype(BF16)
                stage_ref[buf, r0:r0 + DISPATCH_ROWS, D_MODEL:ROW_WIDTH] = _dot_tn(group, gate).astype(BF16)

        short = nch_ref[j] * ROW_CHUNK <= SHORT_ROWS
        pl.when(short)(functools.partial(sort_rows, SHORT_ROWS))
        pl.when(jnp.logical_not(short))(functools.partial(sort_rows, TILE_ROWS))

        for k, (src_ref, slot_ref, n_ref) in enumerate(runs):
            base = j * RUN_TABLE_STRIDE[k]
            _for_each_chunk(n_ref[j], lambda r, k=k, base=base, src_ref=src_ref, slot_ref=slot_ref: run_copy(
                k, buf, src_ref[base + r], slot_ref[base + r]).start())

    @pl.when(j == N_TILES)
    def _():
        zero_ref[...] = jnp.zeros_like(zero_ref)

        def tile_copy(t):
            rows = pl.ds(pl.multiple_of(t * GMM_TILE, GMM_TILE), GMM_TILE)
            return pltpu.make_async_copy(zero_ref, xs_hbm.at[rows, :], sem.at[tile_sem])

        def per_expert(e, total):
            def fill(i, carry):
                _chunk_copy(zero_ref, 0, xs_hbm, pstart_ref[e] + i, sem.at[pad_sem]).start()
                return carry
            lax.fori_loop(0, pcount_ref[e], fill, 0)
            return total + pcount_ref[e]
        n_pad = lax.fori_loop(0, N_EXPERTS, per_expert, 0)

        def fill_tile(t, carry):
            tile_copy(t).start()
            return carry
        lax.fori_loop(nu_ref[0], MAX_GMM_TILES, fill_tile, 0)
        wait_tile(0, N_TILES - 2)
        wait_tile(1, N_TILES - 1)

        def drain(i, carry):
            _chunk_copy(zero_ref, 0, xs_hbm, 0, sem.at[pad_sem]).wait()
            return carry
        lax.fori_loop(0, n_pad, drain, 0)

        def drain_tile(t, carry):
            tile_copy(0).wait()
            return carry
        lax.fori_loop(nu_ref[0], MAX_GMM_TILES, drain_tile, 0)


def _dispatch(h2, gate, apos, copies, n_chunks, pad_start, pad_count, n_used):
    last = N_TILES - 1
    tok = lambda n: pl.BlockSpec((TOK_TILE, n), lambda j, *_: (jnp.minimum(j, last), 0))
    grid_spec = pltpu.PrefetchScalarGridSpec(
        num_scalar_prefetch=len(copies) + 4,
        grid=(N_TILES + 1,),
        in_specs=[tok(D_MODEL), tok(GATE_LANES), tok(TOP_K)],
        out_specs=[pl.BlockSpec(memory_space=pl.ANY), tok(TILE_ROWS)],
        scratch_shapes=[pltpu.VMEM((2, TILE_ROWS, ROW_WIDTH), BF16), pltpu.VMEM((GMM_TILE, ROW_WIDTH), BF16),
                        pltpu.SemaphoreType.DMA((2 * len(RUN_CHUNKS) + 2,))],
    )
    return pl.pallas_call(
        _dispatch_kernel,
        out_shape=[jax.ShapeDtypeStruct((SORTED_ROWS, ROW_WIDTH), BF16),
                   jax.ShapeDtypeStruct((N_TOK, TILE_ROWS), BF16)],
        grid_spec=grid_spec,
        compiler_params=pltpu.CompilerParams(dimension_semantics=("arbitrary",), vmem_limit_bytes=VMEM_LIMIT),
        name="moe_dispatch",
    )(*copies, n_chunks, pad_start, pad_count, n_used, h2, gate, apos)


def _gmm_kernel(off_ref, cnt_ref, nu_ref, x_hbm, wg_ref, wu_ref, wd_ref, y_hbm, xbuf, ybuf, in_sem, out_sem):
    e = pl.program_id(0)
    n_used = nu_ref[0]

    def rows(g):
        return pl.ds(pl.multiple_of(g * GMM_TILE, GMM_TILE), GMM_TILE)

    def fetch(g):
        slot = g % GMM_IN_SLOTS
        return pltpu.make_async_copy(x_hbm.at[rows(g), :], xbuf.at[slot], in_sem.at[slot])

    def flush(g):
        slot = g % GMM_OUT_SLOTS
        return pltpu.make_async_copy(ybuf.at[slot], y_hbm.at[rows(g), 0:D_MODEL], out_sem.at[slot])

    @pl.when(e == 0)
    def _():
        for g in range(GMM_AHEAD):
            @pl.when(g < n_used)
            def _(g=g):
                fetch(g).start()


    def process(g0, width):
        for w in range(width):
            @pl.when(g0 + w + GMM_AHEAD < n_used)
            def _(w=w):
                fetch(g0 + w + GMM_AHEAD).start()
        for w in range(width):
            fetch(g0 + w).wait()
        xg = jnp.concatenate([xbuf[(g0 + w) % GMM_IN_SLOTS] for w in range(width)], axis=0)
        x = xg[:, 0:D_MODEL]
        gates = xg[:, D_MODEL:ROW_WIDTH].astype(F32)
        lane = lax.broadcasted_iota(jnp.int32, gates.shape, 1)
        weight = jnp.sum(jnp.where(lane == e, gates, 0.0), axis=-1, keepdims=True)
        hid = _silu(_dot(x, wg_ref[...].astype(BF16))) * _dot(x, wu_ref[...].astype(BF16))
        y = (_dot(hid.astype(BF16), wd_ref[...].astype(BF16)) * weight).astype(BF16)
        for w in range(width):
            g = g0 + w

            @pl.when(g >= GMM_OUT_SLOTS)
            def _(g=g):
                flush(g - GMM_OUT_SLOTS).wait()

            ybuf[g % GMM_OUT_SLOTS] = y[w * GMM_TILE:(w + 1) * GMM_TILE]
            flush(g).start()

    first = off_ref[e]
    count = cnt_ref[e]

    done = 0
    width = GMM_WIDTH
    while width >= 1:
        trips = (count - done) // width

        def block(p, carry, done=done, width=width):
            process(first + done + width * p, width)
            return carry

        lax.fori_loop(0, trips, block, 0)
        done = done + trips * width
        width //= 2

    @pl.when(e == N_EXPERTS - 1)
    def _():
        for k in range(GMM_OUT_SLOTS):
            @pl.when(n_used - 1 - k >= 0)
            def _(k=k):
                flush(n_used - 1 - k).wait()


def _gmm(xs, weg, weu, wed, tile_off, tiles, n_used, layer):
    w_in_spec = pl.BlockSpec((None, None, D_MODEL, D_EXPERT), lambda e, *_: (layer, e, 0, 0))
    grid_spec = pltpu.PrefetchScalarGridSpec(
        num_scalar_prefetch=3,
        grid=(N_EXPERTS,),
        in_specs=[
            pl.BlockSpec(memory_space=pl.ANY),
            w_in_spec, w_in_spec,
            pl.BlockSpec((None, None, D_EXPERT, D_MODEL), lambda e, *_: (layer, e, 0, 0)),
        ],
        out_specs=pl.BlockSpec(memory_space=pl.ANY),
        scratch_shapes=[pltpu.VMEM((GMM_IN_SLOTS, GMM_TILE, ROW_WIDTH), BF16),
                        pltpu.VMEM((GMM_OUT_SLOTS, GMM_TILE, D_MODEL), BF16),
                        pltpu.SemaphoreType.DMA((GMM_IN_SLOTS,)), pltpu.SemaphoreType.DMA((GMM_OUT_SLOTS,))],
    )
    return pl.pallas_call(
        _gmm_kernel,
        out_shape=jax.ShapeDtypeStruct((SORTED_ROWS, ROW_WIDTH), BF16),
        grid_spec=grid_spec,
        input_output_aliases={3: 0},
        compiler_params=pltpu.CompilerParams(dimension_semantics=("arbitrary",), vmem_limit_bytes=VMEM_LIMIT),
        name="moe_gmm",
    )(tile_off, tiles, n_used, xs, weg, weu, wed)


def _combine_kernel(psrc_ref, pdst_ref, pn_ref, ssrc_ref, sdst_ref, sn_ref, nch_ref, ys_hbm, pt_ref, h_ref, x1_ref,
                    mod_ref, wsg_ref, wsu_ref, wsd_ref, g2_ref, b2_ref, oc_ref, ol_ref, stage_ref, routed_ref, sem):
    j = pl.program_id(0)
    buf = j % 2
    runs = ((psrc_ref, pdst_ref, pn_ref), (ssrc_ref, sdst_ref, sn_ref))

    def run_fetch(k, b, slot, chunk):
        return pltpu.make_async_copy(ys_hbm.at[_run_rows(slot, RUN_CHUNKS[k]), 0:D_MODEL],
                                     stage_ref.at[b, _run_rows(chunk, RUN_CHUNKS[k]), :], sem.at[2 * k + b])

    def fetch(jj, b):
        for k, (chunk_ref, slot_ref, n_ref) in enumerate(runs):
            base = jj * RUN_TABLE_STRIDE[k]
            _for_each_chunk(n_ref[jj], lambda r, k=k, base=base, chunk_ref=chunk_ref, slot_ref=slot_ref: run_fetch(
                k, b, slot_ref[base + r], chunk_ref[base + r]).start())

    @pl.when(j == 0)
    def _():
        stage_ref[...] = jnp.zeros_like(stage_ref)
        fetch(0, 0)

    @pl.when(j + 1 < N_TILES)
    def _():
        fetch(j + 1, 1 - buf)

    h = h_ref[...]
    hid = _silu(_dot(h, wsg_ref[...])) * _dot(h, wsu_ref[...])
    shared = _dot(hid.astype(BF16), wsd_ref[...])

    for k, (_, _, n_ref) in enumerate(runs):
        _for_each_chunk(n_ref[j], lambda r, k=k: run_fetch(k, buf, 0, 0).wait())

    def gather_rows(n_rows):
        routed_ref[...] = _dot(pt_ref[:, 0:n_rows], stage_ref[buf, 0:n_rows, :])

    short = nch_ref[j] * ROW_CHUNK <= SHORT_ROWS
    pl.when(short)(functools.partial(gather_rows, SHORT_ROWS))
    pl.when(jnp.logical_not(short))(functools.partial(gather_rows, TILE_ROWS))
    r = ALPHA * x1_ref[...] + mod_ref[5:6, :] * (routed_ref[...] + shared)
    out = _ln(r) * g2_ref[...] + b2_ref[...]

    @pl.when(j < CTX_TILES)
    def _():
        oc_ref[...] = out

    @pl.when(j >= CTX_TILES)
    def _():
        ol_ref[...] = out


def _combine(ys, pt, h2, x1, mod, wsg, wsu, wsd, ln_g, ln_b, copies, n_chunks, layer):
    tok = lambda n: pl.BlockSpec((TOK_TILE, n), lambda j, *_: (j, 0))
    per_layer = lambda r, c: pl.BlockSpec((None, r, c), lambda j, *_: (layer, 0, 0))
    grid_spec = pltpu.PrefetchScalarGridSpec(
        num_scalar_prefetch=len(copies) + 1,
        grid=(N_TILES,),
        in_specs=[
            pl.BlockSpec(memory_space=pl.ANY),
            tok(TILE_ROWS), tok(D_MODEL), tok(D_MODEL),
            pl.BlockSpec((None, None, 6, D_MODEL), lambda j, *_: (layer, _cond_row_of_tile(j, TOK_TILE), 0, 0)),
            per_layer(D_MODEL, D_SHARED), per_layer(D_MODEL, D_SHARED), per_layer(D_SHARED, D_MODEL),
            per_layer(1, D_MODEL), per_layer(1, D_MODEL),
        ],
        out_specs=_stream_specs(2),
        scratch_shapes=[pltpu.VMEM((2, TILE_ROWS, D_MODEL), BF16), pltpu.VMEM((TOK_TILE, D_MODEL), F32),
                        pltpu.SemaphoreType.DMA((2 * len(RUN_CHUNKS),))],
    )
    return pl.pallas_call(
        _combine_kernel,
        out_shape=[jax.ShapeDtypeStruct((N_CTX, D_MODEL), F32), jax.ShapeDtypeStruct((N_LAT, D_MODEL), F32)],
        grid_spec=grid_spec,
        compiler_params=pltpu.CompilerParams(dimension_semantics=("arbitrary",), vmem_limit_bytes=VMEM_LIMIT),
        name="moe_combine",
    )(*copies, n_chunks, ys, pt, h2, x1, mod, wsg, wsu, wsd, ln_g, ln_b)


def kernel(x_prompt, x_sample, c, state_hgrn, c_ctx, w_ada, b_ada, w_in, b_gate, hgrn_lb, hgrn_norm, conv_w,
           w_proj_hgrn, w_proj_fourier, w_proj_conv, w_out, ln1_g, ln1_b, ln2_g, ln2_b, w_router, b_router,
           w_exp_gate, w_exp_up, w_exp_down, w_sh_gate, w_sh_up, w_sh_down):
    lb_sm = jax.nn.softmax(hgrn_lb.astype(F32), axis=0)
    lb_all = jnp.cumsum(lb_sm, axis=0) - lb_sm[:1]

    pos = jnp.asarray(_pos_emb_table())
    dft_ctx = jnp.asarray(_dft_time(SEQ)).astype(BF16)
    dft_lat = jnp.asarray(_dft_time(DEC_SEQ)).astype(BF16)
    bdc, bds = (jnp.asarray(m).astype(BF16) for m in _dft_channel())

    cond = jnp.concatenate([c_ctx[None, :], c, jnp.zeros((N_COND - 1 - DEC_BATCH, D_MODEL), F32)], axis=0)
    mod = _ada_mod(cond, w_ada, b_ada).reshape(DEPTH, N_COND, 6, D_MODEL)

    wp_b = jnp.concatenate([w_proj_hgrn, w_proj_fourier, w_proj_conv], axis=1).astype(BF16)
    wo_b = w_out.astype(BF16)
    wr_hi = w_router.astype(BF16)
    wr_lo = (w_router - wr_hi.astype(F32)).astype(BF16)
    wsg_b, wsu_b, wsd_b = w_sh_gate.astype(BF16), w_sh_up.astype(BF16), w_sh_down.astype(BF16)
    rows = lambda a: a[:, None, :]
    b_gate3, ln1_g3, ln1_b3, ln2_g3, ln2_b3, b_router3 = map(rows, (b_gate, ln1_g, ln1_b, ln2_g, ln2_b, b_router))

    stream = (x_prompt.reshape(N_CTX, D_MODEL), x_sample.reshape(N_LAT, D_MODEL), pos)
    states = None
    for l in range(DEPTH):
        uh, uf, uc, gates = _in_proj(stream, mod, w_in, b_gate3, l)
        small = (lb_all[l], hgrn_norm[l][None, :], conv_w[l])
        mix_ctx, states = _mixer(uh, uf, uc, *small, dft_ctx, bdc, bds, T=SEQ, n_seq=BATCH, blk0=0,
                                 prev_states=states)
        (mix_lat,) = _mixer(uh, uf, uc, *small, dft_lat, bdc, bds, T=DEC_SEQ, n_seq=DEC_BATCH,
                            blk0=N_CTX // DEC_SEQ, state0=state_hgrn, layer=l)
        x1, h2, apos, gate, n16 = _merge(stream, mix_ctx, mix_lat, gates, mod, wp_b, wo_b, ln1_g3, ln1_b3,
                                         wr_hi, wr_lo, b_router3, l)
        copies, n_chunks, tile_off, tiles, n_used, pad_start, pad_count = _routing_tables(
            n16.reshape(N_TILES, N_EXPERTS))
        xs, pt = _dispatch(h2, gate, apos, copies, n_chunks, pad_start, pad_count, n_used)
        ys = _gmm(xs, w_exp_gate, w_exp_up, w_exp_down, tile_off, tiles, n_used, l)
        stream = tuple(_combine(ys, pt, h2, x1, mod, wsg_b, wsu_b, wsd_b, ln2_g3, ln2_b3, copies, n_chunks, l))
    y_prompt = stream[0].reshape(BATCH, SEQ, D_MODEL)
    y_sample = stream[1].reshape(DEC_BATCH, DEC_SEQ, D_MODEL)
    return (y_prompt, y_sample, states)
```

```python
import functools
import math

import numpy as np
import jax
import jax.numpy as jnp
from jax import lax
from jax.experimental import pallas as pl
from jax.experimental.pallas import tpu as pltpu

F32 = jnp.float32
BF16 = jnp.bfloat16

D_MODEL = 1024
BATCH = 16
SEQ = 256
DEPTH = 2
DEC_BATCH = 4
DEC_SEQ = 1024
GRID_W = 64
HEADS = 4
HEAD_DIM = 128
HGRN_WIDTH = HEADS * HEAD_DIM
FOURIER_GROUPS = 4
FOURIER_GROUP_DIM = 64
FOURIER_WIDTH = FOURIER_GROUPS * FOURIER_GROUP_DIM
CONV_WIDTH = 256
CONV_K = 3
N_BRANCHES = 3
HGRN_COLS = 5 * HGRN_WIDTH
CONV_COLS = 3 * CONV_WIDTH
GATE_COLS = N_BRANCHES * D_MODEL
IN_COLS = HGRN_COLS + FOURIER_WIDTH + CONV_COLS + GATE_COLS
N_EXPERTS = 64
TOP_K = 8
D_EXPERT = 256
D_SHARED = 256
ROUTED_SCALE = 2.5
ALPHA = (2 * DEPTH) ** 0.25
LN_EPS = 1e-6
RMS_EPS = 1e-6
F_MIN = 1e-30

N_CTX = BATCH * SEQ
N_LAT = DEC_BATCH * DEC_SEQ
N_TOK = N_CTX + N_LAT
N_COND = 8

TOK_TILE = 256
CHUNK = 64
CHUNK_UNROLL = 2
WINDOW = 120.0
STATIC_WINDOWS = 2
ADA_TILE = 1536
VMEM_LIMIT = 56 * 1024 * 1024

N_TILES = N_TOK // TOK_TILE
ROW_CHUNK = 16
CHUNK_LOOP_UNROLL = 4
TILE_CHUNKS = (TOK_TILE * TOP_K + N_EXPERTS * (ROW_CHUNK - 1)) // ROW_CHUNK
RUN_CHUNKS = (2, 1)
MAX_PAIRS = TILE_CHUNKS // 2
RUN_TABLE_STRIDE = (MAX_PAIRS, N_EXPERTS)
DISPATCH_ROWS = 192
EXTENT_STEP = 2 * DISPATCH_ROWS
TILE_ROWS = -(-TILE_CHUNKS * ROW_CHUNK // EXTENT_STEP) * EXTENT_STEP
SHORT_ROWS = TILE_ROWS - EXTENT_STEP
GMM_TILE = 256
GMM_WIDTH = 4
GMM_AHEAD = 6
GMM_IN_SLOTS = GMM_WIDTH + GMM_AHEAD
GMM_OUT_SLOTS = 2 * GMM_WIDTH
CHUNKS_PER_GMM_TILE = GMM_TILE // ROW_CHUNK
MAX_GMM_TILES = (N_TILES * TILE_CHUNKS + N_EXPERTS * (CHUNKS_PER_GMM_TILE - 1)) // CHUNKS_PER_GMM_TILE
SORTED_ROWS = MAX_GMM_TILES * GMM_TILE
GATE_LANES = 128
ROW_WIDTH = D_MODEL + GATE_LANES


def _cond_row_of_tile(i, tile):
    ctx_tiles = N_CTX // tile
    per_seq = DEC_SEQ // tile
    return jnp.where(i < ctx_tiles, 0, 1 + (i - ctx_tiles) // per_seq)


def _silu(x):
    return x * jax.nn.sigmoid(x)


def _ln(x):
    mu = jnp.mean(x, axis=-1, keepdims=True)
    xc = x - mu
    var = jnp.mean(xc * xc, axis=-1, keepdims=True)
    return xc * lax.rsqrt(var + LN_EPS)


def _dot(a, b):
    return jnp.dot(a, b, preferred_element_type=F32)


def _dot_nt(a, b):
    return lax.dot_general(a, b, (((1,), (1,)), ((), ())), preferred_element_type=F32)


def _dot_tn(a, b):
    return lax.dot_general(a, b, (((0,), (0,)), ((), ())), preferred_element_type=F32)


@functools.lru_cache(maxsize=None)
def _pos_emb_table():
    rows = DEC_SEQ // GRID_W
    t = np.arange(rows * GRID_W)
    r = (t // GRID_W).astype(np.float32)
    col = (t % GRID_W).astype(np.float32)
    quarter = D_MODEL // 4
    omega = (1.0 / (np.float32(10000.0) ** (np.arange(quarter, dtype=np.float32) / np.float32(quarter)))).astype(np.float32)
    ar = (r[:, None] * omega).astype(np.float32)
    ac = (col[:, None] * omega).astype(np.float32)
    return np.concatenate([np.sin(ar), np.cos(ar), np.sin(ac), np.cos(ac)], axis=-1).astype(np.float32)


@functools.lru_cache(maxsize=None)
def _dft_time(T):
    k = np.arange(T)
    ph = 2.0 * np.pi * ((k[:, None] * k[None, :]) % T) / T
    return np.concatenate([np.cos(ph), -np.sin(ph)], axis=1).astype(np.float32)


@functools.lru_cache(maxsize=None)
def _dft_channel():
    n = FOURIER_GROUP_DIM
    k = np.arange(n)
    ph = 2.0 * np.pi * ((k[:, None] * k[None, :]) % n) / n
    eye = np.eye(FOURIER_GROUPS)
    return np.kron(eye, np.cos(ph)).astype(np.float32), np.kron(eye, np.sin(ph)).astype(np.float32)


CTX_TILES = N_CTX // TOK_TILE
LAT_TILES_PER_SEQ = DEC_SEQ // TOK_TILE


def _stream_specs(n):
    specs = [pl.BlockSpec((TOK_TILE, D_MODEL), lambda i, *_: (jnp.minimum(i, CTX_TILES - 1), 0)),
             pl.BlockSpec((TOK_TILE, D_MODEL), lambda i, *_: (jnp.maximum(i - CTX_TILES, 0), 0))]
    if n == 3:
        specs.append(pl.BlockSpec((TOK_TILE, D_MODEL),
                                  lambda i, *_: (jnp.maximum(i - CTX_TILES, 0) % LAT_TILES_PER_SEQ, 0)))
    return specs


def _stream_tile(refs):
    latent = refs[1][...]
    if len(refs) == 3:
        latent = latent + refs[2][...]
    return jnp.where(pl.program_id(0) < CTX_TILES, refs[0][...], latent)


def _ada_kernel(c_ref, w_ref, b_ref, o_ref):
    s = _silu(c_ref[...]).astype(BF16)
    o_ref[...] = _dot(s, w_ref[...].astype(BF16)) + b_ref[...]


def _ada_mod(cond, w_ada, b_ada):
    n_col = 6 * D_MODEL
    return pl.pallas_call(
        _ada_kernel,
        out_shape=jax.ShapeDtypeStruct((DEPTH, N_COND, n_col), F32),
        grid=(DEPTH, n_col // ADA_TILE),
        in_specs=[
            pl.BlockSpec((N_COND, D_MODEL), lambda l, j: (0, 0)),
            pl.BlockSpec((None, D_MODEL, ADA_TILE), lambda l, j: (l, 0, j)),
            pl.BlockSpec((None, 1, ADA_TILE), lambda l, j: (l, 0, j)),
        ],
        out_specs=pl.BlockSpec((None, N_COND, ADA_TILE), lambda l, j: (l, 0, j)),
        compiler_params=pltpu.CompilerParams(dimension_semantics=("arbitrary", "arbitrary"),
                                             vmem_limit_bytes=VMEM_LIMIT),
        name="ada_mod",
    )(cond, w_ada, b_ada.reshape(DEPTH, 1, n_col))


def _in_proj_kernel(*refs, n_stream):
    mod_ref, w_ref, bg_ref, uh_ref, uf_ref, uc_ref, gates_ref = refs[n_stream:]
    sh1 = mod_ref[0:1, :]
    sc1 = mod_ref[1:2, :]
    h = (_ln(_stream_tile(refs[:n_stream])) * (1.0 + sc1) + sh1).astype(BF16)
    c0 = 0
    for ref, n in ((uh_ref, HGRN_COLS), (uf_ref, FOURIER_WIDTH), (uc_ref, CONV_COLS)):
        ref[...] = _dot(h, w_ref[:, c0:c0 + n].astype(BF16))
        c0 += n
    gates_ref[...] = jax.nn.sigmoid(_dot(h, w_ref[:, c0:c0 + GATE_COLS].astype(BF16)) + bg_ref[...]).astype(BF16)


def _in_proj(stream, mod, w_in, b_gate, layer):
    n_tiles = N_TOK // TOK_TILE
    outs = ((HGRN_COLS, F32), (FOURIER_WIDTH, F32), (CONV_COLS, F32), (GATE_COLS, BF16))
    return pl.pallas_call(
        functools.partial(_in_proj_kernel, n_stream=len(stream)),
        out_shape=[jax.ShapeDtypeStruct((N_TOK, n), dt) for n, dt in outs],
        grid=(n_tiles,),
        in_specs=_stream_specs(len(stream)) + [
            pl.BlockSpec((None, None, 6, D_MODEL), lambda i: (layer, _cond_row_of_tile(i, TOK_TILE), 0, 0)),
            pl.BlockSpec((None, D_MODEL, IN_COLS), lambda i: (layer, 0, 0), pipeline_mode=pl.Buffered(1)),
            pl.BlockSpec((None, 1, GATE_COLS), lambda i: (layer, 0, 0)),
        ],
        out_specs=[pl.BlockSpec((TOK_TILE, n), lambda i: (i, 0)) for n, _ in outs],
        compiler_params=pltpu.CompilerParams(dimension_semantics=("arbitrary",), vmem_limit_bytes=VMEM_LIMIT),
        name="in_proj",
    )(*stream, mod, w_in, b_gate)


def _mixer_kernel(*refs, T, has_state, n_prev):
    uh_ref, uf_ref, uc_ref, lb_ref, ng_ref, cw_ref, dft_ref, bdc_ref, bds_ref = refs[:9]
    s0_ref = sfin_ref = prev_ref = None
    if has_state:
        s0_ref, mix_ref, st_ref, o_ref = refs[9:]
    elif n_prev:
        prev_ref, mix_ref, sfin_ref, st_ref, o_ref = refs[9:]
    else:
        mix_ref, sfin_ref, st_ref, o_ref = refs[9:]
    n_chunks = T // CHUNK
    W = HGRN_WIDTH

    for d in range(2):
        for h in range(HEADS):
            if has_state:
                st_ref[d, h] = s0_ref[0, d, h].T
            else:
                st_ref[d, h] = jnp.zeros((HEAD_DIM, HEAD_DIM), F32)
    o_ref[...] = jnp.zeros_like(o_ref)

    row = lax.broadcasted_iota(jnp.int32, (CHUNK, CHUNK), 0)
    col = lax.broadcasted_iota(jnp.int32, (CHUNK, CHUNK), 1)
    keep = (col <= row, col >= row)
    tri = tuple(k.astype(F32).astype(BF16) for k in keep)

    def chunk_step(i, carry):
        deep = []
        for sub, d in ((s, d) for s in range(CHUNK_UNROLL) for d in range(2)):
            c = i * CHUNK_UNROLL + sub
            c = c if d == 0 else n_chunks - 1 - c
            rows = pl.ds(pl.multiple_of(c * CHUNK, CHUNK), CHUNK)
            q = _silu(uh_ref[rows, 0:W])
            v = uh_ref[rows, W:2 * W].astype(BF16)
            z = uh_ref[rows, (2 + d) * W:(3 + d) * W]
            lb = lb_ref[d:d + 1, :]
            e = jnp.exp(-jnp.abs(z))
            r = 1.0 / (1.0 + e)
            er = e * r
            pos = z >= 0.0
            sig_p = jnp.where(pos, r, er)
            sig_n = jnp.where(pos, er, r)
            f = lb + (1.0 - lb) * sig_p
            lf = jnp.log(jnp.maximum(f, F_MIN))
            k = (1.0 - lb) * sig_n
            lf_hi = lf.astype(BF16)
            lf_lo = (lf - lf_hi.astype(F32)).astype(BF16)
            b = _dot(tri[d], lf_hi) + _dot(tri[d], lf_lo)
            g = b[CHUNK - 1:CHUNK, :] if d == 0 else b[0:1, :]

            def window(p_level, q=q, k=k, b=b):
                dist = -p_level - b
                inside = jnp.logical_and(dist >= 0.0, dist < WINDOW)
                k_w = jnp.where(inside, k * jnp.exp(jnp.minimum(dist, WINDOW) - 0.5 * WINDOW), 0.0)
                q_w = q * jnp.exp(jnp.minimum(-dist, 0.0) + 0.5 * WINDOW)
                return q_w.astype(BF16), k_w.astype(BF16)

            wins = [window(WINDOW * p) for p in range(STATIC_WINDOWS)]
            q_abs = (q * jnp.exp(b)).astype(BF16)
            k_end = (k * jnp.exp(g - b)).astype(BF16)
            decay = jnp.exp(g)
            outs = []
            for h in range(HEADS):
                ls = slice(h * HEAD_DIM, (h + 1) * HEAD_DIM)
                q_cat = jnp.concatenate([w[0][:, ls] for w in wins], axis=-1)
                k_cat = jnp.concatenate([w[1][:, ls] for w in wins], axis=-1)
                a = jnp.where(keep[d], _dot_nt(q_cat, k_cat), 0.0).astype(BF16)
                s_t = st_ref[d, h]
                o_h = _dot(a, v[:, ls]) + _dot_nt(q_abs[:, ls], s_t.astype(BF16))
                st_ref[d, h] = s_t * decay[:, ls] + _dot_tn(v[:, ls], k_end[:, ls])
                outs.append(o_h)
            o_ref[rows, :] += jnp.concatenate(outs, axis=-1)
            deep.append((d, window, v, rows, jnp.min(b)))

        deepest = functools.reduce(jnp.minimum, [entry[4] for entry in deep])
        n_windows = jnp.floor(-deepest * (1.0 / WINDOW)).astype(jnp.int32) + 1

        def extra_window(p, carry2):
            for d, window, v, rows, _ in deep:
                q_w, k_w = window(WINDOW * p.astype(F32))
                extra = []
                for h in range(HEADS):
                    ls = slice(h * HEAD_DIM, (h + 1) * HEAD_DIM)
                    a = jnp.where(keep[d], _dot_nt(q_w[:, ls], k_w[:, ls]), 0.0).astype(BF16)
                    extra.append(_dot(a, v[:, ls]))
                o_ref[rows, :] += jnp.concatenate(extra, axis=-1)
            return carry2

        lax.fori_loop(STATIC_WINDOWS, n_windows, extra_window, 0)
        return carry

    lax.fori_loop(0, n_chunks // CHUNK_UNROLL, chunk_step, 0)

    if sfin_ref is not None:
        if n_prev:
            sfin_ref[0, 0:n_prev] = prev_ref[0]
        for d in range(2):
            for h in range(HEADS):
                sfin_ref[0, n_prev, d, h] = st_ref[d, h].T

    o = o_ref[...]
    parts = []
    for h in range(HEADS):
        oh = o[:, h * HEAD_DIM:(h + 1) * HEAD_DIM]
        parts.append(oh * lax.rsqrt(jnp.mean(oh * oh, axis=-1, keepdims=True) + RMS_EPS))
    a_out = jnp.concatenate(parts, axis=-1) * ng_ref[...] * _silu(uh_ref[:, 4 * W:5 * W])
    mix_ref[:, 0:W] = a_out.astype(BF16)

    zf = uf_ref[...].astype(BF16)
    zc = _dot(zf, bdc_ref[...]).astype(BF16)
    zs = _dot(zf, bds_ref[...]).astype(BF16)
    four = _dot(dft_ref[...], jnp.concatenate([zc, zs], axis=0)) * (1.0 / math.sqrt(T * FOURIER_GROUP_DIM))
    mix_ref[:, W:W + FOURIER_WIDTH] = four.astype(BF16)

    cb = uc_ref[:, 0:CONV_WIDTH]
    zz = uc_ref[:, CONV_WIDTH:2 * CONV_WIDTH] * uc_ref[:, 2 * CONV_WIDTH:3 * CONV_WIDTH]
    t_idx = lax.broadcasted_iota(jnp.int32, (T, CONV_WIDTH), 0)
    z_prev = jnp.where(t_idx == 0, 0.0, pltpu.roll(zz, 1, axis=0))
    z_next = jnp.where(t_idx == T - 1, 0.0, pltpu.roll(zz, T - 1, axis=0))
    y = cw_ref[0:1, :] * z_prev + cw_ref[1:2, :] * zz + cw_ref[2:3, :] * z_next
    mix_ref[:, W + FOURIER_WIDTH:] = (cb * y).astype(BF16)


def _mixer(uh, uf, uc, lb, norm_g, conv_w, dft, bdc, bds, T, n_seq, blk0, state0=None, layer=0, prev_states=None):
    has_state = state0 is not None
    n_prev = 0 if prev_states is None else prev_states.shape[1]
    const2 = lambda s: (0, 0)
    in_specs = [
        pl.BlockSpec((T, HGRN_COLS), lambda s: (blk0 + s, 0)),
        pl.BlockSpec((T, FOURIER_WIDTH), lambda s: (blk0 + s, 0)),
        pl.BlockSpec((T, CONV_COLS), lambda s: (blk0 + s, 0)),
        pl.BlockSpec((None, 2, HGRN_WIDTH), lambda s: (layer, 0, 0)),
        pl.BlockSpec((None, 1, HGRN_WIDTH), lambda s: (layer, 0, 0)),
        pl.BlockSpec((None, CONV_K, CONV_WIDTH), lambda s: (layer, 0, 0)),
        pl.BlockSpec((T, 2 * T), const2),
        pl.BlockSpec((FOURIER_WIDTH, FOURIER_WIDTH), const2),
        pl.BlockSpec((FOURIER_WIDTH, FOURIER_WIDTH), const2),
    ]
    args = [uh, uf, uc, lb, norm_g, conv_w, dft, bdc, bds]
    mix_shape = jax.ShapeDtypeStruct((n_seq * T, D_MODEL), BF16)
    mix_spec = pl.BlockSpec((T, D_MODEL), lambda s: (s, 0))
    st_block = (1, 2, HEADS, HEAD_DIM, HEAD_DIM)
    if has_state:
        in_specs.append(pl.BlockSpec((1, None) + st_block[1:], lambda s: (s, layer, 0, 0, 0, 0)))
        args.append(state0)
        out_shape = [mix_shape]
        out_specs = [mix_spec]
    else:
        stack_block = lambda n: pl.BlockSpec((1, n) + st_block[1:], lambda s: (s, 0, 0, 0, 0, 0))
        if n_prev:
            in_specs.append(stack_block(n_prev))
            args.append(prev_states)
        out_shape = [mix_shape, jax.ShapeDtypeStruct((n_seq, n_prev + 1) + st_block[1:], F32)]
        out_specs = [mix_spec, stack_block(n_prev + 1)]
    return pl.pallas_call(
        functools.partial(_mixer_kernel, T=T, has_state=has_state, n_prev=n_prev),
        out_shape=out_shape,
        grid=(n_seq,),
        in_specs=in_specs,
        out_specs=out_specs,
        scratch_shapes=[pltpu.VMEM((2, HEADS, HEAD_DIM, HEAD_DIM), F32), pltpu.VMEM((T, HGRN_WIDTH), F32)],
        compiler_params=pltpu.CompilerParams(dimension_semantics=("arbitrary",), vmem_limit_bytes=VMEM_LIMIT),
        name="mixer_latent" if has_state else "mixer_context",
    )(*args)


def _merge_kernel(*refs, n_stream):
    (mixc_ref, mixl_ref, gates_ref, mod_ref, wp_ref, wo_ref, g1_ref, b1_ref, wrh_ref, wrl_ref, br_ref,
     x1_ref, h2_ref, apos_ref, gate_ref, n16_ref) = refs[n_stream:]
    W = HGRN_WIDTH
    gates = gates_ref[...].astype(F32)
    mix = _stream_tile((mixc_ref, mixl_ref))
    edges = (0, W, W + FOURIER_WIDTH, D_MODEL)
    merged = None
    for j in range(N_BRANCHES):
        p = _dot(mix[:, edges[j]:edges[j + 1]], wp_ref[edges[j]:edges[j + 1], :])
        term = gates[:, j * D_MODEL:(j + 1) * D_MODEL] * p
        merged = term if merged is None else merged + term
    y = _dot(merged.astype(BF16), wo_ref[...])
    x1 = _ln(ALPHA * _stream_tile(refs[:n_stream]) + mod_ref[2:3, :] * y) * g1_ref[...] + b1_ref[...]
    x1_ref[...] = x1
    h2 = _ln(x1) * (1.0 + mod_ref[4:5, :]) + mod_ref[3:4, :]
    h2_hi = h2.astype(BF16)
    h2_ref[...] = h2_hi
    h2_lo = (h2 - h2_hi.astype(F32)).astype(BF16)
    logits = _dot(h2_hi, wrh_ref[...]) + (_dot(h2_hi, wrl_ref[...]) + _dot(h2_lo, wrh_ref[...]))
    scores = jax.nn.sigmoid(logits)
    sel = scores + br_ref[...]
    lane_f = lax.broadcasted_iota(jnp.int32, sel.shape, 1).astype(F32)
    hits = []
    for _ in range(TOP_K):
        top = jnp.max(sel, axis=-1, keepdims=True)
        first = jnp.min(jnp.where(sel == top, lane_f, float(N_EXPERTS)), axis=-1, keepdims=True)
        hit = lane_f == first
        hits.append(hit)
        sel = jnp.where(hit, -jnp.inf, sel)
    chosen = functools.reduce(jnp.logical_or, hits)
    picked = jnp.where(chosen, scores, 0.0)
    denom = jnp.sum(picked, axis=-1, keepdims=True) + 1e-20
    gate = ROUTED_SCALE * picked / denom

    chosen_f = chosen.astype(F32)
    r_i = lax.broadcasted_iota(jnp.int32, (TOK_TILE, TOK_TILE), 0)
    c_i = lax.broadcasted_iota(jnp.int32, (TOK_TILE, TOK_TILE), 1)
    earlier = (c_i < r_i).astype(F32).astype(BF16)
    rank = _dot(earlier, chosen_f.astype(BF16))
    count = jnp.sum(chosen_f, axis=0, keepdims=True)
    n16 = jnp.floor((count + (ROW_CHUNK - 1)) * (1.0 / ROW_CHUNK))
    e_r = lax.broadcasted_iota(jnp.int32, (N_EXPERTS, N_EXPERTS), 0)
    e_c = lax.broadcasted_iota(jnp.int32, (N_EXPERTS, N_EXPERTS), 1)
    before = (e_r < e_c).astype(F32).astype(BF16)
    seg_start = ROW_CHUNK * _dot(jnp.broadcast_to(n16, (8, N_EXPERTS)).astype(BF16), before)[0:1, :]
    row_of = seg_start + rank
    lane_k = lax.broadcasted_iota(jnp.int32, (TOK_TILE, TOP_K), 1)
    apos = jnp.zeros((TOK_TILE, TOP_K), F32)
    for k, hit in enumerate(hits):
        apos = jnp.where(lane_k == k, jnp.sum(jnp.where(hit, row_of, 0.0), axis=-1, keepdims=True), apos)
    apos_ref[...] = apos
    gate_ref[...] = jnp.concatenate([gate, jnp.zeros((TOK_TILE, GATE_LANES - N_EXPERTS), F32)], axis=-1).astype(BF16)
    n16_ref[...] = n16


def _merge(stream, mix_ctx, mix_lat, gates, mod, wp, wo, ln_g, ln_b, wr_hi, wr_lo, b_router, layer):
    tok = lambda n: pl.BlockSpec((TOK_TILE, n), lambda i: (i, 0))
    per_layer = lambda r, c: pl.BlockSpec((None, r, c), lambda i: (layer, 0, 0))
    return pl.pallas_call(
        functools.partial(_merge_kernel, n_stream=len(stream)),
        out_shape=[jax.ShapeDtypeStruct((N_TOK, D_MODEL), F32), jax.ShapeDtypeStruct((N_TOK, D_MODEL), BF16),
                   jax.ShapeDtypeStruct((N_TOK, TOP_K), F32), jax.ShapeDtypeStruct((N_TOK, GATE_LANES), BF16),
                   jax.ShapeDtypeStruct((N_TILES, 1, N_EXPERTS), F32)],
        grid=(N_TILES,),
        in_specs=_stream_specs(len(stream)) + _stream_specs(2) + [
            tok(GATE_COLS),
            pl.BlockSpec((None, None, 6, D_MODEL), lambda i: (layer, _cond_row_of_tile(i, TOK_TILE), 0, 0)),
            per_layer(D_MODEL, D_MODEL), per_layer(D_MODEL, D_MODEL), per_layer(1, D_MODEL), per_layer(1, D_MODEL),
            per_layer(D_MODEL, N_EXPERTS), per_layer(D_MODEL, N_EXPERTS), per_layer(1, N_EXPERTS),
        ],
        out_specs=[tok(D_MODEL), tok(D_MODEL), tok(TOP_K), tok(GATE_LANES),
                   pl.BlockSpec((None, 1, N_EXPERTS), lambda i: (i, 0, 0))],
        compiler_params=pltpu.CompilerParams(dimension_semantics=("arbitrary",), vmem_limit_bytes=VMEM_LIMIT),
        name="merge_router",
    )(*stream, mix_ctx, mix_lat, gates, mod, wp, wo, ln_g, ln_b, wr_hi, wr_lo, b_router)


def _routing_tables(n16):
    n16 = n16.astype(jnp.int32)
    hi = jnp.cumsum(n16, axis=1)
    lo = hi - n16
    n_chunks = hi[:, -1]
    eo = jnp.cumsum(n16, axis=0) - n16
    tc = jnp.sum(n16, axis=0)
    tiles = (tc + CHUNKS_PER_GMM_TILE - 1) // CHUNKS_PER_GMM_TILE
    tile_end = jnp.cumsum(tiles)
    tile_off = tile_end - tiles
    n_used = tile_end[-1]
    slot = CHUNKS_PER_GMM_TILE * tile_off[None, :] + eo
    copies = (_copy_runs(n16 // 2, lo, slot, 2, MAX_PAIRS)
              + _copy_runs(n16 % 2, lo + n16 - 1, slot + n16 - 1, 1, N_EXPERTS))
    pad_start = CHUNKS_PER_GMM_TILE * tile_off + tc
    pad_count = CHUNKS_PER_GMM_TILE * tiles - tc
    return (copies, n_chunks, tile_off.astype(jnp.int32), tiles.astype(jnp.int32),
            jnp.reshape(n_used, (1,)).astype(jnp.int32), pad_start.astype(jnp.int32), pad_count.astype(jnp.int32))


def _copy_runs(count, first_src, first_dst, stride, max_runs):
    hi = jnp.cumsum(count, axis=1)
    lo = hi - count
    r = jnp.arange(max_runs, dtype=jnp.int32)[None, :, None]
    owner = jnp.logical_and(r >= lo[:, None, :], r < hi[:, None, :])

    def table(first):
        picked = jnp.sum(jnp.where(owner, (first - stride * lo)[:, None, :], 0), axis=-1)
        return (stride * r[:, :, 0] + picked).reshape(-1).astype(jnp.int32)

    return table(first_src), table(first_dst), hi[:, -1].astype(jnp.int32)


def _one_hot_rows(apos_ref, n_rows):
    r = lax.broadcasted_iota(jnp.int32, (TOK_TILE, n_rows), 1).astype(F32)
    pt = jnp.zeros((TOK_TILE, n_rows), F32)
    for k in range(TOP_K):
        pt = jnp.where(r == apos_ref[:, k:k + 1], 1.0, pt)
    return pt.astype(BF16)


def _chunk_copy(src_ref, src_chunk, dst_ref, dst_chunk, sem):
    rows = lambda c: pl.ds(pl.multiple_of(c * ROW_CHUNK, ROW_CHUNK), ROW_CHUNK)
    return pltpu.make_async_copy(src_ref.at[rows(src_chunk), :], dst_ref.at[rows(dst_chunk), :], sem)


def _run_rows(chunk, n_chunks):
    return pl.ds(pl.multiple_of(chunk * ROW_CHUNK, ROW_CHUNK), n_chunks * ROW_CHUNK)


def _for_each_chunk(n, fn):
    def trip(i, carry):
        for u in range(CHUNK_LOOP_UNROLL):
            c = i * CHUNK_LOOP_UNROLL + u
            if u == 0:
                fn(c)
            else:
                pl.when(c < n)(functools.partial(fn, c))
        return carry
    lax.fori_loop(0, (n + (CHUNK_LOOP_UNROLL - 1)) // CHUNK_LOOP_UNROLL, trip, 0)


def _dispatch_kernel(psrc_ref, pdst_ref, pn_ref, ssrc_ref, sdst_ref, sn_ref, nch_ref, pstart_ref, pcount_ref, nu_ref,
                     h_ref, gate_ref, apos_ref, xs_hbm, pt_ref, stage_ref, zero_ref, sem):
    j = pl.program_id(0)
    buf = j % 2
    runs = ((psrc_ref, pdst_ref, pn_ref), (ssrc_ref, sdst_ref, sn_ref))
    pad_sem, tile_sem = 2 * len(runs), 2 * len(runs) + 1

    def run_copy(k, b, src_chunk, slot):
        return pltpu.make_async_copy(stage_ref.at[b, _run_rows(src_chunk, RUN_CHUNKS[k]), :],
                                     xs_hbm.at[_run_rows(slot, RUN_CHUNKS[k]), :], sem.at[2 * k + b])

    def wait_tile(b, jj):
        for k, (_, _, n_ref) in enumerate(runs):
            _for_each_chunk(n_ref[jj], lambda r, k=k: run_copy(k, b, 0, 0).wait())

    @pl.when(j < N_TILES)
    def _():
        @pl.when(j >= 2)
        def _():
            wait_tile(buf, j - 2)

        def sort_rows(n_rows):
            pt = _one_hot_rows(apos_ref, n_rows)
            pt_ref[:, 0:n_rows] = pt
            if n_rows < TILE_ROWS:
                pt_ref[:, n_rows:TILE_ROWS] = jnp.zeros((TOK_TILE, TILE_ROWS - n_rows), BF16)
            h = h_ref[...]
            gate = gate_ref[...]
            for r0 in range(0, n_rows, DISPATCH_ROWS):
                group = pt[:, r0:r0 + DISPATCH_ROWS]
                stage_ref[buf, r0:r0 + DISPATCH_ROWS, 0:D_MODEL] = _dot_tn(group, h).astype(BF16)
                stage_ref[buf, r0:r0 + DISPATCH_ROWS, D_MODEL:ROW_WIDTH] = _dot_tn(group, gate).astype(BF16)

        short = nch_ref[j] * ROW_CHUNK <= SHORT_ROWS
        pl.when(short)(functools.partial(sort_rows, SHORT_ROWS))
        pl.when(jnp.logical_not(short))(functools.partial(sort_rows, TILE_ROWS))

        for k, (src_ref, slot_ref, n_ref) in enumerate(runs):
            base = j * RUN_TABLE_STRIDE[k]
            _for_each_chunk(n_ref[j], lambda r, k=k, base=base, src_ref=src_ref, slot_ref=slot_ref: run_copy(
                k, buf, src_ref[base + r], slot_ref[base + r]).start())

    @pl.when(j == N_TILES)
    def _():
        zero_ref[...] = jnp.zeros_like(zero_ref)

        def tile_copy(t):
            rows = pl.ds(pl.multiple_of(t * GMM_TILE, GMM_TILE), GMM_TILE)
            return pltpu.make_async_copy(zero_ref, xs_hbm.at[rows, :], sem.at[tile_sem])

        def per_expert(e, total):
            def fill(i, carry):
                _chunk_copy(zero_ref, 0, xs_hbm, pstart_ref[e] + i, sem.at[pad_sem]).start()
                return carry
            lax.fori_loop(0, pcount_ref[e], fill, 0)
            return total + pcount_ref[e]
        n_pad = lax.fori_loop(0, N_EXPERTS, per_expert, 0)

        def fill_tile(t, carry):
            tile_copy(t).start()
            return carry
        lax.fori_loop(nu_ref[0], MAX_GMM_TILES, fill_tile, 0)
        wait_tile(0, N_TILES - 2)
        wait_tile(1, N_TILES - 1)

        def drain(i, carry):
            _chunk_copy(zero_ref, 0, xs_hbm, 0, sem.at[pad_sem]).wait()
            return carry
        lax.fori_loop(0, n_pad, drain, 0)

        def drain_tile(t, carry):
            tile_copy(0).wait()
            return carry
        lax.fori_loop(nu_ref[0], MAX_GMM_TILES, drain_tile, 0)


def _dispatch(h2, gate, apos, copies, n_chunks, pad_start, pad_count, n_used):
    last = N_TILES - 1
    tok = lambda n: pl.BlockSpec((TOK_TILE, n), lambda j, *_: (jnp.minimum(j, last), 0))
    grid_spec = pltpu.PrefetchScalarGridSpec(
        num_scalar_prefetch=len(copies) + 4,
        grid=(N_TILES + 1,),
        in_specs=[tok(D_MODEL), tok(GATE_LANES), tok(TOP_K)],
        out_specs=[pl.BlockSpec(memory_space=pl.ANY), tok(TILE_ROWS)],
        scratch_shapes=[pltpu.VMEM((2, TILE_ROWS, ROW_WIDTH), BF16), pltpu.VMEM((GMM_TILE, ROW_WIDTH), BF16),
                        pltpu.SemaphoreType.DMA((2 * len(RUN_CHUNKS) + 2,))],
    )
    return pl.pallas_call(
        _dispatch_kernel,
        out_shape=[jax.ShapeDtypeStruct((SORTED_ROWS, ROW_WIDTH), BF16),
                   jax.ShapeDtypeStruct((N_TOK, TILE_ROWS), BF16)],
        grid_spec=grid_spec,
        compiler_params=pltpu.CompilerParams(dimension_semantics=("arbitrary",), vmem_limit_bytes=VMEM_LIMIT),
        name="moe_dispatch",
    )(*copies, n_chunks, pad_start, pad_count, n_used, h2, gate, apos)


def _gmm_kernel(off_ref, cnt_ref, nu_ref, x_hbm, wg_ref, wu_ref, wd_ref, y_hbm, xbuf, ybuf, in_sem, out_sem):
    e = pl.program_id(0)
    n_used = nu_ref[0]

    def rows(g):
        return pl.ds(pl.multiple_of(g * GMM_TILE, GMM_TILE), GMM_TILE)

    def fetch(g):
        slot = g % GMM_IN_SLOTS
        return pltpu.make_async_copy(x_hbm.at[rows(g), :], xbuf.at[slot], in_sem.at[slot])

    def flush(g):
        slot = g % GMM_OUT_SLOTS
        return pltpu.make_async_copy(ybuf.at[slot], y_hbm.at[rows(g), 0:D_MODEL], out_sem.at[slot])

    @pl.when(e == 0)
    def _():
        for g in range(GMM_AHEAD):
            @pl.when(g < n_used)
            def _(g=g):
                fetch(g).start()


    def process(g0, width):
        for w in range(width):
            @pl.when(g0 + w + GMM_AHEAD < n_used)
            def _(w=w):
                fetch(g0 + w + GMM_AHEAD).start()
        for w in range(width):
            fetch(g0 + w).wait()
        xg = jnp.concatenate([xbuf[(g0 + w) % GMM_IN_SLOTS] for w in range(width)], axis=0)
        x = xg[:, 0:D_MODEL]
        gates = xg[:, D_MODEL:ROW_WIDTH].astype(F32)
        lane = lax.broadcasted_iota(jnp.int32, gates.shape, 1)
        weight = jnp.sum(jnp.where(lane == e, gates, 0.0), axis=-1, keepdims=True)
        hid = _silu(_dot(x, wg_ref[...].astype(BF16))) * _dot(x, wu_ref[...].astype(BF16))
        y = (_dot(hid.astype(BF16), wd_ref[...].astype(BF16)) * weight).astype(BF16)
        for w in range(width):
            g = g0 + w

            @pl.when(g >= GMM_OUT_SLOTS)
            def _(g=g):
                flush(g - GMM_OUT_SLOTS).wait()

            ybuf[g % GMM_OUT_SLOTS] = y[w * GMM_TILE:(w + 1) * GMM_TILE]
            flush(g).start()

    first = off_ref[e]
    count = cnt_ref[e]

    done = 0
    width = GMM_WIDTH
    while width >= 1:
        trips = (count - done) // width

        def block(p, carry, done=done, width=width):
            process(first + done + width * p, width)
            return carry

        lax.fori_loop(0, trips, block, 0)
        done = done + trips * width
        width //= 2

    @pl.when(e == N_EXPERTS - 1)
    def _():
        for k in range(GMM_OUT_SLOTS):
            @pl.when(n_used - 1 - k >= 0)
            def _(k=k):
                flush(n_used - 1 - k).wait()


def _gmm(xs, weg, weu, wed, tile_off, tiles, n_used, layer):
    w_in_spec = pl.BlockSpec((None, None, D_MODEL, D_EXPERT), lambda e, *_: (layer, e, 0, 0))
    grid_spec = pltpu.PrefetchScalarGridSpec(
        num_scalar_prefetch=3,
        grid=(N_EXPERTS,),
        in_specs=[
            pl.BlockSpec(memory_space=pl.ANY),
            w_in_spec, w_in_spec,
            pl.BlockSpec((None, None, D_EXPERT, D_MODEL), lambda e, *_: (layer, e, 0, 0)),
        ],
        out_specs=pl.BlockSpec(memory_space=pl.ANY),
        scratch_shapes=[pltpu.VMEM((GMM_IN_SLOTS, GMM_TILE, ROW_WIDTH), BF16),
                        pltpu.VMEM((GMM_OUT_SLOTS, GMM_TILE, D_MODEL), BF16),
                        pltpu.SemaphoreType.DMA((GMM_IN_SLOTS,)), pltpu.SemaphoreType.DMA((GMM_OUT_SLOTS,))],
    )
    return pl.pallas_call(
        _gmm_kernel,
        out_shape=jax.ShapeDtypeStruct((SORTED_ROWS, ROW_WIDTH), BF16),
        grid_spec=grid_spec,
        input_output_aliases={3: 0},
        compiler_params=pltpu.CompilerParams(dimension_semantics=("arbitrary",), vmem_limit_bytes=VMEM_LIMIT),
        name="moe_gmm",
    )(tile_off, tiles, n_used, xs, weg, weu, wed)


def _combine_kernel(psrc_ref, pdst_ref, pn_ref, ssrc_ref, sdst_ref, sn_ref, nch_ref, ys_hbm, pt_ref, h_ref, x1_ref,
                    mod_ref, wsg_ref, wsu_ref, wsd_ref, g2_ref, b2_ref, oc_ref, ol_ref, stage_ref, routed_ref, sem):
    j = pl.program_id(0)
    buf = j % 2
    runs = ((psrc_ref, pdst_ref, pn_ref), (ssrc_ref, sdst_ref, sn_ref))

    def run_fetch(k, b, slot, chunk):
        return pltpu.make_async_copy(ys_hbm.at[_run_rows(slot, RUN_CHUNKS[k]), 0:D_MODEL],
                                     stage_ref.at[b, _run_rows(chunk, RUN_CHUNKS[k]), :], sem.at[2 * k + b])

    def fetch(jj, b):
        for k, (chunk_ref, slot_ref, n_ref) in enumerate(runs):
            base = jj * RUN_TABLE_STRIDE[k]
            _for_each_chunk(n_ref[jj], lambda r, k=k, base=base, chunk_ref=chunk_ref, slot_ref=slot_ref: run_fetch(
                k, b, slot_ref[base + r], chunk_ref[base + r]).start())

    @pl.when(j == 0)
    def _():
        stage_ref[...] = jnp.zeros_like(stage_ref)
        fetch(0, 0)

    @pl.when(j + 1 < N_TILES)
    def _():
        fetch(j + 1, 1 - buf)

    h = h_ref[...]
    hid = _silu(_dot(h, wsg_ref[...])) * _dot(h, wsu_ref[...])
    shared = _dot(hid.astype(BF16), wsd_ref[...])

    for k, (_, _, n_ref) in enumerate(runs):
        _for_each_chunk(n_ref[j], lambda r, k=k: run_fetch(k, buf, 0, 0).wait())

    def gather_rows(n_rows):
        routed_ref[...] = _dot(pt_ref[:, 0:n_rows], stage_ref[buf, 0:n_rows, :])

    short = nch_ref[j] * ROW_CHUNK <= SHORT_ROWS
    pl.when(short)(functools.partial(gather_rows, SHORT_ROWS))
    pl.when(jnp.logical_not(short))(functools.partial(gather_rows, TILE_ROWS))
    r = ALPHA * x1_ref[...] + mod_ref[5:6, :] * (routed_ref[...] + shared)
    out = _ln(r) * g2_ref[...] + b2_ref[...]

    @pl.when(j < CTX_TILES)
    def _():
        oc_ref[...] = out

    @pl.when(j >= CTX_TILES)
    def _():
        ol_ref[...] = out


def _combine(ys, pt, h2, x1, mod, wsg, wsu, wsd, ln_g, ln_b, copies, n_chunks, layer):
    tok = lambda n: pl.BlockSpec((TOK_TILE, n), lambda j, *_: (j, 0))
    per_layer = lambda r, c: pl.BlockSpec((None, r, c), lambda j, *_: (layer, 0, 0))
    grid_spec = pltpu.PrefetchScalarGridSpec(
        num_scalar_prefetch=len(copies) + 1,
        grid=(N_TILES,),
        in_specs=[
            pl.BlockSpec(memory_space=pl.ANY),
            tok(TILE_ROWS), tok(D_MODEL), tok(D_MODEL),
            pl.BlockSpec((None, None, 6, D_MODEL), lambda j, *_: (layer, _cond_row_of_tile(j, TOK_TILE), 0, 0)),
            per_layer(D_MODEL, D_SHARED), per_layer(D_MODEL, D_SHARED), per_layer(D_SHARED, D_MODEL),
            per_layer(1, D_MODEL), per_layer(1, D_MODEL),
        ],
        out_specs=_stream_specs(2),
        scratch_shapes=[pltpu.VMEM((2, TILE_ROWS, D_MODEL), BF16), pltpu.VMEM((TOK_TILE, D_MODEL), F32),
                        pltpu.SemaphoreType.DMA((2 * len(RUN_CHUNKS),))],
    )
    return pl.pallas_call(
        _combine_kernel,
        out_shape=[jax.ShapeDtypeStruct((N_CTX, D_MODEL), F32), jax.ShapeDtypeStruct((N_LAT, D_MODEL), F32)],
        grid_spec=grid_spec,
        compiler_params=pltpu.CompilerParams(dimension_semantics=("arbitrary",), vmem_limit_bytes=VMEM_LIMIT),
        name="moe_combine",
    )(*copies, n_chunks, ys, pt, h2, x1, mod, wsg, wsu, wsd, ln_g, ln_b)


def kernel(x_prompt, x_sample, c, state_hgrn, c_ctx, w_ada, b_ada, w_in, b_gate, hgrn_lb, hgrn_norm, conv_w,
           w_proj_hgrn, w_proj_fourier, w_proj_conv, w_out, ln1_g, ln1_b, ln2_g, ln2_b, w_router, b_router,
           w_exp_gate, w_exp_up, w_exp_down, w_sh_gate, w_sh_up, w_sh_down):
    lb_sm = jax.nn.softmax(hgrn_lb.astype(F32), axis=0)
    lb_all = jnp.cumsum(lb_sm, axis=0) - lb_sm[:1]

    pos = jnp.asarray(_pos_emb_table())
    dft_ctx = jnp.asarray(_dft_time(SEQ)).astype(BF16)
    dft_lat = jnp.asarray(_dft_time(DEC_SEQ)).astype(BF16)
    bdc, bds = (jnp.asarray(m).astype(BF16) for m in _dft_channel())

    cond = jnp.concatenate([c_ctx[None, :], c, jnp.zeros((N_COND - 1 - DEC_BATCH, D_MODEL), F32)], axis=0)
    mod = _ada_mod(cond, w_ada, b_ada).reshape(DEPTH, N_COND, 6, D_MODEL)

    wp_b = jnp.concatenate([w_proj_hgrn, w_proj_fourier, w_proj_conv], axis=1).astype(BF16)
    wo_b = w_out.astype(BF16)
    wr_hi = w_router.astype(BF16)
    wr_lo = (w_router - wr_hi.astype(F32)).astype(BF16)
    wsg_b, wsu_b, wsd_b = w_sh_gate.astype(BF16), w_sh_up.astype(BF16), w_sh_down.astype(BF16)
    rows = lambda a: a[:, None, :]
    b_gate3, ln1_g3, ln1_b3, ln2_g3, ln2_b3, b_router3 = map(rows, (b_gate, ln1_g, ln1_b, ln2_g, ln2_b, b_router))

    stream = (x_prompt.reshape(N_CTX, D_MODEL), x_sample.reshape(N_LAT, D_MODEL), pos)
    states = None
    for l in range(DEPTH):
        uh, uf, uc, gates = _in_proj(stream, mod, w_in, b_gate3, l)
        small = (lb_all, hgrn_norm[:, None, :], conv_w)
        mix_ctx, states = _mixer(uh, uf, uc, *small, dft_ctx, bdc, bds, T=SEQ, n_seq=BATCH, blk0=0, layer=l,
                                 prev_states=states)
        (mix_lat,) = _mixer(uh, uf, uc, *small, dft_lat, bdc, bds, T=DEC_SEQ, n_seq=DEC_BATCH,
                            blk0=N_CTX // DEC_SEQ, state0=state_hgrn, layer=l)
        x1, h2, apos, gate, n16 = _merge(stream, mix_ctx, mix_lat, gates, mod, wp_b, wo_b, ln1_g3, ln1_b3,
                                         wr_hi, wr_lo, b_router3, l)
        copies, n_chunks, tile_off, tiles, n_used, pad_start, pad_count = _routing_tables(
            n16.reshape(N_TILES, N_EXPERTS))
        xs, pt = _dispatch(h2, gate, apos, copies, n_chunks, pad_start, pad_count, n_used)
        ys = _gmm(xs, w_exp_gate, w_exp_up, w_exp_down, tile_off, tiles, n_used, l)
        stream = tuple(_combine(ys, pt, h2, x1, mod, wsg_b, wsu_b, wsd_b, ln2_g3, ln2_b3, copies, n_chunks, l))
    y_prompt = stream[0].reshape(BATCH, SEQ, D_MODEL)
    y_sample = stream[1].reshape(DEC_BATCH, DEC_SEQ, D_MODEL)
    return (y_prompt, y_sample, states)
```
